```python
import jax, jax.numpy as jnp
from jax import lax
import numpy as np

D_MODEL = 1024
BATCH = 1
SEQ = 16384
DEPTH = 4

HEAD_DIM = 64
NSA_HEADS = 8
NSA_KV_GROUPS = 1
NSA_HPG = NSA_HEADS // NSA_KV_GROUPS
SB_HEADS = 4
SB_HEAD_DIM = 128
ROPE_THETA = 500000.0
ROT_DIM = HEAD_DIM // 4
CMP_BLOCK = 32
CMP_STRIDE = 16
CMP_HIDDEN = 256
SEL_BLOCK = 64
SEL_TOP_N = 8
WINDOW = 512
Q_BLOCK = 128
D_FF = 4 * D_MODEL
RMS_EPS = 1e-6
NSA_W = NSA_HEADS * HEAD_DIM
KV_W = NSA_KV_GROUPS * HEAD_DIM
SB_W = SB_HEADS * SB_HEAD_DIM
IN_W = NSA_W + 6 * KV_W + 3 * NSA_HEADS + 3 * SB_W + 2 * D_MODEL

kernel_name = 'hybrid_nsa_stickbreaking_sqrelu_block'


def rms_norm(x, g):
    xf = x.astype(jnp.float32)
    y = xf * lax.rsqrt(jnp.mean(xf * xf, axis=-1, keepdims=True) + RMS_EPS)
    return (y * g.astype(jnp.float32)).astype(x.dtype)


def to_heads(t, n, dh):
    b, s = t.shape[:2]
    return t.reshape(b, s, n, dh).transpose(0, 2, 1, 3)


def partial_rope(x, positions):
    half = ROT_DIM // 2
    inv_freq = jnp.power(ROPE_THETA, jnp.arange(half, dtype=jnp.float32) * (-2.0 / ROT_DIM))
    ang = positions.astype(jnp.float32)[:, None, :, None] * inv_freq
    cos, sin = jnp.cos(ang), jnp.sin(ang)
    xf = x.astype(jnp.float32)
    x1, x2, rest = xf[..., :half], xf[..., half:ROT_DIM], xf[..., ROT_DIM:]
    out = jnp.concatenate([x1 * cos - x2 * sin, x2 * cos + x1 * sin, rest], axis=-1)
    return out.astype(x.dtype)


def compress_blocks(kv, pe, w1, w2):
    s = kv.shape[2]
    nc = (s - CMP_BLOCK) // CMP_STRIDE + 1
    idx = CMP_STRIDE * jnp.arange(nc)[:, None] + jnp.arange(CMP_BLOCK)[None, :]
    blocks = kv[:, :, idx] + pe
    flat = blocks.reshape(blocks.shape[:3] + (CMP_BLOCK * HEAD_DIM,))
    return jax.nn.gelu(flat @ w1) @ w2


def nsa_attention(q_rope, q_plain, k_cmp, v_cmp, k_slc, v_slc, k_win, v_win, gates):
    out_dtype = q_rope.dtype
    f32 = jnp.float32
    scale = HEAD_DIM ** -0.5
    q_rope, q_plain, gates = q_rope.astype(f32) * scale, q_plain.astype(f32) * scale, gates.astype(f32)
    k_cmp, v_cmp = k_cmp.astype(f32), v_cmp.astype(f32)
    b, g, h, s, dh = q_rope.shape
    nb = s // Q_BLOCK
    nc = k_cmp.shape[2]
    ns = s // SEL_BLOCK
    n_sel = min(SEL_TOP_N, ns)
    cmp_last = CMP_STRIDE * jnp.arange(nc) + CMP_BLOCK - 1
    c_start = CMP_STRIDE * jnp.arange(nc)[:, None]
    s_start = SEL_BLOCK * jnp.arange(ns)[None, :]
    overlap = ((c_start < s_start + SEL_BLOCK) & (c_start + CMP_BLOCK > s_start)).astype(f32)
    ks_blocks = k_slc.astype(f32).reshape(b, g, ns, SEL_BLOCK, dh)
    vs_blocks = v_slc.astype(f32).reshape(b, g, ns, SEL_BLOCK, dh)
    pad = ((0, 0), (0, 0), (WINDOW, 0), (0, 0))
    kw_pad = jnp.pad(k_win.astype(f32), pad)
    vw_pad = jnp.pad(v_win.astype(f32), pad)
    b_ix = jnp.arange(b)[:, None, None, None]
    g_ix = jnp.arange(g)[None, :, None, None]
    blk = jnp.arange(ns)

    def block(i):
        q0 = i * Q_BLOCK
        t = q0 + jnp.arange(Q_BLOCK)
        qr = lax.dynamic_slice_in_dim(q_rope, q0, Q_BLOCK, axis=3)
        qp = lax.dynamic_slice_in_dim(q_plain, q0, Q_BLOCK, axis=3)
        gt = lax.dynamic_slice_in_dim(gates, q0, Q_BLOCK, axis=3)
        vis_c = cmp_last[None, :] <= t[:, None]
        s_c = jnp.einsum('bghqd,bgcd->bghqc', qp, k_cmp)
        p_c = jax.nn.softmax(jnp.where(vis_c, s_c, -1e30), axis=-1) * vis_c.astype(f32)
        o_c = jnp.einsum('bghqc,bgcd->bghqd', p_c, v_cmp)
        imp = jnp.einsum('bghqc,cn->bgqn', p_c, overlap)
        cur = t // SEL_BLOCK
        valid = blk[None, :] <= cur[:, None]
        forced = (blk[None, :] == 0) | (blk[None, :] == cur[:, None]) | (blk[None, :] == cur[:, None] - 1)
        score = jnp.where(valid, jnp.where(forced, jnp.inf, imp), -1.0)
        _, sel = lax.top_k(score, n_sel)
        k_g = ks_blocks[b_ix, g_ix, sel].reshape(b, g, Q_BLOCK, n_sel * SEL_BLOCK, dh)
        v_g = vs_blocks[b_ix, g_ix, sel].reshape(b, g, Q_BLOCK, n_sel * SEL_BLOCK, dh)
        pos = sel[..., None] * SEL_BLOCK + jnp.arange(SEL_BLOCK)
        vis_s = (pos <= t[:, None, None]).reshape(b, g, Q_BLOCK, n_sel * SEL_BLOCK)
        s_s = jnp.einsum('bghqd,bgqkd->bghqk', qr, k_g)
        p_s = jax.nn.softmax(jnp.where(vis_s[:, :, None], s_s, -1e30), axis=-1)
        o_s = jnp.einsum('bghqk,bgqkd->bghqd', p_s, v_g)
        kw = lax.dynamic_slice_in_dim(kw_pad, q0, Q_BLOCK + WINDOW, axis=2)
        vw = lax.dynamic_slice_in_dim(vw_pad, q0, Q_BLOCK + WINDOW, axis=2)
        kpos = q0 - WINDOW + jnp.arange(Q_BLOCK + WINDOW)
        vis_w = (kpos[None, :] <= t[:, None]) & (kpos[None, :] > t[:, None] - WINDOW) & (kpos[None, :] >= 0)
        s_w = jnp.einsum('bghqd,bgkd->bghqk', qr, kw)
        p_w = jax.nn.softmax(jnp.where(vis_w, s_w, -1e30), axis=-1)
        o_w = jnp.einsum('bghqk,bgkd->bghqd', p_w, vw)
        return gt[..., 0:1] * o_c + gt[..., 1:2] * o_s + gt[..., 2:3] * o_w

    out = lax.map(block, jnp.arange(nb))
    out = out.transpose(1, 0, 4, 2, 3, 5).reshape(b, s, g * h * dh)
    return out.astype(out_dtype)


def stick_breaking_attention(q, k, v):
    out_dtype = q.dtype
    f32 = jnp.float32
    b, hh, s, dh = q.shape
    qf = q.astype(f32) * (SB_HEAD_DIM ** -0.5)
    kf, vf = k.astype(f32), v.astype(f32)
    nb = s // Q_BLOCK
    r = jnp.arange(Q_BLOCK)
    tri = (r[:, None] >= r[None, :]).astype(f32)
    diag_mask = r[None, :] < r[:, None]
    outs = []
    for i in range(nb):
        q0 = i * Q_BLOCK
        qb = qf[:, :, q0:q0 + Q_BLOCK]
        z_d = jnp.einsum('bhqd,bhkd->bhqk', qb, kf[:, :, q0:q0 + Q_BLOCK])
        l_d = jnp.where(diag_mask, jax.nn.log_sigmoid(-z_d), 0.0)
        c_d = l_d @ tri
        a_d = jnp.where(diag_mask, jnp.exp(z_d + c_d), 0.0)
        o = jnp.einsum('bhqk,bhkd->bhqd', a_d, vf[:, :, q0:q0 + Q_BLOCK])
        if i > 0:
            z_o = jnp.einsum('bhqd,bhkd->bhqk', qb, kf[:, :, :q0]).reshape(b, hh, Q_BLOCK, i, Q_BLOCK)
            c_o = jax.nn.log_sigmoid(-z_o) @ tri
            ci = jnp.arange(i)
            upper = (ci[:, None] > ci[None, :]).astype(f32)
            off = c_o[..., 0] @ upper + c_d[..., 0:1]
            a_o = jnp.exp(z_o + c_o + off[..., None])
            o = o + jnp.einsum('bhqck,bhckd->bhqd', a_o, vf[:, :, :q0].reshape(b, hh, i, Q_BLOCK, dh))
        outs.append(o)
    out = jnp.concatenate(outs, axis=2)
    return out.transpose(0, 2, 1, 3).reshape(b, s, hh * dh).astype(out_dtype)


def split_columns(proj):
    sizes = [NSA_W, KV_W, KV_W, KV_W, KV_W, KV_W, KV_W, 3 * NSA_HEADS, SB_W, SB_W, SB_W, 2 * D_MODEL]
    parts, start = [], 0
    for n in sizes:
        parts.append(proj[..., start:start + n])
        start += n
    return parts


def hybrid_layer(x, positions, norm_g, w_in, cmp_pe, cmp_w1, cmp_w2, w_nsa_o, w_sb_o, w_out, w_ff1, w_ff2):
    b, s, _ = x.shape
    hn = rms_norm(x, norm_g[0])
    (nsa_q, k_c, v_c, k_s, v_s, k_w, v_w, nsa_g, sb_q, sb_k, sb_v, merge_g) = split_columns(hn @ w_in)
    grp = (b, NSA_KV_GROUPS, NSA_HPG, s, HEAD_DIM)
    q_plain = to_heads(nsa_q, NSA_HEADS, HEAD_DIM)
    q_rope = partial_rope(q_plain, positions).reshape(grp)
    q_plain = q_plain.reshape(grp)
    kc = compress_blocks(to_heads(k_c, NSA_KV_GROUPS, HEAD_DIM), cmp_pe[0], cmp_w1[0], cmp_w2[0])
    vc = compress_blocks(to_heads(v_c, NSA_KV_GROUPS, HEAD_DIM), cmp_pe[1], cmp_w1[1], cmp_w2[1])
    ks = partial_rope(to_heads(k_s, NSA_KV_GROUPS, HEAD_DIM), positions)
    kw = partial_rope(to_heads(k_w, NSA_KV_GROUPS, HEAD_DIM), positions)
    gates = jax.nn.sigmoid(nsa_g).reshape(b, s, NSA_HEADS, 3).transpose(0, 2, 1, 3).reshape(grp[:4] + (3,))
    y_nsa = nsa_attention(q_rope, q_plain, kc, vc, ks, to_heads(v_s, NSA_KV_GROUPS, HEAD_DIM),
                          kw, to_heads(v_w, NSA_KV_GROUPS, HEAD_DIM), gates) @ w_nsa_o
    y_sb = stick_breaking_attention(to_heads(sb_q, SB_HEADS, SB_HEAD_DIM), to_heads(sb_k, SB_HEADS, SB_HEAD_DIM),
                                    to_heads(sb_v, SB_HEADS, SB_HEAD_DIM)) @ w_sb_o
    gate = jax.nn.sigmoid(merge_g)
    mixed = (gate[..., :D_MODEL] * y_nsa + gate[..., D_MODEL:] * y_sb) @ w_out
    x = x + rms_norm(mixed, norm_g[1])
    hf = rms_norm(x, norm_g[2])
    ff = jnp.square(jax.nn.relu(hf @ w_ff1)) @ w_ff2
    return x + rms_norm(ff, norm_g[3])


def setup_inputs(seed: int = 0) -> dict:
    key = jax.random.key(seed)
    ks = jax.random.split(key, 12)
    f32 = jnp.float32

    def nrm(k, shape, fan_in):
        return jax.random.normal(k, shape, f32) * (fan_in ** -0.5)

    x = jax.random.normal(ks[0], (BATCH, SEQ, D_MODEL), f32)
    positions = jnp.broadcast_to(jnp.arange(SEQ, dtype=jnp.int32)[None, :], (BATCH, SEQ))
    norm_g = 1.0 + 0.05 * jax.random.normal(ks[1], (DEPTH, 4, D_MODEL), f32)
    w_in = nrm(ks[2], (DEPTH, D_MODEL, IN_W), D_MODEL)
    cmp_pe = 0.5 * jax.random.normal(ks[3], (DEPTH, 2, CMP_BLOCK, HEAD_DIM), f32)
    cmp_w1 = nrm(ks[4], (DEPTH, 2, CMP_BLOCK * HEAD_DIM, CMP_HIDDEN), CMP_BLOCK * HEAD_DIM)
    cmp_w2 = nrm(ks[5], (DEPTH, 2, CMP_HIDDEN, HEAD_DIM), CMP_HIDDEN)
    w_nsa_o = nrm(ks[6], (DEPTH, NSA_W, D_MODEL), NSA_W)
    w_sb_o = nrm(ks[7], (DEPTH, SB_W, D_MODEL), SB_W)
    w_out = nrm(ks[8], (DEPTH, D_MODEL, D_MODEL), D_MODEL)
    w_ff1 = nrm(ks[9], (DEPTH, D_MODEL, D_FF), D_MODEL)
    w_ff2 = nrm(ks[10], (DEPTH, D_FF, D_MODEL), D_FF)
    return {'x': x, 'positions': positions, 'norm_g': norm_g, 'w_in': w_in, 'cmp_pe': cmp_pe,
            'cmp_w1': cmp_w1, 'cmp_w2': cmp_w2, 'w_nsa_o': w_nsa_o, 'w_sb_o': w_sb_o,
            'w_out': w_out, 'w_ff1': w_ff1, 'w_ff2': w_ff2}


def reference(x, positions, norm_g, w_in, cmp_pe, cmp_w1, cmp_w2, w_nsa_o, w_sb_o, w_out, w_ff1, w_ff2):
    for layer in range(DEPTH):
        x = hybrid_layer(x, positions, norm_g[layer], w_in[layer], cmp_pe[layer], cmp_w1[layer],
                         cmp_w2[layer], w_nsa_o[layer], w_sb_o[layer], w_out[layer],
                         w_ff1[layer], w_ff2[layer])
    return x
```

```python
import functools

import jax
import jax.numpy as jnp
from jax import lax
from jax.experimental import pallas as pl
from jax.experimental.pallas import tpu as pltpu

F32 = jnp.float32
MXU_DTYPE = jnp.bfloat16

HEAD_DIM = 64
NSA_HEADS = 8
SB_HEADS = 4
SB_HEAD_DIM = 128
ROPE_THETA = 500000.0
ROT_DIM = HEAD_DIM // 4
CMP_BLOCK = 32
CMP_STRIDE = 16
SEL_BLOCK = 64
SEL_TOP_N = 8
WINDOW = 512
RMS_EPS = 1e-6
NSA_W = NSA_HEADS * HEAD_DIM
SB_W = SB_HEADS * SB_HEAD_DIM
LANES = 128
MASKED = -32768.0
VMEM_LIMIT = 56 * 1024 * 1024

ROW_TILE = 512
NSA_Q_BLOCK = 128
NSA_KEY_CHUNK = 512
SB_BLOCK = 256


def _dot(a, b):
    return jnp.dot(a, b, preferred_element_type=F32)


def _dot_nt(a, b):
    return lax.dot_general(a, b, (((1,), (1,)), ((), ())), preferred_element_type=F32)


def _rms(x, g):
    return x * lax.rsqrt(jnp.mean(x * x, axis=-1, keepdims=True) + RMS_EPS) * g


def _params(n_grid_dims):
    return pltpu.CompilerParams(dimension_semantics=("arbitrary",) * n_grid_dims,
                                vmem_limit_bytes=VMEM_LIMIT)


def _resident(block_shape, index_map):
    return pl.BlockSpec(block_shape, index_map, pipeline_mode=pl.Buffered(1))


_C_KV = NSA_W
_C_GATE = _C_KV + 6 * HEAD_DIM
_C_SB = _C_GATE + LANES
_C_MERGE = _C_SB + 3 * SB_W


def _inproj_kernel(x_ref, g_ref, pos_ref, invf_ref, w_ref,
                   qp_ref, qr_ref, kc_ref, vc_ref, ksa_ref, ksb_ref, kwa_ref, kwb_ref,
                   gate_ref, sbq_ref, sbk_ref, sbv_ref, mg_ref):
    d_model = x_ref.shape[1]
    hb = _rms(x_ref[...], g_ref[...]).astype(MXU_DTYPE)

    ang = pos_ref[...].astype(F32) * invf_ref[...]
    cos, sin = jnp.cos(ang), jnp.sin(ang)
    lane = lax.broadcasted_iota(jnp.int32, (1, LANES), 1)
    dim = lane % HEAD_DIM
    half = ROT_DIM // 2
    sin_up = jnp.where((dim >= half) & (dim < ROT_DIM), sin, 0.0)
    sin_dn = jnp.where(dim < half, -sin, 0.0)
    first = lane < HEAD_DIM

    def rope(xg, c, s_up, s_dn):
        return xg * c + pltpu.roll(xg, half, 1) * s_up + pltpu.roll(xg, LANES - half, 1) * s_dn

    pa = _dot(hb, w_ref[:, 0:_C_GATE])
    scale = HEAD_DIM ** -0.5
    for j in range(NSA_W // LANES):
        qg = pa[:, j * LANES:(j + 1) * LANES]
        qp_ref[:, j * LANES:(j + 1) * LANES] = (qg * scale).astype(qp_ref.dtype)
        qr_ref[:, j * LANES:(j + 1) * LANES] = (rope(qg, cos, sin_up, sin_dn) * scale).astype(qr_ref.dtype)
    kc_ref[...] = pa[:, _C_KV:_C_KV + HEAD_DIM]
    vc_ref[...] = pa[:, _C_KV + HEAD_DIM:_C_KV + 2 * HEAD_DIM]
    cos_k = jnp.where(first, cos, 1.0)
    sin_up_k = jnp.where(first, sin_up, 0.0)
    sin_dn_k = jnp.where(first, sin_dn, 0.0)
    for a_ref, b_ref, c0 in ((ksa_ref, ksb_ref, _C_KV + 2 * HEAD_DIM), (kwa_ref, kwb_ref, _C_KV + 4 * HEAD_DIM)):
        kv = rope(pa[:, c0:c0 + LANES], cos_k, sin_up_k, sin_dn_k)
        a_ref[...] = kv.astype(a_ref.dtype)
        b_ref[...] = pltpu.roll(kv, HEAD_DIM, 1).astype(b_ref.dtype)

    gate_ref[...] = jax.nn.sigmoid(_dot(hb, w_ref[:, _C_GATE:_C_SB]))
    sb = _dot(hb, w_ref[:, _C_SB:_C_MERGE])
    sbq_ref[...] = (sb[:, 0:SB_W] * (SB_HEAD_DIM ** -0.5)).astype(sbq_ref.dtype)
    sbk_ref[...] = sb[:, SB_W:2 * SB_W].astype(sbk_ref.dtype)
    sbv_ref[...] = sb[:, 2 * SB_W:3 * SB_W].astype(sbv_ref.dtype)
    for c in range(2):
        mg_ref[:, c * d_model:(c + 1) * d_model] = jax.nn.sigmoid(
            _dot(hb, w_ref[:, _C_MERGE + c * d_model:_C_MERGE + (c + 1) * d_model]))


def _inproj(x, g, pos, invf, w_all, layer):
    s, d = x.shape
    t = min(ROW_TILE, s)
    wcols = w_all.shape[2]
    row = lambda n: pl.BlockSpec((t, n), lambda i: (i, 0))
    out_shape = [
        jax.ShapeDtypeStruct((s, NSA_W), MXU_DTYPE), jax.ShapeDtypeStruct((s, NSA_W), MXU_DTYPE),
        jax.ShapeDtypeStruct((s, HEAD_DIM), F32), jax.ShapeDtypeStruct((s, HEAD_DIM), F32),
        jax.ShapeDtypeStruct((s, LANES), MXU_DTYPE), jax.ShapeDtypeStruct((s, LANES), MXU_DTYPE),
        jax.ShapeDtypeStruct((s, LANES), MXU_DTYPE), jax.ShapeDtypeStruct((s, LANES), MXU_DTYPE),
        jax.ShapeDtypeStruct((s, LANES), F32),
        jax.ShapeDtypeStruct((s, SB_W), MXU_DTYPE), jax.ShapeDtypeStruct((s, SB_W), MXU_DTYPE),
        jax.ShapeDtypeStruct((s, SB_W), MXU_DTYPE),
        jax.ShapeDtypeStruct((s, 2 * d), F32),
    ]
    out_specs = [row(NSA_W), row(NSA_W), row(HEAD_DIM), row(HEAD_DIM), row(LANES), row(LANES), row(LANES),
                 row(LANES), row(LANES), row(SB_W), row(SB_W), row(SB_W), row(2 * d)]
    return pl.pallas_call(
        _inproj_kernel,
        grid=(s // t,),
        in_specs=[row(d),
                  _resident((None, 1, d), lambda i: (layer, 0, 0)),
                  row(1),
                  _resident((1, LANES), lambda i: (0, 0)),
                  _resident((None, d, wcols), lambda i: (layer, 0, 0))],
        out_specs=out_specs,
        out_shape=out_shape,
        compiler_params=_params(1),
        name="inproj",
    )(x, g, pos, invf, w_all)


def _compress_kernel(kc_ref, vc_ref, pe_ref, w1_ref, w2_ref, outa_ref, outb_ref):
    nr, half_w = kc_ref.shape
    outs = []
    for kv, r_ref in enumerate((kc_ref, vc_ref)):
        r = r_ref[...]
        ha = _dot((r + pe_ref[kv, 0:1, :]).astype(MXU_DTYPE), w1_ref[kv, 0:half_w, :])
        hb = _dot((r + pe_ref[kv, 1:2, :]).astype(MXU_DTYPE), w1_ref[kv, half_w:2 * half_w, :])
        hid = ha + pltpu.roll(hb, nr - 1, 0)
        outs.append(_dot(jax.nn.gelu(hid).astype(MXU_DTYPE), w2_ref[kv]))
    outa_ref[...] = jnp.concatenate([outs[0], outs[1]], axis=1).astype(outa_ref.dtype)
    outb_ref[...] = jnp.concatenate([outs[1], outs[0]], axis=1).astype(outb_ref.dtype)


def _compress(kc, vc, pe_all, w1_all, w2_all, layer):
    nr, half_w = kc.shape
    hidden = w1_all.shape[3]
    full = lambda shape: pl.BlockSpec(shape, lambda i: (0,) * len(shape))
    return pl.pallas_call(
        _compress_kernel,
        grid=(1,),
        in_specs=[full((nr, half_w)), full((nr, half_w)),
                  pl.BlockSpec((None, 2, 2, half_w), lambda i: (layer, 0, 0, 0)),
                  pl.BlockSpec((None, 2, 2 * half_w, hidden), lambda i: (layer, 0, 0, 0)),
                  pl.BlockSpec((None, 2, hidden, HEAD_DIM), lambda i: (layer, 0, 0, 0))],
        out_specs=[full((nr, LANES)), full((nr, LANES))],
        out_shape=[jax.ShapeDtypeStruct((nr, LANES), MXU_DTYPE)] * 2,
        compiler_params=_params(1),
        name="compress",
    )(kc, vc, pe_all, w1_all, w2_all)


def _nsa_kernel(qp_ref, qr_ref, gate_ref, cmpa_ref, cmpb_ref, ovl_ref, ksa_ref, ksb_ref, kwa_ref, kwb_ref,
                out_ref, m_scr, l_scr, acc_scr, *, n_sel):
    tq = qp_ref.shape[0]
    seq = ksa_ref.shape[0]
    ncp = cmpa_ref.shape[0]
    ns = ovl_ref.shape[1]
    npair = NSA_HEADS // 2
    ck = min(NSA_KEY_CHUNK, seq)
    q0 = pl.program_id(0) * tq
    t = q0 + lax.broadcasted_iota(jnp.int32, (tq, 1), 0)
    lane = lax.broadcasted_iota(jnp.int32, (1, LANES), 1)
    low = lane < HEAD_DIM

    def stack(ref):
        groups = [ref[:, j * LANES:(j + 1) * LANES] for j in range(npair)]
        zero = jnp.zeros_like(groups[0])
        even = jnp.concatenate([jnp.where(low, g, zero) for g in groups], axis=0)
        odd = jnp.concatenate([jnp.where(low, zero, g) for g in groups], axis=0)
        return even, odd

    qp = stack(qp_ref)
    qr = stack(qr_ref)

    cmp_k = (cmpa_ref, cmpb_ref)
    cmp_v = (cmpb_ref, cmpa_ref)
    cmp_last = CMP_STRIDE * lax.broadcasted_iota(jnp.int32, (1, ncp), 1) + (CMP_BLOCK - 1)
    vis_c = cmp_last <= t
    psum = jnp.zeros((tq, ncp), F32)
    o_cmp = []
    for par in range(2):
        sc = _dot_nt(qp[par], cmp_k[par][...]).reshape(npair, tq, ncp)
        sc = jnp.where(vis_c[None], sc, -1e30)
        e = jnp.where(vis_c[None], jnp.exp(sc - jnp.max(sc, axis=-1, keepdims=True)), 0.0)
        den = jnp.sum(e, axis=-1, keepdims=True)
        p = e * (1.0 / jnp.where(den > 0.0, den, 1.0))
        psum = psum + jnp.sum(p, axis=0)
        o_cmp.append(_dot(p.reshape(npair * tq, ncp).astype(MXU_DTYPE), cmp_v[par][...]).reshape(npair, tq, LANES))
    p_hi = psum.astype(MXU_DTYPE)
    p_lo = (psum - p_hi.astype(F32)).astype(MXU_DTYPE)
    imp = _dot(p_hi, ovl_ref[...]) + _dot(p_lo, ovl_ref[...])

    blk = lax.broadcasted_iota(jnp.int32, (1, ns), 1)
    cur = t // SEL_BLOCK
    valid = blk <= cur
    forced = (blk == 0) | (blk == cur) | (blk == cur - 1)
    score = jnp.where(valid, jnp.where(forced, jnp.inf, imp), -1.0)
    chosen = jnp.zeros((tq, ns), F32)
    for _ in range(n_sel):
        best = jnp.max(score, axis=-1, keepdims=True)
        idx = jnp.min(jnp.where(score == best, blk, ns), axis=-1, keepdims=True)
        hit = blk == idx
        chosen = jnp.where(hit, 1.0, chosen)
        score = jnp.where(hit, -jnp.inf, score)
    neg_unchosen = ((chosen - 1.0) * (-MASKED)).astype(MXU_DTYPE)

    sel_k = (ksa_ref, ksb_ref)
    sel_v = (ksb_ref, ksa_ref)
    m_scr[...] = jnp.full(m_scr.shape, -1e30, F32)
    l_scr[...] = jnp.zeros(l_scr.shape, F32)
    acc_scr[...] = jnp.zeros(acc_scr.shape, F32)
    blk_rows = lax.broadcasted_iota(jnp.int32, (ns, 1), 0)

    def sel_chunk(c, carry):
        k0 = pl.multiple_of(c * ck, ck)
        kpos = k0 + lax.broadcasted_iota(jnp.int32, (1, ck), 1)
        expand = jnp.where(blk_rows == kpos // SEL_BLOCK, 1.0, 0.0).astype(MXU_DTYPE)
        bias = jnp.where(kpos <= t, _dot(neg_unchosen, expand), MASKED)
        for par in range(2):
            s = _dot_nt(qr[par], sel_k[par][pl.ds(k0, ck), :]).reshape(npair, tq, ck) + bias[None]
            m_old = m_scr[par]
            m_new = jnp.maximum(m_old, jnp.max(s, axis=-1, keepdims=True))
            alpha = jnp.exp(m_old - m_new)
            p = jnp.exp(s - m_new)
            l_scr[par] = alpha * l_scr[par] + jnp.sum(p, axis=-1, keepdims=True)
            pv = _dot(p.reshape(npair * tq, ck).astype(MXU_DTYPE), sel_v[par][pl.ds(k0, ck), :])
            acc_scr[par] = alpha * acc_scr[par] + pv.reshape(npair, tq, LANES)
            m_scr[par] = m_new
        return carry

    lax.fori_loop(0, (q0 + tq - 1) // ck + 1, sel_chunk, 0)

    win_k = (kwa_ref, kwb_ref)
    win_v = (kwb_ref, kwa_ref)
    span = min(WINDOW + tq, seq)
    start = pl.multiple_of(jnp.maximum(q0 + tq - span, 0), tq)
    kpos_w = start + lax.broadcasted_iota(jnp.int32, (1, span), 1)
    vis_w = (kpos_w <= t) & (kpos_w > t - WINDOW)
    o_win = []
    for par in range(2):
        sw = _dot_nt(qr[par], win_k[par][pl.ds(start, span), :]).reshape(npair, tq, span)
        sw = jnp.where(vis_w[None], sw, -1e30)
        e = jnp.exp(sw - jnp.max(sw, axis=-1, keepdims=True))
        p = e * (1.0 / jnp.sum(e, axis=-1, keepdims=True))
        o_win.append(_dot(p.reshape(npair * tq, span).astype(MXU_DTYPE),
                          win_v[par][pl.ds(start, span), :]).reshape(npair, tq, LANES))

    gates = gate_ref[...]
    for j in range(npair):
        halves = []
        for par in range(2):
            h = 2 * j + par
            o_sel = acc_scr[par, j] * (1.0 / l_scr[par, j])
            halves.append(gates[:, 3 * h:3 * h + 1] * o_cmp[par][j]
                          + gates[:, 3 * h + 1:3 * h + 2] * o_sel
                          + gates[:, 3 * h + 2:3 * h + 3] * o_win[par][j])
        out_ref[:, j * LANES:(j + 1) * LANES] = jnp.where(low, halves[0], halves[1]).astype(out_ref.dtype)


def _nsa(qp, qr, gates, cmpa, cmpb, ovl, ksa, ksb, kwa, kwb):
    s = qp.shape[0]
    tq = min(NSA_Q_BLOCK, s)
    ncp, ns = ovl.shape
    npair = NSA_HEADS // 2
    row = lambda n: pl.BlockSpec((tq, n), lambda i: (i, 0))
    res = lambda a: _resident(a.shape, lambda i: (0, 0))
    return pl.pallas_call(
        functools.partial(_nsa_kernel, n_sel=min(SEL_TOP_N, ns)),
        grid=(s // tq,),
        in_specs=[row(NSA_W), row(NSA_W), row(LANES), res(cmpa), res(cmpb), res(ovl),
                  res(ksa), res(ksb), res(kwa), res(kwb)],
        out_specs=row(NSA_W),
        out_shape=jax.ShapeDtypeStruct((s, NSA_W), MXU_DTYPE),
        scratch_shapes=[pltpu.VMEM((2, npair, tq, 1), F32), pltpu.VMEM((2, npair, tq, 1), F32),
                        pltpu.VMEM((2, npair, tq, LANES), F32)],
        compiler_params=_params(1),
        name="nsa",
    )(qp, qr, gates, cmpa, cmpb, ovl, ksa, ksb, kwa, kwb)


def _log_sigmoid_neg(z):
    return -(jnp.maximum(z, 0.0) + jnp.log1p(jnp.exp(-jnp.abs(z))))


def _sb_kernel(q_ref, k_ref, v_ref, out_ref):
    tb = q_ref.shape[0]
    i = pl.program_id(1)
    q = q_ref[...]
    r = lax.broadcasted_iota(jnp.int32, (tb, tb), 0)
    c = lax.broadcasted_iota(jnp.int32, (tb, tb), 1)
    tri = jnp.where(r >= c, 1.0, 0.0).astype(MXU_DTYPE)
    before = c < r

    k0 = pl.multiple_of(i * tb, tb)
    z = _dot_nt(q, k_ref[pl.ds(k0, tb), :])
    ls = jnp.where(before, _log_sigmoid_neg(z), 0.0)
    cs = _dot(ls.astype(MXU_DTYPE), tri)
    a = jnp.where(before, jnp.exp(z + cs), 0.0)
    acc = _dot(a.astype(MXU_DTYPE), v_ref[pl.ds(k0, tb), :])
    run = cs[:, 0:1]

    def chunk(n, carry):
        acc, run = carry
        k0 = pl.multiple_of((i - 1 - n) * tb, tb)
        z = _dot_nt(q, k_ref[pl.ds(k0, tb), :])
        cs = _dot(_log_sigmoid_neg(z).astype(MXU_DTYPE), tri)
        a = jnp.exp(z + cs + run)
        acc = acc + _dot(a.astype(MXU_DTYPE), v_ref[pl.ds(k0, tb), :])
        return acc, run + cs[:, 0:1]

    acc, _ = lax.fori_loop(0, i, chunk, (acc, run))
    out_ref[...] = acc.astype(out_ref.dtype)


def _sb(q, k, v):
    s = q.shape[0]
    tb = min(SB_BLOCK, s)
    return pl.pallas_call(
        _sb_kernel,
        grid=(SB_HEADS, s // tb),
        in_specs=[pl.BlockSpec((tb, SB_HEAD_DIM), lambda h, i: (i, h)),
                  pl.BlockSpec((s, SB_HEAD_DIM), lambda h, i: (0, h)),
                  pl.BlockSpec((s, SB_HEAD_DIM), lambda h, i: (0, h))],
        out_specs=pl.BlockSpec((tb, SB_HEAD_DIM), lambda h, i: (i, h)),
        out_shape=jax.ShapeDtypeStruct((s, SB_W), MXU_DTYPE),
        compiler_params=_params(2),
        name="sb",
    )(q, k, v)


def _mix_kernel(x_ref, nsa_ref, sb_ref, mg_ref, wn_ref, ws_ref, wo_ref, g_ref, out_ref):
    d = x_ref.shape[1]
    y_nsa = _dot(nsa_ref[...], wn_ref[...])
    y_sb = _dot(sb_ref[...], ws_ref[...])
    merged = mg_ref[:, 0:d] * y_nsa + mg_ref[:, d:2 * d] * y_sb
    mixed = _dot(merged.astype(MXU_DTYPE), wo_ref[...])
    out_ref[...] = x_ref[...] + _rms(mixed, g_ref[...])


def _mix(x, nsa_o, sb_o, mg, wn_all, ws_all, wo_all, g_all, layer):
    s, d = x.shape
    t = min(ROW_TILE, s)
    row = lambda n: pl.BlockSpec((t, n), lambda i: (i, 0))
    lay = lambda a: _resident((None,) + a.shape[1:], lambda i: (layer, 0, 0))
    return pl.pallas_call(
        _mix_kernel,
        grid=(s // t,),
        in_specs=[row(d), row(NSA_W), row(SB_W), row(2 * d), lay(wn_all), lay(ws_all), lay(wo_all), lay(g_all)],
        out_specs=row(d),
        out_shape=jax.ShapeDtypeStruct((s, d), F32),
        compiler_params=_params(1),
        name="mix",
    )(x, nsa_o, sb_o, mg, wn_all, ws_all, wo_all, g_all)


def _ffn_kernel(x_ref, gin_ref, w1_ref, w2_ref, gout_ref, out_ref):
    x = x_ref[...]
    d = x.shape[1]
    hb = _rms(x, gin_ref[...]).astype(MXU_DTYPE)
    ff = jnp.zeros_like(x)
    for c in range(w1_ref.shape[1] // d):
        up = _dot(hb, w1_ref[:, c * d:(c + 1) * d])
        ff = ff + _dot(jnp.square(jnp.maximum(up, 0.0)).astype(MXU_DTYPE), w2_ref[c * d:(c + 1) * d, :])
    out_ref[...] = x + _rms(ff, gout_ref[...])


def _ffn(x, gin_all, w1_all, w2_all, gout_all, layer):
    s, d = x.shape
    t = min(ROW_TILE, s)
    row = lambda n: pl.BlockSpec((t, n), lambda i: (i, 0))
    lay = lambda a: _resident((None,) + a.shape[1:], lambda i: (layer, 0, 0))
    return pl.pallas_call(
        _ffn_kernel,
        grid=(s // t,),
        in_specs=[row(d), lay(gin_all), lay(w1_all), lay(w2_all), lay(gout_all)],
        out_specs=row(d),
        out_shape=jax.ShapeDtypeStruct((s, d), F32),
        compiler_params=_params(1),
        name="ffn",
    )(x, gin_all, w1_all, w2_all, gout_all)


def _regroup_w_in(w_in):
    gate_lo, gate_hi = _C_GATE, _C_GATE + 3 * NSA_HEADS
    pad = jnp.zeros(w_in.shape[:2] + (LANES - 3 * NSA_HEADS,), w_in.dtype)
    return jnp.concatenate([w_in[..., :gate_lo], w_in[..., gate_lo:gate_hi], pad, w_in[..., gate_hi:]],
                           axis=-1).astype(MXU_DTYPE)


def kernel(x, positions, norm_g, w_in, cmp_pe, cmp_w1, cmp_w2, w_nsa_o, w_sb_o, w_out, w_ff1, w_ff2):
    b, s, d = x.shape
    depth = w_in.shape[0]
    ncp, ns = s // CMP_STRIDE, s // SEL_BLOCK
    half_w = CMP_STRIDE * HEAD_DIM

    w_in_r = _regroup_w_in(w_in)
    pe = cmp_pe.reshape(depth, 2, 2, half_w)
    w1 = cmp_w1.astype(MXU_DTYPE)
    w2 = cmp_w2.astype(MXU_DTYPE)
    wn, ws, wo = w_nsa_o.astype(MXU_DTYPE), w_sb_o.astype(MXU_DTYPE), w_out.astype(MXU_DTYPE)
    wf1, wf2 = w_ff1.astype(MXU_DTYPE), w_ff2.astype(MXU_DTYPE)
    g_pre, g_mix, g_ffn_in, g_ffn_out = (norm_g[:, n][:, None, :] for n in range(4))

    dim = jnp.arange(LANES) % HEAD_DIM
    half = ROT_DIM // 2
    inv_freq = jnp.power(ROPE_THETA, (dim % half).astype(F32) * (-2.0 / ROT_DIM))
    invf = jnp.where(dim < ROT_DIM, inv_freq, 0.0)[None, :].astype(F32)
    c_start = CMP_STRIDE * jnp.arange(ncp)[:, None]
    s_start = SEL_BLOCK * jnp.arange(ns)[None, :]
    ovl = ((c_start < s_start + SEL_BLOCK) & (c_start + CMP_BLOCK > s_start)).astype(MXU_DTYPE)

    outs = []
    for bi in range(b):
        xb = x[bi]
        pos = positions[bi][:, None]
        for layer in range(depth):
            (qp, qr, kc, vc, ksa, ksb, kwa, kwb, gates, sbq, sbk, sbv, mg) = _inproj(
                xb, g_pre, pos, invf, w_in_r, layer)
            cmpa, cmpb = _compress(kc.reshape(ncp, half_w), vc.reshape(ncp, half_w), pe, w1, w2, layer)
            nsa_o = _nsa(qp, qr, gates, cmpa, cmpb, ovl, ksa, ksb, kwa, kwb)
            sb_o = _sb(sbq, sbk, sbv)
            xb = _mix(xb, nsa_o, sb_o, mg, wn, ws, wo, g_mix, layer)
            xb = _ffn(xb, g_ffn_in, wf1, wf2, g_ffn_out, layer)
        outs.append(xb)
    return jnp.stack(outs, axis=0)
```

```python
import functools

import jax
import jax.numpy as jnp
from jax import lax
from jax.experimental import pallas as pl
from jax.experimental.pallas import tpu as pltpu

F32 = jnp.float32
MXU_DTYPE = jnp.bfloat16

HEAD_DIM = 64
NSA_HEADS = 8
SB_HEADS = 4
SB_HEAD_DIM = 128
ROPE_THETA = 500000.0
ROT_DIM = HEAD_DIM // 4
CMP_BLOCK = 32
CMP_STRIDE = 16
SEL_BLOCK = 64
SEL_TOP_N = 8
WINDOW = 512
RMS_EPS = 1e-6
NSA_W = NSA_HEADS * HEAD_DIM
SB_W = SB_HEADS * SB_HEAD_DIM
LANES = 128
MASKED = -32768.0
LOG2_E = 1.4426950408889634
VMEM_LIMIT = 56 * 1024 * 1024

ROW_TILE = 512
NSA_Q_BLOCK = 128
NSA_KEY_CHUNK = 512
SEL_CODE_BLOCKS = HEAD_DIM
SB_BLOCK = 512
SB_KEY_CHUNK = 256


def _dot(a, b):
    return jnp.dot(a, b, preferred_element_type=F32)


def _rms(x, g):
    return x * lax.rsqrt(jnp.mean(x * x, axis=-1, keepdims=True) + RMS_EPS) * g


def _params(n_grid_dims):
    return pltpu.CompilerParams(dimension_semantics=("arbitrary",) * n_grid_dims,
                                vmem_limit_bytes=VMEM_LIMIT)


def _resident(block_shape, index_map):
    return pl.BlockSpec(block_shape, index_map, pipeline_mode=pl.Buffered(1))


_C_KV = NSA_W
_C_GATE = _C_KV + 6 * HEAD_DIM
_C_SB = _C_GATE + LANES
_C_MERGE = _C_SB + 3 * SB_W


def _inproj_kernel(x_ref, g_ref, pos_ref, invf_ref, w_ref,
                   qpt_ref, qrt_ref, kc_ref, vc_ref, ks_ref, vst_ref, kw_ref, vwt_ref,
                   gate_ref, sbq_ref, sbk_ref, sbv_ref, mg_ref):
    t, d_model = x_ref.shape
    hb = _rms(x_ref[...], g_ref[...]).astype(MXU_DTYPE)

    ang = pos_ref[...].astype(F32) * invf_ref[...]
    cos, sin = jnp.cos(ang), jnp.sin(ang)
    lane = lax.broadcasted_iota(jnp.int32, (1, LANES), 1)
    dim = lane % HEAD_DIM
    half = ROT_DIM // 2
    sin_up = jnp.where((dim >= half) & (dim < ROT_DIM), sin, 0.0)
    sin_dn = jnp.where(dim < half, -sin, 0.0)
    low = lane < HEAD_DIM

    def rope(xg):
        return xg * cos + pltpu.roll(xg, half, 1) * sin_up + pltpu.roll(xg, LANES - half, 1) * sin_dn

    pa = _dot(hb, w_ref[:, 0:_C_GATE])
    scale = LOG2_E * HEAD_DIM ** -0.5
    for j in range(NSA_W // LANES):
        qg = pa[:, j * LANES:(j + 1) * LANES]
        qpt_ref[j * LANES:(j + 1) * LANES, :] = (qg * scale).T.astype(qpt_ref.dtype)
        qrt_ref[j * LANES:(j + 1) * LANES, :] = (rope(qg) * scale).T.astype(qrt_ref.dtype)
    kc_ref[...] = pa[:, _C_KV:_C_KV + HEAD_DIM]
    vc_ref[...] = pa[:, _C_KV + HEAD_DIM:_C_KV + 2 * HEAD_DIM]

    row = pl.program_id(0) * t + lax.broadcasted_iota(jnp.int32, (t, 1), 0)
    code = jnp.where(lane - HEAD_DIM == (row // SEL_BLOCK) % SEL_CODE_BLOCKS, 1.0, 0.0)
    ksg = pa[:, _C_KV + 2 * HEAD_DIM:_C_KV + 4 * HEAD_DIM]
    ks_ref[...] = jnp.where(low, rope(ksg), code).astype(ks_ref.dtype)
    vst_ref[0] = jnp.where(low, pltpu.roll(ksg, HEAD_DIM, 1), 1.0).T.astype(vst_ref.dtype)
    kwg = pa[:, _C_KV + 4 * HEAD_DIM:_C_KV + 6 * HEAD_DIM]
    kw_ref[...] = jnp.where(low, rope(kwg), 0.0).astype(kw_ref.dtype)
    vw_t = jnp.where(low, pltpu.roll(kwg, HEAD_DIM, 1), 1.0).T
    for c in range(vwt_ref.shape[0]):
        vwt_ref[c] = vw_t[:, c * vwt_ref.shape[2]:(c + 1) * vwt_ref.shape[2]].astype(vwt_ref.dtype)

    gate_ref[...] = jax.nn.sigmoid(_dot(hb, w_ref[:, _C_GATE:_C_SB])).T
    sb = _dot(hb, w_ref[:, _C_SB:_C_MERGE])
    sbq_ref[...] = (sb[:, 0:SB_W] * (LOG2_E * SB_HEAD_DIM ** -0.5)).astype(sbq_ref.dtype)
    sbk_t = sb[:, SB_W:2 * SB_W].T
    for c in range(sbk_ref.shape[0]):
        sbk_ref[c] = sbk_t[:, c * sbk_ref.shape[2]:(c + 1) * sbk_ref.shape[2]].astype(sbk_ref.dtype)
    sbv_ref[...] = sb[:, 2 * SB_W:3 * SB_W].astype(sbv_ref.dtype)
    for c in range(2):
        mg_ref[:, c * d_model:(c + 1) * d_model] = jax.nn.sigmoid(
            _dot(hb, w_ref[:, _C_MERGE + c * d_model:_C_MERGE + (c + 1) * d_model]))


def _inproj(x, g, pos, invf, w_all, layer):
    s, d = x.shape
    t = min(ROW_TILE, s)
    wcols = w_all.shape[2]
    sck = min(SB_KEY_CHUNK, s)
    wck = min(NSA_Q_BLOCK, s)
    row = lambda n: pl.BlockSpec((t, n), lambda i: (i, 0))
    col = lambda n: pl.BlockSpec((n, t), lambda i: (0, i))
    slab = lambda n, ck: pl.BlockSpec((t // ck, n, ck), lambda i: (i, 0, 0))
    sds = jax.ShapeDtypeStruct
    out_shape = [
        sds((NSA_W, s), MXU_DTYPE), sds((NSA_W, s), MXU_DTYPE),
        sds((s, HEAD_DIM), F32), sds((s, HEAD_DIM), F32),
        sds((s, LANES), MXU_DTYPE), sds((s // t, LANES, t), MXU_DTYPE),
        sds((s, LANES), MXU_DTYPE), sds((s // wck, LANES, wck), MXU_DTYPE),
        sds((LANES, s), F32),
        sds((s, SB_W), MXU_DTYPE), sds((s // sck, SB_W, sck), MXU_DTYPE), sds((s, SB_W), MXU_DTYPE),
        sds((s, 2 * d), F32),
    ]
    out_specs = [col(NSA_W), col(NSA_W), row(HEAD_DIM), row(HEAD_DIM),
                 row(LANES), slab(LANES, t), row(LANES), slab(LANES, wck),
                 col(LANES), row(SB_W), slab(SB_W, sck), row(SB_W), row(2 * d)]
    return pl.pallas_call(
        _inproj_kernel,
        grid=(s // t,),
        in_specs=[row(d),
                  _resident((None, 1, d), lambda i: (layer, 0, 0)),
                  row(1),
                  _resident((1, LANES), lambda i: (0, 0)),
                  _resident((None, d, wcols), lambda i: (layer, 0, 0))],
        out_specs=out_specs,
        out_shape=out_shape,
        compiler_params=_params(1),
        name="inproj",
    )(x, g, pos, invf, w_all)


def _compress_kernel(kc_ref, vc_ref, pe_ref, w1_ref, w2_ref, kcmp_ref, vcmpt_ref):
    nr, half_w = kc_ref.shape
    outs = []
    for kv, r_ref in enumerate((kc_ref, vc_ref)):
        r = r_ref[...]
        ha = _dot((r + pe_ref[kv, 0:1, :]).astype(MXU_DTYPE), w1_ref[kv, 0:half_w, :])
        hb = _dot((r + pe_ref[kv, 1:2, :]).astype(MXU_DTYPE), w1_ref[kv, half_w:2 * half_w, :])
        hid = ha + pltpu.roll(hb, nr - 1, 0)
        outs.append(_dot(jax.nn.gelu(hid).astype(MXU_DTYPE), w2_ref[kv]))
    pad = jnp.zeros_like(outs[0])
    kcmp_ref[...] = jnp.concatenate([outs[0], pad], axis=1).astype(kcmp_ref.dtype)
    vcmpt_ref[...] = jnp.concatenate([outs[1], pad], axis=1).T.astype(vcmpt_ref.dtype)


def _compress(kc, vc, pe_all, w1_all, w2_all, layer):
    nr, half_w = kc.shape
    hidden = w1_all.shape[3]
    full = lambda shape: pl.BlockSpec(shape, lambda i: (0,) * len(shape))
    return pl.pallas_call(
        _compress_kernel,
        grid=(1,),
        in_specs=[full((nr, half_w)), full((nr, half_w)),
                  pl.BlockSpec((None, 2, 2, half_w), lambda i: (layer, 0, 0, 0)),
                  pl.BlockSpec((None, 2, 2 * half_w, hidden), lambda i: (layer, 0, 0, 0)),
                  pl.BlockSpec((None, 2, hidden, HEAD_DIM), lambda i: (layer, 0, 0, 0))],
        out_specs=[full((nr, LANES)), full((LANES, nr))],
        out_shape=[jax.ShapeDtypeStruct((nr, LANES), MXU_DTYPE), jax.ShapeDtypeStruct((LANES, nr), MXU_DTYPE)],
        compiler_params=_params(1),
        name="compress",
    )(kc, vc, pe_all, w1_all, w2_all)


def _nsa_kernel(qpt_ref, qrt_ref, gate_ref, kcmp_ref, vcmpt_ref, ovlt_ref, ks_ref, vst_ref, kw_ref, vwt_ref,
                out_ref, qaug_scr, mask_scr, m_scr, acc_scr, *, n_sel):
    tq = qpt_ref.shape[1]
    seq = ks_ref.shape[0]
    ncp = kcmp_ref.shape[0]
    nsp = ovlt_ref.shape[0]
    ck = vst_ref.shape[2]
    wck = vwt_ref.shape[2]
    nh = NSA_HEADS
    q0 = pl.program_id(0) * tq
    t = q0 + lax.broadcasted_iota(jnp.int32, (1, tq), 1)
    cols = [slice(h * tq, (h + 1) * tq) for h in range(nh)]

    def aug(qt_ref, h, tail):
        return jnp.concatenate([qt_ref[h * HEAD_DIM:(h + 1) * HEAD_DIM, :], tail], axis=0)

    zeros_tail = jnp.zeros((HEAD_DIM, tq), MXU_DTYPE)
    qp_aug = jnp.concatenate([aug(qpt_ref, h, zeros_tail) for h in range(nh)], axis=1)
    sc_all = _dot(kcmp_ref[...], qp_aug)
    cmp_last = CMP_STRIDE * lax.broadcasted_iota(jnp.int32, (ncp, 1), 0) + (CMP_BLOCK - 1)
    vis_c = cmp_last <= t
    psum = jnp.zeros((ncp, tq), F32)
    p_cmp = []
    for h in range(nh):
        sc = jnp.where(vis_c, sc_all[:, cols[h]], -1e30)
        e = jnp.where(vis_c, jnp.exp2(sc - jnp.max(sc, axis=0, keepdims=True)), 0.0)
        den = jnp.sum(e, axis=0, keepdims=True)
        p = e * (1.0 / jnp.where(den > 0.0, den, 1.0))
        psum = psum + p
        p_cmp.append(p.astype(MXU_DTYPE))
    o_cmp = _dot(vcmpt_ref[...], jnp.concatenate(p_cmp, axis=1))
    p_hi = psum.astype(MXU_DTYPE)
    p_lo = (psum - p_hi.astype(F32)).astype(MXU_DTYPE)
    imp = _dot(ovlt_ref[...], p_hi) + _dot(ovlt_ref[...], p_lo)

    blk = lax.broadcasted_iota(jnp.int32, (nsp, 1), 0)
    cur = t // SEL_BLOCK
    valid = blk <= cur
    forced = (blk == 0) | (blk == cur) | (blk == cur - 1)
    score = jnp.where(valid, jnp.where(forced, jnp.inf, imp), -1.0)
    chosen = jnp.zeros((nsp, tq), F32)
    for _ in range(n_sel):
        best = jnp.max(score, axis=0, keepdims=True)
        idx = jnp.min(jnp.where(score == best, blk, nsp), axis=0, keepdims=True)
        hit = blk == idx
        chosen = jnp.where(hit, 1.0, chosen)
        score = jnp.where(hit, -jnp.inf, score)
    mask_scr[...] = ((chosen - 1.0) * (-MASKED)).astype(mask_scr.dtype)

    for h in range(nh):
        qaug_scr[0:HEAD_DIM, cols[h]] = qrt_ref[h * HEAD_DIM:(h + 1) * HEAD_DIM, :]
    m_scr[...] = jnp.full(m_scr.shape, -1e30, F32)
    acc_scr[...] = jnp.zeros(acc_scr.shape, F32)

    def sel_chunk(c, bias):
        k0 = pl.multiple_of(c * ck, ck)
        code0 = pl.multiple_of((k0 // (SEL_BLOCK * SEL_CODE_BLOCKS)) * SEL_CODE_BLOCKS, SEL_CODE_BLOCKS)
        mrows = mask_scr[pl.ds(code0, SEL_CODE_BLOCKS), :]
        for h in range(nh):
            qaug_scr[HEAD_DIM:2 * HEAD_DIM, cols[h]] = mrows
        s_all = _dot(ks_ref[pl.ds(k0, ck), :], qaug_scr[...])
        m_old = m_scr[...]
        m_new, p_all = [], []
        for h in range(nh):
            s = s_all[:, cols[h]]
            if bias is not None:
                s = s + bias
            mh = jnp.maximum(m_old[:, cols[h]], jnp.max(s, axis=0, keepdims=True))
            p_all.append(jnp.exp2(s - mh).astype(MXU_DTYPE))
            m_new.append(mh)
        m_new = jnp.concatenate(m_new, axis=1)
        pv = _dot(vst_ref[c], jnp.concatenate(p_all, axis=1))
        acc_scr[...] = jnp.exp2(m_old - m_new) * acc_scr[...] + pv
        m_scr[...] = m_new

    c_diag = q0 // ck
    lax.fori_loop(0, c_diag, lambda c, carry: (sel_chunk(c, None), carry)[1], 0)
    kpos_d = c_diag * ck + lax.broadcasted_iota(jnp.int32, (ck, 1), 0)
    sel_chunk(c_diag, jnp.where(kpos_d <= t, 0.0, MASKED))
    acc = acc_scr[...]
    o_sel = acc[0:HEAD_DIM] * (1.0 / acc[HEAD_DIM:HEAD_DIM + 1])

    span = min(WINDOW + tq, seq)
    start = pl.multiple_of(jnp.maximum(q0 + tq - span, 0), wck)
    kpos_w = start + lax.broadcasted_iota(jnp.int32, (span, 1), 0)
    vis_w = (kpos_w <= t) & (kpos_w > t - WINDOW)
    sw_all = _dot(kw_ref[pl.ds(start, span), :], qaug_scr[...])
    p_win = []
    for h in range(nh):
        sw = jnp.where(vis_w, sw_all[:, cols[h]], -1e30)
        p_win.append(jnp.exp2(sw - jnp.max(sw, axis=0, keepdims=True)).astype(MXU_DTYPE))
    p_win = jnp.concatenate(p_win, axis=1)
    ow = jnp.zeros((LANES, nh * tq), F32)
    for j in range(span // wck):
        ow = ow + _dot(vwt_ref[start // wck + j], p_win[j * wck:(j + 1) * wck, :])
    o_win = ow[0:HEAD_DIM] * (1.0 / ow[HEAD_DIM:HEAD_DIM + 1])

    gates = gate_ref[...]
    merged = []
    for h in range(nh):
        merged.append(gates[3 * h:3 * h + 1, :] * o_cmp[0:HEAD_DIM, cols[h]]
                      + gates[3 * h + 1:3 * h + 2, :] * o_sel[:, cols[h]]
                      + gates[3 * h + 2:3 * h + 3, :] * o_win[:, cols[h]])
    out_ref[...] = jnp.concatenate(merged, axis=0).T.astype(out_ref.dtype)


def _nsa(qpt, qrt, gates_t, kcmp, vcmpt, ovlt, ks, vst, kw, vwt):
    s = ks.shape[0]
    tq = min(NSA_Q_BLOCK, s)
    nsp = ovlt.shape[0]
    col = lambda n: pl.BlockSpec((n, tq), lambda i: (0, i))
    res = lambda a: _resident(a.shape, lambda i: (0,) * a.ndim)
    return pl.pallas_call(
        functools.partial(_nsa_kernel, n_sel=min(SEL_TOP_N, s // SEL_BLOCK)),
        grid=(s // tq,),
        in_specs=[col(NSA_W), col(NSA_W), col(LANES), res(kcmp), res(vcmpt), res(ovlt),
                  res(ks), res(vst), res(kw), res(vwt)],
        out_specs=pl.BlockSpec((tq, NSA_W), lambda i: (i, 0)),
        out_shape=jax.ShapeDtypeStruct((s, NSA_W), MXU_DTYPE),
        scratch_shapes=[pltpu.VMEM((LANES, NSA_HEADS * tq), MXU_DTYPE), pltpu.VMEM((nsp, tq), MXU_DTYPE),
                        pltpu.VMEM((1, NSA_HEADS * tq), F32), pltpu.VMEM((LANES, NSA_HEADS * tq), F32)],
        compiler_params=_params(1),
        name="nsa",
    )(qpt, qrt, gates_t, kcmp, vcmpt, ovlt, ks, vst, kw, vwt)


def _softplus2(z2):
    return jnp.maximum(z2, 0.0) + jnp.log2(1.0 + jnp.exp2(-jnp.abs(z2)))


def _sb_kernel(q_ref, k_ref, v_ref, out_ref, acc_scr, run_scr):
    tb = q_ref.shape[0]
    ck = k_ref.shape[2]
    nsub = tb // ck
    i = pl.program_id(0)
    r = lax.broadcasted_iota(jnp.int32, (ck, ck), 0)
    c = lax.broadcasted_iota(jnp.int32, (ck, ck), 1)
    tri = jnp.where(r >= c, 1.0, 0.0).astype(MXU_DTYPE)
    before = c < r
    heads = [slice(h * SB_HEAD_DIM, (h + 1) * SB_HEAD_DIM) for h in range(SB_HEADS)]

    def step(h, rows, chunk, diag, first):
        hs = heads[h]
        z = _dot(q_ref[rows, hs], k_ref[chunk, hs, :])
        sp = _softplus2(z)
        if diag:
            sp = jnp.where(before, sp, 0.0)
        cs = _dot(sp.astype(MXU_DTYPE), tri)
        if first:
            a = jnp.exp2(z - cs)
            run_scr[h, rows] = cs[:, 0:1]
        else:
            run = run_scr[h, rows]
            a = jnp.exp2(z - cs - run)
            run_scr[h, rows] = run + cs[:, 0:1]
        if diag:
            a = jnp.where(before, a, 0.0)
        pv = _dot(a.astype(MXU_DTYPE), v_ref[pl.ds(pl.multiple_of(chunk * ck, ck), ck), hs])
        if first:
            acc_scr[h, rows] = pv
        else:
            acc_scr[h, rows] += pv

    for g in range(nsub):
        rows = slice(g * ck, (g + 1) * ck)
        for h in range(SB_HEADS):
            step(h, rows, i * nsub + g, True, True)
        for back in range(g):
            for h in range(SB_HEADS):
                step(h, rows, i * nsub + g - 1 - back, False, False)

    def chunk(n, carry):
        for h in range(SB_HEADS):
            step(h, slice(0, tb), i * nsub - 1 - n, False, False)
        return carry

    lax.fori_loop(0, i * nsub, chunk, 0)
    for h, hs in enumerate(heads):
        out_ref[:, hs] = acc_scr[h].astype(out_ref.dtype)


def _sb(q, kt, v):
    s = q.shape[0]
    tb = min(SB_BLOCK, s)
    return pl.pallas_call(
        _sb_kernel,
        grid=(s // tb,),
        in_specs=[pl.BlockSpec((tb, SB_W), lambda i: (i, 0)),
                  _resident(kt.shape, lambda i: (0, 0, 0)),
                  _resident((s, SB_W), lambda i: (0, 0))],
        out_specs=pl.BlockSpec((tb, SB_W), lambda i: (i, 0)),
        out_shape=jax.ShapeDtypeStruct((s, SB_W), MXU_DTYPE),
        scratch_shapes=[pltpu.VMEM((SB_HEADS, tb, SB_HEAD_DIM), F32), pltpu.VMEM((SB_HEADS, tb, 1), F32)],
        compiler_params=_params(1),
        name="sb",
    )(q, kt, v)


def _mix_kernel(x_ref, nsa_ref, sb_ref, mg_ref, wn_ref, ws_ref, wo_ref, g_ref, out_ref):
    d = x_ref.shape[1]
    y_nsa = _dot(nsa_ref[...], wn_ref[...])
    y_sb = _dot(sb_ref[...], ws_ref[...])
    merged = mg_ref[:, 0:d] * y_nsa + mg_ref[:, d:2 * d] * y_sb
    mixed = _dot(merged.astype(MXU_DTYPE), wo_ref[...])
    out_ref[...] = x_ref[...] + _rms(mixed, g_ref[...])


def _mix(x, nsa_o, sb_o, mg, wn_all, ws_all, wo_all, g_all, layer):
    s, d = x.shape
    t = min(ROW_TILE, s)
    row = lambda n: pl.BlockSpec((t, n), lambda i: (i, 0))
    lay = lambda a: _resident((None,) + a.shape[1:], lambda i: (layer, 0, 0))
    return pl.pallas_call(
        _mix_kernel,
        grid=(s // t,),
        in_specs=[row(d), row(NSA_W), row(SB_W), row(2 * d), lay(wn_all), lay(ws_all), lay(wo_all), lay(g_all)],
        out_specs=row(d),
        out_shape=jax.ShapeDtypeStruct((s, d), F32),
        compiler_params=_params(1),
        name="mix",
    )(x, nsa_o, sb_o, mg, wn_all, ws_all, wo_all, g_all)


def _ffn_kernel(x_ref, gin_ref, w1_ref, w2_ref, gout_ref, out_ref):
    x = x_ref[...]
    d = x.shape[1]
    hb = _rms(x, gin_ref[...]).astype(MXU_DTYPE)
    ff = jnp.zeros_like(x)
    for c in range(w1_ref.shape[1] // d):
        up = _dot(hb, w1_ref[:, c * d:(c + 1) * d])
        ff = ff + _dot(jnp.square(jnp.maximum(up, 0.0)).astype(MXU_DTYPE), w2_ref[c * d:(c + 1) * d, :])
    out_ref[...] = x + _rms(ff, gout_ref[...])


def _ffn(x, gin_all, w1_all, w2_all, gout_all, layer):
    s, d = x.shape
    t = min(ROW_TILE, s)
    row = lambda n: pl.BlockSpec((t, n), lambda i: (i, 0))
    lay = lambda a: _resident((None,) + a.shape[1:], lambda i: (layer, 0, 0))
    return pl.pallas_call(
        _ffn_kernel,
        grid=(s // t,),
        in_specs=[row(d), lay(gin_all), lay(w1_all), lay(w2_all), lay(gout_all)],
        out_specs=row(d),
        out_shape=jax.ShapeDtypeStruct((s, d), F32),
        compiler_params=_params(1),
        name="ffn",
    )(x, gin_all, w1_all, w2_all, gout_all)


def _regroup_w_in(w_in):
    gate_lo, gate_hi = _C_GATE, _C_GATE + 3 * NSA_HEADS
    pad = jnp.zeros(w_in.shape[:2] + (LANES - 3 * NSA_HEADS,), w_in.dtype)
    return jnp.concatenate([w_in[..., :gate_lo], w_in[..., gate_lo:gate_hi], pad, w_in[..., gate_hi:]],
                           axis=-1).astype(MXU_DTYPE)


def kernel(x, positions, norm_g, w_in, cmp_pe, cmp_w1, cmp_w2, w_nsa_o, w_sb_o, w_out, w_ff1, w_ff2):
    b, s, d = x.shape
    depth = w_in.shape[0]
    ncp, ns = s // CMP_STRIDE, s // SEL_BLOCK
    nsp = -(-ns // LANES) * LANES
    half_w = CMP_STRIDE * HEAD_DIM

    w_in_r = _regroup_w_in(w_in)
    pe = cmp_pe.reshape(depth, 2, 2, half_w)
    w1 = cmp_w1.astype(MXU_DTYPE)
    w2 = cmp_w2.astype(MXU_DTYPE)
    wn, ws, wo = w_nsa_o.astype(MXU_DTYPE), w_sb_o.astype(MXU_DTYPE), w_out.astype(MXU_DTYPE)
    wf1, wf2 = w_ff1.astype(MXU_DTYPE), w_ff2.astype(MXU_DTYPE)
    g_pre, g_mix, g_ffn_in, g_ffn_out = (norm_g[:, n][:, None, :] for n in range(4))

    dim = jnp.arange(LANES) % HEAD_DIM
    half = ROT_DIM // 2
    inv_freq = jnp.power(ROPE_THETA, (dim % half).astype(F32) * (-2.0 / ROT_DIM))
    invf = jnp.where(dim < ROT_DIM, inv_freq, 0.0)[None, :].astype(F32)
    c_start = CMP_STRIDE * jnp.arange(ncp)[None, :]
    s_start = SEL_BLOCK * jnp.arange(nsp)[:, None]
    ovlt = ((c_start < s_start + SEL_BLOCK) & (c_start + CMP_BLOCK > s_start) & (s_start < s)).astype(MXU_DTYPE)

    outs = []
    for bi in range(b):
        xb = x[bi]
        pos = positions[bi][:, None]
        for layer in range(depth):
            (qpt, qrt, kc, vc, ks, vst, kw, vwt, gates_t, sbq, sbkt, sbv, mg) = _inproj(
                xb, g_pre, pos, invf, w_in_r, layer)
            kcmp, vcmpt = _compress(kc.reshape(ncp, half_w), vc.reshape(ncp, half_w), pe, w1, w2, layer)
            nsa_o = _nsa(qpt, qrt, gates_t, kcmp, vcmpt, ovlt, ks, vst, kw, vwt)
            sb_o = _sb(sbq, sbkt, sbv)
            xb = _mix(xb, nsa_o, sb_o, mg, wn, ws, wo, g_mix, layer)
            xb = _ffn(xb, g_ffn_in, wf1, wf2, g_ffn_out, layer)
        outs.append(xb)
    return jnp.stack(outs, axis=0)
```

```python
import functools

import jax
import jax.numpy as jnp
from jax import lax
from jax.experimental import pallas as pl
from jax.experimental.pallas import tpu as pltpu

F32 = jnp.float32
MXU_DTYPE = jnp.bfloat16

HEAD_DIM = 64
NSA_HEADS = 8
SB_HEADS = 4
SB_HEAD_DIM = 128
ROPE_THETA = 500000.0
ROT_DIM = HEAD_DIM // 4
CMP_BLOCK = 32
CMP_STRIDE = 16
SEL_BLOCK = 64
SEL_TOP_N = 8
WINDOW = 512
RMS_EPS = 1e-6
NSA_W = NSA_HEADS * HEAD_DIM
SB_W = SB_HEADS * SB_HEAD_DIM
LANES = 128
MASKED = -32768.0
LOG2_E = 1.4426950408889634
VMEM_LIMIT = 56 * 1024 * 1024

ROW_TILE = 512
NSA_Q_BLOCK = 128
NSA_KEY_CHUNK = 512
SEL_CODE_BLOCKS = HEAD_DIM
SB_BLOCK = 512
SB_KEY_CHUNK = 256


def _dot(a, b):
    return jnp.dot(a, b, preferred_element_type=F32)


def _rms(x, g):
    return x * lax.rsqrt(jnp.mean(x * x, axis=-1, keepdims=True) + RMS_EPS) * g


def _params(n_grid_dims):
    return pltpu.CompilerParams(dimension_semantics=("arbitrary",) * n_grid_dims,
                                vmem_limit_bytes=VMEM_LIMIT)


def _resident(block_shape, index_map):
    return pl.BlockSpec(block_shape, index_map, pipeline_mode=pl.Buffered(1))


_C_KV = NSA_W
_C_GATE = _C_KV + 6 * HEAD_DIM
_C_SB = _C_GATE + LANES
_C_MERGE = _C_SB + 3 * SB_W


def _inproj_kernel(x_ref, g_ref, pos_ref, invf_ref, w_ref,
                   qpt_ref, qrt_ref, kc_ref, vc_ref, ks_ref, vst_ref, kw_ref, vwt_ref,
                   gate_ref, sbq_ref, sbk_ref, sbv_ref, mg_ref):
    t, d_model = x_ref.shape
    hb = _rms(x_ref[...], g_ref[...]).astype(MXU_DTYPE)

    ang = pos_ref[...].astype(F32) * invf_ref[...]
    cos, sin = jnp.cos(ang), jnp.sin(ang)
    lane = lax.broadcasted_iota(jnp.int32, (1, LANES), 1)
    dim = lane % HEAD_DIM
    half = ROT_DIM // 2
    sin_up = jnp.where((dim >= half) & (dim < ROT_DIM), sin, 0.0)
    sin_dn = jnp.where(dim < half, -sin, 0.0)
    low = lane < HEAD_DIM

    def rope(xg):
        return xg * cos + pltpu.roll(xg, half, 1) * sin_up + pltpu.roll(xg, LANES - half, 1) * sin_dn

    pa = _dot(hb, w_ref[:, 0:_C_GATE])
    scale = LOG2_E * HEAD_DIM ** -0.5
    for j in range(NSA_W // LANES):
        qg = pa[:, j * LANES:(j + 1) * LANES]
        qpt_ref[j * LANES:(j + 1) * LANES, :] = (qg * scale).T.astype(qpt_ref.dtype)
        qrt_ref[j * LANES:(j + 1) * LANES, :] = (rope(qg) * scale).T.astype(qrt_ref.dtype)
    kc_ref[...] = pa[:, _C_KV:_C_KV + HEAD_DIM]
    vc_ref[...] = pa[:, _C_KV + HEAD_DIM:_C_KV + 2 * HEAD_DIM]

    row = pl.program_id(0) * t + lax.broadcasted_iota(jnp.int32, (t, 1), 0)
    code = jnp.where(lane - HEAD_DIM == (row // SEL_BLOCK) % SEL_CODE_BLOCKS, 1.0, 0.0)
    ksg = pa[:, _C_KV + 2 * HEAD_DIM:_C_KV + 4 * HEAD_DIM]
    ks_ref[...] = jnp.where(low, rope(ksg), code).astype(ks_ref.dtype)
    vst_ref[0] = jnp.where(low, pltpu.roll(ksg, HEAD_DIM, 1), 1.0).T.astype(vst_ref.dtype)
    kwg = pa[:, _C_KV + 4 * HEAD_DIM:_C_KV + 6 * HEAD_DIM]
    kw_ref[...] = jnp.where(low, rope(kwg), 0.0).astype(kw_ref.dtype)
    vw_t = jnp.where(low, pltpu.roll(kwg, HEAD_DIM, 1), 1.0).T
    for c in range(vwt_ref.shape[0]):
        vwt_ref[c] = vw_t[:, c * vwt_ref.shape[2]:(c + 1) * vwt_ref.shape[2]].astype(vwt_ref.dtype)

    gate_ref[...] = jax.nn.sigmoid(_dot(hb, w_ref[:, _C_GATE:_C_SB])).T
    sb = _dot(hb, w_ref[:, _C_SB:_C_MERGE])
    sbq_ref[...] = (sb[:, 0:SB_W] * (LOG2_E * SB_HEAD_DIM ** -0.5)).astype(sbq_ref.dtype)
    sbk_t = sb[:, SB_W:2 * SB_W].T
    for c in range(sbk_ref.shape[0]):
        sbk_ref[c] = sbk_t[:, c * sbk_ref.shape[2]:(c + 1) * sbk_ref.shape[2]].astype(sbk_ref.dtype)
    sbv_ref[...] = sb[:, 2 * SB_W:3 * SB_W].astype(sbv_ref.dtype)
    for c in range(2):
        mg_ref[:, c * d_model:(c + 1) * d_model] = jax.nn.sigmoid(
            _dot(hb, w_ref[:, _C_MERGE + c * d_model:_C_MERGE + (c + 1) * d_model]))


def _inproj(x, g, pos, invf, w_all, layer):
    s, d = x.shape
    t = min(ROW_TILE, s)
    wcols = w_all.shape[2]
    sck = min(SB_KEY_CHUNK, s)
    wck = min(NSA_Q_BLOCK, s)
    row = lambda n: pl.BlockSpec((t, n), lambda i: (i, 0))
    col = lambda n: pl.BlockSpec((n, t), lambda i: (0, i))
    slab = lambda n, ck: pl.BlockSpec((t // ck, n, ck), lambda i: (i, 0, 0))
    sds = jax.ShapeDtypeStruct
    out_shape = [
        sds((NSA_W, s), MXU_DTYPE), sds((NSA_W, s), MXU_DTYPE),
        sds((s, HEAD_DIM), F32), sds((s, HEAD_DIM), F32),
        sds((s, LANES), MXU_DTYPE), sds((s // t, LANES, t), MXU_DTYPE),
        sds((s, LANES), MXU_DTYPE), sds((s // wck, LANES, wck), MXU_DTYPE),
        sds((LANES, s), F32),
        sds((s, SB_W), MXU_DTYPE), sds((s // sck, SB_W, sck), MXU_DTYPE), sds((s, SB_W), MXU_DTYPE),
        sds((s, 2 * d), F32),
    ]
    out_specs = [col(NSA_W), col(NSA_W), row(HEAD_DIM), row(HEAD_DIM),
                 row(LANES), slab(LANES, t), row(LANES), slab(LANES, wck),
                 col(LANES), row(SB_W), slab(SB_W, sck), row(SB_W), row(2 * d)]
    return pl.pallas_call(
        _inproj_kernel,
        grid=(s // t,),
        in_specs=[row(d),
                  _resident((None, 1, d), lambda i: (layer, 0, 0)),
                  row(1),
                  _resident((1, LANES), lambda i: (0, 0)),
                  _resident((None, d, wcols), lambda i: (layer, 0, 0))],
        out_specs=out_specs,
        out_shape=out_shape,
        compiler_params=_params(1),
        name="inproj",
    )(x, g, pos, invf, w_all)


def _compress_kernel(kc_ref, vc_ref, pe_ref, w1_ref, w2_ref, kcmp_ref, vcmpt_ref):
    nr, half_w = kc_ref.shape
    outs = []
    for kv, r_ref in enumerate((kc_ref, vc_ref)):
        r = r_ref[...]
        ha = _dot((r + pe_ref[kv, 0:1, :]).astype(MXU_DTYPE), w1_ref[kv, 0:half_w, :])
        hb = _dot((r + pe_ref[kv, 1:2, :]).astype(MXU_DTYPE), w1_ref[kv, half_w:2 * half_w, :])
        hid = ha + pltpu.roll(hb, nr - 1, 0)
        outs.append(_dot(jax.nn.gelu(hid).astype(MXU_DTYPE), w2_ref[kv]))
    pad = jnp.zeros_like(outs[0])
    kcmp_ref[...] = jnp.concatenate([outs[0], pad], axis=1).astype(kcmp_ref.dtype)
    vcmpt_ref[...] = jnp.concatenate([outs[1], pad], axis=1).T.astype(vcmpt_ref.dtype)


def _compress(kc, vc, pe_all, w1_all, w2_all, layer):
    nr, half_w = kc.shape
    hidden = w1_all.shape[3]
    full = lambda shape: pl.BlockSpec(shape, lambda i: (0,) * len(shape))
    return pl.pallas_call(
        _compress_kernel,
        grid=(1,),
        in_specs=[full((nr, half_w)), full((nr, half_w)),
                  pl.BlockSpec((None, 2, 2, half_w), lambda i: (layer, 0, 0, 0)),
                  pl.BlockSpec((None, 2, 2 * half_w, hidden), lambda i: (layer, 0, 0, 0)),
                  pl.BlockSpec((None, 2, hidden, HEAD_DIM), lambda i: (layer, 0, 0, 0))],
        out_specs=[full((nr, LANES)), full((LANES, nr))],
        out_shape=[jax.ShapeDtypeStruct((nr, LANES), MXU_DTYPE), jax.ShapeDtypeStruct((LANES, nr), MXU_DTYPE)],
        compiler_params=_params(1),
        name="compress",
    )(kc, vc, pe_all, w1_all, w2_all)


def _nsa_kernel(qpt_ref, qrt_ref, gate_ref, kcmp_ref, vcmpt_ref, ovlt_ref, ks_ref, vst_ref, kw_ref, vwt_ref,
                out_ref, qaug_scr, mask_scr, m_scr, acc_scr, sa_scr, sb_scr, ta_scr, tb_scr, ocmp_scr, owin_scr, *, n_sel):
    tq = qpt_ref.shape[1]
    seq = ks_ref.shape[0]
    ncp = kcmp_ref.shape[0]
    nsp = ovlt_ref.shape[0]
    ck = vst_ref.shape[2]
    wck = vwt_ref.shape[2]
    nh = NSA_HEADS
    q0 = pl.program_id(0) * tq
    t = q0 + lax.broadcasted_iota(jnp.int32, (1, tq), 1)
    cols = [slice(h * tq, (h + 1) * tq) for h in range(nh)]

    def aug(qt_ref, h, tail):
        return jnp.concatenate([qt_ref[h * HEAD_DIM:(h + 1) * HEAD_DIM, :], tail], axis=0)

    zeros_tail = jnp.zeros((HEAD_DIM, tq), MXU_DTYPE)
    qp_aug = jnp.concatenate([aug(qpt_ref, h, zeros_tail) for h in range(nh)], axis=1)
    sc_all = _dot(kcmp_ref[...], qp_aug)
    cmp_last = CMP_STRIDE * lax.broadcasted_iota(jnp.int32, (ncp, 1), 0) + (CMP_BLOCK - 1)
    vis_c = cmp_last <= t
    psum = jnp.zeros((ncp, tq), F32)
    p_cmp = []
    for h in range(nh):
        sc = jnp.where(vis_c, sc_all[:, cols[h]], -1e30)
        e = jnp.where(vis_c, jnp.exp2(sc - jnp.max(sc, axis=0, keepdims=True)), 0.0)
        den = jnp.sum(e, axis=0, keepdims=True)
        p = e * (1.0 / jnp.where(den > 0.0, den, 1.0))
        psum = psum + p
        p_cmp.append(p.astype(MXU_DTYPE))
    o_cmp = _dot(vcmpt_ref[...], jnp.concatenate(p_cmp, axis=1))
    p_hi = psum.astype(MXU_DTYPE)
    p_lo = (psum - p_hi.astype(F32)).astype(MXU_DTYPE)
    imp = _dot(ovlt_ref[...], p_hi) + _dot(ovlt_ref[...], p_lo)
    ocmp_scr[...] = o_cmp[0:HEAD_DIM]

    qr_aug = jnp.concatenate([aug(qrt_ref, h, zeros_tail) for h in range(nh)], axis=1)
    span = min(WINDOW + tq, seq)
    start = pl.multiple_of(jnp.maximum(q0 + tq - span, 0), wck)
    kpos_w = start + lax.broadcasted_iota(jnp.int32, (span, 1), 0)
    vis_w = (kpos_w <= t) & (kpos_w > t - WINDOW)
    sw_all = _dot(kw_ref[pl.ds(start, span), :], qr_aug)
    p_win = []
    for h in range(nh):
        sw = jnp.where(vis_w, sw_all[:, cols[h]], -1e30)
        p_win.append(jnp.exp2(sw - jnp.max(sw, axis=0, keepdims=True)).astype(MXU_DTYPE))
    p_win = jnp.concatenate(p_win, axis=1)
    ow = jnp.zeros((LANES, nh * tq), F32)
    for j in range(span // wck):
        ow = ow + _dot(vwt_ref[start // wck + j], p_win[j * wck:(j + 1) * wck, :])
    owin_scr[...] = ow[0:HEAD_DIM] * (1.0 / ow[HEAD_DIM:HEAD_DIM + 1])

    blk = lax.broadcasted_iota(jnp.int32, (nsp, 1), 0)
    cur = t // SEL_BLOCK
    valid = blk <= cur
    forced = (blk == 0) | (blk == cur) | (blk == cur - 1)
    score = jnp.where(valid, jnp.where(forced, jnp.inf, imp), -1.0)
    chosen = jnp.zeros((nsp, tq), F32)
    for _ in range(n_sel):
        best = jnp.max(score, axis=0, keepdims=True)
        idx = jnp.min(jnp.where(score == best, blk, nsp), axis=0, keepdims=True)
        hit = blk == idx
        chosen = jnp.where(hit, 1.0, chosen)
        score = jnp.where(hit, -jnp.inf, score)
    mask_scr[...] = ((chosen - 1.0) * (-MASKED)).astype(mask_scr.dtype)

    for h in range(nh):
        qaug_scr[0:HEAD_DIM, cols[h]] = qrt_ref[h * HEAD_DIM:(h + 1) * HEAD_DIM, :]
    m_scr[...] = jnp.full(m_scr.shape, -1e30, F32)
    acc_scr[...] = jnp.zeros(acc_scr.shape, F32)

    last_chunk = seq // ck - 1

    def sel_scores(c, s_ref, top_ref):
        k0 = pl.multiple_of(jnp.minimum(c, last_chunk) * ck, ck)
        code0 = pl.multiple_of((k0 // (SEL_BLOCK * SEL_CODE_BLOCKS)) * SEL_CODE_BLOCKS, SEL_CODE_BLOCKS)
        mrows = mask_scr[pl.ds(code0, SEL_CODE_BLOCKS), :]
        for h in range(nh):
            qaug_scr[HEAD_DIM:2 * HEAD_DIM, cols[h]] = mrows
        s_all = _dot(ks_ref[pl.ds(k0, ck), :], qaug_scr[...])
        s_ref[...] = s_all
        top_ref[...] = jnp.max(s_all, axis=0, keepdims=True)

    def sel_update(c, s_ref, top_ref, causal):
        m_old = m_scr[...]
        if causal:
            kpos = c * ck + lax.broadcasted_iota(jnp.int32, (ck, 1), 0)
            bias = jnp.where(kpos <= t, 0.0, MASKED)
            tops = [jnp.max(s_ref[:, cols[h]] + bias, axis=0, keepdims=True) for h in range(nh)]
            m_new = jnp.maximum(m_old, jnp.concatenate(tops, axis=1))
        else:
            m_new = jnp.maximum(m_old, top_ref[...])
        p_all = []
        for h in range(nh):
            s = s_ref[:, cols[h]]
            if causal:
                s = s + bias
            p_all.append(jnp.exp2(s - m_new[:, cols[h]]).astype(MXU_DTYPE))
        pv = _dot(vst_ref[jnp.minimum(c, last_chunk)], jnp.concatenate(p_all, axis=1))
        acc_scr[...] = jnp.exp2(m_old - m_new) * acc_scr[...] + pv
        m_scr[...] = m_new

    def sel_pair(j, causal):
        c = 2 * j
        sel_scores(c + 1, sb_scr, tb_scr)
        sel_update(c, sa_scr, ta_scr, causal)
        sel_scores(c + 2, sa_scr, ta_scr)
        sel_update(c + 1, sb_scr, tb_scr, causal)

    sel_scores(0, sa_scr, ta_scr)
    last_pair = (q0 // ck) // 2
    lax.fori_loop(0, last_pair, lambda j, carry: (sel_pair(j, False), carry)[1], 0)
    sel_pair(last_pair, True)
    acc = acc_scr[...]
    o_sel = acc[0:HEAD_DIM] * (1.0 / acc[HEAD_DIM:HEAD_DIM + 1])

    o_cmp = ocmp_scr[...]
    o_win = owin_scr[...]
    gates = gate_ref[...]
    merged = []
    for h in range(nh):
        merged.append(gates[3 * h:3 * h + 1, :] * o_cmp[:, cols[h]]
                      + gates[3 * h + 1:3 * h + 2, :] * o_sel[:, cols[h]]
                      + gates[3 * h + 2:3 * h + 3, :] * o_win[:, cols[h]])
    out_ref[...] = jnp.concatenate(merged, axis=0).T.astype(out_ref.dtype)


def _nsa(qpt, qrt, gates_t, kcmp, vcmpt, ovlt, ks, vst, kw, vwt):
    s = ks.shape[0]
    tq = min(NSA_Q_BLOCK, s)
    nsp = ovlt.shape[0]
    col = lambda n: pl.BlockSpec((n, tq), lambda i: (0, i))
    res = lambda a: _resident(a.shape, lambda i: (0,) * a.ndim)
    return pl.pallas_call(
        functools.partial(_nsa_kernel, n_sel=min(SEL_TOP_N, s // SEL_BLOCK)),
        grid=(s // tq,),
        in_specs=[col(NSA_W), col(NSA_W), col(LANES), res(kcmp), res(vcmpt), res(ovlt),
                  res(ks), res(vst), res(kw), res(vwt)],
        out_specs=pl.BlockSpec((tq, NSA_W), lambda i: (i, 0)),
        out_shape=jax.ShapeDtypeStruct((s, NSA_W), MXU_DTYPE),
        scratch_shapes=[pltpu.VMEM((LANES, NSA_HEADS * tq), MXU_DTYPE), pltpu.VMEM((nsp, tq), MXU_DTYPE),
                        pltpu.VMEM((1, NSA_HEADS * tq), F32), pltpu.VMEM((LANES, NSA_HEADS * tq), F32),
                        pltpu.VMEM((vst.shape[2], NSA_HEADS * tq), F32),
                        pltpu.VMEM((vst.shape[2], NSA_HEADS * tq), F32),
                        pltpu.VMEM((1, NSA_HEADS * tq), F32), pltpu.VMEM((1, NSA_HEADS * tq), F32),
                        pltpu.VMEM((HEAD_DIM, NSA_HEADS * tq), F32), pltpu.VMEM((HEAD_DIM, NSA_HEADS * tq), F32)],
        compiler_params=_params(1),
        name="nsa",
    )(qpt, qrt, gates_t, kcmp, vcmpt, ovlt, ks, vst, kw, vwt)


def _softplus2(z2):
    return jnp.maximum(z2, 0.0) + jnp.log2(1.0 + jnp.exp2(-jnp.abs(z2)))


def _sb_kernel(q_ref, k_ref, v_ref, out_ref, acc_scr, run_scr, za_scr, zb_scr):
    assert (q_ref.shape[0] // k_ref.shape[2]) % 2 == 0
    tb = q_ref.shape[0]
    ck = k_ref.shape[2]
    nsub = tb // ck
    i = pl.program_id(0)
    r = lax.broadcasted_iota(jnp.int32, (ck, ck), 0)
    c = lax.broadcasted_iota(jnp.int32, (ck, ck), 1)
    tri = jnp.where(r >= c, 1.0, 0.0).astype(MXU_DTYPE)
    before = c < r
    heads = [slice(h * SB_HEAD_DIM, (h + 1) * SB_HEAD_DIM) for h in range(SB_HEADS)]

    def logits(h, rows, chunk):
        return _dot(q_ref[rows, heads[h]], k_ref[chunk, heads[h], :])

    def step(h, rows, chunk, diag, first, z=None):
        hs = heads[h]
        if z is None:
            z = logits(h, rows, chunk)
        sp = _softplus2(z)
        if diag:
            sp = jnp.where(before, sp, 0.0)
        cs = _dot(sp.astype(MXU_DTYPE), tri)
        if first:
            a = jnp.exp2(z - cs)
            run_scr[h, rows] = cs[:, 0:1]
        else:
            run = run_scr[h, rows]
            a = jnp.exp2(z - cs - run)
            run_scr[h, rows] = run + cs[:, 0:1]
        if diag:
            a = jnp.where(before, a, 0.0)
        pv = _dot(a.astype(MXU_DTYPE), v_ref[pl.ds(pl.multiple_of(chunk * ck, ck), ck), hs])
        if first:
            acc_scr[h, rows] = pv
        else:
            acc_scr[h, rows] += pv

    for g in range(nsub):
        rows = slice(g * ck, (g + 1) * ck)
        for h in range(SB_HEADS):
            step(h, rows, i * nsub + g, True, True)
        for back in range(g):
            for h in range(SB_HEADS):
                step(h, rows, i * nsub + g - 1 - back, False, False)

    everything = slice(0, tb)

    def lookahead(chunk, z_ref):
        for h in range(SB_HEADS):
            z_ref[h] = logits(h, everything, jnp.maximum(chunk, 0))

    def pair(j, carry):
        cur = i * nsub - 1 - 2 * j
        lookahead(cur - 1, zb_scr)
        for h in range(SB_HEADS):
            step(h, everything, cur, False, False, za_scr[h])
        lookahead(cur - 2, za_scr)
        for h in range(SB_HEADS):
            step(h, everything, cur - 1, False, False, zb_scr[h])
        return carry

    lookahead(i * nsub - 1, za_scr)
    lax.fori_loop(0, (i * nsub) // 2, pair, 0)
    for h, hs in enumerate(heads):
        out_ref[:, hs] = acc_scr[h].astype(out_ref.dtype)


def _sb(q, kt, v):
    s = q.shape[0]
    tb = min(SB_BLOCK, s)
    return pl.pallas_call(
        _sb_kernel,
        grid=(s // tb,),
        in_specs=[pl.BlockSpec((tb, SB_W), lambda i: (i, 0)),
                  _resident(kt.shape, lambda i: (0, 0, 0)),
                  _resident((s, SB_W), lambda i: (0, 0))],
        out_specs=pl.BlockSpec((tb, SB_W), lambda i: (i, 0)),
        out_shape=jax.ShapeDtypeStruct((s, SB_W), MXU_DTYPE),
        scratch_shapes=[pltpu.VMEM((SB_HEADS, tb, SB_HEAD_DIM), F32), pltpu.VMEM((SB_HEADS, tb, 1), F32),
                        pltpu.VMEM((SB_HEADS, tb, kt.shape[2]), F32), pltpu.VMEM((SB_HEADS, tb, kt.shape[2]), F32)],
        compiler_params=_params(1),
        name="sb",
    )(q, kt, v)


def _mix_kernel(x_ref, nsa_ref, sb_ref, mg_ref, wn_ref, ws_ref, wo_ref, g_ref, out_ref):
    d = x_ref.shape[1]
    y_nsa = _dot(nsa_ref[...], wn_ref[...])
    y_sb = _dot(sb_ref[...], ws_ref[...])
    merged = mg_ref[:, 0:d] * y_nsa + mg_ref[:, d:2 * d] * y_sb
    mixed = _dot(merged.astype(MXU_DTYPE), wo_ref[...])
    out_ref[...] = x_ref[...] + _rms(mixed, g_ref[...])


def _mix(x, nsa_o, sb_o, mg, wn_all, ws_all, wo_all, g_all, layer):
    s, d = x.shape
    t = min(ROW_TILE, s)
    row = lambda n: pl.BlockSpec((t, n), lambda i: (i, 0))
    lay = lambda a: _resident((None,) + a.shape[1:], lambda i: (layer, 0, 0))
    return pl.pallas_call(
        _mix_kernel,
        grid=(s // t,),
        in_specs=[row(d), row(NSA_W), row(SB_W), row(2 * d), lay(wn_all), lay(ws_all), lay(wo_all), lay(g_all)],
        out_specs=row(d),
        out_shape=jax.ShapeDtypeStruct((s, d), F32),
        compiler_params=_params(1),
        name="mix",
    )(x, nsa_o, sb_o, mg, wn_all, ws_all, wo_all, g_all)


def _ffn_kernel(x_ref, gin_ref, w1_ref, w2_ref, gout_ref, out_ref):
    x = x_ref[...]
    d = x.shape[1]
    hb = _rms(x, gin_ref[...]).astype(MXU_DTYPE)
    ff = jnp.zeros_like(x)
    for c in range(w1_ref.shape[1] // d):
        up = _dot(hb, w1_ref[:, c * d:(c + 1) * d])
        ff = ff + _dot(jnp.square(jnp.maximum(up, 0.0)).astype(MXU_DTYPE), w2_ref[c * d:(c + 1) * d, :])
    out_ref[...] = x + _rms(ff, gout_ref[...])


def _ffn(x, gin_all, w1_all, w2_all, gout_all, layer):
    s, d = x.shape
    t = min(ROW_TILE, s)
    row = lambda n: pl.BlockSpec((t, n), lambda i: (i, 0))
    lay = lambda a: _resident((None,) + a.shape[1:], lambda i: (layer, 0, 0))
    return pl.pallas_call(
        _ffn_kernel,
        grid=(s // t,),
        in_specs=[row(d), lay(gin_all), lay(w1_all), lay(w2_all), lay(gout_all)],
        out_specs=row(d),
        out_shape=jax.ShapeDtypeStruct((s, d), F32),
        compiler_params=_params(1),
        name="ffn",
    )(x, gin_all, w1_all, w2_all, gout_all)


def _regroup_w_in(w_in):
    gate_lo, gate_hi = _C_GATE, _C_GATE + 3 * NSA_HEADS
    pad = jnp.zeros(w_in.shape[:2] + (LANES - 3 * NSA_HEADS,), w_in.dtype)
    return jnp.concatenate([w_in[..., :gate_lo], w_in[..., gate_lo:gate_hi], pad, w_in[..., gate_hi:]],
                           axis=-1).astype(MXU_DTYPE)


def kernel(x, positions, norm_g, w_in, cmp_pe, cmp_w1, cmp_w2, w_nsa_o, w_sb_o, w_out, w_ff1, w_ff2):
    b, s, d = x.shape
    depth = w_in.shape[0]
    ncp, ns = s // CMP_STRIDE, s // SEL_BLOCK
    nsp = -(-ns // LANES) * LANES
    half_w = CMP_STRIDE * HEAD_DIM

    w_in_r = _regroup_w_in(w_in)
    pe = cmp_pe.reshape(depth, 2, 2, half_w)
    w1 = cmp_w1.astype(MXU_DTYPE)
    w2 = cmp_w2.astype(MXU_DTYPE)
    wn, ws, wo = w_nsa_o.astype(MXU_DTYPE), w_sb_o.astype(MXU_DTYPE), w_out.astype(MXU_DTYPE)
    wf1, wf2 = w_ff1.astype(MXU_DTYPE), w_ff2.astype(MXU_DTYPE)
    g_pre, g_mix, g_ffn_in, g_ffn_out = (norm_g[:, n][:, None, :] for n in range(4))

    dim = jnp.arange(LANES) % HEAD_DIM
    half = ROT_DIM // 2
    inv_freq = jnp.power(ROPE_THETA, (dim % half).astype(F32) * (-2.0 / ROT_DIM))
    invf = jnp.where(dim < ROT_DIM, inv_freq, 0.0)[None, :].astype(F32)
    c_start = CMP_STRIDE * jnp.arange(ncp)[None, :]
    s_start = SEL_BLOCK * jnp.arange(nsp)[:, None]
    ovlt = ((c_start < s_start + SEL_BLOCK) & (c_start + CMP_BLOCK > s_start) & (s_start < s)).astype(MXU_DTYPE)

    outs = []
    for bi in range(b):
        xb = x[bi]
        pos = positions[bi][:, None]
        for layer in range(depth):
            (qpt, qrt, kc, vc, ks, vst, kw, vwt, gates_t, sbq, sbkt, sbv, mg) = _inproj(
                xb, g_pre, pos, invf, w_in_r, layer)
            kcmp, vcmpt = _compress(kc.reshape(ncp, half_w), vc.reshape(ncp, half_w), pe, w1, w2, layer)
            nsa_o = _nsa(qpt, qrt, gates_t, kcmp, vcmpt, ovlt, ks, vst, kw, vwt)
            sb_o = _sb(sbq, sbkt, sbv)
            xb = _mix(xb, nsa_o, sb_o, mg, wn, ws, wo, g_mix, layer)
            xb = _ffn(xb, g_ffn_in, wf1, wf2, g_ffn_out, layer)
        outs.append(xb)
    return jnp.stack(outs, axis=0)
```

```python
import functools

import jax
import jax.numpy as jnp
from jax import lax
from jax.experimental import pallas as pl
from jax.experimental.pallas import tpu as pltpu

F32 = jnp.float32
MXU_DTYPE = jnp.bfloat16

HEAD_DIM = 64
NSA_HEADS = 8
SB_HEADS = 4
SB_HEAD_DIM = 128
ROPE_THETA = 500000.0
ROT_DIM = HEAD_DIM // 4
CMP_BLOCK = 32
CMP_STRIDE = 16
SEL_BLOCK = 64
SEL_TOP_N = 8
WINDOW = 512
RMS_EPS = 1e-6
NSA_W = NSA_HEADS * HEAD_DIM
SB_W = SB_HEADS * SB_HEAD_DIM
LANES = 128
MASKED = -32768.0
LOG2_E = 1.4426950408889634
VMEM_LIMIT = 56 * 1024 * 1024

ROW_TILE = 512
NSA_Q_BLOCK = 128
NSA_KEY_CHUNK = 512
SEL_CODE_BLOCKS = HEAD_DIM
CMP_VARIANTS = 4
SB_BLOCK = 512
SB_KEY_CHUNK = 256


def _dot(a, b):
    return jnp.dot(a, b, preferred_element_type=F32)


def _rms(x, g):
    return x * lax.rsqrt(jnp.mean(x * x, axis=-1, keepdims=True) + RMS_EPS) * g


def _params(n_grid_dims):
    return pltpu.CompilerParams(dimension_semantics=("arbitrary",) * n_grid_dims,
                                vmem_limit_bytes=VMEM_LIMIT)


def _resident(block_shape, index_map):
    return pl.BlockSpec(block_shape, index_map, pipeline_mode=pl.Buffered(1))


_C_KV = NSA_W
_C_GATE = _C_KV + 6 * HEAD_DIM
_C_SB = _C_GATE + LANES
_C_MERGE = _C_SB + 3 * SB_W


def _inproj_kernel(x_ref, g_ref, pos_ref, invf_ref, w_ref,
                   qpt_ref, qrt_ref, kc_ref, vc_ref, ks_ref, vst_ref, kw_ref, vwt_ref,
                   gate_ref, sbq_ref, sbk_ref, sbv_ref, mg_ref):
    t, d_model = x_ref.shape
    hb = _rms(x_ref[...], g_ref[...]).astype(MXU_DTYPE)

    ang = pos_ref[...].astype(F32) * invf_ref[...]
    cos, sin = jnp.cos(ang), jnp.sin(ang)
    lane = lax.broadcasted_iota(jnp.int32, (1, LANES), 1)
    dim = lane % HEAD_DIM
    half = ROT_DIM // 2
    sin_up = jnp.where((dim >= half) & (dim < ROT_DIM), sin, 0.0)
    sin_dn = jnp.where(dim < half, -sin, 0.0)
    low = lane < HEAD_DIM

    def rope(xg):
        return xg * cos + pltpu.roll(xg, half, 1) * sin_up + pltpu.roll(xg, LANES - half, 1) * sin_dn

    pa = _dot(hb, w_ref[:, 0:_C_GATE])
    scale = LOG2_E * HEAD_DIM ** -0.5
    for j in range(NSA_W // LANES):
        qg = pa[:, j * LANES:(j + 1) * LANES]
        qpt_ref[j * LANES:(j + 1) * LANES, :] = (qg * scale).T.astype(qpt_ref.dtype)
        qrt_ref[j * LANES:(j + 1) * LANES, :] = (rope(qg) * scale).T.astype(qrt_ref.dtype)
    kc_ref[...] = pa[:, _C_KV:_C_KV + HEAD_DIM]
    vc_ref[...] = pa[:, _C_KV + HEAD_DIM:_C_KV + 2 * HEAD_DIM]

    row = pl.program_id(0) * t + lax.broadcasted_iota(jnp.int32, (t, 1), 0)
    code = jnp.where(lane - HEAD_DIM == (row // SEL_BLOCK) % SEL_CODE_BLOCKS, 1.0, 0.0)
    ksg = pa[:, _C_KV + 2 * HEAD_DIM:_C_KV + 4 * HEAD_DIM]
    ks_ref[...] = jnp.where(low, rope(ksg), code).astype(ks_ref.dtype)
    vst_ref[0] = jnp.where(low, pltpu.roll(ksg, HEAD_DIM, 1), 1.0).T.astype(vst_ref.dtype)
    kwg = pa[:, _C_KV + 4 * HEAD_DIM:_C_KV + 6 * HEAD_DIM]
    kw_ref[...] = jnp.where(low, rope(kwg), 0.0).astype(kw_ref.dtype)
    vw_t = jnp.where(low, pltpu.roll(kwg, HEAD_DIM, 1), 1.0).T
    for c in range(vwt_ref.shape[0]):
        vwt_ref[c] = vw_t[:, c * vwt_ref.shape[2]:(c + 1) * vwt_ref.shape[2]].astype(vwt_ref.dtype)

    gate_ref[...] = jax.nn.sigmoid(_dot(hb, w_ref[:, _C_GATE:_C_SB])).T
    sb = _dot(hb, w_ref[:, _C_SB:_C_MERGE])
    sbq_ref[...] = (sb[:, 0:SB_W] * (LOG2_E * SB_HEAD_DIM ** -0.5)).astype(sbq_ref.dtype)
    sbk_t = sb[:, SB_W:2 * SB_W].T
    for c in range(sbk_ref.shape[0]):
        sbk_ref[c] = sbk_t[:, c * sbk_ref.shape[2]:(c + 1) * sbk_ref.shape[2]].astype(sbk_ref.dtype)
    sbv_ref[...] = sb[:, 2 * SB_W:3 * SB_W].astype(sbv_ref.dtype)
    for c in range(2):
        mg_ref[:, c * d_model:(c + 1) * d_model] = jax.nn.sigmoid(
            _dot(hb, w_ref[:, _C_MERGE + c * d_model:_C_MERGE + (c + 1) * d_model]))


def _inproj(x, g, pos, invf, w_all, layer):
    s, d = x.shape
    t = min(ROW_TILE, s)
    wcols = w_all.shape[2]
    sck = min(SB_KEY_CHUNK, s)
    wck = min(NSA_Q_BLOCK, s)
    row = lambda n: pl.BlockSpec((t, n), lambda i: (i, 0))
    col = lambda n: pl.BlockSpec((n, t), lambda i: (0, i))
    slab = lambda n, ck: pl.BlockSpec((t // ck, n, ck), lambda i: (i, 0, 0))
    sds = jax.ShapeDtypeStruct
    out_shape = [
        sds((NSA_W, s), MXU_DTYPE), sds((NSA_W, s), MXU_DTYPE),
        sds((s, HEAD_DIM), F32), sds((s, HEAD_DIM), F32),
        sds((s, LANES), MXU_DTYPE), sds((s // t, LANES, t), MXU_DTYPE),
        sds((s, LANES), MXU_DTYPE), sds((s // wck, LANES, wck), MXU_DTYPE),
        sds((LANES, s), F32),
        sds((s, SB_W), MXU_DTYPE), sds((s // sck, SB_W, sck), MXU_DTYPE), sds((s, SB_W), MXU_DTYPE),
        sds((s, 2 * d), F32),
    ]
    out_specs = [col(NSA_W), col(NSA_W), row(HEAD_DIM), row(HEAD_DIM),
                 row(LANES), slab(LANES, t), row(LANES), slab(LANES, wck),
                 col(LANES), row(SB_W), slab(SB_W, sck), row(SB_W), row(2 * d)]
    return pl.pallas_call(
        _inproj_kernel,
        grid=(s // t,),
        in_specs=[row(d),
                  _resident((None, 1, d), lambda i: (layer, 0, 0)),
                  row(1),
                  _resident((1, LANES), lambda i: (0, 0)),
                  _resident((None, d, wcols), lambda i: (layer, 0, 0))],
        out_specs=out_specs,
        out_shape=out_shape,
        compiler_params=_params(1),
        name="inproj",
    )(x, g, pos, invf, w_all)


def _compress_kernel(kc_ref, vc_ref, pe_ref, w1_ref, w2_ref, kcmp_ref, vcmpt_ref):
    nr, half_w = kc_ref.shape
    outs = []
    for kv, r_ref in enumerate((kc_ref, vc_ref)):
        r = r_ref[...]
        ha = _dot((r + pe_ref[kv, 0:1, :]).astype(MXU_DTYPE), w1_ref[kv, 0:half_w, :])
        hb = _dot((r + pe_ref[kv, 1:2, :]).astype(MXU_DTYPE), w1_ref[kv, half_w:2 * half_w, :])
        hid = ha + pltpu.roll(hb, nr - 1, 0)
        outs.append(_dot(jax.nn.gelu(hid).astype(MXU_DTYPE), w2_ref[kv]))
    pad = jnp.zeros_like(outs[0])
    kcmp_ref[...] = jnp.concatenate([outs[0], pad], axis=1).astype(kcmp_ref.dtype)
    vcmpt_ref[...] = jnp.concatenate([outs[1], pad], axis=1).T.astype(vcmpt_ref.dtype)


def _compress(kc, vc, pe_all, w1_all, w2_all, layer):
    nr, half_w = kc.shape
    hidden = w1_all.shape[3]
    full = lambda shape: pl.BlockSpec(shape, lambda i: (0,) * len(shape))
    return pl.pallas_call(
        _compress_kernel,
        grid=(1,),
        in_specs=[full((nr, half_w)), full((nr, half_w)),
                  pl.BlockSpec((None, 2, 2, half_w), lambda i: (layer, 0, 0, 0)),
                  pl.BlockSpec((None, 2, 2 * half_w, hidden), lambda i: (layer, 0, 0, 0)),
                  pl.BlockSpec((None, 2, hidden, HEAD_DIM), lambda i: (layer, 0, 0, 0))],
        out_specs=[full((nr, LANES)), full((LANES, nr))],
        out_shape=[jax.ShapeDtypeStruct((nr, LANES), MXU_DTYPE), jax.ShapeDtypeStruct((LANES, nr), MXU_DTYPE)],
        compiler_params=_params(1),
        name="compress",
    )(kc, vc, pe_all, w1_all, w2_all)


def _nsa_kernel(qpt_ref, qrt_ref, gate_ref, kcmp_ref, vcmpt_ref, ovlt_ref, ks_ref, vst_ref, kw_ref, vwt_ref,
                out_ref, qaug_scr, mask_scr, m_scr, acc_scr, sa_scr, sb_scr, ta_scr, tb_scr, ocmp_scr, owin_scr, imp_scr, *, n_sel):
    tq = qpt_ref.shape[1]
    seq = ks_ref.shape[0]
    ncp = kcmp_ref.shape[0]
    nsp = ovlt_ref.shape[0]
    ck = vst_ref.shape[2]
    wck = vwt_ref.shape[2]
    nh = NSA_HEADS
    q0 = pl.program_id(0) * tq
    t = q0 + lax.broadcasted_iota(jnp.int32, (1, tq), 1)
    cols = [slice(h * tq, (h + 1) * tq) for h in range(nh)]

    def aug(qt_ref, h, tail):
        return jnp.concatenate([qt_ref[h * HEAD_DIM:(h + 1) * HEAD_DIM, :], tail], axis=0)

    zeros_tail = jnp.zeros((HEAD_DIM, tq), MXU_DTYPE)
    def cmp_branch(rows):
        qp_aug = jnp.concatenate([aug(qpt_ref, h, zeros_tail) for h in range(nh)], axis=1)
        sc_all = _dot(kcmp_ref[0:rows, :], qp_aug)
        cmp_last = CMP_STRIDE * lax.broadcasted_iota(jnp.int32, (rows, 1), 0) + (CMP_BLOCK - 1)
        vis_c = cmp_last <= t
        psum = jnp.zeros((rows, tq), F32)
        p_cmp = []
        for h in range(nh):
            sc = jnp.where(vis_c, sc_all[:, cols[h]], -1e30)
            e = jnp.where(vis_c, jnp.exp2(sc - jnp.max(sc, axis=0, keepdims=True)), 0.0)
            den = jnp.sum(e, axis=0, keepdims=True)
            p = e * (1.0 / jnp.where(den > 0.0, den, 1.0))
            psum = psum + p
            p_cmp.append(p.astype(MXU_DTYPE))
        o_cmp = _dot(vcmpt_ref[:, 0:rows], jnp.concatenate(p_cmp, axis=1))
        p_hi = psum.astype(MXU_DTYPE)
        p_lo = (psum - p_hi.astype(F32)).astype(MXU_DTYPE)
        imp_scr[...] = _dot(ovlt_ref[:, 0:rows], p_hi) + _dot(ovlt_ref[:, 0:rows], p_lo)
        ocmp_scr[...] = o_cmp[0:HEAD_DIM]

    n_var = CMP_VARIANTS if ncp % (CMP_VARIANTS * LANES) == 0 else 1
    step = ncp // n_var
    n_vis = (q0 + tq - CMP_BLOCK) // CMP_STRIDE + 1
    variant = jnp.clip((n_vis + step - 1) // step - 1, 0, n_var - 1)
    for v in range(n_var):
        pl.when(variant == v)(functools.partial(cmp_branch, (v + 1) * step))
    imp = imp_scr[...]

    qr_aug = jnp.concatenate([aug(qrt_ref, h, zeros_tail) for h in range(nh)], axis=1)
    span = min(WINDOW + tq, seq)
    start = pl.multiple_of(jnp.maximum(q0 + tq - span, 0), wck)
    kpos_w = start + lax.broadcasted_iota(jnp.int32, (span, 1), 0)
    vis_w = (kpos_w <= t) & (kpos_w > t - WINDOW)
    sw_all = _dot(kw_ref[pl.ds(start, span), :], qr_aug)
    p_win = []
    for h in range(nh):
        sw = jnp.where(vis_w, sw_all[:, cols[h]], -1e30)
        p_win.append(jnp.exp2(sw - jnp.max(sw, axis=0, keepdims=True)).astype(MXU_DTYPE))
    p_win = jnp.concatenate(p_win, axis=1)
    ow = jnp.zeros((LANES, nh * tq), F32)
    for j in range(span // wck):
        ow = ow + _dot(vwt_ref[start // wck + j], p_win[j * wck:(j + 1) * wck, :])
    owin_scr[...] = ow[0:HEAD_DIM] * (1.0 / ow[HEAD_DIM:HEAD_DIM + 1])

    blk = lax.broadcasted_iota(jnp.int32, (nsp, 1), 0)
    cur = t // SEL_BLOCK
    valid = blk <= cur
    forced = (blk == 0) | (blk == cur) | (blk == cur - 1)
    score = jnp.where(valid, jnp.where(forced, jnp.inf, imp), -1.0)
    chosen = jnp.zeros((nsp, tq), F32)
    for _ in range(n_sel):
        best = jnp.max(score, axis=0, keepdims=True)
        idx = jnp.min(jnp.where(score == best, blk, nsp), axis=0, keepdims=True)
        hit = blk == idx
        chosen = jnp.where(hit, 1.0, chosen)
        score = jnp.where(hit, -jnp.inf, score)
    mask_scr[0:nsp, :] = ((chosen - 1.0) * (-MASKED)).astype(mask_scr.dtype)
    mask_scr[nsp:nsp + SEL_CODE_BLOCKS, :] = jnp.full((SEL_CODE_BLOCKS, tq), MASKED, mask_scr.dtype)

    for h in range(nh):
        qaug_scr[0:HEAD_DIM, cols[h]] = qrt_ref[h * HEAD_DIM:(h + 1) * HEAD_DIM, :]
    m_scr[...] = jnp.full(m_scr.shape, -1e30, F32)
    acc_scr[...] = jnp.zeros(acc_scr.shape, F32)

    c_diag = q0 // ck

    def sel_scores(c, s_ref, top_ref, limit):
        k0 = pl.multiple_of(jnp.minimum(c, c_diag) * ck, ck)
        code0 = jnp.where(c < limit, (k0 // (SEL_BLOCK * SEL_CODE_BLOCKS)) * SEL_CODE_BLOCKS, nsp)
        mrows = mask_scr[pl.ds(pl.multiple_of(code0, SEL_CODE_BLOCKS), SEL_CODE_BLOCKS), :]
        for h in range(nh):
            qaug_scr[HEAD_DIM:2 * HEAD_DIM, cols[h]] = mrows
        s_all = _dot(ks_ref[pl.ds(k0, ck), :], qaug_scr[...])
        s_ref[...] = s_all
        top_ref[...] = jnp.max(s_all, axis=0, keepdims=True)

    def sel_update(c, s_ref, top_ref, causal):
        m_old = m_scr[...]
        if causal:
            kpos = c * ck + lax.broadcasted_iota(jnp.int32, (ck, 1), 0)
            bias = jnp.where(kpos <= t, 0.0, MASKED)
            tops = [jnp.max(s_ref[:, cols[h]] + bias, axis=0, keepdims=True) for h in range(nh)]
            m_new = jnp.maximum(m_old, jnp.concatenate(tops, axis=1))
        else:
            m_new = jnp.maximum(m_old, top_ref[...])
        p_all = []
        for h in range(nh):
            s = s_ref[:, cols[h]]
            if causal:
                s = s + bias
            p_all.append(jnp.exp2(s - m_new[:, cols[h]]).astype(MXU_DTYPE))
        pv = _dot(vst_ref[jnp.minimum(c, c_diag)], jnp.concatenate(p_all, axis=1))
        acc_scr[...] = jnp.exp2(m_old - m_new) * acc_scr[...] + pv
        m_scr[...] = m_new

    def sel_pair(j, carry):
        c = 2 * j
        sel_scores(c + 1, sa_scr, ta_scr, c_diag)
        sel_update(c, sb_scr, tb_scr, False)
        sel_scores(c + 2, sb_scr, tb_scr, c_diag)
        sel_update(c + 1, sa_scr, ta_scr, False)
        return carry

    sel_scores(c_diag, sa_scr, ta_scr, c_diag + 1)
    sel_scores(0, sb_scr, tb_scr, c_diag)
    sel_update(c_diag, sa_scr, ta_scr, True)
    lax.fori_loop(0, (c_diag + 1) // 2, sel_pair, 0)
    acc = acc_scr[...]
    o_sel = acc[0:HEAD_DIM] * (1.0 / acc[HEAD_DIM:HEAD_DIM + 1])

    o_cmp = ocmp_scr[...]
    o_win = owin_scr[...]
    gates = gate_ref[...]
    merged = []
    for h in range(nh):
        merged.append(gates[3 * h:3 * h + 1, :] * o_cmp[:, cols[h]]
                      + gates[3 * h + 1:3 * h + 2, :] * o_sel[:, cols[h]]
                      + gates[3 * h + 2:3 * h + 3, :] * o_win[:, cols[h]])
    out_ref[...] = jnp.concatenate(merged, axis=0).T.astype(out_ref.dtype)


def _nsa(qpt, qrt, gates_t, kcmp, vcmpt, ovlt, ks, vst, kw, vwt):
    s = ks.shape[0]
    tq = min(NSA_Q_BLOCK, s)
    nsp = ovlt.shape[0]
    col = lambda n: pl.BlockSpec((n, tq), lambda i: (0, i))
    res = lambda a: _resident(a.shape, lambda i: (0,) * a.ndim)
    return pl.pallas_call(
        functools.partial(_nsa_kernel, n_sel=min(SEL_TOP_N, s // SEL_BLOCK)),
        grid=(s // tq,),
        in_specs=[col(NSA_W), col(NSA_W), col(LANES), res(kcmp), res(vcmpt), res(ovlt),
                  res(ks), res(vst), res(kw), res(vwt)],
        out_specs=pl.BlockSpec((tq, NSA_W), lambda i: (i, 0)),
        out_shape=jax.ShapeDtypeStruct((s, NSA_W), MXU_DTYPE),
        scratch_shapes=[pltpu.VMEM((LANES, NSA_HEADS * tq), MXU_DTYPE),
                        pltpu.VMEM((nsp + SEL_CODE_BLOCKS, tq), MXU_DTYPE),
                        pltpu.VMEM((1, NSA_HEADS * tq), F32), pltpu.VMEM((LANES, NSA_HEADS * tq), F32),
                        pltpu.VMEM((vst.shape[2], NSA_HEADS * tq), F32),
                        pltpu.VMEM((vst.shape[2], NSA_HEADS * tq), F32),
                        pltpu.VMEM((1, NSA_HEADS * tq), F32), pltpu.VMEM((1, NSA_HEADS * tq), F32),
                        pltpu.VMEM((HEAD_DIM, NSA_HEADS * tq), F32), pltpu.VMEM((HEAD_DIM, NSA_HEADS * tq), F32),
                        pltpu.VMEM((nsp, tq), F32)],
        compiler_params=_params(1),
        name="nsa",
    )(qpt, qrt, gates_t, kcmp, vcmpt, ovlt, ks, vst, kw, vwt)


def _softplus2(z2):
    neg_abs = lax.bitcast_convert_type(lax.bitcast_convert_type(z2, jnp.uint32) | jnp.uint32(0x80000000), F32)
    return jnp.maximum(z2, 0.0) + jnp.log2(1.0 + jnp.exp2(neg_abs))


def _sb_kernel(q_ref, k_ref, v_ref, out_ref, acc_scr, run_scr, za_scr, zb_scr):
    assert (q_ref.shape[0] // k_ref.shape[2]) % 2 == 0
    tb = q_ref.shape[0]
    ck = k_ref.shape[2]
    nsub = tb // ck
    i = pl.program_id(0)
    r = lax.broadcasted_iota(jnp.int32, (ck, ck), 0)
    c = lax.broadcasted_iota(jnp.int32, (ck, ck), 1)
    tri = jnp.where(r >= c, 1.0, 0.0).astype(MXU_DTYPE)
    before = c < r
    heads = [slice(h * SB_HEAD_DIM, (h + 1) * SB_HEAD_DIM) for h in range(SB_HEADS)]

    def logits(h, rows, chunk):
        return _dot(q_ref[rows, heads[h]], k_ref[chunk, heads[h], :])

    def step(h, rows, chunk, diag, first, z=None):
        hs = heads[h]
        if z is None:
            z = logits(h, rows, chunk)
        sp = _softplus2(z)
        if diag:
            sp = jnp.where(before, sp, 0.0)
        cs = _dot(sp.astype(MXU_DTYPE), tri)
        if first:
            a = jnp.exp2(z - cs)
            run_scr[h, rows] = cs[:, 0:1]
        else:
            run = run_scr[h, rows]
            a = jnp.exp2(z - cs - run)
            run_scr[h, rows] = run + cs[:, 0:1]
        if diag:
            a = jnp.where(before, a, 0.0)
        pv = _dot(a.astype(MXU_DTYPE), v_ref[pl.ds(pl.multiple_of(chunk * ck, ck), ck), hs])
        if first:
            acc_scr[h, rows] = pv
        else:
            acc_scr[h, rows] += pv

    for g in range(nsub):
        rows = slice(g * ck, (g + 1) * ck)
        for h in range(SB_HEADS):
            step(h, rows, i * nsub + g, True, True)
        for back in range(g):
            for h in range(SB_HEADS):
                step(h, rows, i * nsub + g - 1 - back, False, False)

    everything = slice(0, tb)

    def lookahead(chunk, z_ref):
        for h in range(SB_HEADS):
            z_ref[h] = logits(h, everything, jnp.maximum(chunk, 0))

    def pair(j, carry):
        cur = i * nsub - 1 - 2 * j
        lookahead(cur - 1, zb_scr)
        for h in range(SB_HEADS):
            step(h, everything, cur, False, False, za_scr[h])
        lookahead(cur - 2, za_scr)
        for h in range(SB_HEADS):
            step(h, everything, cur - 1, False, False, zb_scr[h])
        return carry

    lookahead(i * nsub - 1, za_scr)
    lax.fori_loop(0, (i * nsub) // 2, pair, 0)
    for h, hs in enumerate(heads):
        out_ref[:, hs] = acc_scr[h].astype(out_ref.dtype)


def _sb(q, kt, v):
    s = q.shape[0]
    tb = min(SB_BLOCK, s)
    return pl.pallas_call(
        _sb_kernel,
        grid=(s // tb,),
        in_specs=[pl.BlockSpec((tb, SB_W), lambda i: (i, 0)),
                  _resident(kt.shape, lambda i: (0, 0, 0)),
                  _resident((s, SB_W), lambda i: (0, 0))],
        out_specs=pl.BlockSpec((tb, SB_W), lambda i: (i, 0)),
        out_shape=jax.ShapeDtypeStruct((s, SB_W), MXU_DTYPE),
        scratch_shapes=[pltpu.VMEM((SB_HEADS, tb, SB_HEAD_DIM), F32), pltpu.VMEM((SB_HEADS, tb, 1), F32),
                        pltpu.VMEM((SB_HEADS, tb, kt.shape[2]), F32), pltpu.VMEM((SB_HEADS, tb, kt.shape[2]), F32)],
        compiler_params=_params(1),
        name="sb",
    )(q, kt, v)


def _mix_kernel(x_ref, nsa_ref, sb_ref, mg_ref, wn_ref, ws_ref, wo_ref, g_ref, out_ref):
    d = x_ref.shape[1]
    y_nsa = _dot(nsa_ref[...], wn_ref[...])
    y_sb = _dot(sb_ref[...], ws_ref[...])
    merged = mg_ref[:, 0:d] * y_nsa + mg_ref[:, d:2 * d] * y_sb
    mixed = _dot(merged.astype(MXU_DTYPE), wo_ref[...])
    out_ref[...] = x_ref[...] + _rms(mixed, g_ref[...])


def _mix(x, nsa_o, sb_o, mg, wn_all, ws_all, wo_all, g_all, layer):
    s, d = x.shape
    t = min(ROW_TILE, s)
    row = lambda n: pl.BlockSpec((t, n), lambda i: (i, 0))
    lay = lambda a: _resident((None,) + a.shape[1:], lambda i: (layer, 0, 0))
    return pl.pallas_call(
        _mix_kernel,
        grid=(s // t,),
        in_specs=[row(d), row(NSA_W), row(SB_W), row(2 * d), lay(wn_all), lay(ws_all), lay(wo_all), lay(g_all)],
        out_specs=row(d),
        out_shape=jax.ShapeDtypeStruct((s, d), F32),
        compiler_params=_params(1),
        name="mix",
    )(x, nsa_o, sb_o, mg, wn_all, ws_all, wo_all, g_all)


def _ffn_kernel(x_ref, gin_ref, w1_ref, w2_ref, gout_ref, out_ref):
    x = x_ref[...]
    d = x.shape[1]
    hb = _rms(x, gin_ref[...]).astype(MXU_DTYPE)
    ff = jnp.zeros_like(x)
    for c in range(w1_ref.shape[1] // d):
        up = _dot(hb, w1_ref[:, c * d:(c + 1) * d])
        ff = ff + _dot(jnp.square(jnp.maximum(up, 0.0)).astype(MXU_DTYPE), w2_ref[c * d:(c + 1) * d, :])
    out_ref[...] = x + _rms(ff, gout_ref[...])


def _ffn(x, gin_all, w1_all, w2_all, gout_all, layer):
    s, d = x.shape
    t = min(ROW_TILE, s)
    row = lambda n: pl.BlockSpec((t, n), lambda i: (i, 0))
    lay = lambda a: _resident((None,) + a.shape[1:], lambda i: (layer, 0, 0))
    return pl.pallas_call(
        _ffn_kernel,
        grid=(s // t,),
        in_specs=[row(d), lay(gin_all), lay(w1_all), lay(w2_all), lay(gout_all)],
        out_specs=row(d),
        out_shape=jax.ShapeDtypeStruct((s, d), F32),
        compiler_params=_params(1),
        name="ffn",
    )(x, gin_all, w1_all, w2_all, gout_all)


def _regroup_w_in(w_in):
    gate_lo, gate_hi = _C_GATE, _C_GATE + 3 * NSA_HEADS
    pad = jnp.zeros(w_in.shape[:2] + (LANES - 3 * NSA_HEADS,), w_in.dtype)
    return jnp.concatenate([w_in[..., :gate_lo], w_in[..., gate_lo:gate_hi], pad, w_in[..., gate_hi:]],
                           axis=-1).astype(MXU_DTYPE)


def kernel(x, positions, norm_g, w_in, cmp_pe, cmp_w1, cmp_w2, w_nsa_o, w_sb_o, w_out, w_ff1, w_ff2):
    b, s, d = x.shape
    depth = w_in.shape[0]
    ncp, ns = s // CMP_STRIDE, s // SEL_BLOCK
    nsp = -(-ns // LANES) * LANES
    half_w = CMP_STRIDE * HEAD_DIM

    w_in_r = _regroup_w_in(w_in)
    pe = cmp_pe.reshape(depth, 2, 2, half_w)
    w1 = cmp_w1.astype(MXU_DTYPE)
    w2 = cmp_w2.astype(MXU_DTYPE)
    wn, ws, wo = w_nsa_o.astype(MXU_DTYPE), w_sb_o.astype(MXU_DTYPE), w_out.astype(MXU_DTYPE)
    wf1, wf2 = w_ff1.astype(MXU_DTYPE), w_ff2.astype(MXU_DTYPE)
    g_pre, g_mix, g_ffn_in, g_ffn_out = (norm_g[:, n][:, None, :] for n in range(4))

    dim = jnp.arange(LANES) % HEAD_DIM
    half = ROT_DIM // 2
    inv_freq = jnp.power(ROPE_THETA, (dim % half).astype(F32) * (-2.0 / ROT_DIM))
    invf = jnp.where(dim < ROT_DIM, inv_freq, 0.0)[None, :].astype(F32)
    c_start = CMP_STRIDE * jnp.arange(ncp)[None, :]
    s_start = SEL_BLOCK * jnp.arange(nsp)[:, None]
    ovlt = ((c_start < s_start + SEL_BLOCK) & (c_start + CMP_BLOCK > s_start) & (s_start < s)).astype(MXU_DTYPE)

    outs = []
    for bi in range(b):
        xb = x[bi]
        pos = positions[bi][:, None]
        for layer in range(depth):
            (qpt, qrt, kc, vc, ks, vst, kw, vwt, gates_t, sbq, sbkt, sbv, mg) = _inproj(
                xb, g_pre, pos, invf, w_in_r, layer)
            kcmp, vcmpt = _compress(kc.reshape(ncp, half_w), vc.reshape(ncp, half_w), pe, w1, w2, layer)
            nsa_o = _nsa(qpt, qrt, gates_t, kcmp, vcmpt, ovlt, ks, vst, kw, vwt)
            sb_o = _sb(sbq, sbkt, sbv)
            xb = _mix(xb, nsa_o, sb_o, mg, wn, ws, wo, g_mix, layer)
            xb = _ffn(xb, g_ffn_in, wf1, wf2, g_ffn_out, layer)
        outs.append(xb)
    return jnp.stack(outs, axis=0)
```

```python
import functools

import jax
import jax.numpy as jnp
from jax import lax
from jax.experimental import pallas as pl
from jax.experimental.pallas import tpu as pltpu

F32 = jnp.float32
MXU_DTYPE = jnp.bfloat16

HEAD_DIM = 64
NSA_HEADS = 8
SB_HEADS = 4
SB_HEAD_DIM = 128
ROPE_THETA = 500000.0
ROT_DIM = HEAD_DIM // 4
CMP_BLOCK = 32
CMP_STRIDE = 16
SEL_BLOCK = 64
SEL_TOP_N = 8
WINDOW = 512
RMS_EPS = 1e-6
NSA_W = NSA_HEADS * HEAD_DIM
SB_W = SB_HEADS * SB_HEAD_DIM
LANES = 128
MASKED = -32768.0
LOG2_E = 1.4426950408889634
VMEM_LIMIT = 56 * 1024 * 1024

ROW_TILE = 512
NSA_Q_BLOCK = 128
NSA_KEY_CHUNK = 512
SEL_CODE_BLOCKS = HEAD_DIM
CMP_VARIANTS = 4
SB_BLOCK = 512
SB_KEY_CHUNK = 256
SB_EXHAUSTED_LOG2 = 1100.0


def _dot(a, b):
    return jnp.dot(a, b, preferred_element_type=F32)


def _rms(x, g):
    return x * lax.rsqrt(jnp.mean(x * x, axis=-1, keepdims=True) + RMS_EPS) * g


def _params(n_grid_dims):
    return pltpu.CompilerParams(dimension_semantics=("arbitrary",) * n_grid_dims,
                                vmem_limit_bytes=VMEM_LIMIT)


def _resident(block_shape, index_map):
    return pl.BlockSpec(block_shape, index_map, pipeline_mode=pl.Buffered(1))


_C_KV = NSA_W
_C_GATE = _C_KV + 6 * HEAD_DIM
_C_SB = _C_GATE + LANES
_C_MERGE = _C_SB + 3 * SB_W


def _inproj_kernel(x_ref, g_ref, pos_ref, invf_ref, w_ref,
                   qpt_ref, qrt_ref, kc_ref, vc_ref, ks_ref, vst_ref, kw_ref, vwt_ref,
                   gate_ref, sbq_ref, sbk_ref, sbv_ref, mg_ref):
    t, d_model = x_ref.shape
    hb = _rms(x_ref[...], g_ref[...]).astype(MXU_DTYPE)

    ang = pos_ref[...].astype(F32) * invf_ref[...]
    cos, sin = jnp.cos(ang), jnp.sin(ang)
    lane = lax.broadcasted_iota(jnp.int32, (1, LANES), 1)
    dim = lane % HEAD_DIM
    half = ROT_DIM // 2
    sin_up = jnp.where((dim >= half) & (dim < ROT_DIM), sin, 0.0)
    sin_dn = jnp.where(dim < half, -sin, 0.0)
    low = lane < HEAD_DIM

    def rope(xg):
        return xg * cos + pltpu.roll(xg, half, 1) * sin_up + pltpu.roll(xg, LANES - half, 1) * sin_dn

    pa = _dot(hb, w_ref[:, 0:_C_GATE])
    scale = LOG2_E * HEAD_DIM ** -0.5
    for j in range(NSA_W // LANES):
        qg = pa[:, j * LANES:(j + 1) * LANES]
        qpt_ref[j * LANES:(j + 1) * LANES, :] = (qg * scale).T.astype(qpt_ref.dtype)
        qrt_ref[j * LANES:(j + 1) * LANES, :] = (rope(qg) * scale).T.astype(qrt_ref.dtype)
    kc_ref[...] = pa[:, _C_KV:_C_KV + HEAD_DIM]
    vc_ref[...] = pa[:, _C_KV + HEAD_DIM:_C_KV + 2 * HEAD_DIM]

    row = pl.program_id(0) * t + lax.broadcasted_iota(jnp.int32, (t, 1), 0)
    code = jnp.where(lane - HEAD_DIM == (row // SEL_BLOCK) % SEL_CODE_BLOCKS, 1.0, 0.0)
    ksg = pa[:, _C_KV + 2 * HEAD_DIM:_C_KV + 4 * HEAD_DIM]
    ks_ref[...] = jnp.where(low, rope(ksg), code).astype(ks_ref.dtype)
    vst_ref[0] = jnp.where(low, pltpu.roll(ksg, HEAD_DIM, 1), 1.0).T.astype(vst_ref.dtype)
    kwg = pa[:, _C_KV + 4 * HEAD_DIM:_C_KV + 6 * HEAD_DIM]
    kw_ref[...] = jnp.where(low, rope(kwg), 0.0).astype(kw_ref.dtype)
    vw_t = jnp.where(low, pltpu.roll(kwg, HEAD_DIM, 1), 1.0).T
    for c in range(vwt_ref.shape[0]):
        vwt_ref[c] = vw_t[:, c * vwt_ref.shape[2]:(c + 1) * vwt_ref.shape[2]].astype(vwt_ref.dtype)

    gate_ref[...] = jax.nn.sigmoid(_dot(hb, w_ref[:, _C_GATE:_C_SB])).T
    sb = _dot(hb, w_ref[:, _C_SB:_C_MERGE])
    sbq_ref[...] = (sb[:, 0:SB_W] * (LOG2_E * SB_HEAD_DIM ** -0.5)).astype(sbq_ref.dtype)
    sbk_t = sb[:, SB_W:2 * SB_W].T
    for c in range(sbk_ref.shape[0]):
        sbk_ref[c] = sbk_t[:, c * sbk_ref.shape[2]:(c + 1) * sbk_ref.shape[2]].astype(sbk_ref.dtype)
    sbv_ref[...] = sb[:, 2 * SB_W:3 * SB_W].astype(sbv_ref.dtype)
    for c in range(2):
        mg_ref[:, c * d_model:(c + 1) * d_model] = jax.nn.sigmoid(
            _dot(hb, w_ref[:, _C_MERGE + c * d_model:_C_MERGE + (c + 1) * d_model]))


def _inproj(x, g, pos, invf, w_all, layer):
    s, d = x.shape
    t = min(ROW_TILE, s)
    wcols = w_all.shape[2]
    sck = min(SB_KEY_CHUNK, s)
    wck = min(NSA_Q_BLOCK, s)
    row = lambda n: pl.BlockSpec((t, n), lambda i: (i, 0))
    col = lambda n: pl.BlockSpec((n, t), lambda i: (0, i))
    slab = lambda n, ck: pl.BlockSpec((t // ck, n, ck), lambda i: (i, 0, 0))
    sds = jax.ShapeDtypeStruct
    out_shape = [
        sds((NSA_W, s), MXU_DTYPE), sds((NSA_W, s), MXU_DTYPE),
        sds((s, HEAD_DIM), F32), sds((s, HEAD_DIM), F32),
        sds((s, LANES), MXU_DTYPE), sds((s // t, LANES, t), MXU_DTYPE),
        sds((s, LANES), MXU_DTYPE), sds((s // wck, LANES, wck), MXU_DTYPE),
        sds((LANES, s), F32),
        sds((s, SB_W), MXU_DTYPE), sds((s // sck, SB_W, sck), MXU_DTYPE), sds((s, SB_W), MXU_DTYPE),
        sds((s, 2 * d), F32),
    ]
    out_specs = [col(NSA_W), col(NSA_W), row(HEAD_DIM), row(HEAD_DIM),
                 row(LANES), slab(LANES, t), row(LANES), slab(LANES, wck),
                 col(LANES), row(SB_W), slab(SB_W, sck), row(SB_W), row(2 * d)]
    return pl.pallas_call(
        _inproj_kernel,
        grid=(s // t,),
        in_specs=[row(d),
                  _resident((None, 1, d), lambda i: (layer, 0, 0)),
                  row(1),
                  _resident((1, LANES), lambda i: (0, 0)),
                  _resident((None, d, wcols), lambda i: (layer, 0, 0))],
        out_specs=out_specs,
        out_shape=out_shape,
        compiler_params=_params(1),
        name="inproj",
    )(x, g, pos, invf, w_all)


def _compress_kernel(kc_ref, vc_ref, pe_ref, w1_ref, w2_ref, kcmp_ref, vcmpt_ref):
    nr, half_w = kc_ref.shape
    outs = []
    for kv, r_ref in enumerate((kc_ref, vc_ref)):
        r = r_ref[...]
        ha = _dot((r + pe_ref[kv, 0:1, :]).astype(MXU_DTYPE), w1_ref[kv, 0:half_w, :])
        hb = _dot((r + pe_ref[kv, 1:2, :]).astype(MXU_DTYPE), w1_ref[kv, half_w:2 * half_w, :])
        hid = ha + pltpu.roll(hb, nr - 1, 0)
        outs.append(_dot(jax.nn.gelu(hid).astype(MXU_DTYPE), w2_ref[kv]))
    pad = jnp.zeros_like(outs[0])
    kcmp_ref[...] = jnp.concatenate([outs[0], pad], axis=1).astype(kcmp_ref.dtype)
    vcmpt_ref[...] = jnp.concatenate([outs[1], pad], axis=1).T.astype(vcmpt_ref.dtype)


def _compress(kc, vc, pe_all, w1_all, w2_all, layer):
    nr, half_w = kc.shape
    hidden = w1_all.shape[3]
    full = lambda shape: pl.BlockSpec(shape, lambda i: (0,) * len(shape))
    return pl.pallas_call(
        _compress_kernel,
        grid=(1,),
        in_specs=[full((nr, half_w)), full((nr, half_w)),
                  pl.BlockSpec((None, 2, 2, half_w), lambda i: (layer, 0, 0, 0)),
                  pl.BlockSpec((None, 2, 2 * half_w, hidden), lambda i: (layer, 0, 0, 0)),
                  pl.BlockSpec((None, 2, hidden, HEAD_DIM), lambda i: (layer, 0, 0, 0))],
        out_specs=[full((nr, LANES)), full((LANES, nr))],
        out_shape=[jax.ShapeDtypeStruct((nr, LANES), MXU_DTYPE), jax.ShapeDtypeStruct((LANES, nr), MXU_DTYPE)],
        compiler_params=_params(1),
        name="compress",
    )(kc, vc, pe_all, w1_all, w2_all)


def _nsa_kernel(qpt_ref, qrt_ref, gate_ref, kcmp_ref, vcmpt_ref, ovlt_ref, ks_ref, vst_ref, kw_ref, vwt_ref,
                out_ref, qaug_scr, mask_scr, m_scr, acc_scr, sa_scr, sb_scr, ta_scr, tb_scr, ocmp_scr, owin_scr, imp_scr, *, n_sel):
    tq = qpt_ref.shape[1]
    seq = ks_ref.shape[0]
    ncp = kcmp_ref.shape[0]
    nsp = ovlt_ref.shape[0]
    ck = vst_ref.shape[2]
    wck = vwt_ref.shape[2]
    nh = NSA_HEADS
    q0 = pl.program_id(0) * tq
    t = q0 + lax.broadcasted_iota(jnp.int32, (1, tq), 1)
    cols = [slice(h * tq, (h + 1) * tq) for h in range(nh)]

    def aug(qt_ref, h, tail):
        return jnp.concatenate([qt_ref[h * HEAD_DIM:(h + 1) * HEAD_DIM, :], tail], axis=0)

    zeros_tail = jnp.zeros((HEAD_DIM, tq), MXU_DTYPE)
    def cmp_branch(rows):
        qp_aug = jnp.concatenate([aug(qpt_ref, h, zeros_tail) for h in range(nh)], axis=1)
        sc_all = _dot(kcmp_ref[0:rows, :], qp_aug)
        cmp_last = CMP_STRIDE * lax.broadcasted_iota(jnp.int32, (rows, 1), 0) + (CMP_BLOCK - 1)
        vis_c = cmp_last <= t
        psum = jnp.zeros((rows, tq), F32)
        p_cmp = []
        for h in range(nh):
            sc = jnp.where(vis_c, sc_all[:, cols[h]], -1e30)
            e = jnp.where(vis_c, jnp.exp2(sc - jnp.max(sc, axis=0, keepdims=True)), 0.0)
            den = jnp.sum(e, axis=0, keepdims=True)
            p = e * (1.0 / jnp.where(den > 0.0, den, 1.0))
            psum = psum + p
            p_cmp.append(p.astype(MXU_DTYPE))
        o_cmp = _dot(vcmpt_ref[:, 0:rows], jnp.concatenate(p_cmp, axis=1))
        p_hi = psum.astype(MXU_DTYPE)
        p_lo = (psum - p_hi.astype(F32)).astype(MXU_DTYPE)
        imp_scr[...] = _dot(ovlt_ref[:, 0:rows], p_hi) + _dot(ovlt_ref[:, 0:rows], p_lo)
        ocmp_scr[...] = o_cmp[0:HEAD_DIM]

    n_var = CMP_VARIANTS if ncp % (CMP_VARIANTS * LANES) == 0 else 1
    step = ncp // n_var
    n_vis = (q0 + tq - CMP_BLOCK) // CMP_STRIDE + 1
    variant = jnp.clip((n_vis + step - 1) // step - 1, 0, n_var - 1)
    for v in range(n_var):
        pl.when(variant == v)(functools.partial(cmp_branch, (v + 1) * step))
    imp = imp_scr[...]

    qr_aug = jnp.concatenate([aug(qrt_ref, h, zeros_tail) for h in range(nh)], axis=1)
    span = min(WINDOW + tq, seq)
    start = pl.multiple_of(jnp.maximum(q0 + tq - span, 0), wck)
    kpos_w = start + lax.broadcasted_iota(jnp.int32, (span, 1), 0)
    vis_w = (kpos_w <= t) & (kpos_w > t - WINDOW)
    sw_all = _dot(kw_ref[pl.ds(start, span), :], qr_aug)
    p_win = []
    for h in range(nh):
        sw = jnp.where(vis_w, sw_all[:, cols[h]], -1e30)
        p_win.append(jnp.exp2(sw - jnp.max(sw, axis=0, keepdims=True)).astype(MXU_DTYPE))
    p_win = jnp.concatenate(p_win, axis=1)
    ow = jnp.zeros((LANES, nh * tq), F32)
    for j in range(span // wck):
        ow = ow + _dot(vwt_ref[start // wck + j], p_win[j * wck:(j + 1) * wck, :])
    owin_scr[...] = ow[0:HEAD_DIM] * (1.0 / ow[HEAD_DIM:HEAD_DIM + 1])

    blk = lax.broadcasted_iota(jnp.int32, (nsp, 1), 0)
    cur = t // SEL_BLOCK
    valid = blk <= cur
    forced = (blk == 0) | (blk == cur) | (blk == cur - 1)
    score = jnp.where(valid, jnp.where(forced, jnp.inf, imp), -1.0)
    chosen = jnp.zeros((nsp, tq), F32)
    for _ in range(n_sel):
        best = jnp.max(score, axis=0, keepdims=True)
        idx = jnp.min(jnp.where(score == best, blk, nsp), axis=0, keepdims=True)
        hit = blk == idx
        chosen = jnp.where(hit, 1.0, chosen)
        score = jnp.where(hit, -jnp.inf, score)
    mask_scr[0:nsp, :] = ((chosen - 1.0) * (-MASKED)).astype(mask_scr.dtype)
    mask_scr[nsp:nsp + SEL_CODE_BLOCKS, :] = jnp.full((SEL_CODE_BLOCKS, tq), MASKED, mask_scr.dtype)

    for h in range(nh):
        qaug_scr[0:HEAD_DIM, cols[h]] = qrt_ref[h * HEAD_DIM:(h + 1) * HEAD_DIM, :]
    m_scr[...] = jnp.full(m_scr.shape, -1e30, F32)
    acc_scr[...] = jnp.zeros(acc_scr.shape, F32)

    c_diag = q0 // ck

    def sel_scores(c, s_ref, top_ref, limit):
        k0 = pl.multiple_of(jnp.minimum(c, c_diag) * ck, ck)
        code0 = jnp.where(c < limit, (k0 // (SEL_BLOCK * SEL_CODE_BLOCKS)) * SEL_CODE_BLOCKS, nsp)
        mrows = mask_scr[pl.ds(pl.multiple_of(code0, SEL_CODE_BLOCKS), SEL_CODE_BLOCKS), :]
        for h in range(nh):
            qaug_scr[HEAD_DIM:2 * HEAD_DIM, cols[h]] = mrows
        s_all = _dot(ks_ref[pl.ds(k0, ck), :], qaug_scr[...])
        s_ref[...] = s_all
        top_ref[...] = jnp.max(s_all, axis=0, keepdims=True)

    def sel_update(c, s_ref, top_ref, causal):
        m_old = m_scr[...]
        if causal:
            kpos = c * ck + lax.broadcasted_iota(jnp.int32, (ck, 1), 0)
            bias = jnp.where(kpos <= t, 0.0, MASKED)
            tops = [jnp.max(s_ref[:, cols[h]] + bias, axis=0, keepdims=True) for h in range(nh)]
            m_new = jnp.maximum(m_old, jnp.concatenate(tops, axis=1))
        else:
            m_new = jnp.maximum(m_old, top_ref[...])
        p_all = []
        for h in range(nh):
            s = s_ref[:, cols[h]]
            if causal:
                s = s + bias
            p_all.append(jnp.exp2(s - m_new[:, cols[h]]).astype(MXU_DTYPE))
        pv = _dot(vst_ref[jnp.minimum(c, c_diag)], jnp.concatenate(p_all, axis=1))
        acc_scr[...] = jnp.exp2(m_old - m_new) * acc_scr[...] + pv
        m_scr[...] = m_new

    def sel_pair(j, carry):
        c = 2 * j
        sel_scores(c + 1, sa_scr, ta_scr, c_diag)
        sel_update(c, sb_scr, tb_scr, False)
        sel_scores(c + 2, sb_scr, tb_scr, c_diag)
        sel_update(c + 1, sa_scr, ta_scr, False)
        return carry

    sel_scores(c_diag, sa_scr, ta_scr, c_diag + 1)
    sel_scores(0, sb_scr, tb_scr, c_diag)
    sel_update(c_diag, sa_scr, ta_scr, True)
    lax.fori_loop(0, (c_diag + 1) // 2, sel_pair, 0)
    acc = acc_scr[...]
    o_sel = acc[0:HEAD_DIM] * (1.0 / acc[HEAD_DIM:HEAD_DIM + 1])

    o_cmp = ocmp_scr[...]
    o_win = owin_scr[...]
    gates = gate_ref[...]
    merged = []
    for h in range(nh):
        merged.append(gates[3 * h:3 * h + 1, :] * o_cmp[:, cols[h]]
                      + gates[3 * h + 1:3 * h + 2, :] * o_sel[:, cols[h]]
                      + gates[3 * h + 2:3 * h + 3, :] * o_win[:, cols[h]])
    out_ref[...] = jnp.concatenate(merged, axis=0).T.astype(out_ref.dtype)


def _nsa(qpt, qrt, gates_t, kcmp, vcmpt, ovlt, ks, vst, kw, vwt):
    s = ks.shape[0]
    tq = min(NSA_Q_BLOCK, s)
    nsp = ovlt.shape[0]
    col = lambda n: pl.BlockSpec((n, tq), lambda i: (0, i))
    res = lambda a: _resident(a.shape, lambda i: (0,) * a.ndim)
    return pl.pallas_call(
        functools.partial(_nsa_kernel, n_sel=min(SEL_TOP_N, s // SEL_BLOCK)),
        grid=(s // tq,),
        in_specs=[col(NSA_W), col(NSA_W), col(LANES), res(kcmp), res(vcmpt), res(ovlt),
                  res(ks), res(vst), res(kw), res(vwt)],
        out_specs=pl.BlockSpec((tq, NSA_W), lambda i: (i, 0)),
        out_shape=jax.ShapeDtypeStruct((s, NSA_W), MXU_DTYPE),
        scratch_shapes=[pltpu.VMEM((LANES, NSA_HEADS * tq), MXU_DTYPE),
                        pltpu.VMEM((nsp + SEL_CODE_BLOCKS, tq), MXU_DTYPE),
                        pltpu.VMEM((1, NSA_HEADS * tq), F32), pltpu.VMEM((LANES, NSA_HEADS * tq), F32),
                        pltpu.VMEM((vst.shape[2], NSA_HEADS * tq), F32),
                        pltpu.VMEM((vst.shape[2], NSA_HEADS * tq), F32),
                        pltpu.VMEM((1, NSA_HEADS * tq), F32), pltpu.VMEM((1, NSA_HEADS * tq), F32),
                        pltpu.VMEM((HEAD_DIM, NSA_HEADS * tq), F32), pltpu.VMEM((HEAD_DIM, NSA_HEADS * tq), F32),
                        pltpu.VMEM((nsp, tq), F32)],
        compiler_params=_params(1),
        name="nsa",
    )(qpt, qrt, gates_t, kcmp, vcmpt, ovlt, ks, vst, kw, vwt)


def _softplus2(z2):
    neg_abs = lax.bitcast_convert_type(lax.bitcast_convert_type(z2, jnp.uint32) | jnp.uint32(0x80000000), F32)
    return jnp.maximum(z2, 0.0) + jnp.log2(1.0 + jnp.exp2(neg_abs))


def _sb_kernel(q_ref, k_ref, v_ref, out_ref, acc_scr, run_scr, za_scr, zb_scr):
    assert (q_ref.shape[0] // k_ref.shape[2]) % 2 == 0
    tb = q_ref.shape[0]
    ck = k_ref.shape[2]
    nsub = tb // ck
    i = pl.program_id(0)
    r = lax.broadcasted_iota(jnp.int32, (ck, ck), 0)
    c = lax.broadcasted_iota(jnp.int32, (ck, ck), 1)
    tri = jnp.where(r >= c, 1.0, 0.0).astype(MXU_DTYPE)
    before = c < r
    heads = [slice(h * SB_HEAD_DIM, (h + 1) * SB_HEAD_DIM) for h in range(SB_HEADS)]

    def logits(h, rows, chunk):
        return _dot(q_ref[rows, heads[h]], k_ref[chunk, heads[h], :])

    def step(h, rows, chunk, diag, first, z=None):
        hs = heads[h]
        if z is None:
            z = logits(h, rows, chunk)
        sp = _softplus2(z)
        if diag:
            sp = jnp.where(before, sp, 0.0)
        cs = _dot(sp.astype(MXU_DTYPE), tri)
        own = jnp.minimum(z - cs, 0.0)
        if first:
            a = jnp.exp2(own)
            run_scr[h, rows] = cs[:, 0:1]
        else:
            run = run_scr[h, rows]
            a = jnp.exp2(own - run)
            run_scr[h, rows] = run + cs[:, 0:1]
        if diag:
            a = jnp.where(before, a, 0.0)
        pv = _dot(a.astype(MXU_DTYPE), v_ref[pl.ds(pl.multiple_of(chunk * ck, ck), ck), hs])
        if first:
            acc_scr[h, rows] = pv
        else:
            acc_scr[h, rows] += pv

    for g in range(nsub):
        rows = slice(g * ck, (g + 1) * ck)
        for h in range(SB_HEADS):
            step(h, rows, i * nsub + g, True, True)
        for back in range(g):
            for h in range(SB_HEADS):
                step(h, rows, i * nsub + g - 1 - back, False, False)

    everything = slice(0, tb)

    def lookahead(chunk, z_ref):
        for h in range(SB_HEADS):
            z_ref[h] = logits(h, everything, jnp.maximum(chunk, 0))

    def pair(state):
        j, _ = state
        cur = i * nsub - 1 - 2 * j
        lookahead(cur - 1, zb_scr)
        for h in range(SB_HEADS):
            step(h, everything, cur, False, False, za_scr[h])
        lookahead(cur - 2, za_scr)
        for h in range(SB_HEADS):
            step(h, everything, cur - 1, False, False, zb_scr[h])
        least = functools.reduce(jnp.minimum, [run_scr[h] for h in range(SB_HEADS)])
        return j + 1, (jnp.min(least) >= SB_EXHAUSTED_LOG2).astype(jnp.int32)

    lookahead(i * nsub - 1, za_scr)
    lax.while_loop(lambda state: (state[0] < (i * nsub) // 2) & (state[1] == 0), pair,
                   (jnp.int32(0), jnp.int32(0)))
    for h, hs in enumerate(heads):
        out_ref[:, hs] = acc_scr[h].astype(out_ref.dtype)


def _sb(q, kt, v):
    s = q.shape[0]
    tb = min(SB_BLOCK, s)
    return pl.pallas_call(
        _sb_kernel,
        grid=(s // tb,),
        in_specs=[pl.BlockSpec((tb, SB_W), lambda i: (i, 0)),
                  _resident(kt.shape, lambda i: (0, 0, 0)),
                  _resident((s, SB_W), lambda i: (0, 0))],
        out_specs=pl.BlockSpec((tb, SB_W), lambda i: (i, 0)),
        out_shape=jax.ShapeDtypeStruct((s, SB_W), MXU_DTYPE),
        scratch_shapes=[pltpu.VMEM((SB_HEADS, tb, SB_HEAD_DIM), F32), pltpu.VMEM((SB_HEADS, tb, 1), F32),
                        pltpu.VMEM((SB_HEADS, tb, kt.shape[2]), F32), pltpu.VMEM((SB_HEADS, tb, kt.shape[2]), F32)],
        compiler_params=_params(1),
        name="sb",
    )(q, kt, v)


def _mix_kernel(x_ref, nsa_ref, sb_ref, mg_ref, wn_ref, ws_ref, wo_ref, g_ref, out_ref):
    d = x_ref.shape[1]
    y_nsa = _dot(nsa_ref[...], wn_ref[...])
    y_sb = _dot(sb_ref[...], ws_ref[...])
    merged = mg_ref[:, 0:d] * y_nsa + mg_ref[:, d:2 * d] * y_sb
    mixed = _dot(merged.astype(MXU_DTYPE), wo_ref[...])
    out_ref[...] = x_ref[...] + _rms(mixed, g_ref[...])


def _mix(x, nsa_o, sb_o, mg, wn_all, ws_all, wo_all, g_all, layer):
    s, d = x.shape
    t = min(ROW_TILE, s)
    row = lambda n: pl.BlockSpec((t, n), lambda i: (i, 0))
    lay = lambda a: _resident((None,) + a.shape[1:], lambda i: (layer, 0, 0))
    return pl.pallas_call(
        _mix_kernel,
        grid=(s // t,),
        in_specs=[row(d), row(NSA_W), row(SB_W), row(2 * d), lay(wn_all), lay(ws_all), lay(wo_all), lay(g_all)],
        out_specs=row(d),
        out_shape=jax.ShapeDtypeStruct((s, d), F32),
        compiler_params=_params(1),
        name="mix",
    )(x, nsa_o, sb_o, mg, wn_all, ws_all, wo_all, g_all)


def _ffn_kernel(x_ref, gin_ref, w1_ref, w2_ref, gout_ref, out_ref):
    x = x_ref[...]
    d = x.shape[1]
    hb = _rms(x, gin_ref[...]).astype(MXU_DTYPE)
    ff = jnp.zeros_like(x)
    for c in range(w1_ref.shape[1] // d):
        up = _dot(hb, w1_ref[:, c * d:(c + 1) * d])
        ff = ff + _dot(jnp.square(jnp.maximum(up, 0.0)).astype(MXU_DTYPE), w2_ref[c * d:(c + 1) * d, :])
    out_ref[...] = x + _rms(ff, gout_ref[...])


def _ffn(x, gin_all, w1_all, w2_all, gout_all, layer):
    s, d = x.shape
    t = min(ROW_TILE, s)
    row = lambda n: pl.BlockSpec((t, n), lambda i: (i, 0))
    lay = lambda a: _resident((None,) + a.shape[1:], lambda i: (layer, 0, 0))
    return pl.pallas_call(
        _ffn_kernel,
        grid=(s // t,),
        in_specs=[row(d), lay(gin_all), lay(w1_all), lay(w2_all), lay(gout_all)],
        out_specs=row(d),
        out_shape=jax.ShapeDtypeStruct((s, d), F32),
        compiler_params=_params(1),
        name="ffn",
    )(x, gin_all, w1_all, w2_all, gout_all)


def _regroup_w_in(w_in):
    gate_lo, gate_hi = _C_GATE, _C_GATE + 3 * NSA_HEADS
    pad = jnp.zeros(w_in.shape[:2] + (LANES - 3 * NSA_HEADS,), w_in.dtype)
    return jnp.concatenate([w_in[..., :gate_lo], w_in[..., gate_lo:gate_hi], pad, w_in[..., gate_hi:]],
                           axis=-1).astype(MXU_DTYPE)


def kernel(x, positions, norm_g, w_in, cmp_pe, cmp_w1, cmp_w2, w_nsa_o, w_sb_o, w_out, w_ff1, w_ff2):
    b, s, d = x.shape
    depth = w_in.shape[0]
    ncp, ns = s // CMP_STRIDE, s // SEL_BLOCK
    nsp = -(-ns // LANES) * LANES
    half_w = CMP_STRIDE * HEAD_DIM

    w_in_r = _regroup_w_in(w_in)
    pe = cmp_pe.reshape(depth, 2, 2, half_w)
    w1 = cmp_w1.astype(MXU_DTYPE)
    w2 = cmp_w2.astype(MXU_DTYPE)
    wn, ws, wo = w_nsa_o.astype(MXU_DTYPE), w_sb_o.astype(MXU_DTYPE), w_out.astype(MXU_DTYPE)
    wf1, wf2 = w_ff1.astype(MXU_DTYPE), w_ff2.astype(MXU_DTYPE)
    g_pre, g_mix, g_ffn_in, g_ffn_out = (norm_g[:, n][:, None, :] for n in range(4))

    dim = jnp.arange(LANES) % HEAD_DIM
    half = ROT_DIM // 2
    inv_freq = jnp.power(ROPE_THETA, (dim % half).astype(F32) * (-2.0 / ROT_DIM))
    invf = jnp.where(dim < ROT_DIM, inv_freq, 0.0)[None, :].astype(F32)
    c_start = CMP_STRIDE * jnp.arange(ncp)[None, :]
    s_start = SEL_BLOCK * jnp.arange(nsp)[:, None]
    ovlt = ((c_start < s_start + SEL_BLOCK) & (c_start + CMP_BLOCK > s_start) & (s_start < s)).astype(MXU_DTYPE)

    outs = []
    for bi in range(b):
        xb = x[bi]
        pos = positions[bi][:, None]
        for layer in range(depth):
            (qpt, qrt, kc, vc, ks, vst, kw, vwt, gates_t, sbq, sbkt, sbv, mg) = _inproj(
                xb, g_pre, pos, invf, w_in_r, layer)
            kcmp, vcmpt = _compress(kc.reshape(ncp, half_w), vc.reshape(ncp, half_w), pe, w1, w2, layer)
            nsa_o = _nsa(qpt, qrt, gates_t, kcmp, vcmpt, ovlt, ks, vst, kw, vwt)
            sb_o = _sb(sbq, sbkt, sbv)
            xb = _mix(xb, nsa_o, sb_o, mg, wn, ws, wo, g_mix, layer)
            xb = _ffn(xb, g_ffn_in, wf1, wf2, g_ffn_out, layer)
        outs.append(xb)
    return jnp.stack(outs, axis=0)
```

```python
import functools

import jax
import jax.numpy as jnp
from jax import lax
from jax.experimental import pallas as pl
from jax.experimental.pallas import tpu as pltpu

F32 = jnp.float32
MXU_DTYPE = jnp.bfloat16

HEAD_DIM = 64
NSA_HEADS = 8
SB_HEADS = 4
SB_HEAD_DIM = 128
ROPE_THETA = 500000.0
ROT_DIM = HEAD_DIM // 4
CMP_BLOCK = 32
CMP_STRIDE = 16
SEL_BLOCK = 64
SEL_TOP_N = 8
WINDOW = 512
RMS_EPS = 1e-6
NSA_W = NSA_HEADS * HEAD_DIM
SB_W = SB_HEADS * SB_HEAD_DIM
LANES = 128
VT_ROWS = HEAD_DIM + 16
MASKED = -32768.0
LOG2_E = 1.4426950408889634
VMEM_LIMIT = 56 * 1024 * 1024

ROW_TILE = 512
NSA_Q_BLOCK = 128
NSA_KEY_CHUNK = 512
SEL_CODE_BLOCKS = HEAD_DIM
CMP_VARIANTS = 4
SB_BLOCK = 512
SB_KEY_CHUNK = 256
SB_EXHAUSTED_LOG2 = 1100.0


def _dot(a, b):
    return jnp.dot(a, b, preferred_element_type=F32)


def _rms(x, g):
    return x * lax.rsqrt(jnp.mean(x * x, axis=-1, keepdims=True) + RMS_EPS) * g


def _params(n_grid_dims):
    return pltpu.CompilerParams(dimension_semantics=("arbitrary",) * n_grid_dims,
                                vmem_limit_bytes=VMEM_LIMIT)


def _resident(block_shape, index_map):
    return pl.BlockSpec(block_shape, index_map, pipeline_mode=pl.Buffered(1))


_C_KV = NSA_W
_C_GATE = _C_KV + 6 * HEAD_DIM
_C_SB = _C_GATE + LANES
_C_MERGE = _C_SB + 3 * SB_W


def _inproj_kernel(x_ref, g_ref, pos_ref, invf_ref, w_ref,
                   qpt_ref, qrt_ref, kc_ref, vc_ref, ks_ref, vst_ref, kw_ref, vwt_ref,
                   gate_ref, sbq_ref, sbk_ref, sbv_ref, mg_ref):
    t, d_model = x_ref.shape
    hb = _rms(x_ref[...], g_ref[...]).astype(MXU_DTYPE)

    ang = pos_ref[...].astype(F32) * invf_ref[...]
    cos, sin = jnp.cos(ang), jnp.sin(ang)
    lane = lax.broadcasted_iota(jnp.int32, (1, LANES), 1)
    dim = lane % HEAD_DIM
    half = ROT_DIM // 2
    sin_up = jnp.where((dim >= half) & (dim < ROT_DIM), sin, 0.0)
    sin_dn = jnp.where(dim < half, -sin, 0.0)
    low = lane < HEAD_DIM

    def rope(xg):
        return xg * cos + pltpu.roll(xg, half, 1) * sin_up + pltpu.roll(xg, LANES - half, 1) * sin_dn

    pa = _dot(hb, w_ref[:, 0:_C_GATE])
    scale = LOG2_E * HEAD_DIM ** -0.5
    for j in range(NSA_W // LANES):
        qg = pa[:, j * LANES:(j + 1) * LANES]
        qpt_ref[j * LANES:(j + 1) * LANES, :] = (qg * scale).T.astype(qpt_ref.dtype)
        qrt_ref[j * LANES:(j + 1) * LANES, :] = (rope(qg) * scale).T.astype(qrt_ref.dtype)
    kc_ref[...] = pa[:, _C_KV:_C_KV + HEAD_DIM]
    vc_ref[...] = pa[:, _C_KV + HEAD_DIM:_C_KV + 2 * HEAD_DIM]

    row = pl.program_id(0) * t + lax.broadcasted_iota(jnp.int32, (t, 1), 0)
    code = jnp.where(lane - HEAD_DIM == (row // SEL_BLOCK) % SEL_CODE_BLOCKS, 1.0, 0.0)
    ksg = pa[:, _C_KV + 2 * HEAD_DIM:_C_KV + 4 * HEAD_DIM]
    ks_ref[...] = jnp.where(low, rope(ksg), code).astype(ks_ref.dtype)
    vst_ref[0] = jnp.where(low, pltpu.roll(ksg, HEAD_DIM, 1), 1.0).T[0:VT_ROWS].astype(vst_ref.dtype)
    kwg = pa[:, _C_KV + 4 * HEAD_DIM:_C_KV + 6 * HEAD_DIM]
    kw_ref[...] = jnp.where(low, rope(kwg), 0.0).astype(kw_ref.dtype)
    vw_t = jnp.where(low, pltpu.roll(kwg, HEAD_DIM, 1), 1.0).T[0:VT_ROWS]
    for c in range(vwt_ref.shape[0]):
        vwt_ref[c] = vw_t[:, c * vwt_ref.shape[2]:(c + 1) * vwt_ref.shape[2]].astype(vwt_ref.dtype)

    gate_ref[...] = jax.nn.sigmoid(_dot(hb, w_ref[:, _C_GATE:_C_SB])).T
    sb = _dot(hb, w_ref[:, _C_SB:_C_MERGE])
    sbq_ref[...] = (sb[:, 0:SB_W] * (LOG2_E * SB_HEAD_DIM ** -0.5)).astype(sbq_ref.dtype)
    sbk_t = sb[:, SB_W:2 * SB_W].T
    for c in range(sbk_ref.shape[0]):
        sbk_ref[c] = sbk_t[:, c * sbk_ref.shape[2]:(c + 1) * sbk_ref.shape[2]].astype(sbk_ref.dtype)
    sbv_ref[...] = sb[:, 2 * SB_W:3 * SB_W].astype(sbv_ref.dtype)
    for c in range(2):
        mg_ref[:, c * d_model:(c + 1) * d_model] = jax.nn.sigmoid(
            _dot(hb, w_ref[:, _C_MERGE + c * d_model:_C_MERGE + (c + 1) * d_model]))


def _inproj(x, g, pos, invf, w_all, layer):
    s, d = x.shape
    t = min(ROW_TILE, s)
    wcols = w_all.shape[2]
    sck = min(SB_KEY_CHUNK, s)
    wck = min(NSA_Q_BLOCK, s)
    row = lambda n: pl.BlockSpec((t, n), lambda i: (i, 0))
    col = lambda n: pl.BlockSpec((n, t), lambda i: (0, i))
    slab = lambda n, ck: pl.BlockSpec((t // ck, n, ck), lambda i: (i, 0, 0))
    sds = jax.ShapeDtypeStruct
    out_shape = [
        sds((NSA_W, s), MXU_DTYPE), sds((NSA_W, s), MXU_DTYPE),
        sds((s, HEAD_DIM), F32), sds((s, HEAD_DIM), F32),
        sds((s, LANES), MXU_DTYPE), sds((s // t, VT_ROWS, t), MXU_DTYPE),
        sds((s, LANES), MXU_DTYPE), sds((s // wck, VT_ROWS, wck), MXU_DTYPE),
        sds((LANES, s), F32),
        sds((s, SB_W), MXU_DTYPE), sds((s // sck, SB_W, sck), MXU_DTYPE), sds((s, SB_W), MXU_DTYPE),
        sds((s, 2 * d), F32),
    ]
    out_specs = [col(NSA_W), col(NSA_W), row(HEAD_DIM), row(HEAD_DIM),
                 row(LANES), slab(VT_ROWS, t), row(LANES), slab(VT_ROWS, wck),
                 col(LANES), row(SB_W), slab(SB_W, sck), row(SB_W), row(2 * d)]
    return pl.pallas_call(
        _inproj_kernel,
        grid=(s // t,),
        in_specs=[row(d),
                  _resident((None, 1, d), lambda i: (layer, 0, 0)),
                  row(1),
                  _resident((1, LANES), lambda i: (0, 0)),
                  _resident((None, d, wcols), lambda i: (layer, 0, 0))],
        out_specs=out_specs,
        out_shape=out_shape,
        compiler_params=_params(1),
        name="inproj",
    )(x, g, pos, invf, w_all)


def _compress_kernel(kc_ref, vc_ref, pe_ref, w1_ref, w2_ref, kcmp_ref, vcmpt_ref):
    nr, half_w = kc_ref.shape
    outs = []
    for kv, r_ref in enumerate((kc_ref, vc_ref)):
        r = r_ref[...]
        ha = _dot((r + pe_ref[kv, 0:1, :]).astype(MXU_DTYPE), w1_ref[kv, 0:half_w, :])
        hb = _dot((r + pe_ref[kv, 1:2, :]).astype(MXU_DTYPE), w1_ref[kv, half_w:2 * half_w, :])
        hid = ha + pltpu.roll(hb, nr - 1, 0)
        outs.append(_dot(jax.nn.gelu(hid).astype(MXU_DTYPE), w2_ref[kv]))
    pad = jnp.zeros_like(outs[0])
    kcmp_ref[...] = jnp.concatenate([outs[0], pad], axis=1).astype(kcmp_ref.dtype)
    vcmpt_ref[...] = jnp.concatenate([outs[1], pad], axis=1).T[0:HEAD_DIM].astype(vcmpt_ref.dtype)


def _compress(kc, vc, pe_all, w1_all, w2_all, layer):
    nr, half_w = kc.shape
    hidden = w1_all.shape[3]
    full = lambda shape: pl.BlockSpec(shape, lambda i: (0,) * len(shape))
    return pl.pallas_call(
        _compress_kernel,
        grid=(1,),
        in_specs=[full((nr, half_w)), full((nr, half_w)),
                  pl.BlockSpec((None, 2, 2, half_w), lambda i: (layer, 0, 0, 0)),
                  pl.BlockSpec((None, 2, 2 * half_w, hidden), lambda i: (layer, 0, 0, 0)),
                  pl.BlockSpec((None, 2, hidden, HEAD_DIM), lambda i: (layer, 0, 0, 0))],
        out_specs=[full((nr, LANES)), full((HEAD_DIM, nr))],
        out_shape=[jax.ShapeDtypeStruct((nr, LANES), MXU_DTYPE), jax.ShapeDtypeStruct((HEAD_DIM, nr), MXU_DTYPE)],
        compiler_params=_params(1),
        name="compress",
    )(kc, vc, pe_all, w1_all, w2_all)


def _nsa_kernel(qpt_ref, qrt_ref, gate_ref, kcmp_ref, vcmpt_ref, ovlt_ref, ks_ref, vst_ref, kw_ref, vwt_ref,
                out_ref, qaug_scr, mask_scr, m_scr, acc_scr, sa_scr, sb_scr, ta_scr, tb_scr, ocmp_scr, owin_scr, imp_scr, *, n_sel):
    tq = qpt_ref.shape[1]
    seq = ks_ref.shape[0]
    ncp = kcmp_ref.shape[0]
    nsp = ovlt_ref.shape[0]
    ck = vst_ref.shape[2]
    wck = vwt_ref.shape[2]
    nh = NSA_HEADS
    q0 = pl.program_id(0) * tq
    t = q0 + lax.broadcasted_iota(jnp.int32, (1, tq), 1)
    cols = [slice(h * tq, (h + 1) * tq) for h in range(nh)]

    def aug(qt_ref, h, tail):
        return jnp.concatenate([qt_ref[h * HEAD_DIM:(h + 1) * HEAD_DIM, :], tail], axis=0)

    zeros_tail = jnp.zeros((HEAD_DIM, tq), MXU_DTYPE)
    def cmp_branch(rows):
        qp_aug = jnp.concatenate([aug(qpt_ref, h, zeros_tail) for h in range(nh)], axis=1)
        sc_all = _dot(kcmp_ref[0:rows, :], qp_aug)
        cmp_last = CMP_STRIDE * lax.broadcasted_iota(jnp.int32, (rows, 1), 0) + (CMP_BLOCK - 1)
        vis_c = cmp_last <= t
        psum = jnp.zeros((rows, tq), F32)
        p_cmp = []
        for h in range(nh):
            sc = jnp.where(vis_c, sc_all[:, cols[h]], -1e30)
            top = jnp.max(sc, axis=0, keepdims=True)
            e = jnp.exp2(sc - jnp.where(top > -1e29, top, 0.0))
            den = jnp.sum(e, axis=0, keepdims=True)
            p = e * (1.0 / jnp.where(den > 0.0, den, 1.0))
            psum = psum + p
            p_cmp.append(p.astype(MXU_DTYPE))
        o_cmp = _dot(vcmpt_ref[:, 0:rows], jnp.concatenate(p_cmp, axis=1))
        p_hi = psum.astype(MXU_DTYPE)
        p_lo = (psum - p_hi.astype(F32)).astype(MXU_DTYPE)
        imp_scr[...] = _dot(ovlt_ref[:, 0:rows], p_hi) + _dot(ovlt_ref[:, 0:rows], p_lo)
        ocmp_scr[...] = o_cmp

    n_var = CMP_VARIANTS if ncp % (CMP_VARIANTS * LANES) == 0 else 1
    step = ncp // n_var
    n_vis = (q0 + tq - CMP_BLOCK) // CMP_STRIDE + 1
    variant = jnp.clip((n_vis + step - 1) // step - 1, 0, n_var - 1)
    for v in range(n_var):
        pl.when(variant == v)(functools.partial(cmp_branch, (v + 1) * step))
    imp = imp_scr[...]

    qr_aug = jnp.concatenate([aug(qrt_ref, h, zeros_tail) for h in range(nh)], axis=1)
    span = min(WINDOW + tq, seq)
    start = pl.multiple_of(jnp.maximum(q0 + tq - span, 0), wck)
    kpos_w = start + lax.broadcasted_iota(jnp.int32, (span, 1), 0)
    vis_w = (kpos_w <= t) & (kpos_w > t - WINDOW)
    sw_all = _dot(kw_ref[pl.ds(start, span), :], qr_aug)
    p_win = []
    for h in range(nh):
        sw = jnp.where(vis_w, sw_all[:, cols[h]], -1e30)
        p_win.append(jnp.exp2(sw - jnp.max(sw, axis=0, keepdims=True)).astype(MXU_DTYPE))
    p_win = jnp.concatenate(p_win, axis=1)
    ow = jnp.zeros((VT_ROWS, nh * tq), F32)
    for j in range(span // wck):
        ow = ow + _dot(vwt_ref[start // wck + j], p_win[j * wck:(j + 1) * wck, :])
    owin_scr[...] = ow[0:HEAD_DIM] * (1.0 / ow[HEAD_DIM:HEAD_DIM + 1])

    blk = lax.broadcasted_iota(jnp.int32, (nsp, 1), 0)
    cur = t // SEL_BLOCK
    valid = blk <= cur
    forced = (blk == 0) | (blk == cur) | (blk == cur - 1)
    score = jnp.where(valid, jnp.where(forced, jnp.inf, imp), -1.0)
    chosen = jnp.zeros((nsp, tq), F32)
    for _ in range(n_sel):
        best = jnp.max(score, axis=0, keepdims=True)
        idx = jnp.min(jnp.where(score == best, blk, nsp), axis=0, keepdims=True)
        hit = blk == idx
        chosen = jnp.where(hit, 1.0, chosen)
        score = jnp.where(hit, -jnp.inf, score)
    mask_scr[0:nsp, :] = ((chosen - 1.0) * (-MASKED)).astype(mask_scr.dtype)
    mask_scr[nsp:nsp + SEL_CODE_BLOCKS, :] = jnp.full((SEL_CODE_BLOCKS, tq), MASKED, mask_scr.dtype)

    for h in range(nh):
        qaug_scr[0:HEAD_DIM, cols[h]] = qrt_ref[h * HEAD_DIM:(h + 1) * HEAD_DIM, :]
    m_scr[...] = jnp.full(m_scr.shape, -1e30, F32)
    acc_scr[...] = jnp.zeros(acc_scr.shape, F32)

    c_diag = q0 // ck

    def sel_scores(c, s_ref, top_ref, limit):
        k0 = pl.multiple_of(jnp.minimum(c, c_diag) * ck, ck)
        code0 = jnp.where(c < limit, (k0 // (SEL_BLOCK * SEL_CODE_BLOCKS)) * SEL_CODE_BLOCKS, nsp)
        mrows = mask_scr[pl.ds(pl.multiple_of(code0, SEL_CODE_BLOCKS), SEL_CODE_BLOCKS), :]
        for h in range(nh):
            qaug_scr[HEAD_DIM:2 * HEAD_DIM, cols[h]] = mrows
        s_all = _dot(ks_ref[pl.ds(k0, ck), :], qaug_scr[...])
        s_ref[...] = s_all
        top_ref[...] = jnp.max(s_all, axis=0, keepdims=True)

    def sel_update(c, s_ref, top_ref, causal):
        m_old = m_scr[...]
        if causal:
            kpos = c * ck + lax.broadcasted_iota(jnp.int32, (ck, 1), 0)
            bias = jnp.where(kpos <= t, 0.0, MASKED)
            tops = [jnp.max(s_ref[:, cols[h]] + bias, axis=0, keepdims=True) for h in range(nh)]
            m_new = jnp.maximum(m_old, jnp.concatenate(tops, axis=1))
        else:
            m_new = jnp.maximum(m_old, top_ref[...])
        p_all = []
        for h in range(nh):
            s = s_ref[:, cols[h]]
            if causal:
                s = s + bias
            p_all.append(jnp.exp2(s - m_new[:, cols[h]]).astype(MXU_DTYPE))
        pv = _dot(vst_ref[jnp.minimum(c, c_diag)], jnp.concatenate(p_all, axis=1))
        acc_scr[...] = jnp.exp2(m_old - m_new) * acc_scr[...] + pv
        m_scr[...] = m_new

    def sel_pair(j, carry):
        c = 2 * j
        sel_scores(c + 1, sa_scr, ta_scr, c_diag)
        sel_update(c, sb_scr, tb_scr, False)
        sel_scores(c + 2, sb_scr, tb_scr, c_diag)
        sel_update(c + 1, sa_scr, ta_scr, False)
        return carry

    sel_scores(c_diag, sa_scr, ta_scr, c_diag + 1)
    sel_scores(0, sb_scr, tb_scr, c_diag)
    sel_update(c_diag, sa_scr, ta_scr, True)
    lax.fori_loop(0, (c_diag + 1) // 2, sel_pair, 0)
    acc = acc_scr[...]
    o_sel = acc[0:HEAD_DIM] * (1.0 / acc[HEAD_DIM:HEAD_DIM + 1])

    o_cmp = ocmp_scr[...]
    o_win = owin_scr[...]
    gates = gate_ref[...]
    merged = []
    for h in range(nh):
        merged.append(gates[3 * h:3 * h + 1, :] * o_cmp[:, cols[h]]
                      + gates[3 * h + 1:3 * h + 2, :] * o_sel[:, cols[h]]
                      + gates[3 * h + 2:3 * h + 3, :] * o_win[:, cols[h]])
    out_ref[...] = jnp.concatenate(merged, axis=0).T.astype(out_ref.dtype)


def _nsa(qpt, qrt, gates_t, kcmp, vcmpt, ovlt, ks, vst, kw, vwt):
    s = ks.shape[0]
    tq = min(NSA_Q_BLOCK, s)
    nsp = ovlt.shape[0]
    col = lambda n: pl.BlockSpec((n, tq), lambda i: (0, i))
    res = lambda a: _resident(a.shape, lambda i: (0,) * a.ndim)
    return pl.pallas_call(
        functools.partial(_nsa_kernel, n_sel=min(SEL_TOP_N, s // SEL_BLOCK)),
        grid=(s // tq,),
        in_specs=[col(NSA_W), col(NSA_W), col(LANES), res(kcmp), res(vcmpt), res(ovlt),
                  res(ks), res(vst), res(kw), res(vwt)],
        out_specs=pl.BlockSpec((tq, NSA_W), lambda i: (i, 0)),
        out_shape=jax.ShapeDtypeStruct((s, NSA_W), MXU_DTYPE),
        scratch_shapes=[pltpu.VMEM((LANES, NSA_HEADS * tq), MXU_DTYPE),
                        pltpu.VMEM((nsp + SEL_CODE_BLOCKS, tq), MXU_DTYPE),
                        pltpu.VMEM((1, NSA_HEADS * tq), F32), pltpu.VMEM((VT_ROWS, NSA_HEADS * tq), F32),
                        pltpu.VMEM((vst.shape[2], NSA_HEADS * tq), F32),
                        pltpu.VMEM((vst.shape[2], NSA_HEADS * tq), F32),
                        pltpu.VMEM((1, NSA_HEADS * tq), F32), pltpu.VMEM((1, NSA_HEADS * tq), F32),
                        pltpu.VMEM((HEAD_DIM, NSA_HEADS * tq), F32), pltpu.VMEM((HEAD_DIM, NSA_HEADS * tq), F32),
                        pltpu.VMEM((nsp, tq), F32)],
        compiler_params=_params(1),
        name="nsa",
    )(qpt, qrt, gates_t, kcmp, vcmpt, ovlt, ks, vst, kw, vwt)


def _softplus2(z2):
    neg_abs = lax.bitcast_convert_type(lax.bitcast_convert_type(z2, jnp.uint32) | jnp.uint32(0x80000000), F32)
    return jnp.maximum(z2, 0.0) + jnp.log2(1.0 + jnp.exp2(neg_abs))


def _sb_kernel(q_ref, k_ref, v_ref, out_ref, acc_scr, run_scr, za_scr, zb_scr):
    assert (q_ref.shape[0] // k_ref.shape[2]) % 2 == 0
    tb = q_ref.shape[0]
    ck = k_ref.shape[2]
    nsub = tb // ck
    i = pl.program_id(0)
    r = lax.broadcasted_iota(jnp.int32, (ck, ck), 0)
    c = lax.broadcasted_iota(jnp.int32, (ck, ck), 1)
    tri = jnp.where(r >= c, 1.0, 0.0).astype(MXU_DTYPE)
    before = c < r
    heads = [slice(h * SB_HEAD_DIM, (h + 1) * SB_HEAD_DIM) for h in range(SB_HEADS)]

    def logits(h, rows, chunk):
        return _dot(q_ref[rows, heads[h]], k_ref[chunk, heads[h], :])

    def step(h, rows, chunk, diag, first, z=None):
        hs = heads[h]
        if z is None:
            z = logits(h, rows, chunk)
        sp = _softplus2(z)
        if diag:
            sp = jnp.where(before, sp, 0.0)
        cs = _dot(sp.astype(MXU_DTYPE), tri)
        own = jnp.minimum(z - cs, 0.0)
        if first:
            a = jnp.exp2(own)
            run_scr[h, rows] = cs[:, 0:1]
        else:
            run = run_scr[h, rows]
            a = jnp.exp2(own - run)
            run_scr[h, rows] = run + cs[:, 0:1]
        if diag:
            a = jnp.where(before, a, 0.0)
        pv = _dot(a.astype(MXU_DTYPE), v_ref[pl.ds(pl.multiple_of(chunk * ck, ck), ck), hs])
        if first:
            acc_scr[h, rows] = pv
        else:
            acc_scr[h, rows] += pv

    for g in range(nsub):
        rows = slice(g * ck, (g + 1) * ck)
        for h in range(SB_HEADS):
            step(h, rows, i * nsub + g, True, True)
        for back in range(g):
            for h in range(SB_HEADS):
                step(h, rows, i * nsub + g - 1 - back, False, False)

    everything = slice(0, tb)

    def lookahead(chunk, z_ref):
        for h in range(SB_HEADS):
            z_ref[h] = logits(h, everything, jnp.maximum(chunk, 0))

    def pair(state):
        j, _ = state
        cur = i * nsub - 1 - 2 * j
        lookahead(cur - 1, zb_scr)
        for h in range(SB_HEADS):
            step(h, everything, cur, False, False, za_scr[h])
        lookahead(cur - 2, za_scr)
        for h in range(SB_HEADS):
            step(h, everything, cur - 1, False, False, zb_scr[h])
        least = functools.reduce(jnp.minimum, [run_scr[h] for h in range(SB_HEADS)])
        return j + 1, (jnp.min(least) >= SB_EXHAUSTED_LOG2).astype(jnp.int32)

    lookahead(i * nsub - 1, za_scr)
    lax.while_loop(lambda state: (state[0] < (i * nsub) // 2) & (state[1] == 0), pair,
                   (jnp.int32(0), jnp.int32(0)))
    for h, hs in enumerate(heads):
        out_ref[:, hs] = acc_scr[h].astype(out_ref.dtype)


def _sb(q, kt, v):
    s = q.shape[0]
    tb = min(SB_BLOCK, s)
    return pl.pallas_call(
        _sb_kernel,
        grid=(s // tb,),
        in_specs=[pl.BlockSpec((tb, SB_W), lambda i: (i, 0)),
                  _resident(kt.shape, lambda i: (0, 0, 0)),
                  _resident((s, SB_W), lambda i: (0, 0))],
        out_specs=pl.BlockSpec((tb, SB_W), lambda i: (i, 0)),
        out_shape=jax.ShapeDtypeStruct((s, SB_W), MXU_DTYPE),
        scratch_shapes=[pltpu.VMEM((SB_HEADS, tb, SB_HEAD_DIM), F32), pltpu.VMEM((SB_HEADS, tb, 1), F32),
                        pltpu.VMEM((SB_HEADS, tb, kt.shape[2]), F32), pltpu.VMEM((SB_HEADS, tb, kt.shape[2]), F32)],
        compiler_params=_params(1),
        name="sb",
    )(q, kt, v)


def _mix_kernel(x_ref, nsa_ref, sb_ref, mg_ref, wn_ref, ws_ref, wo_ref, g_ref, out_ref):
    d = x_ref.shape[1]
    y_nsa = _dot(nsa_ref[...], wn_ref[...])
    y_sb = _dot(sb_ref[...], ws_ref[...])
    merged = mg_ref[:, 0:d] * y_nsa + mg_ref[:, d:2 * d] * y_sb
    mixed = _dot(merged.astype(MXU_DTYPE), wo_ref[...])
    out_ref[...] = x_ref[...] + _rms(mixed, g_ref[...])


def _mix(x, nsa_o, sb_o, mg, wn_all, ws_all, wo_all, g_all, layer):
    s, d = x.shape
    t = min(ROW_TILE, s)
    row = lambda n: pl.BlockSpec((t, n), lambda i: (i, 0))
    lay = lambda a: _resident((None,) + a.shape[1:], lambda i: (layer, 0, 0))
    return pl.pallas_call(
        _mix_kernel,
        grid=(s // t,),
        in_specs=[row(d), row(NSA_W), row(SB_W), row(2 * d), lay(wn_all), lay(ws_all), lay(wo_all), lay(g_all)],
        out_specs=row(d),
        out_shape=jax.ShapeDtypeStruct((s, d), F32),
        compiler_params=_params(1),
        name="mix",
    )(x, nsa_o, sb_o, mg, wn_all, ws_all, wo_all, g_all)


def _ffn_kernel(x_ref, gin_ref, w1_ref, w2_ref, gout_ref, out_ref):
    x = x_ref[...]
    d = x.shape[1]
    hb = _rms(x, gin_ref[...]).astype(MXU_DTYPE)
    ff = jnp.zeros_like(x)
    for c in range(w1_ref.shape[1] // d):
        up = _dot(hb, w1_ref[:, c * d:(c + 1) * d])
        ff = ff + _dot(jnp.square(jnp.maximum(up, 0.0)).astype(MXU_DTYPE), w2_ref[c * d:(c + 1) * d, :])
    out_ref[...] = x + _rms(ff, gout_ref[...])


def _ffn(x, gin_all, w1_all, w2_all, gout_all, layer):
    s, d = x.shape
    t = min(ROW_TILE, s)
    row = lambda n: pl.BlockSpec((t, n), lambda i: (i, 0))
    lay = lambda a: _resident((None,) + a.shape[1:], lambda i: (layer, 0, 0))
    return pl.pallas_call(
        _ffn_kernel,
        grid=(s // t,),
        in_specs=[row(d), lay(gin_all), lay(w1_all), lay(w2_all), lay(gout_all)],
        out_specs=row(d),
        out_shape=jax.ShapeDtypeStruct((s, d), F32),
        compiler_params=_params(1),
        name="ffn",
    )(x, gin_all, w1_all, w2_all, gout_all)


def _regroup_w_in(w_in):
    gate_lo, gate_hi = _C_GATE, _C_GATE + 3 * NSA_HEADS
    pad = jnp.zeros(w_in.shape[:2] + (LANES - 3 * NSA_HEADS,), w_in.dtype)
    return jnp.concatenate([w_in[..., :gate_lo], w_in[..., gate_lo:gate_hi], pad, w_in[..., gate_hi:]],
                           axis=-1).astype(MXU_DTYPE)


def kernel(x, positions, norm_g, w_in, cmp_pe, cmp_w1, cmp_w2, w_nsa_o, w_sb_o, w_out, w_ff1, w_ff2):
    b, s, d = x.shape
    depth = w_in.shape[0]
    ncp, ns = s // CMP_STRIDE, s // SEL_BLOCK
    nsp = -(-ns // LANES) * LANES
    half_w = CMP_STRIDE * HEAD_DIM

    w_in_r = _regroup_w_in(w_in)
    pe = cmp_pe.reshape(depth, 2, 2, half_w)
    w1 = cmp_w1.astype(MXU_DTYPE)
    w2 = cmp_w2.astype(MXU_DTYPE)
    wn, ws, wo = w_nsa_o.astype(MXU_DTYPE), w_sb_o.astype(MXU_DTYPE), w_out.astype(MXU_DTYPE)
    wf1, wf2 = w_ff1.astype(MXU_DTYPE), w_ff2.astype(MXU_DTYPE)
    g_pre, g_mix, g_ffn_in, g_ffn_out = (norm_g[:, n][:, None, :] for n in range(4))

    dim = jnp.arange(LANES) % HEAD_DIM
    half = ROT_DIM // 2
    inv_freq = jnp.power(ROPE_THETA, (dim % half).astype(F32) * (-2.0 / ROT_DIM))
    invf = jnp.where(dim < ROT_DIM, inv_freq, 0.0)[None, :].astype(F32)
    c_start = CMP_STRIDE * jnp.arange(ncp)[None, :]
    s_start = SEL_BLOCK * jnp.arange(nsp)[:, None]
    ovlt = ((c_start < s_start + SEL_BLOCK) & (c_start + CMP_BLOCK > s_start) & (s_start < s)).astype(MXU_DTYPE)

    outs = []
    for bi in range(b):
        xb = x[bi]
        pos = positions[bi][:, None]
        for layer in range(depth):
            (qpt, qrt, kc, vc, ks, vst, kw, vwt, gates_t, sbq, sbkt, sbv, mg) = _inproj(
                xb, g_pre, pos, invf, w_in_r, layer)
            kcmp, vcmpt = _compress(kc.reshape(ncp, half_w), vc.reshape(ncp, half_w), pe, w1, w2, layer)
            nsa_o = _nsa(qpt, qrt, gates_t, kcmp, vcmpt, ovlt, ks, vst, kw, vwt)
            sb_o = _sb(sbq, sbkt, sbv)
            xb = _mix(xb, nsa_o, sb_o, mg, wn, ws, wo, g_mix, layer)
            xb = _ffn(xb, g_ffn_in, wf1, wf2, g_ffn_out, layer)
        outs.append(xb)
    return jnp.stack(outs, axis=0)
```

```python
import functools

import jax
import jax.numpy as jnp
from jax import lax
from jax.experimental import pallas as pl
from jax.experimental.pallas import tpu as pltpu

F32 = jnp.float32
MXU_DTYPE = jnp.bfloat16

HEAD_DIM = 64
NSA_HEADS = 8
SB_HEADS = 4
SB_HEAD_DIM = 128
ROPE_THETA = 500000.0
ROT_DIM = HEAD_DIM // 4
CMP_BLOCK = 32
CMP_STRIDE = 16
SEL_BLOCK = 64
SEL_TOP_N = 8
WINDOW = 512
RMS_EPS = 1e-6
NSA_W = NSA_HEADS * HEAD_DIM
SB_W = SB_HEADS * SB_HEAD_DIM
LANES = 128
VT_ROWS = HEAD_DIM + 16
MASKED = -32768.0
LOG2_E = 1.4426950408889634
VMEM_LIMIT = 56 * 1024 * 1024

ROW_TILE = 512
NSA_Q_BLOCK = 128
NSA_KEY_CHUNK = 512
SEL_CODE_BLOCKS = HEAD_DIM
CMP_VARIANTS = 4
SB_BLOCK = 512
SB_KEY_CHUNK = 256
SB_EXHAUSTED_LOG2 = 1100.0


def _dot(a, b):
    return jnp.dot(a, b, preferred_element_type=F32)


def _rms(x, g):
    return x * lax.rsqrt(jnp.mean(x * x, axis=-1, keepdims=True) + RMS_EPS) * g


def _params(n_grid_dims):
    return pltpu.CompilerParams(dimension_semantics=("arbitrary",) * n_grid_dims,
                                vmem_limit_bytes=VMEM_LIMIT)


def _resident(block_shape, index_map):
    return pl.BlockSpec(block_shape, index_map, pipeline_mode=pl.Buffered(1))


_C_KV = NSA_W
_C_GATE = _C_KV + 6 * HEAD_DIM
_C_SB = _C_GATE + LANES
_C_MERGE = _C_SB + 3 * SB_W


def _inproj_kernel(x_ref, g_ref, pos_ref, invf_ref, w_ref,
                   qpt_ref, qrt_ref, kc_ref, vc_ref, ks_ref, vst_ref, kw_ref, vwt_ref,
                   gate_ref, sbq_ref, sbk_ref, sbv_ref, mg_ref):
    t, d_model = x_ref.shape
    hb = _rms(x_ref[...], g_ref[...]).astype(MXU_DTYPE)

    ang = pos_ref[...].astype(F32) * invf_ref[...]
    cos, sin = jnp.cos(ang), jnp.sin(ang)
    lane = lax.broadcasted_iota(jnp.int32, (1, LANES), 1)
    dim = lane % HEAD_DIM
    half = ROT_DIM // 2
    sin_up = jnp.where((dim >= half) & (dim < ROT_DIM), sin, 0.0)
    sin_dn = jnp.where(dim < half, -sin, 0.0)
    low = lane < HEAD_DIM

    def rope(xg):
        return xg * cos + pltpu.roll(xg, half, 1) * sin_up + pltpu.roll(xg, LANES - half, 1) * sin_dn

    pa = _dot(hb, w_ref[:, 0:_C_GATE])
    scale = LOG2_E * HEAD_DIM ** -0.5
    for j in range(NSA_W // LANES):
        qg = pa[:, j * LANES:(j + 1) * LANES]
        qpt_ref[j * LANES:(j + 1) * LANES, :] = (qg * scale).T.astype(qpt_ref.dtype)
        qrt_ref[j * LANES:(j + 1) * LANES, :] = (rope(qg) * scale).T.astype(qrt_ref.dtype)
    kc_ref[...] = pa[:, _C_KV:_C_KV + HEAD_DIM]
    vc_ref[...] = pa[:, _C_KV + HEAD_DIM:_C_KV + 2 * HEAD_DIM]

    row = pl.program_id(0) * t + lax.broadcasted_iota(jnp.int32, (t, 1), 0)
    code = jnp.where(lane - HEAD_DIM == (row // SEL_BLOCK) % SEL_CODE_BLOCKS, 1.0, 0.0)
    ksg = pa[:, _C_KV + 2 * HEAD_DIM:_C_KV + 4 * HEAD_DIM]
    ks_ref[...] = jnp.where(low, rope(ksg), code).astype(ks_ref.dtype)
    vst_ref[0] = jnp.where(low, pltpu.roll(ksg, HEAD_DIM, 1), 1.0).T[0:VT_ROWS].astype(vst_ref.dtype)
    kwg = pa[:, _C_KV + 4 * HEAD_DIM:_C_KV + 6 * HEAD_DIM]
    kw_ref[...] = jnp.where(low, rope(kwg), 0.0).astype(kw_ref.dtype)
    vw_t = jnp.where(low, pltpu.roll(kwg, HEAD_DIM, 1), 1.0).T[0:VT_ROWS]
    for c in range(vwt_ref.shape[0]):
        vwt_ref[c] = vw_t[:, c * vwt_ref.shape[2]:(c + 1) * vwt_ref.shape[2]].astype(vwt_ref.dtype)

    gate_ref[...] = jax.nn.sigmoid(_dot(hb, w_ref[:, _C_GATE:_C_SB])).T
    sb = _dot(hb, w_ref[:, _C_SB:_C_MERGE])
    sbq_ref[...] = (sb[:, 0:SB_W] * (LOG2_E * SB_HEAD_DIM ** -0.5)).astype(sbq_ref.dtype)
    sbk_t = sb[:, SB_W:2 * SB_W].T
    for c in range(sbk_ref.shape[0]):
        sbk_ref[c] = sbk_t[:, c * sbk_ref.shape[2]:(c + 1) * sbk_ref.shape[2]].astype(sbk_ref.dtype)
    sbv_ref[...] = sb[:, 2 * SB_W:3 * SB_W].astype(sbv_ref.dtype)
    for c in range(2):
        mg_ref[:, c * d_model:(c + 1) * d_model] = jax.nn.sigmoid(
            _dot(hb, w_ref[:, _C_MERGE + c * d_model:_C_MERGE + (c + 1) * d_model])).astype(mg_ref.dtype)


def _inproj(x, g, pos, invf, w_all, layer):
    s, d = x.shape
    t = min(ROW_TILE, s)
    wcols = w_all.shape[2]
    sck = min(SB_KEY_CHUNK, s)
    wck = min(NSA_Q_BLOCK, s)
    row = lambda n: pl.BlockSpec((t, n), lambda i: (i, 0))
    col = lambda n: pl.BlockSpec((n, t), lambda i: (0, i))
    slab = lambda n, ck: pl.BlockSpec((t // ck, n, ck), lambda i: (i, 0, 0))
    sds = jax.ShapeDtypeStruct
    out_shape = [
        sds((NSA_W, s), MXU_DTYPE), sds((NSA_W, s), MXU_DTYPE),
        sds((s, HEAD_DIM), F32), sds((s, HEAD_DIM), F32),
        sds((s, LANES), MXU_DTYPE), sds((s // t, VT_ROWS, t), MXU_DTYPE),
        sds((s, LANES), MXU_DTYPE), sds((s // wck, VT_ROWS, wck), MXU_DTYPE),
        sds((LANES, s), F32),
        sds((s, SB_W), MXU_DTYPE), sds((s // sck, SB_W, sck), MXU_DTYPE), sds((s, SB_W), MXU_DTYPE),
        sds((s, 2 * d), MXU_DTYPE),
    ]
    out_specs = [col(NSA_W), col(NSA_W), row(HEAD_DIM), row(HEAD_DIM),
                 row(LANES), slab(VT_ROWS, t), row(LANES), slab(VT_ROWS, wck),
                 col(LANES), row(SB_W), slab(SB_W, sck), row(SB_W), row(2 * d)]
    return pl.pallas_call(
        _inproj_kernel,
        grid=(s // t,),
        in_specs=[row(d),
                  _resident((None, 1, d), lambda i: (layer, 0, 0)),
                  row(1),
                  _resident((1, LANES), lambda i: (0, 0)),
                  _resident((None, d, wcols), lambda i: (layer, 0, 0))],
        out_specs=out_specs,
        out_shape=out_shape,
        compiler_params=_params(1),
        name="inproj",
    )(x, g, pos, invf, w_all)


def _compress_kernel(kc_ref, vc_ref, pe_ref, w1_ref, w2_ref, kcmp_ref, vcmpt_ref):
    nr, half_w = kc_ref.shape
    outs = []
    for kv, r_ref in enumerate((kc_ref, vc_ref)):
        r = r_ref[...]
        ha = _dot((r + pe_ref[kv, 0:1, :]).astype(MXU_DTYPE), w1_ref[kv, 0:half_w, :])
        hb = _dot((r + pe_ref[kv, 1:2, :]).astype(MXU_DTYPE), w1_ref[kv, half_w:2 * half_w, :])
        hid = ha + pltpu.roll(hb, nr - 1, 0)
        outs.append(_dot(jax.nn.gelu(hid).astype(MXU_DTYPE), w2_ref[kv]))
    pad = jnp.zeros_like(outs[0])
    kcmp_ref[...] = jnp.concatenate([outs[0], pad], axis=1).astype(kcmp_ref.dtype)
    vcmpt_ref[...] = jnp.concatenate([outs[1], pad], axis=1).T[0:HEAD_DIM].astype(vcmpt_ref.dtype)


def _compress(kc, vc, pe_all, w1_all, w2_all, layer):
    nr, half_w = kc.shape
    hidden = w1_all.shape[3]
    full = lambda shape: pl.BlockSpec(shape, lambda i: (0,) * len(shape))
    return pl.pallas_call(
        _compress_kernel,
        grid=(1,),
        in_specs=[full((nr, half_w)), full((nr, half_w)),
                  pl.BlockSpec((None, 2, 2, half_w), lambda i: (layer, 0, 0, 0)),
                  pl.BlockSpec((None, 2, 2 * half_w, hidden), lambda i: (layer, 0, 0, 0)),
                  pl.BlockSpec((None, 2, hidden, HEAD_DIM), lambda i: (layer, 0, 0, 0))],
        out_specs=[full((nr, LANES)), full((HEAD_DIM, nr))],
        out_shape=[jax.ShapeDtypeStruct((nr, LANES), MXU_DTYPE), jax.ShapeDtypeStruct((HEAD_DIM, nr), MXU_DTYPE)],
        compiler_params=_params(1),
        name="compress",
    )(kc, vc, pe_all, w1_all, w2_all)


def _nsa_kernel(qpt_ref, qrt_ref, gate_ref, kcmp_ref, vcmpt_ref, ovlt_ref, ks_ref, vst_ref, kw_ref, vwt_ref,
                out_ref, qaug_scr, mask_scr, m_scr, acc_scr, sa_scr, sb_scr, ta_scr, tb_scr, ocmp_scr, owin_scr, imp_scr, *, n_sel):
    tq = qpt_ref.shape[1]
    seq = ks_ref.shape[0]
    ncp = kcmp_ref.shape[0]
    nsp = ovlt_ref.shape[0]
    ck = vst_ref.shape[2]
    wck = vwt_ref.shape[2]
    nh = NSA_HEADS
    q0 = pl.program_id(0) * tq
    t = q0 + lax.broadcasted_iota(jnp.int32, (1, tq), 1)
    cols = [slice(h * tq, (h + 1) * tq) for h in range(nh)]

    def aug(qt_ref, h, tail):
        return jnp.concatenate([qt_ref[h * HEAD_DIM:(h + 1) * HEAD_DIM, :], tail], axis=0)

    zeros_tail = jnp.zeros((HEAD_DIM, tq), MXU_DTYPE)
    def cmp_branch(rows):
        qp_aug = jnp.concatenate([aug(qpt_ref, h, zeros_tail) for h in range(nh)], axis=1)
        sc_all = _dot(kcmp_ref[0:rows, :], qp_aug)
        cmp_last = CMP_STRIDE * lax.broadcasted_iota(jnp.int32, (rows, 1), 0) + (CMP_BLOCK - 1)
        vis_c = cmp_last <= t
        psum = jnp.zeros((rows, tq), F32)
        p_cmp = []
        for h in range(nh):
            sc = jnp.where(vis_c, sc_all[:, cols[h]], -1e30)
            top = jnp.max(sc, axis=0, keepdims=True)
            e = jnp.exp2(sc - jnp.where(top > -1e29, top, 0.0))
            den = jnp.sum(e, axis=0, keepdims=True)
            p = e * (1.0 / jnp.where(den > 0.0, den, 1.0))
            psum = psum + p
            p_cmp.append(p.astype(MXU_DTYPE))
        o_cmp = _dot(vcmpt_ref[:, 0:rows], jnp.concatenate(p_cmp, axis=1))
        p_hi = psum.astype(MXU_DTYPE)
        p_lo = (psum - p_hi.astype(F32)).astype(MXU_DTYPE)
        imp_scr[...] = _dot(ovlt_ref[:, 0:rows], p_hi) + _dot(ovlt_ref[:, 0:rows], p_lo)
        ocmp_scr[...] = o_cmp

    n_var = CMP_VARIANTS if ncp % (CMP_VARIANTS * LANES) == 0 else 1
    step = ncp // n_var
    n_vis = (q0 + tq - CMP_BLOCK) // CMP_STRIDE + 1
    variant = jnp.clip((n_vis + step - 1) // step - 1, 0, n_var - 1)
    for v in range(n_var):
        pl.when(variant == v)(functools.partial(cmp_branch, (v + 1) * step))
    imp = imp_scr[...]

    qr_aug = jnp.concatenate([aug(qrt_ref, h, zeros_tail) for h in range(nh)], axis=1)
    span = min(WINDOW + tq, seq)
    start = pl.multiple_of(jnp.maximum(q0 + tq - span, 0), wck)
    kpos_w = start + lax.broadcasted_iota(jnp.int32, (span, 1), 0)
    vis_w = (kpos_w <= t) & (kpos_w > t - WINDOW)
    sw_all = _dot(kw_ref[pl.ds(start, span), :], qr_aug)
    p_win = []
    for h in range(nh):
        sw = jnp.where(vis_w, sw_all[:, cols[h]], -1e30)
        p_win.append(jnp.exp2(sw - jnp.max(sw, axis=0, keepdims=True)).astype(MXU_DTYPE))
    p_win = jnp.concatenate(p_win, axis=1)
    ow = jnp.zeros((VT_ROWS, nh * tq), F32)
    for j in range(span // wck):
        ow = ow + _dot(vwt_ref[start // wck + j], p_win[j * wck:(j + 1) * wck, :])
    owin_scr[...] = ow[0:HEAD_DIM] * (1.0 / ow[HEAD_DIM:HEAD_DIM + 1])

    blk = lax.broadcasted_iota(jnp.int32, (nsp, 1), 0)
    cur = t // SEL_BLOCK
    valid = blk <= cur
    forced = (blk == 0) | (blk == cur) | (blk == cur - 1)
    n_forced = 3
    assert n_sel >= n_forced
    score = jnp.where(valid, jnp.where(forced, -jnp.inf, imp), -1.0)
    chosen = jnp.where(forced & valid, 1.0, 0.0)
    for _ in range(n_sel - n_forced):
        best = jnp.max(score, axis=0, keepdims=True)
        idx = jnp.min(jnp.where(score == best, blk, nsp), axis=0, keepdims=True)
        hit = blk == idx
        chosen = jnp.where(hit, 1.0, chosen)
        score = jnp.where(hit, -jnp.inf, score)
    mask_scr[0:nsp, :] = ((chosen - 1.0) * (-MASKED)).astype(mask_scr.dtype)
    mask_scr[nsp:nsp + SEL_CODE_BLOCKS, :] = jnp.full((SEL_CODE_BLOCKS, tq), MASKED, mask_scr.dtype)

    for h in range(nh):
        qaug_scr[0:HEAD_DIM, cols[h]] = qrt_ref[h * HEAD_DIM:(h + 1) * HEAD_DIM, :]
    m_scr[...] = jnp.full(m_scr.shape, -1e30, F32)
    acc_scr[...] = jnp.zeros(acc_scr.shape, F32)

    c_diag = q0 // ck

    def sel_scores(c, s_ref, top_ref, limit):
        k0 = pl.multiple_of(jnp.minimum(c, c_diag) * ck, ck)
        code0 = jnp.where(c < limit, (k0 // (SEL_BLOCK * SEL_CODE_BLOCKS)) * SEL_CODE_BLOCKS, nsp)
        mrows = mask_scr[pl.ds(pl.multiple_of(code0, SEL_CODE_BLOCKS), SEL_CODE_BLOCKS), :]
        for h in range(nh):
            qaug_scr[HEAD_DIM:2 * HEAD_DIM, cols[h]] = mrows
        s_all = _dot(ks_ref[pl.ds(k0, ck), :], qaug_scr[...])
        s_ref[...] = s_all
        top_ref[...] = jnp.max(s_all, axis=0, keepdims=True)

    def sel_update(c, s_ref, top_ref, causal):
        m_old = m_scr[...]
        if causal:
            kpos = c * ck + lax.broadcasted_iota(jnp.int32, (ck, 1), 0)
            bias = jnp.where(kpos <= t, 0.0, MASKED)
            tops = [jnp.max(s_ref[:, cols[h]] + bias, axis=0, keepdims=True) for h in range(nh)]
            m_new = jnp.maximum(m_old, jnp.concatenate(tops, axis=1))
        else:
            m_new = jnp.maximum(m_old, top_ref[...])
        p_all = []
        for h in range(nh):
            s = s_ref[:, cols[h]]
            if causal:
                s = s + bias
            p_all.append(jnp.exp2(s - m_new[:, cols[h]]).astype(MXU_DTYPE))
        pv = _dot(vst_ref[jnp.minimum(c, c_diag)], jnp.concatenate(p_all, axis=1))
        acc_scr[...] = jnp.exp2(m_old - m_new) * acc_scr[...] + pv
        m_scr[...] = m_new

    def sel_pair(c):
        sel_scores(c + 1, sa_scr, ta_scr, c_diag)
        sel_update(c, sb_scr, tb_scr, False)
        sel_scores(c + 2, sb_scr, tb_scr, c_diag)
        sel_update(c + 1, sa_scr, ta_scr, False)

    def sel_quad(j, carry):
        sel_pair(4 * j)
        sel_pair(4 * j + 2)
        return carry

    sel_scores(c_diag, sa_scr, ta_scr, c_diag + 1)
    sel_scores(0, sb_scr, tb_scr, c_diag)
    sel_update(c_diag, sa_scr, ta_scr, True)
    n_quads = c_diag // 4
    lax.fori_loop(0, n_quads, sel_quad, 0)
    lax.fori_loop(0, (c_diag - 4 * n_quads + 1) // 2, lambda j, carry: (sel_pair(4 * n_quads + 2 * j), carry)[1], 0)
    acc = acc_scr[...]
    o_sel = acc[0:HEAD_DIM] * (1.0 / acc[HEAD_DIM:HEAD_DIM + 1])

    o_cmp = ocmp_scr[...]
    o_win = owin_scr[...]
    gates = gate_ref[...]
    merged = []
    for h in range(nh):
        merged.append(gates[3 * h:3 * h + 1, :] * o_cmp[:, cols[h]]
                      + gates[3 * h + 1:3 * h + 2, :] * o_sel[:, cols[h]]
                      + gates[3 * h + 2:3 * h + 3, :] * o_win[:, cols[h]])
    out_ref[...] = jnp.concatenate(merged, axis=0).T.astype(out_ref.dtype)


def _nsa(qpt, qrt, gates_t, kcmp, vcmpt, ovlt, ks, vst, kw, vwt):
    s = ks.shape[0]
    tq = min(NSA_Q_BLOCK, s)
    nsp = ovlt.shape[0]
    col = lambda n: pl.BlockSpec((n, tq), lambda i: (0, i))
    res = lambda a: _resident(a.shape, lambda i: (0,) * a.ndim)
    return pl.pallas_call(
        functools.partial(_nsa_kernel, n_sel=min(SEL_TOP_N, s // SEL_BLOCK)),
        grid=(s // tq,),
        in_specs=[col(NSA_W), col(NSA_W), col(LANES), res(kcmp), res(vcmpt), res(ovlt),
                  res(ks), res(vst), res(kw), res(vwt)],
        out_specs=pl.BlockSpec((tq, NSA_W), lambda i: (i, 0)),
        out_shape=jax.ShapeDtypeStruct((s, NSA_W), MXU_DTYPE),
        scratch_shapes=[pltpu.VMEM((LANES, NSA_HEADS * tq), MXU_DTYPE),
                        pltpu.VMEM((nsp + SEL_CODE_BLOCKS, tq), MXU_DTYPE),
                        pltpu.VMEM((1, NSA_HEADS * tq), F32), pltpu.VMEM((VT_ROWS, NSA_HEADS * tq), F32),
                        pltpu.VMEM((vst.shape[2], NSA_HEADS * tq), F32),
                        pltpu.VMEM((vst.shape[2], NSA_HEADS * tq), F32),
                        pltpu.VMEM((1, NSA_HEADS * tq), F32), pltpu.VMEM((1, NSA_HEADS * tq), F32),
                        pltpu.VMEM((HEAD_DIM, NSA_HEADS * tq), F32), pltpu.VMEM((HEAD_DIM, NSA_HEADS * tq), F32),
                        pltpu.VMEM((nsp, tq), F32)],
        compiler_params=_params(1),
        name="nsa",
    )(qpt, qrt, gates_t, kcmp, vcmpt, ovlt, ks, vst, kw, vwt)


def _softplus2(z2):
    neg_abs = lax.bitcast_convert_type(lax.bitcast_convert_type(z2, jnp.uint32) | jnp.uint32(0x80000000), F32)
    return jnp.maximum(z2, 0.0) + jnp.log2(1.0 + jnp.exp2(neg_abs))


def _sb_kernel(q_ref, k_ref, v_ref, out_ref, acc_scr, run_scr, za_scr, zb_scr):
    assert (q_ref.shape[0] // k_ref.shape[2]) % 2 == 0
    tb = q_ref.shape[0]
    ck = k_ref.shape[2]
    nsub = tb // ck
    i = pl.program_id(0)
    r = lax.broadcasted_iota(jnp.int32, (ck, ck), 0)
    c = lax.broadcasted_iota(jnp.int32, (ck, ck), 1)
    tri = jnp.where(r >= c, 1.0, 0.0).astype(MXU_DTYPE)
    before = c < r
    heads = [slice(h * SB_HEAD_DIM, (h + 1) * SB_HEAD_DIM) for h in range(SB_HEADS)]

    def logits(h, rows, chunk):
        return _dot(q_ref[rows, heads[h]], k_ref[chunk, heads[h], :])

    def step(h, rows, chunk, diag, first, z=None):
        hs = heads[h]
        if z is None:
            z = logits(h, rows, chunk)
        sp = _softplus2(z)
        if diag:
            sp = jnp.where(before, sp, 0.0)
        cs = _dot(sp.astype(MXU_DTYPE), tri)
        own = jnp.minimum(z - cs, 0.0)
        if first:
            a = jnp.exp2(own)
            run_scr[h, rows] = cs[:, 0:1]
        else:
            run = run_scr[h, rows]
            a = jnp.exp2(own - run)
            run_scr[h, rows] = run + cs[:, 0:1]
        if diag:
            a = jnp.where(before, a, 0.0)
        pv = _dot(a.astype(MXU_DTYPE), v_ref[pl.ds(pl.multiple_of(chunk * ck, ck), ck), hs])
        if first:
            acc_scr[h, rows] = pv
        else:
            acc_scr[h, rows] += pv

    for g in range(nsub):
        rows = slice(g * ck, (g + 1) * ck)
        for h in range(SB_HEADS):
            step(h, rows, i * nsub + g, True, True)
        for back in range(g):
            for h in range(SB_HEADS):
                step(h, rows, i * nsub + g - 1 - back, False, False)

    everything = slice(0, tb)

    def lookahead(chunk, z_ref):
        for h in range(SB_HEADS):
            z_ref[h] = logits(h, everything, jnp.maximum(chunk, 0))

    def pair(state):
        j, _ = state
        cur = i * nsub - 1 - 2 * j
        lookahead(cur - 1, zb_scr)
        for h in range(SB_HEADS):
            step(h, everything, cur, False, False, za_scr[h])
        lookahead(cur - 2, za_scr)
        for h in range(SB_HEADS):
            step(h, everything, cur - 1, False, False, zb_scr[h])
        least = functools.reduce(jnp.minimum, [run_scr[h] for h in range(SB_HEADS)])
        return j + 1, (jnp.min(least) >= SB_EXHAUSTED_LOG2).astype(jnp.int32)

    lookahead(i * nsub - 1, za_scr)
    lax.while_loop(lambda state: (state[0] < (i * nsub) // 2) & (state[1] == 0), pair,
                   (jnp.int32(0), jnp.int32(0)))
    for h, hs in enumerate(heads):
        out_ref[:, hs] = acc_scr[h].astype(out_ref.dtype)


def _sb(q, kt, v):
    s = q.shape[0]
    tb = min(SB_BLOCK, s)
    return pl.pallas_call(
        _sb_kernel,
        grid=(s // tb,),
        in_specs=[pl.BlockSpec((tb, SB_W), lambda i: (i, 0)),
                  _resident(kt.shape, lambda i: (0, 0, 0)),
                  _resident((s, SB_W), lambda i: (0, 0))],
        out_specs=pl.BlockSpec((tb, SB_W), lambda i: (i, 0)),
        out_shape=jax.ShapeDtypeStruct((s, SB_W), MXU_DTYPE),
        scratch_shapes=[pltpu.VMEM((SB_HEADS, tb, SB_HEAD_DIM), F32), pltpu.VMEM((SB_HEADS, tb, 1), F32),
                        pltpu.VMEM((SB_HEADS, tb, kt.shape[2]), F32), pltpu.VMEM((SB_HEADS, tb, kt.shape[2]), F32)],
        compiler_params=_params(1),
        name="sb",
    )(q, kt, v)


def _mix_kernel(x_ref, nsa_ref, sb_ref, mg_ref, wn_ref, ws_ref, wo_ref, g_ref, out_ref):
    d = x_ref.shape[1]
    y_nsa = _dot(nsa_ref[...], wn_ref[...])
    y_sb = _dot(sb_ref[...], ws_ref[...])
    merged = mg_ref[:, 0:d].astype(F32) * y_nsa + mg_ref[:, d:2 * d].astype(F32) * y_sb
    mixed = _dot(merged.astype(MXU_DTYPE), wo_ref[...])
    out_ref[...] = x_ref[...] + _rms(mixed, g_ref[...])


def _mix(x, nsa_o, sb_o, mg, wn_all, ws_all, wo_all, g_all, layer):
    s, d = x.shape
    t = min(ROW_TILE, s)
    row = lambda n: pl.BlockSpec((t, n), lambda i: (i, 0))
    lay = lambda a: _resident((None,) + a.shape[1:], lambda i: (layer, 0, 0))
    return pl.pallas_call(
        _mix_kernel,
        grid=(s // t,),
        in_specs=[row(d), row(NSA_W), row(SB_W), row(2 * d), lay(wn_all), lay(ws_all), lay(wo_all), lay(g_all)],
        out_specs=row(d),
        out_shape=jax.ShapeDtypeStruct((s, d), F32),
        compiler_params=_params(1),
        name="mix",
    )(x, nsa_o, sb_o, mg, wn_all, ws_all, wo_all, g_all)


def _ffn_kernel(x_ref, gin_ref, w1_ref, w2_ref, gout_ref, out_ref):
    x = x_ref[...]
    d = x.shape[1]
    hb = _rms(x, gin_ref[...]).astype(MXU_DTYPE)
    ff = jnp.zeros_like(x)
    for c in range(w1_ref.shape[1] // d):
        up = _dot(hb, w1_ref[:, c * d:(c + 1) * d])
        ff = ff + _dot(jnp.square(jnp.maximum(up, 0.0)).astype(MXU_DTYPE), w2_ref[c * d:(c + 1) * d, :])
    out_ref[...] = x + _rms(ff, gout_ref[...])


def _ffn(x, gin_all, w1_all, w2_all, gout_all, layer):
    s, d = x.shape
    t = min(ROW_TILE, s)
    row = lambda n: pl.BlockSpec((t, n), lambda i: (i, 0))
    lay = lambda a: _resident((None,) + a.shape[1:], lambda i: (layer, 0, 0))
    return pl.pallas_call(
        _ffn_kernel,
        grid=(s // t,),
        in_specs=[row(d), lay(gin_all), lay(w1_all), lay(w2_all), lay(gout_all)],
        out_specs=row(d),
        out_shape=jax.ShapeDtypeStruct((s, d), F32),
        compiler_params=_params(1),
        name="ffn",
    )(x, gin_all, w1_all, w2_all, gout_all)


def _regroup_w_in(w_in):
    gate_lo, gate_hi = _C_GATE, _C_GATE + 3 * NSA_HEADS
    pad = jnp.zeros(w_in.shape[:2] + (LANES - 3 * NSA_HEADS,), w_in.dtype)
    return jnp.concatenate([w_in[..., :gate_lo], w_in[..., gate_lo:gate_hi], pad, w_in[..., gate_hi:]],
                           axis=-1).astype(MXU_DTYPE)


def kernel(x, positions, norm_g, w_in, cmp_pe, cmp_w1, cmp_w2, w_nsa_o, w_sb_o, w_out, w_ff1, w_ff2):
    b, s, d = x.shape
    depth = w_in.shape[0]
    ncp, ns = s // CMP_STRIDE, s // SEL_BLOCK
    nsp = -(-ns // LANES) * LANES
    half_w = CMP_STRIDE * HEAD_DIM

    w_in_r = _regroup_w_in(w_in)
    pe = cmp_pe.reshape(depth, 2, 2, half_w)
    w1 = cmp_w1.astype(MXU_DTYPE)
    w2 = cmp_w2.astype(MXU_DTYPE)
    wn, ws, wo = w_nsa_o.astype(MXU_DTYPE), w_sb_o.astype(MXU_DTYPE), w_out.astype(MXU_DTYPE)
    wf1, wf2 = w_ff1.astype(MXU_DTYPE), w_ff2.astype(MXU_DTYPE)
    g_pre, g_mix, g_ffn_in, g_ffn_out = (norm_g[:, n][:, None, :] for n in range(4))

    dim = jnp.arange(LANES) % HEAD_DIM
    half = ROT_DIM // 2
    inv_freq = jnp.power(ROPE_THETA, (dim % half).astype(F32) * (-2.0 / ROT_DIM))
    invf = jnp.where(dim < ROT_DIM, inv_freq, 0.0)[None, :].astype(F32)
    c_start = CMP_STRIDE * jnp.arange(ncp)[None, :]
    s_start = SEL_BLOCK * jnp.arange(nsp)[:, None]
    ovlt = ((c_start < s_start + SEL_BLOCK) & (c_start + CMP_BLOCK > s_start) & (s_start < s)).astype(MXU_DTYPE)

    outs = []
    for bi in range(b):
        xb = x[bi]
        pos = positions[bi][:, None]
        for layer in range(depth):
            (qpt, qrt, kc, vc, ks, vst, kw, vwt, gates_t, sbq, sbkt, sbv, mg) = _inproj(
                xb, g_pre, pos, invf, w_in_r, layer)
            kcmp, vcmpt = _compress(kc.reshape(ncp, half_w), vc.reshape(ncp, half_w), pe, w1, w2, layer)
            nsa_o = _nsa(qpt, qrt, gates_t, kcmp, vcmpt, ovlt, ks, vst, kw, vwt)
            sb_o = _sb(sbq, sbkt, sbv)
            xb = _mix(xb, nsa_o, sb_o, mg, wn, ws, wo, g_mix, layer)
            xb = _ffn(xb, g_ffn_in, wf1, wf2, g_ffn_out, layer)
        outs.append(xb)
    return jnp.stack(outs, axis=0)
```

```python
import functools

import jax
import jax.numpy as jnp
from jax import lax
from jax.experimental import pallas as pl
from jax.experimental.pallas import tpu as pltpu

F32 = jnp.float32
MXU_DTYPE = jnp.bfloat16

HEAD_DIM = 64
NSA_HEADS = 8
SB_HEADS = 4
SB_HEAD_DIM = 128
ROPE_THETA = 500000.0
ROT_DIM = HEAD_DIM // 4
CMP_BLOCK = 32
CMP_STRIDE = 16
SEL_BLOCK = 64
SEL_TOP_N = 8
WINDOW = 512
RMS_EPS = 1e-6
NSA_W = NSA_HEADS * HEAD_DIM
SB_W = SB_HEADS * SB_HEAD_DIM
LANES = 128
VT_ROWS = HEAD_DIM + 16
MASKED = -32768.0
LOG2_E = 1.4426950408889634
VMEM_LIMIT = 56 * 1024 * 1024

ROW_TILE = 512
NSA_Q_BLOCK = 128
NSA_KEY_CHUNK = 512
SEL_CODE_BLOCKS = HEAD_DIM
CMP_VARIANTS = 4
SB_BLOCK = 512
SB_KEY_CHUNK = 256
SB_EXHAUSTED_LOG2 = 1100.0


def _dot(a, b):
    return jnp.dot(a, b, preferred_element_type=F32)


def _rms(x, g):
    return x * lax.rsqrt(jnp.mean(x * x, axis=-1, keepdims=True) + RMS_EPS) * g


def _params(n_grid_dims):
    return pltpu.CompilerParams(dimension_semantics=("arbitrary",) * n_grid_dims,
                                vmem_limit_bytes=VMEM_LIMIT)


def _resident(block_shape, index_map):
    return pl.BlockSpec(block_shape, index_map, pipeline_mode=pl.Buffered(1))


_C_KV = NSA_W
_C_GATE = _C_KV + 6 * HEAD_DIM
_C_SB = _C_GATE + LANES
_C_MERGE = _C_SB + 3 * SB_W


def _inproj_kernel(x_ref, g_ref, pos_ref, invf_ref, w_ref,
                   qpt_ref, qrt_ref, kc_ref, vc_ref, ks_ref, vst_ref, kw_ref, vwt_ref,
                   gate_ref, sbq_ref, sbk_ref, sbv_ref, mg_ref):
    t, d_model = x_ref.shape
    hb = _rms(x_ref[...], g_ref[...]).astype(MXU_DTYPE)

    ang = pos_ref[...].astype(F32) * invf_ref[...]
    cos, sin = jnp.cos(ang), jnp.sin(ang)
    lane = lax.broadcasted_iota(jnp.int32, (1, LANES), 1)
    dim = lane % HEAD_DIM
    half = ROT_DIM // 2
    sin_up = jnp.where((dim >= half) & (dim < ROT_DIM), sin, 0.0)
    sin_dn = jnp.where(dim < half, -sin, 0.0)
    low = lane < HEAD_DIM

    def rope(xg):
        return xg * cos + pltpu.roll(xg, half, 1) * sin_up + pltpu.roll(xg, LANES - half, 1) * sin_dn

    pa = _dot(hb, w_ref[:, 0:_C_GATE])
    scale = LOG2_E * HEAD_DIM ** -0.5
    for j in range(NSA_W // LANES):
        qg = pa[:, j * LANES:(j + 1) * LANES]
        qpt_ref[j * LANES:(j + 1) * LANES, :] = (qg * scale).T.astype(qpt_ref.dtype)
        qrt_ref[j * LANES:(j + 1) * LANES, :] = (rope(qg) * scale).T.astype(qrt_ref.dtype)
    kc_ref[...] = pa[:, _C_KV:_C_KV + HEAD_DIM]
    vc_ref[...] = pa[:, _C_KV + HEAD_DIM:_C_KV + 2 * HEAD_DIM]

    row = pl.program_id(0) * t + lax.broadcasted_iota(jnp.int32, (t, 1), 0)
    code = jnp.where(lane - HEAD_DIM == (row // SEL_BLOCK) % SEL_CODE_BLOCKS, 1.0, 0.0)
    ksg = pa[:, _C_KV + 2 * HEAD_DIM:_C_KV + 4 * HEAD_DIM]
    ks_ref[...] = jnp.where(low, rope(ksg), code).astype(ks_ref.dtype)
    vst_ref[0] = jnp.where(low, pltpu.roll(ksg, HEAD_DIM, 1), 1.0).T[0:VT_ROWS].astype(vst_ref.dtype)
    kwg = pa[:, _C_KV + 4 * HEAD_DIM:_C_KV + 6 * HEAD_DIM]
    kw_ref[...] = jnp.where(low, rope(kwg), 0.0).astype(kw_ref.dtype)
    vw_t = jnp.where(low, pltpu.roll(kwg, HEAD_DIM, 1), 1.0).T[0:VT_ROWS]
    for c in range(vwt_ref.shape[0]):
        vwt_ref[c] = vw_t[:, c * vwt_ref.shape[2]:(c + 1) * vwt_ref.shape[2]].astype(vwt_ref.dtype)

    gate_ref[...] = jax.nn.sigmoid(_dot(hb, w_ref[:, _C_GATE:_C_SB])).T
    sb = _dot(hb, w_ref[:, _C_SB:_C_MERGE])
    sbq_ref[...] = (sb[:, 0:SB_W] * (LOG2_E * SB_HEAD_DIM ** -0.5)).astype(sbq_ref.dtype)
    sbk_t = sb[:, SB_W:2 * SB_W].T
    for c in range(sbk_ref.shape[0]):
        sbk_ref[c] = sbk_t[:, c * sbk_ref.shape[2]:(c + 1) * sbk_ref.shape[2]].astype(sbk_ref.dtype)
    sbv_ref[...] = sb[:, 2 * SB_W:3 * SB_W].astype(sbv_ref.dtype)
    for c in range(2):
        mg_ref[:, c * d_model:(c + 1) * d_model] = jax.nn.sigmoid(
            _dot(hb, w_ref[:, _C_MERGE + c * d_model:_C_MERGE + (c + 1) * d_model])).astype(mg_ref.dtype)


def _inproj(x, g, pos, invf, w_all, layer):
    s, d = x.shape
    t = min(ROW_TILE, s)
    wcols = w_all.shape[2]
    sck = min(SB_KEY_CHUNK, s)
    wck = min(NSA_Q_BLOCK, s)
    row = lambda n: pl.BlockSpec((t, n), lambda i: (i, 0))
    col = lambda n: pl.BlockSpec((n, t), lambda i: (0, i))
    slab = lambda n, ck: pl.BlockSpec((t // ck, n, ck), lambda i: (i, 0, 0))
    sds = jax.ShapeDtypeStruct
    out_shape = [
        sds((NSA_W, s), MXU_DTYPE), sds((NSA_W, s), MXU_DTYPE),
        sds((s, HEAD_DIM), F32), sds((s, HEAD_DIM), F32),
        sds((s, LANES), MXU_DTYPE), sds((s // t, VT_ROWS, t), MXU_DTYPE),
        sds((s, LANES), MXU_DTYPE), sds((s // wck, VT_ROWS, wck), MXU_DTYPE),
        sds((LANES, s), F32),
        sds((s, SB_W), MXU_DTYPE), sds((s // sck, SB_W, sck), MXU_DTYPE), sds((s, SB_W), MXU_DTYPE),
        sds((s, 2 * d), MXU_DTYPE),
    ]
    out_specs = [col(NSA_W), col(NSA_W), row(HEAD_DIM), row(HEAD_DIM),
                 row(LANES), slab(VT_ROWS, t), row(LANES), slab(VT_ROWS, wck),
                 col(LANES), row(SB_W), slab(SB_W, sck), row(SB_W), row(2 * d)]
    return pl.pallas_call(
        _inproj_kernel,
        grid=(s // t,),
        in_specs=[row(d),
                  _resident((None, 1, d), lambda i: (layer, 0, 0)),
                  row(1),
                  _resident((1, LANES), lambda i: (0, 0)),
                  _resident((None, d, wcols), lambda i: (layer, 0, 0))],
        out_specs=out_specs,
        out_shape=out_shape,
        compiler_params=_params(1),
        name="inproj",
    )(x, g, pos, invf, w_all)


def _compress_kernel(kc_ref, vc_ref, pe_ref, w1_ref, w2_ref, kcmp_ref, vcmpt_ref):
    nr, half_w = kc_ref.shape
    outs = []
    for kv, r_ref in enumerate((kc_ref, vc_ref)):
        r = r_ref[...]
        ha = _dot((r + pe_ref[kv, 0:1, :]).astype(MXU_DTYPE), w1_ref[kv, 0:half_w, :])
        hb = _dot((r + pe_ref[kv, 1:2, :]).astype(MXU_DTYPE), w1_ref[kv, half_w:2 * half_w, :])
        hid = ha + pltpu.roll(hb, nr - 1, 0)
        outs.append(_dot(jax.nn.gelu(hid).astype(MXU_DTYPE), w2_ref[kv]))
    pad = jnp.zeros_like(outs[0])
    kcmp_ref[...] = jnp.concatenate([outs[0], pad], axis=1).astype(kcmp_ref.dtype)
    vcmpt_ref[...] = jnp.concatenate([outs[1], pad], axis=1).T[0:HEAD_DIM].astype(vcmpt_ref.dtype)


def _compress(kc, vc, pe_all, w1_all, w2_all, layer):
    nr, half_w = kc.shape
    hidden = w1_all.shape[3]
    full = lambda shape: pl.BlockSpec(shape, lambda i: (0,) * len(shape))
    return pl.pallas_call(
        _compress_kernel,
        grid=(1,),
        in_specs=[full((nr, half_w)), full((nr, half_w)),
                  pl.BlockSpec((None, 2, 2, half_w), lambda i: (layer, 0, 0, 0)),
                  pl.BlockSpec((None, 2, 2 * half_w, hidden), lambda i: (layer, 0, 0, 0)),
                  pl.BlockSpec((None, 2, hidden, HEAD_DIM), lambda i: (layer, 0, 0, 0))],
        out_specs=[full((nr, LANES)), full((HEAD_DIM, nr))],
        out_shape=[jax.ShapeDtypeStruct((nr, LANES), MXU_DTYPE), jax.ShapeDtypeStruct((HEAD_DIM, nr), MXU_DTYPE)],
        compiler_params=_params(1),
        name="compress",
    )(kc, vc, pe_all, w1_all, w2_all)


def _nsa_kernel(qpt_ref, qrt_ref, gate_ref, kcmp_ref, vcmpt_ref, ovlt_ref, ks_ref, vst_ref, kw_ref, vwt_ref,
                out_ref, qaug_scr, mask_scr, m_scr, acc_scr, sa_scr, sb_scr, ta_scr, tb_scr, ocmp_scr, owin_scr, imp_scr, *, n_sel):
    tq = qpt_ref.shape[1]
    seq = ks_ref.shape[0]
    ncp = kcmp_ref.shape[0]
    nsp = ovlt_ref.shape[0]
    ck = vst_ref.shape[2]
    wck = vwt_ref.shape[2]
    nh = NSA_HEADS
    q0 = pl.program_id(0) * tq
    t = q0 + lax.broadcasted_iota(jnp.int32, (1, tq), 1)
    cols = [slice(h * tq, (h + 1) * tq) for h in range(nh)]

    def aug(qt_ref, h, tail):
        return jnp.concatenate([qt_ref[h * HEAD_DIM:(h + 1) * HEAD_DIM, :], tail], axis=0)

    zeros_tail = jnp.zeros((HEAD_DIM, tq), MXU_DTYPE)
    def cmp_branch(rows):
        qp_aug = jnp.concatenate([aug(qpt_ref, h, zeros_tail) for h in range(nh)], axis=1)
        sc_all = _dot(kcmp_ref[0:rows, :], qp_aug)
        cmp_last = CMP_STRIDE * lax.broadcasted_iota(jnp.int32, (rows, 1), 0) + (CMP_BLOCK - 1)
        vis_c = cmp_last <= t
        psum = jnp.zeros((rows, tq), F32)
        p_cmp = []
        for h in range(nh):
            sc = jnp.where(vis_c, sc_all[:, cols[h]], -1e30)
            top = jnp.max(sc, axis=0, keepdims=True)
            e = jnp.exp2(sc - jnp.where(top > -1e29, top, 0.0))
            den = jnp.sum(e, axis=0, keepdims=True)
            p = e * (1.0 / jnp.where(den > 0.0, den, 1.0))
            psum = psum + p
            p_cmp.append(p.astype(MXU_DTYPE))
        o_cmp = _dot(vcmpt_ref[:, 0:rows], jnp.concatenate(p_cmp, axis=1))
        p_hi = psum.astype(MXU_DTYPE)
        p_lo = (psum - p_hi.astype(F32)).astype(MXU_DTYPE)
        imp_scr[...] = _dot(ovlt_ref[:, 0:rows], p_hi) + _dot(ovlt_ref[:, 0:rows], p_lo)
        ocmp_scr[...] = o_cmp

    n_var = CMP_VARIANTS if ncp % (CMP_VARIANTS * LANES) == 0 else 1
    step = ncp // n_var
    n_vis = (q0 + tq - CMP_BLOCK) // CMP_STRIDE + 1
    variant = jnp.clip((n_vis + step - 1) // step - 1, 0, n_var - 1)
    for v in range(n_var):
        pl.when(variant == v)(functools.partial(cmp_branch, (v + 1) * step))
    imp = imp_scr[...]

    qr_aug = jnp.concatenate([aug(qrt_ref, h, zeros_tail) for h in range(nh)], axis=1)
    span = min(WINDOW + tq, seq)
    start = pl.multiple_of(jnp.maximum(q0 + tq - span, 0), wck)
    sw_all = _dot(kw_ref[pl.ds(start, span), :], qr_aug)

    def window_finish():
        kpos_w = start + lax.broadcasted_iota(jnp.int32, (span, 1), 0)
        vis_w = (kpos_w <= t) & (kpos_w > t - WINDOW)
        p_win = []
        for h in range(nh):
            sw = jnp.where(vis_w, sw_all[:, cols[h]], -1e30)
            p_win.append(jnp.exp2(sw - jnp.max(sw, axis=0, keepdims=True)).astype(MXU_DTYPE))
        p_win = jnp.concatenate(p_win, axis=1)
        ow = jnp.zeros((VT_ROWS, nh * tq), F32)
        for j in range(span // wck):
            ow = ow + _dot(vwt_ref[start // wck + j], p_win[j * wck:(j + 1) * wck, :])
        owin_scr[...] = ow[0:HEAD_DIM] * (1.0 / ow[HEAD_DIM:HEAD_DIM + 1])

    blk = lax.broadcasted_iota(jnp.int32, (nsp, 1), 0)
    cur = t // SEL_BLOCK
    valid = blk <= cur
    forced = (blk == 0) | (blk == cur) | (blk == cur - 1)
    n_forced = 3
    assert n_sel >= n_forced
    score = jnp.where(valid, jnp.where(forced, -jnp.inf, imp), -1.0)
    chosen = jnp.where(forced & valid, 1.0, 0.0)
    for _ in range(n_sel - n_forced):
        best = jnp.max(score, axis=0, keepdims=True)
        idx = jnp.min(jnp.where(score == best, blk, nsp), axis=0, keepdims=True)
        hit = blk == idx
        chosen = jnp.where(hit, 1.0, chosen)
        score = jnp.where(hit, -jnp.inf, score)
    mask_scr[0:nsp, :] = ((chosen - 1.0) * (-MASKED)).astype(mask_scr.dtype)
    mask_scr[nsp:nsp + SEL_CODE_BLOCKS, :] = jnp.full((SEL_CODE_BLOCKS, tq), MASKED, mask_scr.dtype)

    for h in range(nh):
        qaug_scr[0:HEAD_DIM, cols[h]] = qrt_ref[h * HEAD_DIM:(h + 1) * HEAD_DIM, :]
    m_scr[...] = jnp.full(m_scr.shape, -1e30, F32)
    acc_scr[...] = jnp.zeros(acc_scr.shape, F32)

    c_diag = q0 // ck

    def sel_scores(c, s_ref, top_ref, limit):
        k0 = pl.multiple_of(jnp.minimum(c, c_diag) * ck, ck)
        code0 = jnp.where(c < limit, (k0 // (SEL_BLOCK * SEL_CODE_BLOCKS)) * SEL_CODE_BLOCKS, nsp)
        mrows = mask_scr[pl.ds(pl.multiple_of(code0, SEL_CODE_BLOCKS), SEL_CODE_BLOCKS), :]
        for h in range(nh):
            qaug_scr[HEAD_DIM:2 * HEAD_DIM, cols[h]] = mrows
        s_all = _dot(ks_ref[pl.ds(k0, ck), :], qaug_scr[...])
        s_ref[...] = s_all
        top_ref[...] = jnp.max(s_all, axis=0, keepdims=True)

    def sel_update(c, s_ref, top_ref, causal):
        m_old = m_scr[...]
        if causal:
            kpos = c * ck + lax.broadcasted_iota(jnp.int32, (ck, 1), 0)
            bias = jnp.where(kpos <= t, 0.0, MASKED)
            tops = [jnp.max(s_ref[:, cols[h]] + bias, axis=0, keepdims=True) for h in range(nh)]
            m_new = jnp.maximum(m_old, jnp.concatenate(tops, axis=1))
        else:
            m_new = jnp.maximum(m_old, top_ref[...])
        p_all = []
        for h in range(nh):
            s = s_ref[:, cols[h]]
            if causal:
                s = s + bias
            p_all.append(jnp.exp2(s - m_new[:, cols[h]]).astype(MXU_DTYPE))
        pv = _dot(vst_ref[jnp.minimum(c, c_diag)], jnp.concatenate(p_all, axis=1))
        acc_scr[...] = jnp.exp2(m_old - m_new) * acc_scr[...] + pv
        m_scr[...] = m_new

    def sel_pair(c):
        sel_scores(c + 1, sa_scr, ta_scr, c_diag)
        sel_update(c, sb_scr, tb_scr, False)
        sel_scores(c + 2, sb_scr, tb_scr, c_diag)
        sel_update(c + 1, sa_scr, ta_scr, False)

    def sel_quad(j, carry):
        sel_pair(4 * j)
        sel_pair(4 * j + 2)
        return carry

    sel_scores(c_diag, sa_scr, ta_scr, c_diag + 1)
    sel_scores(0, sb_scr, tb_scr, c_diag)
    window_finish()
    sel_update(c_diag, sa_scr, ta_scr, True)
    n_quads = c_diag // 4
    lax.fori_loop(0, n_quads, sel_quad, 0)
    lax.fori_loop(0, (c_diag - 4 * n_quads + 1) // 2, lambda j, carry: (sel_pair(4 * n_quads + 2 * j), carry)[1], 0)
    acc = acc_scr[...]
    o_sel = acc[0:HEAD_DIM] * (1.0 / acc[HEAD_DIM:HEAD_DIM + 1])

    o_cmp = ocmp_scr[...]
    o_win = owin_scr[...]
    gates = gate_ref[...]
    merged = []
    for h in range(nh):
        merged.append(gates[3 * h:3 * h + 1, :] * o_cmp[:, cols[h]]
                      + gates[3 * h + 1:3 * h + 2, :] * o_sel[:, cols[h]]
                      + gates[3 * h + 2:3 * h + 3, :] * o_win[:, cols[h]])
    out_ref[...] = jnp.concatenate(merged, axis=0).T.astype(out_ref.dtype)


def _nsa(qpt, qrt, gates_t, kcmp, vcmpt, ovlt, ks, vst, kw, vwt):
    s = ks.shape[0]
    tq = min(NSA_Q_BLOCK, s)
    nsp = ovlt.shape[0]
    col = lambda n: pl.BlockSpec((n, tq), lambda i: (0, i))
    res = lambda a: _resident(a.shape, lambda i: (0,) * a.ndim)
    return pl.pallas_call(
        functools.partial(_nsa_kernel, n_sel=min(SEL_TOP_N, s // SEL_BLOCK)),
        grid=(s // tq,),
        in_specs=[col(NSA_W), col(NSA_W), col(LANES), res(kcmp), res(vcmpt), res(ovlt),
                  res(ks), res(vst), res(kw), res(vwt)],
        out_specs=pl.BlockSpec((tq, NSA_W), lambda i: (i, 0)),
        out_shape=jax.ShapeDtypeStruct((s, NSA_W), MXU_DTYPE),
        scratch_shapes=[pltpu.VMEM((LANES, NSA_HEADS * tq), MXU_DTYPE),
                        pltpu.VMEM((nsp + SEL_CODE_BLOCKS, tq), MXU_DTYPE),
                        pltpu.VMEM((1, NSA_HEADS * tq), F32), pltpu.VMEM((VT_ROWS, NSA_HEADS * tq), F32),
                        pltpu.VMEM((vst.shape[2], NSA_HEADS * tq), F32),
                        pltpu.VMEM((vst.shape[2], NSA_HEADS * tq), F32),
                        pltpu.VMEM((1, NSA_HEADS * tq), F32), pltpu.VMEM((1, NSA_HEADS * tq), F32),
                        pltpu.VMEM((HEAD_DIM, NSA_HEADS * tq), F32), pltpu.VMEM((HEAD_DIM, NSA_HEADS * tq), F32),
                        pltpu.VMEM((nsp, tq), F32)],
        compiler_params=_params(1),
        name="nsa",
    )(qpt, qrt, gates_t, kcmp, vcmpt, ovlt, ks, vst, kw, vwt)


def _softplus2(z2):
    neg_abs = lax.bitcast_convert_type(lax.bitcast_convert_type(z2, jnp.uint32) | jnp.uint32(0x80000000), F32)
    return jnp.maximum(z2, 0.0) + jnp.log2(1.0 + jnp.exp2(neg_abs))


def _sb_kernel(q_ref, k_ref, v_ref, out_ref, acc_scr, run_scr, za_scr, zb_scr):
    assert (q_ref.shape[0] // k_ref.shape[2]) % 2 == 0
    tb = q_ref.shape[0]
    ck = k_ref.shape[2]
    nsub = tb // ck
    i = pl.program_id(0)
    r = lax.broadcasted_iota(jnp.int32, (ck, ck), 0)
    c = lax.broadcasted_iota(jnp.int32, (ck, ck), 1)
    tri = jnp.where(r >= c, 1.0, 0.0).astype(MXU_DTYPE)
    before = c < r
    heads = [slice(h * SB_HEAD_DIM, (h + 1) * SB_HEAD_DIM) for h in range(SB_HEADS)]

    def logits(h, rows, chunk):
        return _dot(q_ref[rows, heads[h]], k_ref[chunk, heads[h], :])

    def step(h, rows, chunk, diag, first, z=None):
        hs = heads[h]
        if z is None:
            z = logits(h, rows, chunk)
        sp = _softplus2(z)
        if diag:
            sp = jnp.where(before, sp, 0.0)
        cs = _dot(sp.astype(MXU_DTYPE), tri)
        own = jnp.minimum(z - cs, 0.0)
        if first:
            a = jnp.exp2(own)
            run_scr[h, rows] = cs[:, 0:1]
        else:
            run = run_scr[h, rows]
            a = jnp.exp2(own - run)
            run_scr[h, rows] = run + cs[:, 0:1]
        if diag:
            a = jnp.where(before, a, 0.0)
        pv = _dot(a.astype(MXU_DTYPE), v_ref[pl.ds(pl.multiple_of(chunk * ck, ck), ck), hs])
        if first:
            acc_scr[h, rows] = pv
        else:
            acc_scr[h, rows] += pv

    for g in range(nsub):
        rows = slice(g * ck, (g + 1) * ck)
        for h in range(SB_HEADS):
            step(h, rows, i * nsub + g, True, True)
        for back in range(g):
            for h in range(SB_HEADS):
                step(h, rows, i * nsub + g - 1 - back, False, False)

    everything = slice(0, tb)

    def lookahead(chunk, z_ref):
        for h in range(SB_HEADS):
            z_ref[h] = logits(h, everything, jnp.maximum(chunk, 0))

    def pair(state):
        j, _ = state
        cur = i * nsub - 1 - 2 * j
        lookahead(cur - 1, zb_scr)
        for h in range(SB_HEADS):
            step(h, everything, cur, False, False, za_scr[h])
        lookahead(cur - 2, za_scr)
        for h in range(SB_HEADS):
            step(h, everything, cur - 1, False, False, zb_scr[h])
        least = functools.reduce(jnp.minimum, [run_scr[h] for h in range(SB_HEADS)])
        return j + 1, (jnp.min(least) >= SB_EXHAUSTED_LOG2).astype(jnp.int32)

    lookahead(i * nsub - 1, za_scr)
    lax.while_loop(lambda state: (state[0] < (i * nsub) // 2) & (state[1] == 0), pair,
                   (jnp.int32(0), jnp.int32(0)))
    for h, hs in enumerate(heads):
        out_ref[:, hs] = acc_scr[h].astype(out_ref.dtype)


def _sb(q, kt, v):
    s = q.shape[0]
    tb = min(SB_BLOCK, s)
    return pl.pallas_call(
        _sb_kernel,
        grid=(s // tb,),
        in_specs=[pl.BlockSpec((tb, SB_W), lambda i: (i, 0)),
                  _resident(kt.shape, lambda i: (0, 0, 0)),
                  _resident((s, SB_W), lambda i: (0, 0))],
        out_specs=pl.BlockSpec((tb, SB_W), lambda i: (i, 0)),
        out_shape=jax.ShapeDtypeStruct((s, SB_W), MXU_DTYPE),
        scratch_shapes=[pltpu.VMEM((SB_HEADS, tb, SB_HEAD_DIM), F32), pltpu.VMEM((SB_HEADS, tb, 1), F32),
                        pltpu.VMEM((SB_HEADS, tb, kt.shape[2]), F32), pltpu.VMEM((SB_HEADS, tb, kt.shape[2]), F32)],
        compiler_params=_params(1),
        name="sb",
    )(q, kt, v)


def _mix_kernel(x_ref, nsa_ref, sb_ref, mg_ref, wn_ref, ws_ref, wo_ref, g_ref, out_ref):
    d = x_ref.shape[1]
    y_nsa = _dot(nsa_ref[...], wn_ref[...])
    y_sb = _dot(sb_ref[...], ws_ref[...])
    merged = mg_ref[:, 0:d].astype(F32) * y_nsa + mg_ref[:, d:2 * d].astype(F32) * y_sb
    mixed = _dot(merged.astype(MXU_DTYPE), wo_ref[...])
    out_ref[...] = x_ref[...] + _rms(mixed, g_ref[...])


def _mix(x, nsa_o, sb_o, mg, wn_all, ws_all, wo_all, g_all, layer):
    s, d = x.shape
    t = min(ROW_TILE, s)
    row = lambda n: pl.BlockSpec((t, n), lambda i: (i, 0))
    lay = lambda a: _resident((None,) + a.shape[1:], lambda i: (layer, 0, 0))
    return pl.pallas_call(
        _mix_kernel,
        grid=(s // t,),
        in_specs=[row(d), row(NSA_W), row(SB_W), row(2 * d), lay(wn_all), lay(ws_all), lay(wo_all), lay(g_all)],
        out_specs=row(d),
        out_shape=jax.ShapeDtypeStruct((s, d), F32),
        compiler_params=_params(1),
        name="mix",
    )(x, nsa_o, sb_o, mg, wn_all, ws_all, wo_all, g_all)


def _ffn_kernel(x_ref, gin_ref, w1_ref, w2_ref, gout_ref, out_ref):
    x = x_ref[...]
    d = x.shape[1]
    hb = _rms(x, gin_ref[...]).astype(MXU_DTYPE)
    ff = jnp.zeros_like(x)
    for c in range(w1_ref.shape[1] // d):
        up = _dot(hb, w1_ref[:, c * d:(c + 1) * d])
        ff = ff + _dot(jnp.square(jnp.maximum(up, 0.0)).astype(MXU_DTYPE), w2_ref[c * d:(c + 1) * d, :])
    out_ref[...] = x + _rms(ff, gout_ref[...])


def _ffn(x, gin_all, w1_all, w2_all, gout_all, layer):
    s, d = x.shape
    t = min(ROW_TILE, s)
    row = lambda n: pl.BlockSpec((t, n), lambda i: (i, 0))
    lay = lambda a: _resident((None,) + a.shape[1:], lambda i: (layer, 0, 0))
    return pl.pallas_call(
        _ffn_kernel,
        grid=(s // t,),
        in_specs=[row(d), lay(gin_all), lay(w1_all), lay(w2_all), lay(gout_all)],
        out_specs=row(d),
        out_shape=jax.ShapeDtypeStruct((s, d), F32),
        compiler_params=_params(1),
        name="ffn",
    )(x, gin_all, w1_all, w2_all, gout_all)


def _regroup_w_in(w_in):
    gate_lo, gate_hi = _C_GATE, _C_GATE + 3 * NSA_HEADS
    pad = jnp.zeros(w_in.shape[:2] + (LANES - 3 * NSA_HEADS,), w_in.dtype)
    return jnp.concatenate([w_in[..., :gate_lo], w_in[..., gate_lo:gate_hi], pad, w_in[..., gate_hi:]],
                           axis=-1).astype(MXU_DTYPE)


def kernel(x, positions, norm_g, w_in, cmp_pe, cmp_w1, cmp_w2, w_nsa_o, w_sb_o, w_out, w_ff1, w_ff2):
    b, s, d = x.shape
    depth = w_in.shape[0]
    ncp, ns = s // CMP_STRIDE, s // SEL_BLOCK
    nsp = -(-ns // LANES) * LANES
    half_w = CMP_STRIDE * HEAD_DIM

    w_in_r = _regroup_w_in(w_in)
    pe = cmp_pe.reshape(depth, 2, 2, half_w)
    w1 = cmp_w1.astype(MXU_DTYPE)
    w2 = cmp_w2.astype(MXU_DTYPE)
    wn, ws, wo = w_nsa_o.astype(MXU_DTYPE), w_sb_o.astype(MXU_DTYPE), w_out.astype(MXU_DTYPE)
    wf1, wf2 = w_ff1.astype(MXU_DTYPE), w_ff2.astype(MXU_DTYPE)
    g_pre, g_mix, g_ffn_in, g_ffn_out = (norm_g[:, n][:, None, :] for n in range(4))

    dim = jnp.arange(LANES) % HEAD_DIM
    half = ROT_DIM // 2
    inv_freq = jnp.power(ROPE_THETA, (dim % half).astype(F32) * (-2.0 / ROT_DIM))
    invf = jnp.where(dim < ROT_DIM, inv_freq, 0.0)[None, :].astype(F32)
    c_start = CMP_STRIDE * jnp.arange(ncp)[None, :]
    s_start = SEL_BLOCK * jnp.arange(nsp)[:, None]
    ovlt = ((c_start < s_start + SEL_BLOCK) & (c_start + CMP_BLOCK > s_start) & (s_start < s)).astype(MXU_DTYPE)

    outs = []
    for bi in range(b):
        xb = x[bi]
        pos = positions[bi][:, None]
        for layer in range(depth):
            (qpt, qrt, kc, vc, ks, vst, kw, vwt, gates_t, sbq, sbkt, sbv, mg) = _inproj(
                xb, g_pre, pos, invf, w_in_r, layer)
            kcmp, vcmpt = _compress(kc.reshape(ncp, half_w), vc.reshape(ncp, half_w), pe, w1, w2, layer)
            nsa_o = _nsa(qpt, qrt, gates_t, kcmp, vcmpt, ovlt, ks, vst, kw, vwt)
            sb_o = _sb(sbq, sbkt, sbv)
            xb = _mix(xb, nsa_o, sb_o, mg, wn, ws, wo, g_mix, layer)
            xb = _ffn(xb, g_ffn_in, wf1, wf2, g_ffn_out, layer)
        outs.append(xb)
    return jnp.stack(outs, axis=0)
```

```python
import functools

import jax
import jax.numpy as jnp
from jax import lax
from jax.experimental import pallas as pl
from jax.experimental.pallas import tpu as pltpu

F32 = jnp.float32
MXU_DTYPE = jnp.bfloat16

HEAD_DIM = 64
NSA_HEADS = 8
SB_HEADS = 4
SB_HEAD_DIM = 128
ROPE_THETA = 500000.0
ROT_DIM = HEAD_DIM // 4
CMP_BLOCK = 32
CMP_STRIDE = 16
SEL_BLOCK = 64
SEL_TOP_N = 8
WINDOW = 512
RMS_EPS = 1e-6
NSA_W = NSA_HEADS * HEAD_DIM
SB_W = SB_HEADS * SB_HEAD_DIM
LANES = 128
VT_ROWS = HEAD_DIM + 16
MASKED = -32768.0
LOG2_E = 1.4426950408889634
VMEM_LIMIT = 56 * 1024 * 1024

ROW_TILE = 512
NSA_Q_BLOCK = 256
NSA_KEY_CHUNK = 512
SEL_CODE_BLOCKS = HEAD_DIM
CMP_VARIANTS = 4
SB_BLOCK = 512
SB_KEY_CHUNK = 256
SB_EXHAUSTED_LOG2 = 1100.0


def _dot(a, b):
    return jnp.dot(a, b, preferred_element_type=F32)


def _rms(x, g):
    return x * lax.rsqrt(jnp.mean(x * x, axis=-1, keepdims=True) + RMS_EPS) * g


def _params(n_grid_dims):
    return pltpu.CompilerParams(dimension_semantics=("arbitrary",) * n_grid_dims,
                                vmem_limit_bytes=VMEM_LIMIT)


def _resident(block_shape, index_map):
    return pl.BlockSpec(block_shape, index_map, pipeline_mode=pl.Buffered(1))


_C_KV = NSA_W
_C_GATE = _C_KV + 6 * HEAD_DIM
_C_SB = _C_GATE + LANES
_C_MERGE = _C_SB + 3 * SB_W


def _inproj_kernel(x_ref, g_ref, pos_ref, invf_ref, w_ref,
                   qpt_ref, qrt_ref, kc_ref, vc_ref, ks_ref, vst_ref, kw_ref, vwt_ref,
                   gate_ref, sbq_ref, sbk_ref, sbv_ref, mg_ref):
    t, d_model = x_ref.shape
    hb = _rms(x_ref[...], g_ref[...]).astype(MXU_DTYPE)

    ang = pos_ref[...].astype(F32) * invf_ref[...]
    cos, sin = jnp.cos(ang), jnp.sin(ang)
    lane = lax.broadcasted_iota(jnp.int32, (1, LANES), 1)
    dim = lane % HEAD_DIM
    half = ROT_DIM // 2
    sin_up = jnp.where((dim >= half) & (dim < ROT_DIM), sin, 0.0)
    sin_dn = jnp.where(dim < half, -sin, 0.0)
    low = lane < HEAD_DIM

    def rope(xg):
        return xg * cos + pltpu.roll(xg, half, 1) * sin_up + pltpu.roll(xg, LANES - half, 1) * sin_dn

    pa = _dot(hb, w_ref[:, 0:_C_GATE])
    scale = LOG2_E * HEAD_DIM ** -0.5
    for j in range(NSA_W // LANES):
        qg = pa[:, j * LANES:(j + 1) * LANES]
        qpt_ref[j * LANES:(j + 1) * LANES, :] = (qg * scale).T.astype(qpt_ref.dtype)
        qrt_ref[j * LANES:(j + 1) * LANES, :] = (rope(qg) * scale).T.astype(qrt_ref.dtype)
    kc_ref[...] = pa[:, _C_KV:_C_KV + HEAD_DIM]
    vc_ref[...] = pa[:, _C_KV + HEAD_DIM:_C_KV + 2 * HEAD_DIM]

    row = pl.program_id(0) * t + lax.broadcasted_iota(jnp.int32, (t, 1), 0)
    code = jnp.where(lane - HEAD_DIM == (row // SEL_BLOCK) % SEL_CODE_BLOCKS, 1.0, 0.0)
    ksg = pa[:, _C_KV + 2 * HEAD_DIM:_C_KV + 4 * HEAD_DIM]
    ks_ref[...] = jnp.where(low, rope(ksg), code).astype(ks_ref.dtype)
    vst_ref[0] = jnp.where(low, pltpu.roll(ksg, HEAD_DIM, 1), 1.0).T[0:VT_ROWS].astype(vst_ref.dtype)
    kwg = pa[:, _C_KV + 4 * HEAD_DIM:_C_KV + 6 * HEAD_DIM]
    kw_ref[...] = jnp.where(low, rope(kwg), 0.0).astype(kw_ref.dtype)
    vw_t = jnp.where(low, pltpu.roll(kwg, HEAD_DIM, 1), 1.0).T[0:VT_ROWS]
    for c in range(vwt_ref.shape[0]):
        vwt_ref[c] = vw_t[:, c * vwt_ref.shape[2]:(c + 1) * vwt_ref.shape[2]].astype(vwt_ref.dtype)

    gate_ref[...] = jax.nn.sigmoid(_dot(hb, w_ref[:, _C_GATE:_C_SB])).T
    sb = _dot(hb, w_ref[:, _C_SB:_C_MERGE])
    sbq_ref[...] = (sb[:, 0:SB_W] * (LOG2_E * SB_HEAD_DIM ** -0.5)).astype(sbq_ref.dtype)
    sbk_t = sb[:, SB_W:2 * SB_W].T
    for c in range(sbk_ref.shape[0]):
        sbk_ref[c] = sbk_t[:, c * sbk_ref.shape[2]:(c + 1) * sbk_ref.shape[2]].astype(sbk_ref.dtype)
    sbv_ref[...] = sb[:, 2 * SB_W:3 * SB_W].astype(sbv_ref.dtype)
    for c in range(2):
        mg_ref[:, c * d_model:(c + 1) * d_model] = jax.nn.sigmoid(
            _dot(hb, w_ref[:, _C_MERGE + c * d_model:_C_MERGE + (c + 1) * d_model])).astype(mg_ref.dtype)


def _inproj(x, g, pos, invf, w_all, layer):
    s, d = x.shape
    t = min(ROW_TILE, s)
    wcols = w_all.shape[2]
    sck = min(SB_KEY_CHUNK, s)
    wck = min(NSA_Q_BLOCK, s)
    row = lambda n: pl.BlockSpec((t, n), lambda i: (i, 0))
    col = lambda n: pl.BlockSpec((n, t), lambda i: (0, i))
    slab = lambda n, ck: pl.BlockSpec((t // ck, n, ck), lambda i: (i, 0, 0))
    sds = jax.ShapeDtypeStruct
    out_shape = [
        sds((NSA_W, s), MXU_DTYPE), sds((NSA_W, s), MXU_DTYPE),
        sds((s, HEAD_DIM), F32), sds((s, HEAD_DIM), F32),
        sds((s, LANES), MXU_DTYPE), sds((s // t, VT_ROWS, t), MXU_DTYPE),
        sds((s, LANES), MXU_DTYPE), sds((s // wck, VT_ROWS, wck), MXU_DTYPE),
        sds((LANES, s), F32),
        sds((s, SB_W), MXU_DTYPE), sds((s // sck, SB_W, sck), MXU_DTYPE), sds((s, SB_W), MXU_DTYPE),
        sds((s, 2 * d), MXU_DTYPE),
    ]
    out_specs = [col(NSA_W), col(NSA_W), row(HEAD_DIM), row(HEAD_DIM),
                 row(LANES), slab(VT_ROWS, t), row(LANES), slab(VT_ROWS, wck),
                 col(LANES), row(SB_W), slab(SB_W, sck), row(SB_W), row(2 * d)]
    return pl.pallas_call(
        _inproj_kernel,
        grid=(s // t,),
        in_specs=[row(d),
                  _resident((None, 1, d), lambda i: (layer, 0, 0)),
                  row(1),
                  _resident((1, LANES), lambda i: (0, 0)),
                  _resident((None, d, wcols), lambda i: (layer, 0, 0))],
        out_specs=out_specs,
        out_shape=out_shape,
        compiler_params=_params(1),
        name="inproj",
    )(x, g, pos, invf, w_all)


def _compress_kernel(kc_ref, vc_ref, pe_ref, w1_ref, w2_ref, kcmp_ref, vcmpt_ref):
    nr, half_w = kc_ref.shape
    outs = []
    for kv, r_ref in enumerate((kc_ref, vc_ref)):
        r = r_ref[...]
        ha = _dot((r + pe_ref[kv, 0:1, :]).astype(MXU_DTYPE), w1_ref[kv, 0:half_w, :])
        hb = _dot((r + pe_ref[kv, 1:2, :]).astype(MXU_DTYPE), w1_ref[kv, half_w:2 * half_w, :])
        hid = ha + pltpu.roll(hb, nr - 1, 0)
        outs.append(_dot(jax.nn.gelu(hid).astype(MXU_DTYPE), w2_ref[kv]))
    pad = jnp.zeros_like(outs[0])
    kcmp_ref[...] = jnp.concatenate([outs[0], pad], axis=1).astype(kcmp_ref.dtype)
    vcmpt_ref[...] = jnp.concatenate([outs[1], pad], axis=1).T[0:HEAD_DIM].astype(vcmpt_ref.dtype)


def _compress(kc, vc, pe_all, w1_all, w2_all, layer):
    nr, half_w = kc.shape
    hidden = w1_all.shape[3]
    full = lambda shape: pl.BlockSpec(shape, lambda i: (0,) * len(shape))
    return pl.pallas_call(
        _compress_kernel,
        grid=(1,),
        in_specs=[full((nr, half_w)), full((nr, half_w)),
                  pl.BlockSpec((None, 2, 2, half_w), lambda i: (layer, 0, 0, 0)),
                  pl.BlockSpec((None, 2, 2 * half_w, hidden), lambda i: (layer, 0, 0, 0)),
                  pl.BlockSpec((None, 2, hidden, HEAD_DIM), lambda i: (layer, 0, 0, 0))],
        out_specs=[full((nr, LANES)), full((HEAD_DIM, nr))],
        out_shape=[jax.ShapeDtypeStruct((nr, LANES), MXU_DTYPE), jax.ShapeDtypeStruct((HEAD_DIM, nr), MXU_DTYPE)],
        compiler_params=_params(1),
        name="compress",
    )(kc, vc, pe_all, w1_all, w2_all)


def _nsa_kernel(qpt_ref, qrt_ref, gate_ref, kcmp_ref, vcmpt_ref, ovlt_ref, ks_ref, vst_ref, kw_ref, vwt_ref,
                out_ref, qaug_scr, mask_scr, m_scr, acc_scr, sa_scr, sb_scr, ta_scr, tb_scr, ocmp_scr, owin_scr, imp_scr, *, n_sel):
    tq = qpt_ref.shape[1]
    seq = ks_ref.shape[0]
    ncp = kcmp_ref.shape[0]
    nsp = ovlt_ref.shape[0]
    ck = vst_ref.shape[2]
    wck = vwt_ref.shape[2]
    nh = NSA_HEADS
    q0 = pl.program_id(0) * tq
    t = q0 + lax.broadcasted_iota(jnp.int32, (1, tq), 1)
    cols = [slice(h * tq, (h + 1) * tq) for h in range(nh)]

    def aug(qt_ref, h, tail):
        return jnp.concatenate([qt_ref[h * HEAD_DIM:(h + 1) * HEAD_DIM, :], tail], axis=0)

    zeros_tail = jnp.zeros((HEAD_DIM, tq), MXU_DTYPE)
    def cmp_branch(rows):
        qp_aug = jnp.concatenate([aug(qpt_ref, h, zeros_tail) for h in range(nh)], axis=1)
        sc_all = _dot(kcmp_ref[0:rows, :], qp_aug)
        cmp_last = CMP_STRIDE * lax.broadcasted_iota(jnp.int32, (rows, 1), 0) + (CMP_BLOCK - 1)
        vis_c = cmp_last <= t
        psum = jnp.zeros((rows, tq), F32)
        p_cmp = []
        for h in range(nh):
            sc = jnp.where(vis_c, sc_all[:, cols[h]], -1e30)
            top = jnp.max(sc, axis=0, keepdims=True)
            e = jnp.exp2(sc - jnp.where(top > -1e29, top, 0.0))
            den = jnp.sum(e, axis=0, keepdims=True)
            p = e * (1.0 / jnp.where(den > 0.0, den, 1.0))
            psum = psum + p
            p_cmp.append(p.astype(MXU_DTYPE))
        o_cmp = _dot(vcmpt_ref[:, 0:rows], jnp.concatenate(p_cmp, axis=1))
        p_hi = psum.astype(MXU_DTYPE)
        p_lo = (psum - p_hi.astype(F32)).astype(MXU_DTYPE)
        imp_scr[...] = _dot(ovlt_ref[:, 0:rows], p_hi) + _dot(ovlt_ref[:, 0:rows], p_lo)
        ocmp_scr[...] = o_cmp

    n_var = CMP_VARIANTS if ncp % (CMP_VARIANTS * LANES) == 0 else 1
    step = ncp // n_var
    n_vis = (q0 + tq - CMP_BLOCK) // CMP_STRIDE + 1
    variant = jnp.clip((n_vis + step - 1) // step - 1, 0, n_var - 1)
    for v in range(n_var):
        pl.when(variant == v)(functools.partial(cmp_branch, (v + 1) * step))
    imp = imp_scr[...]

    qr_aug = jnp.concatenate([aug(qrt_ref, h, zeros_tail) for h in range(nh)], axis=1)
    span = min(WINDOW + tq, seq)
    start = pl.multiple_of(jnp.maximum(q0 + tq - span, 0), wck)
    sw_all = _dot(kw_ref[pl.ds(start, span), :], qr_aug)

    def window_finish():
        kpos_w = start + lax.broadcasted_iota(jnp.int32, (span, 1), 0)
        vis_w = (kpos_w <= t) & (kpos_w > t - WINDOW)
        p_win = []
        for h in range(nh):
            sw = jnp.where(vis_w, sw_all[:, cols[h]], -1e30)
            p_win.append(jnp.exp2(sw - jnp.max(sw, axis=0, keepdims=True)).astype(MXU_DTYPE))
        p_win = jnp.concatenate(p_win, axis=1)
        ow = jnp.zeros((VT_ROWS, nh * tq), F32)
        for j in range(span // wck):
            ow = ow + _dot(vwt_ref[start // wck + j], p_win[j * wck:(j + 1) * wck, :])
        owin_scr[...] = ow[0:HEAD_DIM] * (1.0 / ow[HEAD_DIM:HEAD_DIM + 1])

    blk = lax.broadcasted_iota(jnp.int32, (nsp, 1), 0)
    cur = t // SEL_BLOCK
    valid = blk <= cur
    forced = (blk == 0) | (blk == cur) | (blk == cur - 1)
    n_forced = 3
    assert n_sel >= n_forced
    score = jnp.where(valid, jnp.where(forced, -jnp.inf, imp), -1.0)
    chosen = jnp.where(forced & valid, 1.0, 0.0)
    for _ in range(n_sel - n_forced):
        best = jnp.max(score, axis=0, keepdims=True)
        idx = jnp.min(jnp.where(score == best, blk, nsp), axis=0, keepdims=True)
        hit = blk == idx
        chosen = jnp.where(hit, 1.0, chosen)
        score = jnp.where(hit, -jnp.inf, score)
    mask_scr[0:nsp, :] = ((chosen - 1.0) * (-MASKED)).astype(mask_scr.dtype)
    mask_scr[nsp:nsp + SEL_CODE_BLOCKS, :] = jnp.full((SEL_CODE_BLOCKS, tq), MASKED, mask_scr.dtype)

    for h in range(nh):
        qaug_scr[0:HEAD_DIM, cols[h]] = qrt_ref[h * HEAD_DIM:(h + 1) * HEAD_DIM, :]
    m_scr[...] = jnp.full(m_scr.shape, -1e30, F32)
    acc_scr[...] = jnp.zeros(acc_scr.shape, F32)

    c_diag = q0 // ck

    def sel_scores(c, s_ref, top_ref, limit):
        k0 = pl.multiple_of(jnp.minimum(c, c_diag) * ck, ck)
        code0 = jnp.where(c < limit, (k0 // (SEL_BLOCK * SEL_CODE_BLOCKS)) * SEL_CODE_BLOCKS, nsp)
        mrows = mask_scr[pl.ds(pl.multiple_of(code0, SEL_CODE_BLOCKS), SEL_CODE_BLOCKS), :]
        for h in range(nh):
            qaug_scr[HEAD_DIM:2 * HEAD_DIM, cols[h]] = mrows
        s_all = _dot(ks_ref[pl.ds(k0, ck), :], qaug_scr[...])
        s_ref[...] = s_all
        top_ref[...] = jnp.max(s_all, axis=0, keepdims=True)

    def sel_update(c, s_ref, top_ref, causal):
        m_old = m_scr[...]
        if causal:
            kpos = c * ck + lax.broadcasted_iota(jnp.int32, (ck, 1), 0)
            bias = jnp.where(kpos <= t, 0.0, MASKED)
            tops = [jnp.max(s_ref[:, cols[h]] + bias, axis=0, keepdims=True) for h in range(nh)]
            m_new = jnp.maximum(m_old, jnp.concatenate(tops, axis=1))
        else:
            m_new = jnp.maximum(m_old, top_ref[...])
        p_all = []
        for h in range(nh):
            s = s_ref[:, cols[h]]
            if causal:
                s = s + bias
            p_all.append(jnp.exp2(s - m_new[:, cols[h]]).astype(MXU_DTYPE))
        pv = _dot(vst_ref[jnp.minimum(c, c_diag)], jnp.concatenate(p_all, axis=1))
        acc_scr[...] = jnp.exp2(m_old - m_new) * acc_scr[...] + pv
        m_scr[...] = m_new

    def sel_pair(c):
        sel_scores(c + 1, sa_scr, ta_scr, c_diag)
        sel_update(c, sb_scr, tb_scr, False)
        sel_scores(c + 2, sb_scr, tb_scr, c_diag)
        sel_update(c + 1, sa_scr, ta_scr, False)

    def sel_quad(j, carry):
        sel_pair(4 * j)
        sel_pair(4 * j + 2)
        return carry

    sel_scores(c_diag, sa_scr, ta_scr, c_diag + 1)
    sel_scores(0, sb_scr, tb_scr, c_diag)
    window_finish()
    sel_update(c_diag, sa_scr, ta_scr, True)
    n_quads = c_diag // 4
    lax.fori_loop(0, n_quads, sel_quad, 0)
    lax.fori_loop(0, (c_diag - 4 * n_quads + 1) // 2, lambda j, carry: (sel_pair(4 * n_quads + 2 * j), carry)[1], 0)
    acc = acc_scr[...]
    o_sel = acc[0:HEAD_DIM] * (1.0 / acc[HEAD_DIM:HEAD_DIM + 1])

    o_cmp = ocmp_scr[...]
    o_win = owin_scr[...]
    gates = gate_ref[...]
    merged = []
    for h in range(nh):
        merged.append(gates[3 * h:3 * h + 1, :] * o_cmp[:, cols[h]]
                      + gates[3 * h + 1:3 * h + 2, :] * o_sel[:, cols[h]]
                      + gates[3 * h + 2:3 * h + 3, :] * o_win[:, cols[h]])
    out_ref[...] = jnp.concatenate(merged, axis=0).T.astype(out_ref.dtype)


def _nsa(qpt, qrt, gates_t, kcmp, vcmpt, ovlt, ks, vst, kw, vwt):
    s = ks.shape[0]
    tq = min(NSA_Q_BLOCK, s)
    nsp = ovlt.shape[0]
    col = lambda n: pl.BlockSpec((n, tq), lambda i: (0, i))
    res = lambda a: _resident(a.shape, lambda i: (0,) * a.ndim)
    return pl.pallas_call(
        functools.partial(_nsa_kernel, n_sel=min(SEL_TOP_N, s // SEL_BLOCK)),
        grid=(s // tq,),
        in_specs=[col(NSA_W), col(NSA_W), col(LANES), res(kcmp), res(vcmpt), res(ovlt),
                  res(ks), res(vst), res(kw), res(vwt)],
        out_specs=pl.BlockSpec((tq, NSA_W), lambda i: (i, 0)),
        out_shape=jax.ShapeDtypeStruct((s, NSA_W), MXU_DTYPE),
        scratch_shapes=[pltpu.VMEM((LANES, NSA_HEADS * tq), MXU_DTYPE),
                        pltpu.VMEM((nsp + SEL_CODE_BLOCKS, tq), MXU_DTYPE),
                        pltpu.VMEM((1, NSA_HEADS * tq), F32), pltpu.VMEM((VT_ROWS, NSA_HEADS * tq), F32),
                        pltpu.VMEM((vst.shape[2], NSA_HEADS * tq), F32),
                        pltpu.VMEM((vst.shape[2], NSA_HEADS * tq), F32),
                        pltpu.VMEM((1, NSA_HEADS * tq), F32), pltpu.VMEM((1, NSA_HEADS * tq), F32),
                        pltpu.VMEM((HEAD_DIM, NSA_HEADS * tq), F32), pltpu.VMEM((HEAD_DIM, NSA_HEADS * tq), F32),
                        pltpu.VMEM((nsp, tq), F32)],
        compiler_params=_params(1),
        name="nsa",
    )(qpt, qrt, gates_t, kcmp, vcmpt, ovlt, ks, vst, kw, vwt)


def _softplus2(z2):
    neg_abs = lax.bitcast_convert_type(lax.bitcast_convert_type(z2, jnp.uint32) | jnp.uint32(0x80000000), F32)
    return jnp.maximum(z2, 0.0) + jnp.log2(1.0 + jnp.exp2(neg_abs))


def _sb_kernel(q_ref, k_ref, v_ref, out_ref, acc_scr, run_scr, za_scr, zb_scr):
    assert (q_ref.shape[0] // k_ref.shape[2]) % 2 == 0
    tb = q_ref.shape[0]
    ck = k_ref.shape[2]
    nsub = tb // ck
    i = pl.program_id(0)
    r = lax.broadcasted_iota(jnp.int32, (ck, ck), 0)
    c = lax.broadcasted_iota(jnp.int32, (ck, ck), 1)
    tri = jnp.where(r >= c, 1.0, 0.0).astype(MXU_DTYPE)
    before = c < r
    heads = [slice(h * SB_HEAD_DIM, (h + 1) * SB_HEAD_DIM) for h in range(SB_HEADS)]

    def logits(h, rows, chunk):
        return _dot(q_ref[rows, heads[h]], k_ref[chunk, heads[h], :])

    def step(h, rows, chunk, diag, first, z=None):
        hs = heads[h]
        if z is None:
            z = logits(h, rows, chunk)
        sp = _softplus2(z)
        if diag:
            sp = jnp.where(before, sp, 0.0)
        cs = _dot(sp.astype(MXU_DTYPE), tri)
        own = jnp.minimum(z - cs, 0.0)
        if first:
            a = jnp.exp2(own)
            run_scr[h, rows] = cs[:, 0:1]
        else:
            run = run_scr[h, rows]
            a = jnp.exp2(own - run)
            run_scr[h, rows] = run + cs[:, 0:1]
        if diag:
            a = jnp.where(before, a, 0.0)
        pv = _dot(a.astype(MXU_DTYPE), v_ref[pl.ds(pl.multiple_of(chunk * ck, ck), ck), hs])
        if first:
            acc_scr[h, rows] = pv
        else:
            acc_scr[h, rows] += pv

    for g in range(nsub):
        rows = slice(g * ck, (g + 1) * ck)
        for h in range(SB_HEADS):
            step(h, rows, i * nsub + g, True, True)
        for back in range(g):
            for h in range(SB_HEADS):
                step(h, rows, i * nsub + g - 1 - back, False, False)

    everything = slice(0, tb)

    def lookahead(chunk, z_ref):
        for h in range(SB_HEADS):
            z_ref[h] = logits(h, everything, jnp.maximum(chunk, 0))

    def pair(state):
        j, _ = state
        cur = i * nsub - 1 - 2 * j
        lookahead(cur - 1, zb_scr)
        for h in range(SB_HEADS):
            step(h, everything, cur, False, False, za_scr[h])
        lookahead(cur - 2, za_scr)
        for h in range(SB_HEADS):
            step(h, everything, cur - 1, False, False, zb_scr[h])
        least = functools.reduce(jnp.minimum, [run_scr[h] for h in range(SB_HEADS)])
        return j + 1, (jnp.min(least) >= SB_EXHAUSTED_LOG2).astype(jnp.int32)

    lookahead(i * nsub - 1, za_scr)
    lax.while_loop(lambda state: (state[0] < (i * nsub) // 2) & (state[1] == 0), pair,
                   (jnp.int32(0), jnp.int32(0)))
    for h, hs in enumerate(heads):
        out_ref[:, hs] = acc_scr[h].astype(out_ref.dtype)


def _sb(q, kt, v):
    s = q.shape[0]
    tb = min(SB_BLOCK, s)
    return pl.pallas_call(
        _sb_kernel,
        grid=(s // tb,),
        in_specs=[pl.BlockSpec((tb, SB_W), lambda i: (i, 0)),
                  _resident(kt.shape, lambda i: (0, 0, 0)),
                  _resident((s, SB_W), lambda i: (0, 0))],
        out_specs=pl.BlockSpec((tb, SB_W), lambda i: (i, 0)),
        out_shape=jax.ShapeDtypeStruct((s, SB_W), MXU_DTYPE),
        scratch_shapes=[pltpu.VMEM((SB_HEADS, tb, SB_HEAD_DIM), F32), pltpu.VMEM((SB_HEADS, tb, 1), F32),
                        pltpu.VMEM((SB_HEADS, tb, kt.shape[2]), F32), pltpu.VMEM((SB_HEADS, tb, kt.shape[2]), F32)],
        compiler_params=_params(1),
        name="sb",
    )(q, kt, v)


def _mix_kernel(x_ref, nsa_ref, sb_ref, mg_ref, wn_ref, ws_ref, wo_ref, g_ref, out_ref):
    d = x_ref.shape[1]
    y_nsa = _dot(nsa_ref[...], wn_ref[...])
    y_sb = _dot(sb_ref[...], ws_ref[...])
    merged = mg_ref[:, 0:d].astype(F32) * y_nsa + mg_ref[:, d:2 * d].astype(F32) * y_sb
    mixed = _dot(merged.astype(MXU_DTYPE), wo_ref[...])
    out_ref[...] = x_ref[...] + _rms(mixed, g_ref[...])


def _mix(x, nsa_o, sb_o, mg, wn_all, ws_all, wo_all, g_all, layer):
    s, d = x.shape
    t = min(ROW_TILE, s)
    row = lambda n: pl.BlockSpec((t, n), lambda i: (i, 0))
    lay = lambda a: _resident((None,) + a.shape[1:], lambda i: (layer, 0, 0))
    return pl.pallas_call(
        _mix_kernel,
        grid=(s // t,),
        in_specs=[row(d), row(NSA_W), row(SB_W), row(2 * d), lay(wn_all), lay(ws_all), lay(wo_all), lay(g_all)],
        out_specs=row(d),
        out_shape=jax.ShapeDtypeStruct((s, d), F32),
        compiler_params=_params(1),
        name="mix",
    )(x, nsa_o, sb_o, mg, wn_all, ws_all, wo_all, g_all)


def _ffn_kernel(x_ref, gin_ref, w1_ref, w2_ref, gout_ref, out_ref):
    x = x_ref[...]
    d = x.shape[1]
    hb = _rms(x, gin_ref[...]).astype(MXU_DTYPE)
    ff = jnp.zeros_like(x)
    for c in range(w1_ref.shape[1] // d):
        up = _dot(hb, w1_ref[:, c * d:(c + 1) * d])
        ff = ff + _dot(jnp.square(jnp.maximum(up, 0.0)).astype(MXU_DTYPE), w2_ref[c * d:(c + 1) * d, :])
    out_ref[...] = x + _rms(ff, gout_ref[...])


def _ffn(x, gin_all, w1_all, w2_all, gout_all, layer):
    s, d = x.shape
    t = min(ROW_TILE, s)
    row = lambda n: pl.BlockSpec((t, n), lambda i: (i, 0))
    lay = lambda a: _resident((None,) + a.shape[1:], lambda i: (layer, 0, 0))
    return pl.pallas_call(
        _ffn_kernel,
        grid=(s // t,),
        in_specs=[row(d), lay(gin_all), lay(w1_all), lay(w2_all), lay(gout_all)],
        out_specs=row(d),
        out_shape=jax.ShapeDtypeStruct((s, d), F32),
        compiler_params=_params(1),
        name="ffn",
    )(x, gin_all, w1_all, w2_all, gout_all)


def _regroup_w_in(w_in):
    gate_lo, gate_hi = _C_GATE, _C_GATE + 3 * NSA_HEADS
    pad = jnp.zeros(w_in.shape[:2] + (LANES - 3 * NSA_HEADS,), w_in.dtype)
    return jnp.concatenate([w_in[..., :gate_lo], w_in[..., gate_lo:gate_hi], pad, w_in[..., gate_hi:]],
                           axis=-1).astype(MXU_DTYPE)


def kernel(x, positions, norm_g, w_in, cmp_pe, cmp_w1, cmp_w2, w_nsa_o, w_sb_o, w_out, w_ff1, w_ff2):
    b, s, d = x.shape
    depth = w_in.shape[0]
    ncp, ns = s // CMP_STRIDE, s // SEL_BLOCK
    nsp = -(-ns // LANES) * LANES
    half_w = CMP_STRIDE * HEAD_DIM

    w_in_r = _regroup_w_in(w_in)
    pe = cmp_pe.reshape(depth, 2, 2, half_w)
    w1 = cmp_w1.astype(MXU_DTYPE)
    w2 = cmp_w2.astype(MXU_DTYPE)
    wn, ws, wo = w_nsa_o.astype(MXU_DTYPE), w_sb_o.astype(MXU_DTYPE), w_out.astype(MXU_DTYPE)
    wf1, wf2 = w_ff1.astype(MXU_DTYPE), w_ff2.astype(MXU_DTYPE)
    g_pre, g_mix, g_ffn_in, g_ffn_out = (norm_g[:, n][:, None, :] for n in range(4))

    dim = jnp.arange(LANES) % HEAD_DIM
    half = ROT_DIM // 2
    inv_freq = jnp.power(ROPE_THETA, (dim % half).astype(F32) * (-2.0 / ROT_DIM))
    invf = jnp.where(dim < ROT_DIM, inv_freq, 0.0)[None, :].astype(F32)
    c_start = CMP_STRIDE * jnp.arange(ncp)[None, :]
    s_start = SEL_BLOCK * jnp.arange(nsp)[:, None]
    ovlt = ((c_start < s_start + SEL_BLOCK) & (c_start + CMP_BLOCK > s_start) & (s_start < s)).astype(MXU_DTYPE)

    outs = []
    for bi in range(b):
        xb = x[bi]
        pos = positions[bi][:, None]
        for layer in range(depth):
            (qpt, qrt, kc, vc, ks, vst, kw, vwt, gates_t, sbq, sbkt, sbv, mg) = _inproj(
                xb, g_pre, pos, invf, w_in_r, layer)
            kcmp, vcmpt = _compress(kc.reshape(ncp, half_w), vc.reshape(ncp, half_w), pe, w1, w2, layer)
            nsa_o = _nsa(qpt, qrt, gates_t, kcmp, vcmpt, ovlt, ks, vst, kw, vwt)
            sb_o = _sb(sbq, sbkt, sbv)
            xb = _mix(xb, nsa_o, sb_o, mg, wn, ws, wo, g_mix, layer)
            xb = _ffn(xb, g_ffn_in, wf1, wf2, g_ffn_out, layer)
        outs.append(xb)
    return jnp.stack(outs, axis=0)
```

```python
import functools

import jax
import jax.numpy as jnp
from jax import lax
from jax.experimental import pallas as pl
from jax.experimental.pallas import tpu as pltpu

F32 = jnp.float32
MXU_DTYPE = jnp.bfloat16

HEAD_DIM = 64
NSA_HEADS = 8
SB_HEADS = 4
SB_HEAD_DIM = 128
ROPE_THETA = 500000.0
ROT_DIM = HEAD_DIM // 4
CMP_BLOCK = 32
CMP_STRIDE = 16
SEL_BLOCK = 64
SEL_TOP_N = 8
WINDOW = 512
RMS_EPS = 1e-6
NSA_W = NSA_HEADS * HEAD_DIM
SB_W = SB_HEADS * SB_HEAD_DIM
LANES = 128
VT_ROWS = HEAD_DIM + 16
MASKED = -32768.0
LOG2_E = 1.4426950408889634
VMEM_LIMIT = 56 * 1024 * 1024

ROW_TILE = 1024
NSA_Q_BLOCK = 256
NSA_KEY_CHUNK = 512
SEL_CODE_BLOCKS = HEAD_DIM
CMP_VARIANTS = 4
SB_BLOCK = 512
SB_KEY_CHUNK = 256
SB_EXHAUSTED_LOG2 = 1100.0


def _dot(a, b):
    return jnp.dot(a, b, preferred_element_type=F32)


def _rms(x, g):
    return x * lax.rsqrt(jnp.mean(x * x, axis=-1, keepdims=True) + RMS_EPS) * g


def _params(n_grid_dims):
    return pltpu.CompilerParams(dimension_semantics=("arbitrary",) * n_grid_dims,
                                vmem_limit_bytes=VMEM_LIMIT)


def _resident(block_shape, index_map):
    return pl.BlockSpec(block_shape, index_map, pipeline_mode=pl.Buffered(1))


_C_KV = NSA_W
_C_GATE = _C_KV + 6 * HEAD_DIM
_C_SB = _C_GATE + LANES
_C_MERGE = _C_SB + 3 * SB_W


def _inproj_kernel(x_ref, g_ref, pos_ref, invf_ref, w_ref,
                   qpt_ref, qrt_ref, kc_ref, vc_ref, ks_ref, vst_ref, kw_ref, vwt_ref,
                   gate_ref, sbq_ref, sbk_ref, sbv_ref, mg_ref):
    t, d_model = x_ref.shape
    hb = _rms(x_ref[...], g_ref[...]).astype(MXU_DTYPE)

    ang = pos_ref[...].astype(F32) * invf_ref[...]
    cos, sin = jnp.cos(ang), jnp.sin(ang)
    lane = lax.broadcasted_iota(jnp.int32, (1, LANES), 1)
    dim = lane % HEAD_DIM
    half = ROT_DIM // 2
    sin_up = jnp.where((dim >= half) & (dim < ROT_DIM), sin, 0.0)
    sin_dn = jnp.where(dim < half, -sin, 0.0)
    low = lane < HEAD_DIM

    def rope(xg):
        return xg * cos + pltpu.roll(xg, half, 1) * sin_up + pltpu.roll(xg, LANES - half, 1) * sin_dn

    pa = _dot(hb, w_ref[:, 0:_C_GATE])
    scale = LOG2_E * HEAD_DIM ** -0.5
    for j in range(NSA_W // LANES):
        qg = pa[:, j * LANES:(j + 1) * LANES]
        qpt_ref[j * LANES:(j + 1) * LANES, :] = (qg * scale).T.astype(qpt_ref.dtype)
        qrt_ref[j * LANES:(j + 1) * LANES, :] = (rope(qg) * scale).T.astype(qrt_ref.dtype)
    kc_ref[...] = pa[:, _C_KV:_C_KV + HEAD_DIM]
    vc_ref[...] = pa[:, _C_KV + HEAD_DIM:_C_KV + 2 * HEAD_DIM]

    row = pl.program_id(0) * t + lax.broadcasted_iota(jnp.int32, (t, 1), 0)
    code = jnp.where(lane - HEAD_DIM == (row // SEL_BLOCK) % SEL_CODE_BLOCKS, 1.0, 0.0)
    ksg = pa[:, _C_KV + 2 * HEAD_DIM:_C_KV + 4 * HEAD_DIM]
    ks_ref[...] = jnp.where(low, rope(ksg), code).astype(ks_ref.dtype)
    vs_t = jnp.where(low, pltpu.roll(ksg, HEAD_DIM, 1), 1.0).T[0:VT_ROWS]
    for c in range(vst_ref.shape[0]):
        vst_ref[c] = vs_t[:, c * vst_ref.shape[2]:(c + 1) * vst_ref.shape[2]].astype(vst_ref.dtype)
    kwg = pa[:, _C_KV + 4 * HEAD_DIM:_C_KV + 6 * HEAD_DIM]
    kw_ref[...] = jnp.where(low, rope(kwg), 0.0).astype(kw_ref.dtype)
    vw_t = jnp.where(low, pltpu.roll(kwg, HEAD_DIM, 1), 1.0).T[0:VT_ROWS]
    for c in range(vwt_ref.shape[0]):
        vwt_ref[c] = vw_t[:, c * vwt_ref.shape[2]:(c + 1) * vwt_ref.shape[2]].astype(vwt_ref.dtype)

    gate_ref[...] = jax.nn.sigmoid(_dot(hb, w_ref[:, _C_GATE:_C_SB])).T
    sb = _dot(hb, w_ref[:, _C_SB:_C_MERGE])
    sbq_ref[...] = (sb[:, 0:SB_W] * (LOG2_E * SB_HEAD_DIM ** -0.5)).astype(sbq_ref.dtype)
    sbk_t = sb[:, SB_W:2 * SB_W].T
    for c in range(sbk_ref.shape[0]):
        sbk_ref[c] = sbk_t[:, c * sbk_ref.shape[2]:(c + 1) * sbk_ref.shape[2]].astype(sbk_ref.dtype)
    sbv_ref[...] = sb[:, 2 * SB_W:3 * SB_W].astype(sbv_ref.dtype)
    for c in range(2):
        mg_ref[:, c * d_model:(c + 1) * d_model] = jax.nn.sigmoid(
            _dot(hb, w_ref[:, _C_MERGE + c * d_model:_C_MERGE + (c + 1) * d_model])).astype(mg_ref.dtype)


def _inproj(x, g, pos, invf, w_all, layer):
    s, d = x.shape
    t = min(ROW_TILE, s)
    wcols = w_all.shape[2]
    kck = min(NSA_KEY_CHUNK, s)
    sck = min(SB_KEY_CHUNK, s)
    wck = min(NSA_Q_BLOCK, s)
    row = lambda n: pl.BlockSpec((t, n), lambda i: (i, 0))
    col = lambda n: pl.BlockSpec((n, t), lambda i: (0, i))
    slab = lambda n, ck: pl.BlockSpec((t // ck, n, ck), lambda i: (i, 0, 0))
    sds = jax.ShapeDtypeStruct
    out_shape = [
        sds((NSA_W, s), MXU_DTYPE), sds((NSA_W, s), MXU_DTYPE),
        sds((s, HEAD_DIM), F32), sds((s, HEAD_DIM), F32),
        sds((s, LANES), MXU_DTYPE), sds((s // kck, VT_ROWS, kck), MXU_DTYPE),
        sds((s, LANES), MXU_DTYPE), sds((s // wck, VT_ROWS, wck), MXU_DTYPE),
        sds((LANES, s), F32),
        sds((s, SB_W), MXU_DTYPE), sds((s // sck, SB_W, sck), MXU_DTYPE), sds((s, SB_W), MXU_DTYPE),
        sds((s, 2 * d), MXU_DTYPE),
    ]
    out_specs = [col(NSA_W), col(NSA_W), row(HEAD_DIM), row(HEAD_DIM),
                 row(LANES), slab(VT_ROWS, kck), row(LANES), slab(VT_ROWS, wck),
                 col(LANES), row(SB_W), slab(SB_W, sck), row(SB_W), row(2 * d)]
    return pl.pallas_call(
        _inproj_kernel,
        grid=(s // t,),
        in_specs=[row(d),
                  _resident((None, 1, d), lambda i: (layer, 0, 0)),
                  row(1),
                  _resident((1, LANES), lambda i: (0, 0)),
                  _resident((None, d, wcols), lambda i: (layer, 0, 0))],
        out_specs=out_specs,
        out_shape=out_shape,
        compiler_params=_params(1),
        name="inproj",
    )(x, g, pos, invf, w_all)


def _compress_kernel(kc_ref, vc_ref, pe_ref, w1_ref, w2_ref, kcmp_ref, vcmpt_ref):
    nr, half_w = kc_ref.shape
    outs = []
    for kv, r_ref in enumerate((kc_ref, vc_ref)):
        r = r_ref[...]
        ha = _dot((r + pe_ref[kv, 0:1, :]).astype(MXU_DTYPE), w1_ref[kv, 0:half_w, :])
        hb = _dot((r + pe_ref[kv, 1:2, :]).astype(MXU_DTYPE), w1_ref[kv, half_w:2 * half_w, :])
        hid = ha + pltpu.roll(hb, nr - 1, 0)
        outs.append(_dot(jax.nn.gelu(hid).astype(MXU_DTYPE), w2_ref[kv]))
    pad = jnp.zeros_like(outs[0])
    kcmp_ref[...] = jnp.concatenate([outs[0], pad], axis=1).astype(kcmp_ref.dtype)
    vcmpt_ref[...] = jnp.concatenate([outs[1], pad], axis=1).T[0:HEAD_DIM].astype(vcmpt_ref.dtype)


def _compress(kc, vc, pe_all, w1_all, w2_all, layer):
    nr, half_w = kc.shape
    hidden = w1_all.shape[3]
    full = lambda shape: pl.BlockSpec(shape, lambda i: (0,) * len(shape))
    return pl.pallas_call(
        _compress_kernel,
        grid=(1,),
        in_specs=[full((nr, half_w)), full((nr, half_w)),
                  pl.BlockSpec((None, 2, 2, half_w), lambda i: (layer, 0, 0, 0)),
                  pl.BlockSpec((None, 2, 2 * half_w, hidden), lambda i: (layer, 0, 0, 0)),
                  pl.BlockSpec((None, 2, hidden, HEAD_DIM), lambda i: (layer, 0, 0, 0))],
        out_specs=[full((nr, LANES)), full((HEAD_DIM, nr))],
        out_shape=[jax.ShapeDtypeStruct((nr, LANES), MXU_DTYPE), jax.ShapeDtypeStruct((HEAD_DIM, nr), MXU_DTYPE)],
        compiler_params=_params(1),
        name="compress",
    )(kc, vc, pe_all, w1_all, w2_all)


def _nsa_kernel(qpt_ref, qrt_ref, gate_ref, kcmp_ref, vcmpt_ref, ovlt_ref, ks_ref, vst_ref, kw_ref, vwt_ref,
                out_ref, qaug_scr, mask_scr, m_scr, acc_scr, sa_scr, sb_scr, ta_scr, tb_scr, ocmp_scr, owin_scr, imp_scr, *, n_sel):
    tq = qpt_ref.shape[1]
    seq = ks_ref.shape[0]
    ncp = kcmp_ref.shape[0]
    nsp = ovlt_ref.shape[0]
    ck = vst_ref.shape[2]
    wck = vwt_ref.shape[2]
    nh = NSA_HEADS
    q0 = pl.program_id(0) * tq
    t = q0 + lax.broadcasted_iota(jnp.int32, (1, tq), 1)
    cols = [slice(h * tq, (h + 1) * tq) for h in range(nh)]

    def aug(qt_ref, h, tail):
        return jnp.concatenate([qt_ref[h * HEAD_DIM:(h + 1) * HEAD_DIM, :], tail], axis=0)

    zeros_tail = jnp.zeros((HEAD_DIM, tq), MXU_DTYPE)
    def cmp_branch(rows):
        qp_aug = jnp.concatenate([aug(qpt_ref, h, zeros_tail) for h in range(nh)], axis=1)
        sc_all = _dot(kcmp_ref[0:rows, :], qp_aug)
        cmp_last = CMP_STRIDE * lax.broadcasted_iota(jnp.int32, (rows, 1), 0) + (CMP_BLOCK - 1)
        vis_c = cmp_last <= t
        psum = jnp.zeros((rows, tq), F32)
        p_cmp = []
        for h in range(nh):
            sc = jnp.where(vis_c, sc_all[:, cols[h]], -1e30)
            top = jnp.max(sc, axis=0, keepdims=True)
            e = jnp.exp2(sc - jnp.where(top > -1e29, top, 0.0))
            den = jnp.sum(e, axis=0, keepdims=True)
            p = e * (1.0 / jnp.where(den > 0.0, den, 1.0))
            psum = psum + p
            p_cmp.append(p.astype(MXU_DTYPE))
        o_cmp = _dot(vcmpt_ref[:, 0:rows], jnp.concatenate(p_cmp, axis=1))
        p_hi = psum.astype(MXU_DTYPE)
        p_lo = (psum - p_hi.astype(F32)).astype(MXU_DTYPE)
        imp_scr[...] = _dot(ovlt_ref[:, 0:rows], p_hi) + _dot(ovlt_ref[:, 0:rows], p_lo)
        ocmp_scr[...] = o_cmp

    n_var = CMP_VARIANTS if ncp % (CMP_VARIANTS * LANES) == 0 else 1
    step = ncp // n_var
    n_vis = (q0 + tq - CMP_BLOCK) // CMP_STRIDE + 1
    variant = jnp.clip((n_vis + step - 1) // step - 1, 0, n_var - 1)
    for v in range(n_var):
        pl.when(variant == v)(functools.partial(cmp_branch, (v + 1) * step))
    imp = imp_scr[...]

    qr_aug = jnp.concatenate([aug(qrt_ref, h, zeros_tail) for h in range(nh)], axis=1)
    span = min(WINDOW + tq, seq)
    start = pl.multiple_of(jnp.maximum(q0 + tq - span, 0), wck)
    sw_all = _dot(kw_ref[pl.ds(start, span), :], qr_aug)

    def window_finish():
        kpos_w = start + lax.broadcasted_iota(jnp.int32, (span, 1), 0)
        vis_w = (kpos_w <= t) & (kpos_w > t - WINDOW)
        p_win = []
        for h in range(nh):
            sw = jnp.where(vis_w, sw_all[:, cols[h]], -1e30)
            p_win.append(jnp.exp2(sw - jnp.max(sw, axis=0, keepdims=True)).astype(MXU_DTYPE))
        p_win = jnp.concatenate(p_win, axis=1)
        ow = jnp.zeros((VT_ROWS, nh * tq), F32)
        for j in range(span // wck):
            ow = ow + _dot(vwt_ref[start // wck + j], p_win[j * wck:(j + 1) * wck, :])
        owin_scr[...] = ow[0:HEAD_DIM] * (1.0 / ow[HEAD_DIM:HEAD_DIM + 1])

    blk = lax.broadcasted_iota(jnp.int32, (nsp, 1), 0)
    cur = t // SEL_BLOCK
    valid = blk <= cur
    forced = (blk == 0) | (blk == cur) | (blk == cur - 1)
    n_forced = 3
    assert n_sel >= n_forced
    score = jnp.where(valid, jnp.where(forced, -jnp.inf, imp), -1.0)
    chosen = jnp.where(forced & valid, 1.0, 0.0)
    for _ in range(n_sel - n_forced):
        best = jnp.max(score, axis=0, keepdims=True)
        idx = jnp.min(jnp.where(score == best, blk, nsp), axis=0, keepdims=True)
        hit = blk == idx
        chosen = jnp.where(hit, 1.0, chosen)
        score = jnp.where(hit, -jnp.inf, score)
    mask_scr[0:nsp, :] = ((chosen - 1.0) * (-MASKED)).astype(mask_scr.dtype)
    mask_scr[nsp:nsp + SEL_CODE_BLOCKS, :] = jnp.full((SEL_CODE_BLOCKS, tq), MASKED, mask_scr.dtype)

    for h in range(nh):
        qaug_scr[0:HEAD_DIM, cols[h]] = qrt_ref[h * HEAD_DIM:(h + 1) * HEAD_DIM, :]
    m_scr[...] = jnp.full(m_scr.shape, -1e30, F32)
    acc_scr[...] = jnp.zeros(acc_scr.shape, F32)

    c_diag = q0 // ck

    def sel_scores(c, s_ref, top_ref, limit):
        k0 = pl.multiple_of(jnp.minimum(c, c_diag) * ck, ck)
        code0 = jnp.where(c < limit, (k0 // (SEL_BLOCK * SEL_CODE_BLOCKS)) * SEL_CODE_BLOCKS, nsp)
        mrows = mask_scr[pl.ds(pl.multiple_of(code0, SEL_CODE_BLOCKS), SEL_CODE_BLOCKS), :]
        for h in range(nh):
            qaug_scr[HEAD_DIM:2 * HEAD_DIM, cols[h]] = mrows
        s_all = _dot(ks_ref[pl.ds(k0, ck), :], qaug_scr[...])
        s_ref[...] = s_all
        top_ref[...] = jnp.max(s_all, axis=0, keepdims=True)

    def sel_update(c, s_ref, top_ref, causal):
        m_old = m_scr[...]
        if causal:
            kpos = c * ck + lax.broadcasted_iota(jnp.int32, (ck, 1), 0)
            bias = jnp.where(kpos <= t, 0.0, MASKED)
            tops = [jnp.max(s_ref[:, cols[h]] + bias, axis=0, keepdims=True) for h in range(nh)]
            m_new = jnp.maximum(m_old, jnp.concatenate(tops, axis=1))
        else:
            m_new = jnp.maximum(m_old, top_ref[...])
        p_all = []
        for h in range(nh):
            s = s_ref[:, cols[h]]
            if causal:
                s = s + bias
            p_all.append(jnp.exp2(s - m_new[:, cols[h]]).astype(MXU_DTYPE))
        pv = _dot(vst_ref[jnp.minimum(c, c_diag)], jnp.concatenate(p_all, axis=1))
        acc_scr[...] = jnp.exp2(m_old - m_new) * acc_scr[...] + pv
        m_scr[...] = m_new

    def sel_pair(c):
        sel_scores(c + 1, sa_scr, ta_scr, c_diag)
        sel_update(c, sb_scr, tb_scr, False)
        sel_scores(c + 2, sb_scr, tb_scr, c_diag)
        sel_update(c + 1, sa_scr, ta_scr, False)

    def sel_quad(j, carry):
        sel_pair(4 * j)
        sel_pair(4 * j + 2)
        return carry

    sel_scores(c_diag, sa_scr, ta_scr, c_diag + 1)
    sel_scores(0, sb_scr, tb_scr, c_diag)
    window_finish()
    sel_update(c_diag, sa_scr, ta_scr, True)
    n_quads = c_diag // 4
    lax.fori_loop(0, n_quads, sel_quad, 0)
    lax.fori_loop(0, (c_diag - 4 * n_quads + 1) // 2, lambda j, carry: (sel_pair(4 * n_quads + 2 * j), carry)[1], 0)
    acc = acc_scr[...]
    o_sel = acc[0:HEAD_DIM] * (1.0 / acc[HEAD_DIM:HEAD_DIM + 1])

    o_cmp = ocmp_scr[...]
    o_win = owin_scr[...]
    gates = gate_ref[...]
    merged = []
    for h in range(nh):
        merged.append(gates[3 * h:3 * h + 1, :] * o_cmp[:, cols[h]]
                      + gates[3 * h + 1:3 * h + 2, :] * o_sel[:, cols[h]]
                      + gates[3 * h + 2:3 * h + 3, :] * o_win[:, cols[h]])
    out_ref[...] = jnp.concatenate(merged, axis=0).T.astype(out_ref.dtype)


def _nsa(qpt, qrt, gates_t, kcmp, vcmpt, ovlt, ks, vst, kw, vwt):
    s = ks.shape[0]
    tq = min(NSA_Q_BLOCK, s)
    nsp = ovlt.shape[0]
    col = lambda n: pl.BlockSpec((n, tq), lambda i: (0, i))
    res = lambda a: _resident(a.shape, lambda i: (0,) * a.ndim)
    return pl.pallas_call(
        functools.partial(_nsa_kernel, n_sel=min(SEL_TOP_N, s // SEL_BLOCK)),
        grid=(s // tq,),
        in_specs=[col(NSA_W), col(NSA_W), col(LANES), res(kcmp), res(vcmpt), res(ovlt),
                  res(ks), res(vst), res(kw), res(vwt)],
        out_specs=pl.BlockSpec((tq, NSA_W), lambda i: (i, 0)),
        out_shape=jax.ShapeDtypeStruct((s, NSA_W), MXU_DTYPE),
        scratch_shapes=[pltpu.VMEM((LANES, NSA_HEADS * tq), MXU_DTYPE),
                        pltpu.VMEM((nsp + SEL_CODE_BLOCKS, tq), MXU_DTYPE),
                        pltpu.VMEM((1, NSA_HEADS * tq), F32), pltpu.VMEM((VT_ROWS, NSA_HEADS * tq), F32),
                        pltpu.VMEM((vst.shape[2], NSA_HEADS * tq), F32),
                        pltpu.VMEM((vst.shape[2], NSA_HEADS * tq), F32),
                        pltpu.VMEM((1, NSA_HEADS * tq), F32), pltpu.VMEM((1, NSA_HEADS * tq), F32),
                        pltpu.VMEM((HEAD_DIM, NSA_HEADS * tq), F32), pltpu.VMEM((HEAD_DIM, NSA_HEADS * tq), F32),
                        pltpu.VMEM((nsp, tq), F32)],
        compiler_params=_params(1),
        name="nsa",
    )(qpt, qrt, gates_t, kcmp, vcmpt, ovlt, ks, vst, kw, vwt)


def _softplus2(z2):
    neg_abs = lax.bitcast_convert_type(lax.bitcast_convert_type(z2, jnp.uint32) | jnp.uint32(0x80000000), F32)
    return jnp.maximum(z2, 0.0) + jnp.log2(1.0 + jnp.exp2(neg_abs))


def _sb_kernel(q_ref, k_ref, v_ref, out_ref, acc_scr, run_scr, za_scr, zb_scr):
    assert (q_ref.shape[0] // k_ref.shape[2]) % 2 == 0
    tb = q_ref.shape[0]
    ck = k_ref.shape[2]
    nsub = tb // ck
    i = pl.program_id(0)
    r = lax.broadcasted_iota(jnp.int32, (ck, ck), 0)
    c = lax.broadcasted_iota(jnp.int32, (ck, ck), 1)
    tri = jnp.where(r >= c, 1.0, 0.0).astype(MXU_DTYPE)
    before = c < r
    heads = [slice(h * SB_HEAD_DIM, (h + 1) * SB_HEAD_DIM) for h in range(SB_HEADS)]

    def logits(h, rows, chunk):
        return _dot(q_ref[rows, heads[h]], k_ref[chunk, heads[h], :])

    def step(h, rows, chunk, diag, first, z=None):
        hs = heads[h]
        if z is None:
            z = logits(h, rows, chunk)
        sp = _softplus2(z)
        if diag:
            sp = jnp.where(before, sp, 0.0)
        cs = _dot(sp.astype(MXU_DTYPE), tri)
        own = jnp.minimum(z - cs, 0.0)
        if first:
            a = jnp.exp2(own)
            run_scr[h, rows] = cs[:, 0:1]
        else:
            run = run_scr[h, rows]
            a = jnp.exp2(own - run)
            run_scr[h, rows] = run + cs[:, 0:1]
        if diag:
            a = jnp.where(before, a, 0.0)
        pv = _dot(a.astype(MXU_DTYPE), v_ref[pl.ds(pl.multiple_of(chunk * ck, ck), ck), hs])
        if first:
            acc_scr[h, rows] = pv
        else:
            acc_scr[h, rows] += pv

    for g in range(nsub):
        rows = slice(g * ck, (g + 1) * ck)
        for h in range(SB_HEADS):
            step(h, rows, i * nsub + g, True, True)
        for back in range(g):
            for h in range(SB_HEADS):
                step(h, rows, i * nsub + g - 1 - back, False, False)

    everything = slice(0, tb)

    def lookahead(chunk, z_ref):
        for h in range(SB_HEADS):
            z_ref[h] = logits(h, everything, jnp.maximum(chunk, 0))

    def pair(state):
        j, _ = state
        cur = i * nsub - 1 - 2 * j
        lookahead(cur - 1, zb_scr)
        for h in range(SB_HEADS):
            step(h, everything, cur, False, False, za_scr[h])
        lookahead(cur - 2, za_scr)
        for h in range(SB_HEADS):
            step(h, everything, cur - 1, False, False, zb_scr[h])
        least = functools.reduce(jnp.minimum, [run_scr[h] for h in range(SB_HEADS)])
        return j + 1, (jnp.min(least) >= SB_EXHAUSTED_LOG2).astype(jnp.int32)

    lookahead(i * nsub - 1, za_scr)
    lax.while_loop(lambda state: (state[0] < (i * nsub) // 2) & (state[1] == 0), pair,
                   (jnp.int32(0), jnp.int32(0)))
    for h, hs in enumerate(heads):
        out_ref[:, hs] = acc_scr[h].astype(out_ref.dtype)


def _sb(q, kt, v):
    s = q.shape[0]
    tb = min(SB_BLOCK, s)
    return pl.pallas_call(
        _sb_kernel,
        grid=(s // tb,),
        in_specs=[pl.BlockSpec((tb, SB_W), lambda i: (i, 0)),
                  _resident(kt.shape, lambda i: (0, 0, 0)),
                  _resident((s, SB_W), lambda i: (0, 0))],
        out_specs=pl.BlockSpec((tb, SB_W), lambda i: (i, 0)),
        out_shape=jax.ShapeDtypeStruct((s, SB_W), MXU_DTYPE),
        scratch_shapes=[pltpu.VMEM((SB_HEADS, tb, SB_HEAD_DIM), F32), pltpu.VMEM((SB_HEADS, tb, 1), F32),
                        pltpu.VMEM((SB_HEADS, tb, kt.shape[2]), F32), pltpu.VMEM((SB_HEADS, tb, kt.shape[2]), F32)],
        compiler_params=_params(1),
        name="sb",
    )(q, kt, v)


def _mix_kernel(x_ref, nsa_ref, sb_ref, mg_ref, wn_ref, ws_ref, wo_ref, g_ref, out_ref):
    d = x_ref.shape[1]
    y_nsa = _dot(nsa_ref[...], wn_ref[...])
    y_sb = _dot(sb_ref[...], ws_ref[...])
    merged = mg_ref[:, 0:d].astype(F32) * y_nsa + mg_ref[:, d:2 * d].astype(F32) * y_sb
    mixed = _dot(merged.astype(MXU_DTYPE), wo_ref[...])
    out_ref[...] = x_ref[...] + _rms(mixed, g_ref[...])


def _mix(x, nsa_o, sb_o, mg, wn_all, ws_all, wo_all, g_all, layer):
    s, d = x.shape
    t = min(ROW_TILE, s)
    row = lambda n: pl.BlockSpec((t, n), lambda i: (i, 0))
    lay = lambda a: _resident((None,) + a.shape[1:], lambda i: (layer, 0, 0))
    return pl.pallas_call(
        _mix_kernel,
        grid=(s // t,),
        in_specs=[row(d), row(NSA_W), row(SB_W), row(2 * d), lay(wn_all), lay(ws_all), lay(wo_all), lay(g_all)],
        out_specs=row(d),
        out_shape=jax.ShapeDtypeStruct((s, d), F32),
        compiler_params=_params(1),
        name="mix",
    )(x, nsa_o, sb_o, mg, wn_all, ws_all, wo_all, g_all)


def _ffn_kernel(x_ref, gin_ref, w1_ref, w2_ref, gout_ref, out_ref):
    x = x_ref[...]
    d = x.shape[1]
    hb = _rms(x, gin_ref[...]).astype(MXU_DTYPE)
    ff = jnp.zeros_like(x)
    for c in range(w1_ref.shape[1] // d):
        up = _dot(hb, w1_ref[:, c * d:(c + 1) * d])
        ff = ff + _dot(jnp.square(jnp.maximum(up, 0.0)).astype(MXU_DTYPE), w2_ref[c * d:(c + 1) * d, :])
    out_ref[...] = x + _rms(ff, gout_ref[...])


def _ffn(x, gin_all, w1_all, w2_all, gout_all, layer):
    s, d = x.shape
    t = min(ROW_TILE, s)
    row = lambda n: pl.BlockSpec((t, n), lambda i: (i, 0))
    lay = lambda a: _resident((None,) + a.shape[1:], lambda i: (layer, 0, 0))
    return pl.pallas_call(
        _ffn_kernel,
        grid=(s // t,),
        in_specs=[row(d), lay(gin_all), lay(w1_all), lay(w2_all), lay(gout_all)],
        out_specs=row(d),
        out_shape=jax.ShapeDtypeStruct((s, d), F32),
        compiler_params=_params(1),
        name="ffn",
    )(x, gin_all, w1_all, w2_all, gout_all)


def _regroup_w_in(w_in):
    gate_lo, gate_hi = _C_GATE, _C_GATE + 3 * NSA_HEADS
    pad = jnp.zeros(w_in.shape[:2] + (LANES - 3 * NSA_HEADS,), w_in.dtype)
    return jnp.concatenate([w_in[..., :gate_lo], w_in[..., gate_lo:gate_hi], pad, w_in[..., gate_hi:]],
                           axis=-1).astype(MXU_DTYPE)


def kernel(x, positions, norm_g, w_in, cmp_pe, cmp_w1, cmp_w2, w_nsa_o, w_sb_o, w_out, w_ff1, w_ff2):
    b, s, d = x.shape
    depth = w_in.shape[0]
    ncp, ns = s // CMP_STRIDE, s // SEL_BLOCK
    nsp = -(-ns // LANES) * LANES
    half_w = CMP_STRIDE * HEAD_DIM

    w_in_r = _regroup_w_in(w_in)
    pe = cmp_pe.reshape(depth, 2, 2, half_w)
    w1 = cmp_w1.astype(MXU_DTYPE)
    w2 = cmp_w2.astype(MXU_DTYPE)
    wn, ws, wo = w_nsa_o.astype(MXU_DTYPE), w_sb_o.astype(MXU_DTYPE), w_out.astype(MXU_DTYPE)
    wf1, wf2 = w_ff1.astype(MXU_DTYPE), w_ff2.astype(MXU_DTYPE)
    g_pre, g_mix, g_ffn_in, g_ffn_out = (norm_g[:, n][:, None, :] for n in range(4))

    dim = jnp.arange(LANES) % HEAD_DIM
    half = ROT_DIM // 2
    inv_freq = jnp.power(ROPE_THETA, (dim % half).astype(F32) * (-2.0 / ROT_DIM))
    invf = jnp.where(dim < ROT_DIM, inv_freq, 0.0)[None, :].astype(F32)
    c_start = CMP_STRIDE * jnp.arange(ncp)[None, :]
    s_start = SEL_BLOCK * jnp.arange(nsp)[:, None]
    ovlt = ((c_start < s_start + SEL_BLOCK) & (c_start + CMP_BLOCK > s_start) & (s_start < s)).astype(MXU_DTYPE)

    outs = []
    for bi in range(b):
        xb = x[bi]
        pos = positions[bi][:, None]
        for layer in range(depth):
            (qpt, qrt, kc, vc, ks, vst, kw, vwt, gates_t, sbq, sbkt, sbv, mg) = _inproj(
                xb, g_pre, pos, invf, w_in_r, layer)
            kcmp, vcmpt = _compress(kc.reshape(ncp, half_w), vc.reshape(ncp, half_w), pe, w1, w2, layer)
            nsa_o = _nsa(qpt, qrt, gates_t, kcmp, vcmpt, ovlt, ks, vst, kw, vwt)
            sb_o = _sb(sbq, sbkt, sbv)
            xb = _mix(xb, nsa_o, sb_o, mg, wn, ws, wo, g_mix, layer)
            xb = _ffn(xb, g_ffn_in, wf1, wf2, g_ffn_out, layer)
        outs.append(xb)
    return jnp.stack(outs, axis=0)
```

```python
import functools

import jax
import jax.numpy as jnp
from jax import lax
from jax.experimental import pallas as pl
from jax.experimental.pallas import tpu as pltpu

F32 = jnp.float32
MXU_DTYPE = jnp.bfloat16

HEAD_DIM = 64
NSA_HEADS = 8
SB_HEADS = 4
SB_HEAD_DIM = 128
ROPE_THETA = 500000.0
ROT_DIM = HEAD_DIM // 4
CMP_BLOCK = 32
CMP_STRIDE = 16
SEL_BLOCK = 64
SEL_TOP_N = 8
WINDOW = 512
RMS_EPS = 1e-6
NSA_W = NSA_HEADS * HEAD_DIM
SB_W = SB_HEADS * SB_HEAD_DIM
LANES = 128
VT_ROWS = HEAD_DIM + 16
MASKED = -32768.0
LOG2_E = 1.4426950408889634
VMEM_LIMIT = 56 * 1024 * 1024

ROW_TILE = 1024
NSA_Q_BLOCK = 256
NSA_KEY_CHUNK = 512
SEL_CODE_BLOCKS = HEAD_DIM
CMP_VARIANTS = 4
SB_BLOCK = 512
SB_KEY_CHUNK = 256
SB_EXHAUSTED_LOG2 = 160.0


def _dot(a, b):
    return jnp.dot(a, b, preferred_element_type=F32)


def _rms(x, g):
    return x * lax.rsqrt(jnp.mean(x * x, axis=-1, keepdims=True) + RMS_EPS) * g


def _params(n_grid_dims):
    return pltpu.CompilerParams(dimension_semantics=("arbitrary",) * n_grid_dims,
                                vmem_limit_bytes=VMEM_LIMIT)


def _resident(block_shape, index_map):
    return pl.BlockSpec(block_shape, index_map, pipeline_mode=pl.Buffered(1))


_C_KV = NSA_W
_C_GATE = _C_KV + 6 * HEAD_DIM
_C_SB = _C_GATE + LANES
_C_MERGE = _C_SB + 3 * SB_W


def _inproj_kernel(x_ref, g_ref, pos_ref, invf_ref, w_ref,
                   qpt_ref, qrt_ref, kc_ref, vc_ref, ks_ref, vst_ref, kw_ref, vwt_ref,
                   gate_ref, sbq_ref, sbk_ref, sbv_ref, mg_ref):
    t, d_model = x_ref.shape
    hb = _rms(x_ref[...], g_ref[...]).astype(MXU_DTYPE)

    ang = pos_ref[...].astype(F32) * invf_ref[...]
    cos, sin = jnp.cos(ang), jnp.sin(ang)
    lane = lax.broadcasted_iota(jnp.int32, (1, LANES), 1)
    dim = lane % HEAD_DIM
    half = ROT_DIM // 2
    sin_up = jnp.where((dim >= half) & (dim < ROT_DIM), sin, 0.0)
    sin_dn = jnp.where(dim < half, -sin, 0.0)
    low = lane < HEAD_DIM

    def rope(xg):
        return xg * cos + pltpu.roll(xg, half, 1) * sin_up + pltpu.roll(xg, LANES - half, 1) * sin_dn

    pa = _dot(hb, w_ref[:, 0:_C_GATE])
    scale = LOG2_E * HEAD_DIM ** -0.5
    for j in range(NSA_W // LANES):
        qg = pa[:, j * LANES:(j + 1) * LANES]
        qpt_ref[j * LANES:(j + 1) * LANES, :] = (qg * scale).T.astype(qpt_ref.dtype)
        qrt_ref[j * LANES:(j + 1) * LANES, :] = (rope(qg) * scale).T.astype(qrt_ref.dtype)
    kc_ref[...] = pa[:, _C_KV:_C_KV + HEAD_DIM]
    vc_ref[...] = pa[:, _C_KV + HEAD_DIM:_C_KV + 2 * HEAD_DIM]

    row = pl.program_id(0) * t + lax.broadcasted_iota(jnp.int32, (t, 1), 0)
    code = jnp.where(lane - HEAD_DIM == (row // SEL_BLOCK) % SEL_CODE_BLOCKS, 1.0, 0.0)
    ksg = pa[:, _C_KV + 2 * HEAD_DIM:_C_KV + 4 * HEAD_DIM]
    ks_ref[...] = jnp.where(low, rope(ksg), code).astype(ks_ref.dtype)
    vs_t = jnp.where(low, pltpu.roll(ksg, HEAD_DIM, 1), 1.0).T[0:VT_ROWS]
    for c in range(vst_ref.shape[0]):
        vst_ref[c] = vs_t[:, c * vst_ref.shape[2]:(c + 1) * vst_ref.shape[2]].astype(vst_ref.dtype)
    kwg = pa[:, _C_KV + 4 * HEAD_DIM:_C_KV + 6 * HEAD_DIM]
    kw_ref[...] = jnp.where(low, rope(kwg), 0.0).astype(kw_ref.dtype)
    vw_t = jnp.where(low, pltpu.roll(kwg, HEAD_DIM, 1), 1.0).T[0:VT_ROWS]
    for c in range(vwt_ref.shape[0]):
        vwt_ref[c] = vw_t[:, c * vwt_ref.shape[2]:(c + 1) * vwt_ref.shape[2]].astype(vwt_ref.dtype)

    gate_ref[...] = jax.nn.sigmoid(_dot(hb, w_ref[:, _C_GATE:_C_SB])).T
    sb = _dot(hb, w_ref[:, _C_SB:_C_MERGE])
    sbq_ref[...] = (sb[:, 0:SB_W] * (LOG2_E * SB_HEAD_DIM ** -0.5)).astype(sbq_ref.dtype)
    sbk_t = sb[:, SB_W:2 * SB_W].T
    for c in range(sbk_ref.shape[0]):
        sbk_ref[c] = sbk_t[:, c * sbk_ref.shape[2]:(c + 1) * sbk_ref.shape[2]].astype(sbk_ref.dtype)
    sbv_ref[...] = sb[:, 2 * SB_W:3 * SB_W].astype(sbv_ref.dtype)
    for c in range(2):
        mg_ref[:, c * d_model:(c + 1) * d_model] = jax.nn.sigmoid(
            _dot(hb, w_ref[:, _C_MERGE + c * d_model:_C_MERGE + (c + 1) * d_model])).astype(mg_ref.dtype)


def _inproj(x, g, pos, invf, w_all, layer):
    s, d = x.shape
    t = min(ROW_TILE, s)
    wcols = w_all.shape[2]
    kck = min(NSA_KEY_CHUNK, s)
    sck = min(SB_KEY_CHUNK, s)
    wck = min(NSA_Q_BLOCK, s)
    row = lambda n: pl.BlockSpec((t, n), lambda i: (i, 0))
    col = lambda n: pl.BlockSpec((n, t), lambda i: (0, i))
    slab = lambda n, ck: pl.BlockSpec((t // ck, n, ck), lambda i: (i, 0, 0))
    sds = jax.ShapeDtypeStruct
    out_shape = [
        sds((NSA_W, s), MXU_DTYPE), sds((NSA_W, s), MXU_DTYPE),
        sds((s, HEAD_DIM), F32), sds((s, HEAD_DIM), F32),
        sds((s, LANES), MXU_DTYPE), sds((s // kck, VT_ROWS, kck), MXU_DTYPE),
        sds((s, LANES), MXU_DTYPE), sds((s // wck, VT_ROWS, wck), MXU_DTYPE),
        sds((LANES, s), F32),
        sds((s, SB_W), MXU_DTYPE), sds((s // sck, SB_W, sck), MXU_DTYPE), sds((s, SB_W), MXU_DTYPE),
        sds((s, 2 * d), MXU_DTYPE),
    ]
    out_specs = [col(NSA_W), col(NSA_W), row(HEAD_DIM), row(HEAD_DIM),
                 row(LANES), slab(VT_ROWS, kck), row(LANES), slab(VT_ROWS, wck),
                 col(LANES), row(SB_W), slab(SB_W, sck), row(SB_W), row(2 * d)]
    return pl.pallas_call(
        _inproj_kernel,
        grid=(s // t,),
        in_specs=[row(d),
                  _resident((None, 1, d), lambda i: (layer, 0, 0)),
                  row(1),
                  _resident((1, LANES), lambda i: (0, 0)),
                  _resident((None, d, wcols), lambda i: (layer, 0, 0))],
        out_specs=out_specs,
        out_shape=out_shape,
        compiler_params=_params(1),
        name="inproj",
    )(x, g, pos, invf, w_all)


def _compress_kernel(kc_ref, vc_ref, pe_ref, w1_ref, w2_ref, kcmp_ref, vcmpt_ref):
    nr, half_w = kc_ref.shape
    outs = []
    for kv, r_ref in enumerate((kc_ref, vc_ref)):
        r = r_ref[...]
        ha = _dot((r + pe_ref[kv, 0:1, :]).astype(MXU_DTYPE), w1_ref[kv, 0:half_w, :])
        hb = _dot((r + pe_ref[kv, 1:2, :]).astype(MXU_DTYPE), w1_ref[kv, half_w:2 * half_w, :])
        hid = ha + pltpu.roll(hb, nr - 1, 0)
        outs.append(_dot(jax.nn.gelu(hid).astype(MXU_DTYPE), w2_ref[kv]))
    pad = jnp.zeros_like(outs[0])
    kcmp_ref[...] = jnp.concatenate([outs[0], pad], axis=1).astype(kcmp_ref.dtype)
    vcmpt_ref[...] = jnp.concatenate([outs[1], pad], axis=1).T[0:HEAD_DIM].astype(vcmpt_ref.dtype)


def _compress(kc, vc, pe_all, w1_all, w2_all, layer):
    nr, half_w = kc.shape
    hidden = w1_all.shape[3]
    full = lambda shape: pl.BlockSpec(shape, lambda i: (0,) * len(shape))
    return pl.pallas_call(
        _compress_kernel,
        grid=(1,),
        in_specs=[full((nr, half_w)), full((nr, half_w)),
                  pl.BlockSpec((None, 2, 2, half_w), lambda i: (layer, 0, 0, 0)),
                  pl.BlockSpec((None, 2, 2 * half_w, hidden), lambda i: (layer, 0, 0, 0)),
                  pl.BlockSpec((None, 2, hidden, HEAD_DIM), lambda i: (layer, 0, 0, 0))],
        out_specs=[full((nr, LANES)), full((HEAD_DIM, nr))],
        out_shape=[jax.ShapeDtypeStruct((nr, LANES), MXU_DTYPE), jax.ShapeDtypeStruct((HEAD_DIM, nr), MXU_DTYPE)],
        compiler_params=_params(1),
        name="compress",
    )(kc, vc, pe_all, w1_all, w2_all)


def _nsa_kernel(qpt_ref, qrt_ref, gate_ref, kcmp_ref, vcmpt_ref, ovlt_ref, ks_ref, vst_ref, kw_ref, vwt_ref,
                out_ref, qaug_scr, mask_scr, m_scr, acc_scr, sa_scr, sb_scr, ta_scr, tb_scr, ocmp_scr, owin_scr, imp_scr, *, n_sel):
    tq = qpt_ref.shape[1]
    seq = ks_ref.shape[0]
    ncp = kcmp_ref.shape[0]
    nsp = ovlt_ref.shape[0]
    ck = vst_ref.shape[2]
    wck = vwt_ref.shape[2]
    nh = NSA_HEADS
    q0 = pl.program_id(0) * tq
    t = q0 + lax.broadcasted_iota(jnp.int32, (1, tq), 1)
    cols = [slice(h * tq, (h + 1) * tq) for h in range(nh)]

    def aug(qt_ref, h, tail):
        return jnp.concatenate([qt_ref[h * HEAD_DIM:(h + 1) * HEAD_DIM, :], tail], axis=0)

    zeros_tail = jnp.zeros((HEAD_DIM, tq), MXU_DTYPE)
    def cmp_branch(rows):
        qp_aug = jnp.concatenate([aug(qpt_ref, h, zeros_tail) for h in range(nh)], axis=1)
        sc_all = _dot(kcmp_ref[0:rows, :], qp_aug)
        cmp_last = CMP_STRIDE * lax.broadcasted_iota(jnp.int32, (rows, 1), 0) + (CMP_BLOCK - 1)
        vis_c = cmp_last <= t
        psum = jnp.zeros((rows, tq), F32)
        p_cmp = []
        for h in range(nh):
            sc = jnp.where(vis_c, sc_all[:, cols[h]], -1e30)
            top = jnp.max(sc, axis=0, keepdims=True)
            e = jnp.exp2(sc - jnp.where(top > -1e29, top, 0.0))
            den = jnp.sum(e, axis=0, keepdims=True)
            p = e * (1.0 / jnp.where(den > 0.0, den, 1.0))
            psum = psum + p
            p_cmp.append(p.astype(MXU_DTYPE))
        o_cmp = _dot(vcmpt_ref[:, 0:rows], jnp.concatenate(p_cmp, axis=1))
        p_hi = psum.astype(MXU_DTYPE)
        p_lo = (psum - p_hi.astype(F32)).astype(MXU_DTYPE)
        imp_scr[...] = _dot(ovlt_ref[:, 0:rows], p_hi) + _dot(ovlt_ref[:, 0:rows], p_lo)
        ocmp_scr[...] = o_cmp

    n_var = CMP_VARIANTS if ncp % (CMP_VARIANTS * LANES) == 0 else 1
    step = ncp // n_var
    n_vis = (q0 + tq - CMP_BLOCK) // CMP_STRIDE + 1
    variant = jnp.clip((n_vis + step - 1) // step - 1, 0, n_var - 1)
    for v in range(n_var):
        pl.when(variant == v)(functools.partial(cmp_branch, (v + 1) * step))
    imp = imp_scr[...]

    qr_aug = jnp.concatenate([aug(qrt_ref, h, zeros_tail) for h in range(nh)], axis=1)
    span = min(WINDOW + tq, seq)
    start = pl.multiple_of(jnp.maximum(q0 + tq - span, 0), wck)
    sw_all = _dot(kw_ref[pl.ds(start, span), :], qr_aug)

    def window_finish():
        kpos_w = start + lax.broadcasted_iota(jnp.int32, (span, 1), 0)
        vis_w = (kpos_w <= t) & (kpos_w > t - WINDOW)
        p_win = []
        for h in range(nh):
            sw = jnp.where(vis_w, sw_all[:, cols[h]], -1e30)
            p_win.append(jnp.exp2(sw - jnp.max(sw, axis=0, keepdims=True)).astype(MXU_DTYPE))
        p_win = jnp.concatenate(p_win, axis=1)
        ow = jnp.zeros((VT_ROWS, nh * tq), F32)
        for j in range(span // wck):
            ow = ow + _dot(vwt_ref[start // wck + j], p_win[j * wck:(j + 1) * wck, :])
        owin_scr[...] = ow[0:HEAD_DIM] * (1.0 / ow[HEAD_DIM:HEAD_DIM + 1])

    blk = lax.broadcasted_iota(jnp.int32, (nsp, 1), 0)
    cur = t // SEL_BLOCK
    valid = blk <= cur
    forced = (blk == 0) | (blk == cur) | (blk == cur - 1)
    n_forced = 3
    assert n_sel >= n_forced
    score = jnp.where(valid, jnp.where(forced, -jnp.inf, imp), -1.0)
    chosen = jnp.where(forced & valid, 1.0, 0.0)
    for _ in range(n_sel - n_forced):
        best = jnp.max(score, axis=0, keepdims=True)
        idx = jnp.min(jnp.where(score == best, blk, nsp), axis=0, keepdims=True)
        hit = blk == idx
        chosen = jnp.where(hit, 1.0, chosen)
        score = jnp.where(hit, -jnp.inf, score)
    mask_scr[0:nsp, :] = ((chosen - 1.0) * (-MASKED)).astype(mask_scr.dtype)
    mask_scr[nsp:nsp + SEL_CODE_BLOCKS, :] = jnp.full((SEL_CODE_BLOCKS, tq), MASKED, mask_scr.dtype)

    for h in range(nh):
        qaug_scr[0:HEAD_DIM, cols[h]] = qrt_ref[h * HEAD_DIM:(h + 1) * HEAD_DIM, :]
    m_scr[...] = jnp.full(m_scr.shape, -1e30, F32)
    acc_scr[...] = jnp.zeros(acc_scr.shape, F32)

    c_diag = q0 // ck

    def sel_scores(c, s_ref, top_ref, limit):
        k0 = pl.multiple_of(jnp.minimum(c, c_diag) * ck, ck)
        code0 = jnp.where(c < limit, (k0 // (SEL_BLOCK * SEL_CODE_BLOCKS)) * SEL_CODE_BLOCKS, nsp)
        mrows = mask_scr[pl.ds(pl.multiple_of(code0, SEL_CODE_BLOCKS), SEL_CODE_BLOCKS), :]
        for h in range(nh):
            qaug_scr[HEAD_DIM:2 * HEAD_DIM, cols[h]] = mrows
        s_all = _dot(ks_ref[pl.ds(k0, ck), :], qaug_scr[...])
        s_ref[...] = s_all
        top_ref[...] = jnp.max(s_all, axis=0, keepdims=True)

    def sel_update(c, s_ref, top_ref, causal):
        m_old = m_scr[...]
        if causal:
            kpos = c * ck + lax.broadcasted_iota(jnp.int32, (ck, 1), 0)
            bias = jnp.where(kpos <= t, 0.0, MASKED)
            tops = [jnp.max(s_ref[:, cols[h]] + bias, axis=0, keepdims=True) for h in range(nh)]
            m_new = jnp.maximum(m_old, jnp.concatenate(tops, axis=1))
        else:
            m_new = jnp.maximum(m_old, top_ref[...])
        p_all = []
        for h in range(nh):
            s = s_ref[:, cols[h]]
            if causal:
                s = s + bias
            p_all.append(jnp.exp2(s - m_new[:, cols[h]]).astype(MXU_DTYPE))
        pv = _dot(vst_ref[jnp.minimum(c, c_diag)], jnp.concatenate(p_all, axis=1))
        acc_scr[...] = jnp.exp2(m_old - m_new) * acc_scr[...] + pv
        m_scr[...] = m_new

    def sel_pair(c):
        sel_scores(c + 1, sa_scr, ta_scr, c_diag)
        sel_update(c, sb_scr, tb_scr, False)
        sel_scores(c + 2, sb_scr, tb_scr, c_diag)
        sel_update(c + 1, sa_scr, ta_scr, False)

    def sel_quad(j, carry):
        sel_pair(4 * j)
        sel_pair(4 * j + 2)
        return carry

    sel_scores(c_diag, sa_scr, ta_scr, c_diag + 1)
    sel_scores(0, sb_scr, tb_scr, c_diag)
    window_finish()
    sel_update(c_diag, sa_scr, ta_scr, True)
    n_quads = c_diag // 4
    lax.fori_loop(0, n_quads, sel_quad, 0)
    lax.fori_loop(0, (c_diag - 4 * n_quads + 1) // 2, lambda j, carry: (sel_pair(4 * n_quads + 2 * j), carry)[1], 0)
    acc = acc_scr[...]
    o_sel = acc[0:HEAD_DIM] * (1.0 / acc[HEAD_DIM:HEAD_DIM + 1])

    o_cmp = ocmp_scr[...]
    o_win = owin_scr[...]
    gates = gate_ref[...]
    merged = []
    for h in range(nh):
        merged.append(gates[3 * h:3 * h + 1, :] * o_cmp[:, cols[h]]
                      + gates[3 * h + 1:3 * h + 2, :] * o_sel[:, cols[h]]
                      + gates[3 * h + 2:3 * h + 3, :] * o_win[:, cols[h]])
    out_ref[...] = jnp.concatenate(merged, axis=0).T.astype(out_ref.dtype)


def _nsa(qpt, qrt, gates_t, kcmp, vcmpt, ovlt, ks, vst, kw, vwt):
    s = ks.shape[0]
    tq = min(NSA_Q_BLOCK, s)
    nsp = ovlt.shape[0]
    col = lambda n: pl.BlockSpec((n, tq), lambda i: (0, i))
    res = lambda a: _resident(a.shape, lambda i: (0,) * a.ndim)
    return pl.pallas_call(
        functools.partial(_nsa_kernel, n_sel=min(SEL_TOP_N, s // SEL_BLOCK)),
        grid=(s // tq,),
        in_specs=[col(NSA_W), col(NSA_W), col(LANES), res(kcmp), res(vcmpt), res(ovlt),
                  res(ks), res(vst), res(kw), res(vwt)],
        out_specs=pl.BlockSpec((tq, NSA_W), lambda i: (i, 0)),
        out_shape=jax.ShapeDtypeStruct((s, NSA_W), MXU_DTYPE),
        scratch_shapes=[pltpu.VMEM((LANES, NSA_HEADS * tq), MXU_DTYPE),
                        pltpu.VMEM((nsp + SEL_CODE_BLOCKS, tq), MXU_DTYPE),
                        pltpu.VMEM((1, NSA_HEADS * tq), F32), pltpu.VMEM((VT_ROWS, NSA_HEADS * tq), F32),
                        pltpu.VMEM((vst.shape[2], NSA_HEADS * tq), F32),
                        pltpu.VMEM((vst.shape[2], NSA_HEADS * tq), F32),
                        pltpu.VMEM((1, NSA_HEADS * tq), F32), pltpu.VMEM((1, NSA_HEADS * tq), F32),
                        pltpu.VMEM((HEAD_DIM, NSA_HEADS * tq), F32), pltpu.VMEM((HEAD_DIM, NSA_HEADS * tq), F32),
                        pltpu.VMEM((nsp, tq), F32)],
        compiler_params=_params(1),
        name="nsa",
    )(qpt, qrt, gates_t, kcmp, vcmpt, ovlt, ks, vst, kw, vwt)


def _softplus2(z2):
    neg_abs = lax.bitcast_convert_type(lax.bitcast_convert_type(z2, jnp.uint32) | jnp.uint32(0x80000000), F32)
    return jnp.maximum(z2, 0.0) + jnp.log2(1.0 + jnp.exp2(neg_abs))


def _sb_kernel(q_ref, k_ref, v_ref, out_ref, acc_scr, run_scr, za_scr, zb_scr):
    assert (q_ref.shape[0] // k_ref.shape[2]) % 2 == 0
    tb = q_ref.shape[0]
    ck = k_ref.shape[2]
    nsub = tb // ck
    i = pl.program_id(0)
    r = lax.broadcasted_iota(jnp.int32, (ck, ck), 0)
    c = lax.broadcasted_iota(jnp.int32, (ck, ck), 1)
    tri = jnp.where(r >= c, 1.0, 0.0).astype(MXU_DTYPE)
    before = c < r
    heads = [slice(h * SB_HEAD_DIM, (h + 1) * SB_HEAD_DIM) for h in range(SB_HEADS)]

    def logits(h, rows, chunk):
        return _dot(q_ref[rows, heads[h]], k_ref[chunk, heads[h], :])

    def step(h, rows, chunk, diag, first, z=None):
        hs = heads[h]
        if z is None:
            z = logits(h, rows, chunk)
        sp = _softplus2(z)
        if diag:
            sp = jnp.where(before, sp, 0.0)
        cs = _dot(sp.astype(MXU_DTYPE), tri)
        own = jnp.minimum(z - cs, 0.0)
        if first:
            a = jnp.exp2(own)
            run_scr[h, rows] = cs[:, 0:1]
        else:
            run = run_scr[h, rows]
            a = jnp.exp2(own - run)
            run_scr[h, rows] = run + cs[:, 0:1]
        if diag:
            a = jnp.where(before, a, 0.0)
        pv = _dot(a.astype(MXU_DTYPE), v_ref[pl.ds(pl.multiple_of(chunk * ck, ck), ck), hs])
        if first:
            acc_scr[h, rows] = pv
        else:
            acc_scr[h, rows] += pv

    for g in range(nsub):
        rows = slice(g * ck, (g + 1) * ck)
        for h in range(SB_HEADS):
            step(h, rows, i * nsub + g, True, True)
        for back in range(g):
            for h in range(SB_HEADS):
                step(h, rows, i * nsub + g - 1 - back, False, False)

    everything = slice(0, tb)

    def lookahead(chunk, z_ref):
        for h in range(SB_HEADS):
            z_ref[h] = logits(h, everything, jnp.maximum(chunk, 0))

    def pair(state):
        j, _ = state
        cur = i * nsub - 1 - 2 * j
        lookahead(cur - 1, zb_scr)
        for h in range(SB_HEADS):
            step(h, everything, cur, False, False, za_scr[h])
        lookahead(cur - 2, za_scr)
        for h in range(SB_HEADS):
            step(h, everything, cur - 1, False, False, zb_scr[h])
        least = functools.reduce(jnp.minimum, [run_scr[h] for h in range(SB_HEADS)])
        return j + 1, (jnp.min(least) >= SB_EXHAUSTED_LOG2).astype(jnp.int32)

    lookahead(i * nsub - 1, za_scr)
    lax.while_loop(lambda state: (state[0] < (i * nsub) // 2) & (state[1] == 0), pair,
                   (jnp.int32(0), jnp.int32(0)))
    for h, hs in enumerate(heads):
        out_ref[:, hs] = acc_scr[h].astype(out_ref.dtype)


def _sb(q, kt, v):
    s = q.shape[0]
    tb = min(SB_BLOCK, s)
    return pl.pallas_call(
        _sb_kernel,
        grid=(s // tb,),
        in_specs=[pl.BlockSpec((tb, SB_W), lambda i: (i, 0)),
                  _resident(kt.shape, lambda i: (0, 0, 0)),
                  _resident((s, SB_W), lambda i: (0, 0))],
        out_specs=pl.BlockSpec((tb, SB_W), lambda i: (i, 0)),
        out_shape=jax.ShapeDtypeStruct((s, SB_W), MXU_DTYPE),
        scratch_shapes=[pltpu.VMEM((SB_HEADS, tb, SB_HEAD_DIM), F32), pltpu.VMEM((SB_HEADS, tb, 1), F32),
                        pltpu.VMEM((SB_HEADS, tb, kt.shape[2]), F32), pltpu.VMEM((SB_HEADS, tb, kt.shape[2]), F32)],
        compiler_params=_params(1),
        name="sb",
    )(q, kt, v)


def _mix_kernel(x_ref, nsa_ref, sb_ref, mg_ref, wn_ref, ws_ref, wo_ref, g_ref, out_ref):
    d = x_ref.shape[1]
    y_nsa = _dot(nsa_ref[...], wn_ref[...])
    y_sb = _dot(sb_ref[...], ws_ref[...])
    merged = mg_ref[:, 0:d].astype(F32) * y_nsa + mg_ref[:, d:2 * d].astype(F32) * y_sb
    mixed = _dot(merged.astype(MXU_DTYPE), wo_ref[...])
    out_ref[...] = x_ref[...] + _rms(mixed, g_ref[...])


def _mix(x, nsa_o, sb_o, mg, wn_all, ws_all, wo_all, g_all, layer):
    s, d = x.shape
    t = min(ROW_TILE, s)
    row = lambda n: pl.BlockSpec((t, n), lambda i: (i, 0))
    lay = lambda a: _resident((None,) + a.shape[1:], lambda i: (layer, 0, 0))
    return pl.pallas_call(
        _mix_kernel,
        grid=(s // t,),
        in_specs=[row(d), row(NSA_W), row(SB_W), row(2 * d), lay(wn_all), lay(ws_all), lay(wo_all), lay(g_all)],
        out_specs=row(d),
        out_shape=jax.ShapeDtypeStruct((s, d), F32),
        compiler_params=_params(1),
        name="mix",
    )(x, nsa_o, sb_o, mg, wn_all, ws_all, wo_all, g_all)


def _ffn_kernel(x_ref, gin_ref, w1_ref, w2_ref, gout_ref, out_ref):
    x = x_ref[...]
    d = x.shape[1]
    hb = _rms(x, gin_ref[...]).astype(MXU_DTYPE)
    ff = jnp.zeros_like(x)
    for c in range(w1_ref.shape[1] // d):
        up = _dot(hb, w1_ref[:, c * d:(c + 1) * d])
        ff = ff + _dot(jnp.square(jnp.maximum(up, 0.0)).astype(MXU_DTYPE), w2_ref[c * d:(c + 1) * d, :])
    out_ref[...] = x + _rms(ff, gout_ref[...])


def _ffn(x, gin_all, w1_all, w2_all, gout_all, layer):
    s, d = x.shape
    t = min(ROW_TILE, s)
    row = lambda n: pl.BlockSpec((t, n), lambda i: (i, 0))
    lay = lambda a: _resident((None,) + a.shape[1:], lambda i: (layer, 0, 0))
    return pl.pallas_call(
        _ffn_kernel,
        grid=(s // t,),
        in_specs=[row(d), lay(gin_all), lay(w1_all), lay(w2_all), lay(gout_all)],
        out_specs=row(d),
        out_shape=jax.ShapeDtypeStruct((s, d), F32),
        compiler_params=_params(1),
        name="ffn",
    )(x, gin_all, w1_all, w2_all, gout_all)


def _regroup_w_in(w_in):
    gate_lo, gate_hi = _C_GATE, _C_GATE + 3 * NSA_HEADS
    pad = jnp.zeros(w_in.shape[:2] + (LANES - 3 * NSA_HEADS,), w_in.dtype)
    return jnp.concatenate([w_in[..., :gate_lo], w_in[..., gate_lo:gate_hi], pad, w_in[..., gate_hi:]],
                           axis=-1).astype(MXU_DTYPE)


def kernel(x, positions, norm_g, w_in, cmp_pe, cmp_w1, cmp_w2, w_nsa_o, w_sb_o, w_out, w_ff1, w_ff2):
    b, s, d = x.shape
    depth = w_in.shape[0]
    ncp, ns = s // CMP_STRIDE, s // SEL_BLOCK
    nsp = -(-ns // LANES) * LANES
    half_w = CMP_STRIDE * HEAD_DIM

    w_in_r = _regroup_w_in(w_in)
    pe = cmp_pe.reshape(depth, 2, 2, half_w)
    w1 = cmp_w1.astype(MXU_DTYPE)
    w2 = cmp_w2.astype(MXU_DTYPE)
    wn, ws, wo = w_nsa_o.astype(MXU_DTYPE), w_sb_o.astype(MXU_DTYPE), w_out.astype(MXU_DTYPE)
    wf1, wf2 = w_ff1.astype(MXU_DTYPE), w_ff2.astype(MXU_DTYPE)
    g_pre, g_mix, g_ffn_in, g_ffn_out = (norm_g[:, n][:, None, :] for n in range(4))

    dim = jnp.arange(LANES) % HEAD_DIM
    half = ROT_DIM // 2
    inv_freq = jnp.power(ROPE_THETA, (dim % half).astype(F32) * (-2.0 / ROT_DIM))
    invf = jnp.where(dim < ROT_DIM, inv_freq, 0.0)[None, :].astype(F32)
    c_start = CMP_STRIDE * jnp.arange(ncp)[None, :]
    s_start = SEL_BLOCK * jnp.arange(nsp)[:, None]
    ovlt = ((c_start < s_start + SEL_BLOCK) & (c_start + CMP_BLOCK > s_start) & (s_start < s)).astype(MXU_DTYPE)

    outs = []
    for bi in range(b):
        xb = x[bi]
        pos = positions[bi][:, None]
        for layer in range(depth):
            (qpt, qrt, kc, vc, ks, vst, kw, vwt, gates_t, sbq, sbkt, sbv, mg) = _inproj(
                xb, g_pre, pos, invf, w_in_r, layer)
            kcmp, vcmpt = _compress(kc.reshape(ncp, half_w), vc.reshape(ncp, half_w), pe, w1, w2, layer)
            nsa_o = _nsa(qpt, qrt, gates_t, kcmp, vcmpt, ovlt, ks, vst, kw, vwt)
            sb_o = _sb(sbq, sbkt, sbv)
            xb = _mix(xb, nsa_o, sb_o, mg, wn, ws, wo, g_mix, layer)
            xb = _ffn(xb, g_ffn_in, wf1, wf2, g_ffn_out, layer)
        outs.append(xb)
    return jnp.stack(outs, axis=0)
```

```python
import functools

import jax
import jax.numpy as jnp
from jax import lax
from jax.experimental import pallas as pl
from jax.experimental.pallas import tpu as pltpu

F32 = jnp.float32
MXU_DTYPE = jnp.bfloat16

HEAD_DIM = 64
NSA_HEADS = 8
SB_HEADS = 4
SB_HEAD_DIM = 128
ROPE_THETA = 500000.0
ROT_DIM = HEAD_DIM // 4
CMP_BLOCK = 32
CMP_STRIDE = 16
SEL_BLOCK = 64
SEL_TOP_N = 8
WINDOW = 512
RMS_EPS = 1e-6
NSA_W = NSA_HEADS * HEAD_DIM
SB_W = SB_HEADS * SB_HEAD_DIM
LANES = 128
VT_ROWS = HEAD_DIM + 16
MASKED = -32768.0
LOG2_E = 1.4426950408889634
VMEM_LIMIT = 56 * 1024 * 1024

ROW_TILE = 1024
NSA_Q_BLOCK = 256
NSA_KEY_CHUNK = 512
SEL_CODE_BLOCKS = HEAD_DIM
CMP_VARIANTS = 4
SB_BLOCK = 512
SB_KEY_CHUNK = 256
SB_EXHAUSTED_LOG2 = 160.0


def _dot(a, b):
    return jnp.dot(a, b, preferred_element_type=F32)


def _rms(x, g):
    return x * lax.rsqrt(jnp.mean(x * x, axis=-1, keepdims=True) + RMS_EPS) * g


def _params(n_grid_dims):
    return pltpu.CompilerParams(dimension_semantics=("arbitrary",) * n_grid_dims,
                                vmem_limit_bytes=VMEM_LIMIT)


def _resident(block_shape, index_map):
    return pl.BlockSpec(block_shape, index_map, pipeline_mode=pl.Buffered(1))


_C_KV = NSA_W
_C_GATE = _C_KV + 6 * HEAD_DIM
_C_SB = _C_GATE + LANES
_C_MERGE = _C_SB + 3 * SB_W


def _inproj_kernel(x_ref, g_ref, pos_ref, invf_ref, w_ref,
                   qpt_ref, qrt_ref, kc_ref, vc_ref, ks_ref, vst_ref, kw_ref, vwt_ref,
                   gate_ref, sbq_ref, sbk_ref, sbv_ref, mg_ref):
    t, d_model = x_ref.shape
    hb = _rms(x_ref[...], g_ref[...]).astype(MXU_DTYPE)

    ang = pos_ref[...].astype(F32) * invf_ref[...]
    cos, sin = jnp.cos(ang), jnp.sin(ang)
    lane = lax.broadcasted_iota(jnp.int32, (1, LANES), 1)
    dim = lane % HEAD_DIM
    half = ROT_DIM // 2
    sin_up = jnp.where((dim >= half) & (dim < ROT_DIM), sin, 0.0)
    sin_dn = jnp.where(dim < half, -sin, 0.0)
    low = lane < HEAD_DIM

    def rope(xg):
        return xg * cos + pltpu.roll(xg, half, 1) * sin_up + pltpu.roll(xg, LANES - half, 1) * sin_dn

    pa = _dot(hb, w_ref[:, 0:_C_GATE])
    scale = LOG2_E * HEAD_DIM ** -0.5
    for j in range(NSA_W // LANES):
        qg = pa[:, j * LANES:(j + 1) * LANES]
        qp_t, qr_t = (qg * scale).T, (rope(qg) * scale).T
        for c in range(qpt_ref.shape[0]):
            blk = slice(c * qpt_ref.shape[2], (c + 1) * qpt_ref.shape[2])
            qpt_ref[c, j * LANES:(j + 1) * LANES, :] = qp_t[:, blk].astype(qpt_ref.dtype)
            qrt_ref[c, j * LANES:(j + 1) * LANES, :] = qr_t[:, blk].astype(qrt_ref.dtype)
    kc_ref[...] = pa[:, _C_KV:_C_KV + HEAD_DIM]
    vc_ref[...] = pa[:, _C_KV + HEAD_DIM:_C_KV + 2 * HEAD_DIM]

    row = pl.program_id(0) * t + lax.broadcasted_iota(jnp.int32, (t, 1), 0)
    code = jnp.where(lane - HEAD_DIM == (row // SEL_BLOCK) % SEL_CODE_BLOCKS, 1.0, 0.0)
    ksg = pa[:, _C_KV + 2 * HEAD_DIM:_C_KV + 4 * HEAD_DIM]
    ks_ref[...] = jnp.where(low, rope(ksg), code).astype(ks_ref.dtype)
    vs_t = jnp.where(low, pltpu.roll(ksg, HEAD_DIM, 1), 1.0).T[0:VT_ROWS]
    for c in range(vst_ref.shape[0]):
        vst_ref[c] = vs_t[:, c * vst_ref.shape[2]:(c + 1) * vst_ref.shape[2]].astype(vst_ref.dtype)
    kwg = pa[:, _C_KV + 4 * HEAD_DIM:_C_KV + 6 * HEAD_DIM]
    kw_ref[...] = jnp.where(low, rope(kwg), 0.0).astype(kw_ref.dtype)
    vw_t = jnp.where(low, pltpu.roll(kwg, HEAD_DIM, 1), 1.0).T[0:VT_ROWS]
    for c in range(vwt_ref.shape[0]):
        vwt_ref[c] = vw_t[:, c * vwt_ref.shape[2]:(c + 1) * vwt_ref.shape[2]].astype(vwt_ref.dtype)

    gate_t = jax.nn.sigmoid(_dot(hb, w_ref[:, _C_GATE:_C_SB])).T
    for c in range(gate_ref.shape[0]):
        gate_ref[c] = gate_t[:, c * gate_ref.shape[2]:(c + 1) * gate_ref.shape[2]]
    sb = _dot(hb, w_ref[:, _C_SB:_C_MERGE])
    sbq_ref[...] = (sb[:, 0:SB_W] * (LOG2_E * SB_HEAD_DIM ** -0.5)).astype(sbq_ref.dtype)
    sbk_t = sb[:, SB_W:2 * SB_W].T
    for c in range(sbk_ref.shape[0]):
        sbk_ref[c] = sbk_t[:, c * sbk_ref.shape[2]:(c + 1) * sbk_ref.shape[2]].astype(sbk_ref.dtype)
    sbv_ref[...] = sb[:, 2 * SB_W:3 * SB_W].astype(sbv_ref.dtype)
    for c in range(2):
        mg_ref[:, c * d_model:(c + 1) * d_model] = jax.nn.sigmoid(
            _dot(hb, w_ref[:, _C_MERGE + c * d_model:_C_MERGE + (c + 1) * d_model])).astype(mg_ref.dtype)


def _inproj(x, g, pos, invf, w_all, layer):
    s, d = x.shape
    t = min(ROW_TILE, s)
    wcols = w_all.shape[2]
    kck = min(NSA_KEY_CHUNK, s)
    sck = min(SB_KEY_CHUNK, s)
    wck = min(NSA_Q_BLOCK, s)
    row = lambda n: pl.BlockSpec((t, n), lambda i: (i, 0))
    slab = lambda n, ck: pl.BlockSpec((t // ck, n, ck), lambda i: (i, 0, 0))
    sds = jax.ShapeDtypeStruct
    out_shape = [
        sds((s // wck, NSA_W, wck), MXU_DTYPE), sds((s // wck, NSA_W, wck), MXU_DTYPE),
        sds((s, HEAD_DIM), F32), sds((s, HEAD_DIM), F32),
        sds((s, LANES), MXU_DTYPE), sds((s // kck, VT_ROWS, kck), MXU_DTYPE),
        sds((s, LANES), MXU_DTYPE), sds((s // wck, VT_ROWS, wck), MXU_DTYPE),
        sds((s // wck, LANES, wck), F32),
        sds((s, SB_W), MXU_DTYPE), sds((s // sck, SB_W, sck), MXU_DTYPE), sds((s, SB_W), MXU_DTYPE),
        sds((s, 2 * d), MXU_DTYPE),
    ]
    out_specs = [slab(NSA_W, wck), slab(NSA_W, wck), row(HEAD_DIM), row(HEAD_DIM),
                 row(LANES), slab(VT_ROWS, kck), row(LANES), slab(VT_ROWS, wck),
                 slab(LANES, wck), row(SB_W), slab(SB_W, sck), row(SB_W), row(2 * d)]
    return pl.pallas_call(
        _inproj_kernel,
        grid=(s // t,),
        in_specs=[row(d),
                  _resident((None, 1, d), lambda i: (layer, 0, 0)),
                  row(1),
                  _resident((1, LANES), lambda i: (0, 0)),
                  _resident((None, d, wcols), lambda i: (layer, 0, 0))],
        out_specs=out_specs,
        out_shape=out_shape,
        compiler_params=_params(1),
        name="inproj",
    )(x, g, pos, invf, w_all)


def _compress_kernel(kc_ref, vc_ref, pe_ref, w1_ref, w2_ref, kcmp_ref, vcmpt_ref):
    nr, half_w = kc_ref.shape
    outs = []
    for kv, r_ref in enumerate((kc_ref, vc_ref)):
        r = r_ref[...]
        ha = _dot((r + pe_ref[kv, 0:1, :]).astype(MXU_DTYPE), w1_ref[kv, 0:half_w, :])
        hb = _dot((r + pe_ref[kv, 1:2, :]).astype(MXU_DTYPE), w1_ref[kv, half_w:2 * half_w, :])
        hid = ha + pltpu.roll(hb, nr - 1, 0)
        outs.append(_dot(jax.nn.gelu(hid).astype(MXU_DTYPE), w2_ref[kv]))
    pad = jnp.zeros_like(outs[0])
    kcmp_ref[...] = jnp.concatenate([outs[0], pad], axis=1).astype(kcmp_ref.dtype)
    vcmpt_ref[...] = jnp.concatenate([outs[1], pad], axis=1).T[0:HEAD_DIM].astype(vcmpt_ref.dtype)


def _compress(kc, vc, pe_all, w1_all, w2_all, layer):
    nr, half_w = kc.shape
    hidden = w1_all.shape[3]
    full = lambda shape: pl.BlockSpec(shape, lambda i: (0,) * len(shape))
    return pl.pallas_call(
        _compress_kernel,
        grid=(1,),
        in_specs=[full((nr, half_w)), full((nr, half_w)),
                  pl.BlockSpec((None, 2, 2, half_w), lambda i: (layer, 0, 0, 0)),
                  pl.BlockSpec((None, 2, 2 * half_w, hidden), lambda i: (layer, 0, 0, 0)),
                  pl.BlockSpec((None, 2, hidden, HEAD_DIM), lambda i: (layer, 0, 0, 0))],
        out_specs=[full((nr, LANES)), full((HEAD_DIM, nr))],
        out_shape=[jax.ShapeDtypeStruct((nr, LANES), MXU_DTYPE), jax.ShapeDtypeStruct((HEAD_DIM, nr), MXU_DTYPE)],
        compiler_params=_params(1),
        name="compress",
    )(kc, vc, pe_all, w1_all, w2_all)


def _nsa_kernel(qpt_ref, qrt_ref, gate_ref, kcmp_ref, vcmpt_ref, ovlt_ref, ks_ref, vst_ref, kw_ref, vwt_ref,
                out_ref, qaug_scr, mask_scr, m_scr, acc_scr, sa_scr, sb_scr, ta_scr, tb_scr, ocmp_scr, owin_scr, imp_scr, *, n_sel):
    tq = qpt_ref.shape[1]
    seq = ks_ref.shape[0]
    ncp = kcmp_ref.shape[0]
    nsp = ovlt_ref.shape[0]
    ck = vst_ref.shape[2]
    wck = vwt_ref.shape[2]
    nh = NSA_HEADS
    q0 = pl.program_id(0) * tq
    t = q0 + lax.broadcasted_iota(jnp.int32, (1, tq), 1)
    cols = [slice(h * tq, (h + 1) * tq) for h in range(nh)]

    def aug(qt_ref, h, tail):
        return jnp.concatenate([qt_ref[h * HEAD_DIM:(h + 1) * HEAD_DIM, :], tail], axis=0)

    zeros_tail = jnp.zeros((HEAD_DIM, tq), MXU_DTYPE)
    def cmp_branch(rows):
        qp_aug = jnp.concatenate([aug(qpt_ref, h, zeros_tail) for h in range(nh)], axis=1)
        sc_all = _dot(kcmp_ref[0:rows, :], qp_aug)
        cmp_last = CMP_STRIDE * lax.broadcasted_iota(jnp.int32, (rows, 1), 0) + (CMP_BLOCK - 1)
        vis_c = cmp_last <= t
        psum = jnp.zeros((rows, tq), F32)
        p_cmp = []
        for h in range(nh):
            sc = jnp.where(vis_c, sc_all[:, cols[h]], -1e30)
            top = jnp.max(sc, axis=0, keepdims=True)
            e = jnp.exp2(sc - jnp.where(top > -1e29, top, 0.0))
            den = jnp.sum(e, axis=0, keepdims=True)
            p = e * (1.0 / jnp.where(den > 0.0, den, 1.0))
            psum = psum + p
            p_cmp.append(p.astype(MXU_DTYPE))
        o_cmp = _dot(vcmpt_ref[:, 0:rows], jnp.concatenate(p_cmp, axis=1))
        p_hi = psum.astype(MXU_DTYPE)
        p_lo = (psum - p_hi.astype(F32)).astype(MXU_DTYPE)
        imp_scr[...] = _dot(ovlt_ref[:, 0:rows], p_hi) + _dot(ovlt_ref[:, 0:rows], p_lo)
        ocmp_scr[...] = o_cmp

    n_var = CMP_VARIANTS if ncp % (CMP_VARIANTS * LANES) == 0 else 1
    step = ncp // n_var
    n_vis = (q0 + tq - CMP_BLOCK) // CMP_STRIDE + 1
    variant = jnp.clip((n_vis + step - 1) // step - 1, 0, n_var - 1)
    for v in range(n_var):
        pl.when(variant == v)(functools.partial(cmp_branch, (v + 1) * step))
    imp = imp_scr[...]

    qr_aug = jnp.concatenate([aug(qrt_ref, h, zeros_tail) for h in range(nh)], axis=1)
    span = min(WINDOW + tq, seq)
    start = pl.multiple_of(jnp.maximum(q0 + tq - span, 0), wck)
    sw_all = _dot(kw_ref[pl.ds(start, span), :], qr_aug)

    def window_finish():
        kpos_w = start + lax.broadcasted_iota(jnp.int32, (span, 1), 0)
        vis_w = (kpos_w <= t) & (kpos_w > t - WINDOW)
        p_win = []
        for h in range(nh):
            sw = jnp.where(vis_w, sw_all[:, cols[h]], -1e30)
            p_win.append(jnp.exp2(sw - jnp.max(sw, axis=0, keepdims=True)).astype(MXU_DTYPE))
        p_win = jnp.concatenate(p_win, axis=1)
        ow = jnp.zeros((VT_ROWS, nh * tq), F32)
        for j in range(span // wck):
            ow = ow + _dot(vwt_ref[start // wck + j], p_win[j * wck:(j + 1) * wck, :])
        owin_scr[...] = ow[0:HEAD_DIM] * (1.0 / ow[HEAD_DIM:HEAD_DIM + 1])

    blk = lax.broadcasted_iota(jnp.int32, (nsp, 1), 0)
    cur = t // SEL_BLOCK
    valid = blk <= cur
    forced = (blk == 0) | (blk == cur) | (blk == cur - 1)
    n_forced = 3
    assert n_sel >= n_forced
    score = jnp.where(valid, jnp.where(forced, -jnp.inf, imp), -1.0)
    chosen = jnp.where(forced & valid, 1.0, 0.0)
    for _ in range(n_sel - n_forced):
        best = jnp.max(score, axis=0, keepdims=True)
        idx = jnp.min(jnp.where(score == best, blk, nsp), axis=0, keepdims=True)
        hit = blk == idx
        chosen = jnp.where(hit, 1.0, chosen)
        score = jnp.where(hit, -jnp.inf, score)
    mask_scr[0:nsp, :] = ((chosen - 1.0) * (-MASKED)).astype(mask_scr.dtype)
    mask_scr[nsp:nsp + SEL_CODE_BLOCKS, :] = jnp.full((SEL_CODE_BLOCKS, tq), MASKED, mask_scr.dtype)

    for h in range(nh):
        qaug_scr[0:HEAD_DIM, cols[h]] = qrt_ref[h * HEAD_DIM:(h + 1) * HEAD_DIM, :]
    m_scr[...] = jnp.full(m_scr.shape, -1e30, F32)
    acc_scr[...] = jnp.zeros(acc_scr.shape, F32)

    c_diag = q0 // ck

    def sel_scores(c, s_ref, top_ref, limit):
        k0 = pl.multiple_of(jnp.minimum(c, c_diag) * ck, ck)
        code0 = jnp.where(c < limit, (k0 // (SEL_BLOCK * SEL_CODE_BLOCKS)) * SEL_CODE_BLOCKS, nsp)
        mrows = mask_scr[pl.ds(pl.multiple_of(code0, SEL_CODE_BLOCKS), SEL_CODE_BLOCKS), :]
        for h in range(nh):
            qaug_scr[HEAD_DIM:2 * HEAD_DIM, cols[h]] = mrows
        s_all = _dot(ks_ref[pl.ds(k0, ck), :], qaug_scr[...])
        s_ref[...] = s_all
        top_ref[...] = jnp.max(s_all, axis=0, keepdims=True)

    def sel_update(c, s_ref, top_ref, causal):
        m_old = m_scr[...]
        if causal:
            kpos = c * ck + lax.broadcasted_iota(jnp.int32, (ck, 1), 0)
            bias = jnp.where(kpos <= t, 0.0, MASKED)
            tops = [jnp.max(s_ref[:, cols[h]] + bias, axis=0, keepdims=True) for h in range(nh)]
            m_new = jnp.maximum(m_old, jnp.concatenate(tops, axis=1))
        else:
            m_new = jnp.maximum(m_old, top_ref[...])
        p_all = []
        for h in range(nh):
            s = s_ref[:, cols[h]]
            if causal:
                s = s + bias
            p_all.append(jnp.exp2(s - m_new[:, cols[h]]).astype(MXU_DTYPE))
        pv = _dot(vst_ref[jnp.minimum(c, c_diag)], jnp.concatenate(p_all, axis=1))
        acc_scr[...] = jnp.exp2(m_old - m_new) * acc_scr[...] + pv
        m_scr[...] = m_new

    def sel_pair(c):
        sel_scores(c + 1, sa_scr, ta_scr, c_diag)
        sel_update(c, sb_scr, tb_scr, False)
        sel_scores(c + 2, sb_scr, tb_scr, c_diag)
        sel_update(c + 1, sa_scr, ta_scr, False)

    def sel_quad(j, carry):
        sel_pair(4 * j)
        sel_pair(4 * j + 2)
        return carry

    sel_scores(c_diag, sa_scr, ta_scr, c_diag + 1)
    sel_scores(0, sb_scr, tb_scr, c_diag)
    window_finish()
    sel_update(c_diag, sa_scr, ta_scr, True)
    n_quads = c_diag // 4
    lax.fori_loop(0, n_quads, sel_quad, 0)
    lax.fori_loop(0, (c_diag - 4 * n_quads + 1) // 2, lambda j, carry: (sel_pair(4 * n_quads + 2 * j), carry)[1], 0)
    acc = acc_scr[...]
    o_sel = acc[0:HEAD_DIM] * (1.0 / acc[HEAD_DIM:HEAD_DIM + 1])

    o_cmp = ocmp_scr[...]
    o_win = owin_scr[...]
    gates = gate_ref[...]
    merged = []
    for h in range(nh):
        merged.append(gates[3 * h:3 * h + 1, :] * o_cmp[:, cols[h]]
                      + gates[3 * h + 1:3 * h + 2, :] * o_sel[:, cols[h]]
                      + gates[3 * h + 2:3 * h + 3, :] * o_win[:, cols[h]])
    out_ref[...] = jnp.concatenate(merged, axis=0).T.astype(out_ref.dtype)


def _nsa(qpt, qrt, gates_t, kcmp, vcmpt, ovlt, ks, vst, kw, vwt):
    s = ks.shape[0]
    tq = min(NSA_Q_BLOCK, s)
    nsp = ovlt.shape[0]
    slab = lambda n: pl.BlockSpec((None, n, tq), lambda i: (i, 0, 0))
    res = lambda a: _resident(a.shape, lambda i: (0,) * a.ndim)
    return pl.pallas_call(
        functools.partial(_nsa_kernel, n_sel=min(SEL_TOP_N, s // SEL_BLOCK)),
        grid=(s // tq,),
        in_specs=[slab(NSA_W), slab(NSA_W), slab(LANES), res(kcmp), res(vcmpt), res(ovlt),
                  res(ks), res(vst), res(kw), res(vwt)],
        out_specs=pl.BlockSpec((tq, NSA_W), lambda i: (i, 0)),
        out_shape=jax.ShapeDtypeStruct((s, NSA_W), MXU_DTYPE),
        scratch_shapes=[pltpu.VMEM((LANES, NSA_HEADS * tq), MXU_DTYPE),
                        pltpu.VMEM((nsp + SEL_CODE_BLOCKS, tq), MXU_DTYPE),
                        pltpu.VMEM((1, NSA_HEADS * tq), F32), pltpu.VMEM((VT_ROWS, NSA_HEADS * tq), F32),
                        pltpu.VMEM((vst.shape[2], NSA_HEADS * tq), F32),
                        pltpu.VMEM((vst.shape[2], NSA_HEADS * tq), F32),
                        pltpu.VMEM((1, NSA_HEADS * tq), F32), pltpu.VMEM((1, NSA_HEADS * tq), F32),
                        pltpu.VMEM((HEAD_DIM, NSA_HEADS * tq), F32), pltpu.VMEM((HEAD_DIM, NSA_HEADS * tq), F32),
                        pltpu.VMEM((nsp, tq), F32)],
        compiler_params=_params(1),
        name="nsa",
    )(qpt, qrt, gates_t, kcmp, vcmpt, ovlt, ks, vst, kw, vwt)


def _softplus2(z2):
    neg_abs = lax.bitcast_convert_type(lax.bitcast_convert_type(z2, jnp.uint32) | jnp.uint32(0x80000000), F32)
    return jnp.maximum(z2, 0.0) + jnp.log2(1.0 + jnp.exp2(neg_abs))


def _sb_kernel(q_ref, k_ref, v_ref, out_ref, acc_scr, run_scr, za_scr, zb_scr, done_scr):
    assert (q_ref.shape[0] // k_ref.shape[2]) % 2 == 0
    tb = q_ref.shape[0]
    ck = k_ref.shape[2]
    nsub = tb // ck
    i = pl.program_id(0)
    r = lax.broadcasted_iota(jnp.int32, (ck, ck), 0)
    c = lax.broadcasted_iota(jnp.int32, (ck, ck), 1)
    tri = jnp.where(r >= c, 1.0, 0.0).astype(MXU_DTYPE)
    before = c < r
    heads = [slice(h * SB_HEAD_DIM, (h + 1) * SB_HEAD_DIM) for h in range(SB_HEADS)]

    def logits(h, rows, chunk):
        return _dot(q_ref[rows, heads[h]], k_ref[chunk, heads[h], :])

    def step(h, rows, chunk, diag, first, z=None):
        hs = heads[h]
        if z is None:
            z = logits(h, rows, chunk)
        sp = _softplus2(z)
        if diag:
            sp = jnp.where(before, sp, 0.0)
        cs = _dot(sp.astype(MXU_DTYPE), tri)
        own = jnp.minimum(z - cs, 0.0)
        if first:
            a = jnp.exp2(own)
            run_scr[h, rows] = cs[:, 0:1]
        else:
            run = run_scr[h, rows]
            a = jnp.exp2(own - run)
            run_scr[h, rows] = run + cs[:, 0:1]
        if diag:
            a = jnp.where(before, a, 0.0)
        pv = _dot(a.astype(MXU_DTYPE), v_ref[pl.ds(pl.multiple_of(chunk * ck, ck), ck), hs])
        if first:
            acc_scr[h, rows] = pv
        else:
            acc_scr[h, rows] += pv

    for g in range(nsub):
        rows = slice(g * ck, (g + 1) * ck)
        for h in range(SB_HEADS):
            step(h, rows, i * nsub + g, True, True)
        for back in range(g):
            for h in range(SB_HEADS):
                step(h, rows, i * nsub + g - 1 - back, False, False)

    everything = slice(0, tb)

    def lookahead(chunk, z_ref):
        for h in range(SB_HEADS):
            z_ref[h] = logits(h, everything, jnp.maximum(chunk, 0))

    def exhausted():
        least = functools.reduce(jnp.minimum, [run_scr[h] for h in range(SB_HEADS)])
        return (jnp.min(least) >= SB_EXHAUSTED_LOG2).astype(jnp.int32)

    def pair(state):
        j, _ = state
        cur = i * nsub - 1 - 2 * j
        lookahead(cur - 1, zb_scr)
        for h in range(SB_HEADS):
            step(h, everything, cur, False, False, za_scr[h])
        done_scr[0] = exhausted()

        @pl.when(done_scr[0] == 0)
        def _():
            lookahead(cur - 2, za_scr)
            for h in range(SB_HEADS):
                step(h, everything, cur - 1, False, False, zb_scr[h])
            done_scr[0] = exhausted()

        return j + 1, done_scr[0]

    lookahead(i * nsub - 1, za_scr)
    lax.while_loop(lambda state: (state[0] < (i * nsub) // 2) & (state[1] == 0), pair,
                   (jnp.int32(0), jnp.int32(0)))
    for h, hs in enumerate(heads):
        out_ref[:, hs] = acc_scr[h].astype(out_ref.dtype)


def _sb(q, kt, v):
    s = q.shape[0]
    tb = min(SB_BLOCK, s)
    return pl.pallas_call(
        _sb_kernel,
        grid=(s // tb,),
        in_specs=[pl.BlockSpec((tb, SB_W), lambda i: (i, 0)),
                  _resident(kt.shape, lambda i: (0, 0, 0)),
                  _resident((s, SB_W), lambda i: (0, 0))],
        out_specs=pl.BlockSpec((tb, SB_W), lambda i: (i, 0)),
        out_shape=jax.ShapeDtypeStruct((s, SB_W), MXU_DTYPE),
        scratch_shapes=[pltpu.VMEM((SB_HEADS, tb, SB_HEAD_DIM), F32), pltpu.VMEM((SB_HEADS, tb, 1), F32),
                        pltpu.VMEM((SB_HEADS, tb, kt.shape[2]), F32), pltpu.VMEM((SB_HEADS, tb, kt.shape[2]), F32),
                        pltpu.SMEM((1,), jnp.int32)],
        compiler_params=_params(1),
        name="sb",
    )(q, kt, v)


def _mix_kernel(x_ref, nsa_ref, sb_ref, mg_ref, wn_ref, ws_ref, wo_ref, g_ref, out_ref):
    d = x_ref.shape[1]
    y_nsa = _dot(nsa_ref[...], wn_ref[...])
    y_sb = _dot(sb_ref[...], ws_ref[...])
    merged = mg_ref[:, 0:d].astype(F32) * y_nsa + mg_ref[:, d:2 * d].astype(F32) * y_sb
    mixed = _dot(merged.astype(MXU_DTYPE), wo_ref[...])
    out_ref[...] = x_ref[...] + _rms(mixed, g_ref[...])


def _mix(x, nsa_o, sb_o, mg, wn_all, ws_all, wo_all, g_all, layer):
    s, d = x.shape
    t = min(ROW_TILE, s)
    row = lambda n: pl.BlockSpec((t, n), lambda i: (i, 0))
    lay = lambda a: _resident((None,) + a.shape[1:], lambda i: (layer, 0, 0))
    return pl.pallas_call(
        _mix_kernel,
        grid=(s // t,),
        in_specs=[row(d), row(NSA_W), row(SB_W), row(2 * d), lay(wn_all), lay(ws_all), lay(wo_all), lay(g_all)],
        out_specs=row(d),
        out_shape=jax.ShapeDtypeStruct((s, d), F32),
        compiler_params=_params(1),
        name="mix",
    )(x, nsa_o, sb_o, mg, wn_all, ws_all, wo_all, g_all)


def _ffn_kernel(x_ref, gin_ref, w1_ref, w2_ref, gout_ref, out_ref):
    x = x_ref[...]
    d = x.shape[1]
    hb = _rms(x, gin_ref[...]).astype(MXU_DTYPE)
    ff = jnp.zeros_like(x)
    for c in range(w1_ref.shape[1] // d):
        up = _dot(hb, w1_ref[:, c * d:(c + 1) * d])
        ff = ff + _dot(jnp.square(jnp.maximum(up, 0.0)).astype(MXU_DTYPE), w2_ref[c * d:(c + 1) * d, :])
    out_ref[...] = x + _rms(ff, gout_ref[...])


def _ffn(x, gin_all, w1_all, w2_all, gout_all, layer):
    s, d = x.shape
    t = min(ROW_TILE, s)
    row = lambda n: pl.BlockSpec((t, n), lambda i: (i, 0))
    lay = lambda a: _resident((None,) + a.shape[1:], lambda i: (layer, 0, 0))
    return pl.pallas_call(
        _ffn_kernel,
        grid=(s // t,),
        in_specs=[row(d), lay(gin_all), lay(w1_all), lay(w2_all), lay(gout_all)],
        out_specs=row(d),
        out_shape=jax.ShapeDtypeStruct((s, d), F32),
        compiler_params=_params(1),
        name="ffn",
    )(x, gin_all, w1_all, w2_all, gout_all)


def _regroup_w_in(w_in):
    gate_lo, gate_hi = _C_GATE, _C_GATE + 3 * NSA_HEADS
    pad = jnp.zeros(w_in.shape[:2] + (LANES - 3 * NSA_HEADS,), w_in.dtype)
    return jnp.concatenate([w_in[..., :gate_lo], w_in[..., gate_lo:gate_hi], pad, w_in[..., gate_hi:]],
                           axis=-1).astype(MXU_DTYPE)


def kernel(x, positions, norm_g, w_in, cmp_pe, cmp_w1, cmp_w2, w_nsa_o, w_sb_o, w_out, w_ff1, w_ff2):
    b, s, d = x.shape
    depth = w_in.shape[0]
    ncp, ns = s // CMP_STRIDE, s // SEL_BLOCK
    nsp = -(-ns // LANES) * LANES
    half_w = CMP_STRIDE * HEAD_DIM

    w_in_r = _regroup_w_in(w_in)
    pe = cmp_pe.reshape(depth, 2, 2, half_w)
    w1 = cmp_w1.astype(MXU_DTYPE)
    w2 = cmp_w2.astype(MXU_DTYPE)
    wn, ws, wo = w_nsa_o.astype(MXU_DTYPE), w_sb_o.astype(MXU_DTYPE), w_out.astype(MXU_DTYPE)
    wf1, wf2 = w_ff1.astype(MXU_DTYPE), w_ff2.astype(MXU_DTYPE)
    g_pre, g_mix, g_ffn_in, g_ffn_out = (norm_g[:, n][:, None, :] for n in range(4))

    dim = jnp.arange(LANES) % HEAD_DIM
    half = ROT_DIM // 2
    inv_freq = jnp.power(ROPE_THETA, (dim % half).astype(F32) * (-2.0 / ROT_DIM))
    invf = jnp.where(dim < ROT_DIM, inv_freq, 0.0)[None, :].astype(F32)
    c_start = CMP_STRIDE * jnp.arange(ncp)[None, :]
    s_start = SEL_BLOCK * jnp.arange(nsp)[:, None]
    ovlt = ((c_start < s_start + SEL_BLOCK) & (c_start + CMP_BLOCK > s_start) & (s_start < s)).astype(MXU_DTYPE)

    outs = []
    for bi in range(b):
        xb = x[bi]
        pos = positions[bi][:, None]
        for layer in range(depth):
            (qpt, qrt, kc, vc, ks, vst, kw, vwt, gates_t, sbq, sbkt, sbv, mg) = _inproj(
                xb, g_pre, pos, invf, w_in_r, layer)
            kcmp, vcmpt = _compress(kc.reshape(ncp, half_w), vc.reshape(ncp, half_w), pe, w1, w2, layer)
            nsa_o = _nsa(qpt, qrt, gates_t, kcmp, vcmpt, ovlt, ks, vst, kw, vwt)
            sb_o = _sb(sbq, sbkt, sbv)
            xb = _mix(xb, nsa_o, sb_o, mg, wn, ws, wo, g_mix, layer)
            xb = _ffn(xb, g_ffn_in, wf1, wf2, g_ffn_out, layer)
        outs.append(xb)
    return jnp.stack(outs, axis=0)
```

```python
import functools

import jax
import jax.numpy as jnp
from jax import lax
from jax.experimental import pallas as pl
from jax.experimental.pallas import tpu as pltpu

F32 = jnp.float32
MXU_DTYPE = jnp.bfloat16

HEAD_DIM = 64
NSA_HEADS = 8
SB_HEADS = 4
SB_HEAD_DIM = 128
ROPE_THETA = 500000.0
ROT_DIM = HEAD_DIM // 4
CMP_BLOCK = 32
CMP_STRIDE = 16
SEL_BLOCK = 64
SEL_TOP_N = 8
WINDOW = 512
RMS_EPS = 1e-6
NSA_W = NSA_HEADS * HEAD_DIM
SB_W = SB_HEADS * SB_HEAD_DIM
LANES = 128
VT_ROWS = HEAD_DIM + 16
MASKED = -32768.0
LOG2_E = 1.4426950408889634
VMEM_LIMIT = 56 * 1024 * 1024

ROW_TILE = 1024
NSA_Q_BLOCK = 256
NSA_KEY_CHUNK = 512
SEL_CODE_BLOCKS = HEAD_DIM
CMP_VARIANTS = 4
SB_BLOCK = 512
SB_KEY_CHUNK = 256
SB_EXHAUSTED_LOG2 = 160.0


def _dot(a, b):
    return jnp.dot(a, b, preferred_element_type=F32)


def _rms(x, g):
    return x * lax.rsqrt(jnp.mean(x * x, axis=-1, keepdims=True) + RMS_EPS) * g


def _params(n_grid_dims):
    return pltpu.CompilerParams(dimension_semantics=("arbitrary",) * n_grid_dims,
                                vmem_limit_bytes=VMEM_LIMIT)


def _resident(block_shape, index_map):
    return pl.BlockSpec(block_shape, index_map, pipeline_mode=pl.Buffered(1))


_C_KV = NSA_W
_C_GATE = _C_KV + 6 * HEAD_DIM
_C_SB = _C_GATE + LANES
_C_MERGE = _C_SB + 3 * SB_W


def _inproj_kernel(x_ref, g_ref, pos_ref, invf_ref, w_ref,
                   qpt_ref, qrt_ref, kc_ref, vc_ref, ks_ref, vst_ref, kw_ref, vwt_ref,
                   gate_ref, sbq_ref, sbk_ref, sbv_ref, mg_ref):
    t, d_model = x_ref.shape
    hb = _rms(x_ref[...], g_ref[...]).astype(MXU_DTYPE)

    ang = pos_ref[...].astype(F32) * invf_ref[...]
    cos, sin = jnp.cos(ang), jnp.sin(ang)
    lane = lax.broadcasted_iota(jnp.int32, (1, LANES), 1)
    dim = lane % HEAD_DIM
    half = ROT_DIM // 2
    sin_up = jnp.where((dim >= half) & (dim < ROT_DIM), sin, 0.0)
    sin_dn = jnp.where(dim < half, -sin, 0.0)
    low = lane < HEAD_DIM

    def rope(xg):
        return xg * cos + pltpu.roll(xg, half, 1) * sin_up + pltpu.roll(xg, LANES - half, 1) * sin_dn

    pa = _dot(hb, w_ref[:, 0:_C_GATE])
    scale = LOG2_E * HEAD_DIM ** -0.5
    for j in range(NSA_W // LANES):
        qg = pa[:, j * LANES:(j + 1) * LANES]
        qpt_ref[j * LANES:(j + 1) * LANES, :] = (qg * scale).T.astype(qpt_ref.dtype)
        qrt_ref[j * LANES:(j + 1) * LANES, :] = (rope(qg) * scale).T.astype(qrt_ref.dtype)
    kc_ref[...] = pa[:, _C_KV:_C_KV + HEAD_DIM]
    vc_ref[...] = pa[:, _C_KV + HEAD_DIM:_C_KV + 2 * HEAD_DIM]

    row = pl.program_id(0) * t + lax.broadcasted_iota(jnp.int32, (t, 1), 0)
    code = jnp.where(lane - HEAD_DIM == (row // SEL_BLOCK) % SEL_CODE_BLOCKS, 1.0, 0.0)
    ksg = pa[:, _C_KV + 2 * HEAD_DIM:_C_KV + 4 * HEAD_DIM]
    ks_ref[...] = jnp.where(low, rope(ksg), code).astype(ks_ref.dtype)
    vs_t = jnp.where(low, pltpu.roll(ksg, HEAD_DIM, 1), 1.0).T[0:VT_ROWS]
    for c in range(vst_ref.shape[0]):
        vst_ref[c] = vs_t[:, c * vst_ref.shape[2]:(c + 1) * vst_ref.shape[2]].astype(vst_ref.dtype)
    kwg = pa[:, _C_KV + 4 * HEAD_DIM:_C_KV + 6 * HEAD_DIM]
    kw_ref[...] = jnp.where(low, rope(kwg), 0.0).astype(kw_ref.dtype)
    vw_t = jnp.where(low, pltpu.roll(kwg, HEAD_DIM, 1), 1.0).T[0:VT_ROWS]
    for c in range(vwt_ref.shape[0]):
        vwt_ref[c] = vw_t[:, c * vwt_ref.shape[2]:(c + 1) * vwt_ref.shape[2]].astype(vwt_ref.dtype)

    gate_ref[...] = jax.nn.sigmoid(_dot(hb, w_ref[:, _C_GATE:_C_SB])).T
    sb = _dot(hb, w_ref[:, _C_SB:_C_MERGE])
    sbq_ref[...] = (sb[:, 0:SB_W] * (LOG2_E * SB_HEAD_DIM ** -0.5)).astype(sbq_ref.dtype)
    sbk_t = sb[:, SB_W:2 * SB_W].T
    for c in range(sbk_ref.shape[0]):
        sbk_ref[c] = sbk_t[:, c * sbk_ref.shape[2]:(c + 1) * sbk_ref.shape[2]].astype(sbk_ref.dtype)
    sbv_ref[...] = sb[:, 2 * SB_W:3 * SB_W].astype(sbv_ref.dtype)
    for c in range(2):
        mg_ref[:, c * d_model:(c + 1) * d_model] = jax.nn.sigmoid(
            _dot(hb, w_ref[:, _C_MERGE + c * d_model:_C_MERGE + (c + 1) * d_model])).astype(mg_ref.dtype)


def _inproj(x, g, pos, invf, w_all, layer):
    s, d = x.shape
    t = min(ROW_TILE, s)
    wcols = w_all.shape[2]
    kck = min(NSA_KEY_CHUNK, s)
    sck = min(SB_KEY_CHUNK, s)
    wck = min(NSA_Q_BLOCK, s)
    row = lambda n: pl.BlockSpec((t, n), lambda i: (i, 0))
    col = lambda n: pl.BlockSpec((n, t), lambda i: (0, i))
    slab = lambda n, ck: pl.BlockSpec((t // ck, n, ck), lambda i: (i, 0, 0))
    sds = jax.ShapeDtypeStruct
    out_shape = [
        sds((NSA_W, s), MXU_DTYPE), sds((NSA_W, s), MXU_DTYPE),
        sds((s, HEAD_DIM), F32), sds((s, HEAD_DIM), F32),
        sds((s, LANES), MXU_DTYPE), sds((s // kck, VT_ROWS, kck), MXU_DTYPE),
        sds((s, LANES), MXU_DTYPE), sds((s // wck, VT_ROWS, wck), MXU_DTYPE),
        sds((LANES, s), F32),
        sds((s, SB_W), MXU_DTYPE), sds((s // sck, SB_W, sck), MXU_DTYPE), sds((s, SB_W), MXU_DTYPE),
        sds((s, 2 * d), MXU_DTYPE),
    ]
    out_specs = [col(NSA_W), col(NSA_W), row(HEAD_DIM), row(HEAD_DIM),
                 row(LANES), slab(VT_ROWS, kck), row(LANES), slab(VT_ROWS, wck),
                 col(LANES), row(SB_W), slab(SB_W, sck), row(SB_W), row(2 * d)]
    return pl.pallas_call(
        _inproj_kernel,
        grid=(s // t,),
        in_specs=[row(d),
                  _resident((None, 1, d), lambda i: (layer, 0, 0)),
                  row(1),
                  _resident((1, LANES), lambda i: (0, 0)),
                  _resident((None, d, wcols), lambda i: (layer, 0, 0))],
        out_specs=out_specs,
        out_shape=out_shape,
        compiler_params=_params(1),
        name="inproj",
    )(x, g, pos, invf, w_all)


def _compress_kernel(kc_ref, vc_ref, pe_ref, w1_ref, w2_ref, kcmp_ref, vcmpt_ref):
    nr, half_w = kc_ref.shape
    outs = []
    for kv, r_ref in enumerate((kc_ref, vc_ref)):
        r = r_ref[...]
        ha = _dot((r + pe_ref[kv, 0:1, :]).astype(MXU_DTYPE), w1_ref[kv, 0:half_w, :])
        hb = _dot((r + pe_ref[kv, 1:2, :]).astype(MXU_DTYPE), w1_ref[kv, half_w:2 * half_w, :])
        hid = ha + pltpu.roll(hb, nr - 1, 0)
        outs.append(_dot(jax.nn.gelu(hid).astype(MXU_DTYPE), w2_ref[kv]))
    pad = jnp.zeros_like(outs[0])
    kcmp_ref[...] = jnp.concatenate([outs[0], pad], axis=1).astype(kcmp_ref.dtype)
    vcmpt_ref[...] = jnp.concatenate([outs[1], pad], axis=1).T[0:HEAD_DIM].astype(vcmpt_ref.dtype)


def _compress(kc, vc, pe_all, w1_all, w2_all, layer):
    nr, half_w = kc.shape
    hidden = w1_all.shape[3]
    full = lambda shape: pl.BlockSpec(shape, lambda i: (0,) * len(shape))
    return pl.pallas_call(
        _compress_kernel,
        grid=(1,),
        in_specs=[full((nr, half_w)), full((nr, half_w)),
                  pl.BlockSpec((None, 2, 2, half_w), lambda i: (layer, 0, 0, 0)),
                  pl.BlockSpec((None, 2, 2 * half_w, hidden), lambda i: (layer, 0, 0, 0)),
                  pl.BlockSpec((None, 2, hidden, HEAD_DIM), lambda i: (layer, 0, 0, 0))],
        out_specs=[full((nr, LANES)), full((HEAD_DIM, nr))],
        out_shape=[jax.ShapeDtypeStruct((nr, LANES), MXU_DTYPE), jax.ShapeDtypeStruct((HEAD_DIM, nr), MXU_DTYPE)],
        compiler_params=_params(1),
        name="compress",
    )(kc, vc, pe_all, w1_all, w2_all)


def _nsa_kernel(qpt_ref, qrt_ref, gate_ref, kcmp_ref, vcmpt_ref, ovlt_ref, ks_ref, vst_ref, kw_ref, vwt_ref,
                out_ref, qaug_scr, mask_scr, m_scr, acc_scr, sa_scr, sb_scr, ta_scr, tb_scr, ocmp_scr, owin_scr, imp_scr, *, n_sel):
    tq = qpt_ref.shape[1]
    seq = ks_ref.shape[0]
    ncp = kcmp_ref.shape[0]
    nsp = ovlt_ref.shape[0]
    ck = vst_ref.shape[2]
    wck = vwt_ref.shape[2]
    nh = NSA_HEADS
    q0 = pl.program_id(0) * tq
    t = q0 + lax.broadcasted_iota(jnp.int32, (1, tq), 1)
    cols = [slice(h * tq, (h + 1) * tq) for h in range(nh)]

    def aug(qt_ref, h, tail):
        return jnp.concatenate([qt_ref[h * HEAD_DIM:(h + 1) * HEAD_DIM, :], tail], axis=0)

    zeros_tail = jnp.zeros((HEAD_DIM, tq), MXU_DTYPE)
    def cmp_branch(rows):
        qp_aug = jnp.concatenate([aug(qpt_ref, h, zeros_tail) for h in range(nh)], axis=1)
        sc_all = _dot(kcmp_ref[0:rows, :], qp_aug)
        cmp_last = CMP_STRIDE * lax.broadcasted_iota(jnp.int32, (rows, 1), 0) + (CMP_BLOCK - 1)
        vis_c = cmp_last <= t
        psum = jnp.zeros((rows, tq), F32)
        p_cmp = []
        for h in range(nh):
            sc = jnp.where(vis_c, sc_all[:, cols[h]], -1e30)
            top = jnp.max(sc, axis=0, keepdims=True)
            e = jnp.exp2(sc - jnp.where(top > -1e29, top, 0.0))
            den = jnp.sum(e, axis=0, keepdims=True)
            p = e * (1.0 / jnp.where(den > 0.0, den, 1.0))
            psum = psum + p
            p_cmp.append(p.astype(MXU_DTYPE))
        o_cmp = _dot(vcmpt_ref[:, 0:rows], jnp.concatenate(p_cmp, axis=1))
        p_hi = psum.astype(MXU_DTYPE)
        p_lo = (psum - p_hi.astype(F32)).astype(MXU_DTYPE)
        imp_scr[...] = _dot(ovlt_ref[:, 0:rows], p_hi) + _dot(ovlt_ref[:, 0:rows], p_lo)
        ocmp_scr[...] = o_cmp

    n_var = CMP_VARIANTS if ncp % (CMP_VARIANTS * LANES) == 0 else 1
    step = ncp // n_var
    n_vis = (q0 + tq - CMP_BLOCK) // CMP_STRIDE + 1
    variant = jnp.clip((n_vis + step - 1) // step - 1, 0, n_var - 1)
    for v in range(n_var):
        pl.when(variant == v)(functools.partial(cmp_branch, (v + 1) * step))
    imp = imp_scr[...]

    qr_aug = jnp.concatenate([aug(qrt_ref, h, zeros_tail) for h in range(nh)], axis=1)
    span = min(WINDOW + tq, seq)
    start = pl.multiple_of(jnp.maximum(q0 + tq - span, 0), wck)
    sw_all = _dot(kw_ref[pl.ds(start, span), :], qr_aug)

    def window_finish():
        kpos_w = start + lax.broadcasted_iota(jnp.int32, (span, 1), 0)
        vis_w = (kpos_w <= t) & (kpos_w > t - WINDOW)
        p_win = []
        for h in range(nh):
            sw = jnp.where(vis_w, sw_all[:, cols[h]], -1e30)
            p_win.append(jnp.exp2(sw - jnp.max(sw, axis=0, keepdims=True)).astype(MXU_DTYPE))
        p_win = jnp.concatenate(p_win, axis=1)
        ow = jnp.zeros((VT_ROWS, nh * tq), F32)
        for j in range(span // wck):
            ow = ow + _dot(vwt_ref[start // wck + j], p_win[j * wck:(j + 1) * wck, :])
        owin_scr[...] = ow[0:HEAD_DIM] * (1.0 / ow[HEAD_DIM:HEAD_DIM + 1])

    blk = lax.broadcasted_iota(jnp.int32, (nsp, 1), 0)
    cur = t // SEL_BLOCK
    valid = blk <= cur
    forced = (blk == 0) | (blk == cur) | (blk == cur - 1)
    n_forced = 3
    assert n_sel >= n_forced
    score = jnp.where(valid, jnp.where(forced, -jnp.inf, imp), -1.0)
    chosen = jnp.where(forced & valid, 1.0, 0.0)
    for _ in range(n_sel - n_forced):
        best = jnp.max(score, axis=0, keepdims=True)
        idx = jnp.min(jnp.where(score == best, blk, nsp), axis=0, keepdims=True)
        hit = blk == idx
        chosen = jnp.where(hit, 1.0, chosen)
        score = jnp.where(hit, -jnp.inf, score)
    mask_scr[0:nsp, :] = ((chosen - 1.0) * (-MASKED)).astype(mask_scr.dtype)
    mask_scr[nsp:nsp + SEL_CODE_BLOCKS, :] = jnp.full((SEL_CODE_BLOCKS, tq), MASKED, mask_scr.dtype)

    for h in range(nh):
        qaug_scr[0:HEAD_DIM, cols[h]] = qrt_ref[h * HEAD_DIM:(h + 1) * HEAD_DIM, :]
    m_scr[...] = jnp.full(m_scr.shape, -1e30, F32)
    acc_scr[...] = jnp.zeros(acc_scr.shape, F32)

    c_diag = q0 // ck

    def sel_scores(c, s_ref, top_ref, limit):
        k0 = pl.multiple_of(jnp.minimum(c, c_diag) * ck, ck)
        code0 = jnp.where(c < limit, (k0 // (SEL_BLOCK * SEL_CODE_BLOCKS)) * SEL_CODE_BLOCKS, nsp)
        mrows = mask_scr[pl.ds(pl.multiple_of(code0, SEL_CODE_BLOCKS), SEL_CODE_BLOCKS), :]
        for h in range(nh):
            qaug_scr[HEAD_DIM:2 * HEAD_DIM, cols[h]] = mrows
        s_all = _dot(ks_ref[pl.ds(k0, ck), :], qaug_scr[...])
        s_ref[...] = s_all
        top_ref[...] = jnp.max(s_all, axis=0, keepdims=True)

    def sel_update(c, s_ref, top_ref, causal):
        m_old = m_scr[...]
        if causal:
            kpos = c * ck + lax.broadcasted_iota(jnp.int32, (ck, 1), 0)
            bias = jnp.where(kpos <= t, 0.0, MASKED)
            tops = [jnp.max(s_ref[:, cols[h]] + bias, axis=0, keepdims=True) for h in range(nh)]
            m_new = jnp.maximum(m_old, jnp.concatenate(tops, axis=1))
        else:
            m_new = jnp.maximum(m_old, top_ref[...])
        p_all = []
        for h in range(nh):
            s = s_ref[:, cols[h]]
            if causal:
                s = s + bias
            p_all.append(jnp.exp2(s - m_new[:, cols[h]]).astype(MXU_DTYPE))
        pv = _dot(vst_ref[jnp.minimum(c, c_diag)], jnp.concatenate(p_all, axis=1))
        acc_scr[...] = jnp.exp2(m_old - m_new) * acc_scr[...] + pv
        m_scr[...] = m_new

    def sel_pair(c):
        sel_scores(c + 1, sa_scr, ta_scr, c_diag)
        sel_update(c, sb_scr, tb_scr, False)
        sel_scores(c + 2, sb_scr, tb_scr, c_diag)
        sel_update(c + 1, sa_scr, ta_scr, False)

    def sel_quad(j, carry):
        sel_pair(4 * j)
        sel_pair(4 * j + 2)
        return carry

    sel_scores(c_diag, sa_scr, ta_scr, c_diag + 1)
    sel_scores(0, sb_scr, tb_scr, c_diag)
    window_finish()
    sel_update(c_diag, sa_scr, ta_scr, True)
    n_quads = c_diag // 4
    lax.fori_loop(0, n_quads, sel_quad, 0)
    lax.fori_loop(0, (c_diag - 4 * n_quads + 1) // 2, lambda j, carry: (sel_pair(4 * n_quads + 2 * j), carry)[1], 0)
    acc = acc_scr[...]
    o_sel = acc[0:HEAD_DIM] * (1.0 / acc[HEAD_DIM:HEAD_DIM + 1])

    o_cmp = ocmp_scr[...]
    o_win = owin_scr[...]
    gates = gate_ref[...]
    merged = []
    for h in range(nh):
        merged.append(gates[3 * h:3 * h + 1, :] * o_cmp[:, cols[h]]
                      + gates[3 * h + 1:3 * h + 2, :] * o_sel[:, cols[h]]
                      + gates[3 * h + 2:3 * h + 3, :] * o_win[:, cols[h]])
    out_ref[...] = jnp.concatenate(merged, axis=0).T.astype(out_ref.dtype)


def _nsa(qpt, qrt, gates_t, kcmp, vcmpt, ovlt, ks, vst, kw, vwt):
    s = ks.shape[0]
    tq = min(NSA_Q_BLOCK, s)
    nsp = ovlt.shape[0]
    col = lambda n: pl.BlockSpec((n, tq), lambda i: (0, i))
    res = lambda a: _resident(a.shape, lambda i: (0,) * a.ndim)
    return pl.pallas_call(
        functools.partial(_nsa_kernel, n_sel=min(SEL_TOP_N, s // SEL_BLOCK)),
        grid=(s // tq,),
        in_specs=[col(NSA_W), col(NSA_W), col(LANES), res(kcmp), res(vcmpt), res(ovlt),
                  res(ks), res(vst), res(kw), res(vwt)],
        out_specs=pl.BlockSpec((tq, NSA_W), lambda i: (i, 0)),
        out_shape=jax.ShapeDtypeStruct((s, NSA_W), MXU_DTYPE),
        scratch_shapes=[pltpu.VMEM((LANES, NSA_HEADS * tq), MXU_DTYPE),
                        pltpu.VMEM((nsp + SEL_CODE_BLOCKS, tq), MXU_DTYPE),
                        pltpu.VMEM((1, NSA_HEADS * tq), F32), pltpu.VMEM((VT_ROWS, NSA_HEADS * tq), F32),
                        pltpu.VMEM((vst.shape[2], NSA_HEADS * tq), F32),
                        pltpu.VMEM((vst.shape[2], NSA_HEADS * tq), F32),
                        pltpu.VMEM((1, NSA_HEADS * tq), F32), pltpu.VMEM((1, NSA_HEADS * tq), F32),
                        pltpu.VMEM((HEAD_DIM, NSA_HEADS * tq), F32), pltpu.VMEM((HEAD_DIM, NSA_HEADS * tq), F32),
                        pltpu.VMEM((nsp, tq), F32)],
        compiler_params=_params(1),
        name="nsa",
    )(qpt, qrt, gates_t, kcmp, vcmpt, ovlt, ks, vst, kw, vwt)


def _softplus2(z2):
    neg_abs = lax.bitcast_convert_type(lax.bitcast_convert_type(z2, jnp.uint32) | jnp.uint32(0x80000000), F32)
    return jnp.maximum(z2, 0.0) + jnp.log2(1.0 + jnp.exp2(neg_abs))


def _sb_kernel(q_ref, k_ref, v_ref, out_ref, acc_scr, run_scr, za_scr, zb_scr, done_scr):
    assert (q_ref.shape[0] // k_ref.shape[2]) % 2 == 0
    tb = q_ref.shape[0]
    ck = k_ref.shape[2]
    nsub = tb // ck
    i = pl.program_id(0)
    r = lax.broadcasted_iota(jnp.int32, (ck, ck), 0)
    c = lax.broadcasted_iota(jnp.int32, (ck, ck), 1)
    tri = jnp.where(r >= c, 1.0, 0.0).astype(MXU_DTYPE)
    before = c < r
    heads = [slice(h * SB_HEAD_DIM, (h + 1) * SB_HEAD_DIM) for h in range(SB_HEADS)]

    def logits(h, rows, chunk):
        return _dot(q_ref[rows, heads[h]], k_ref[chunk, heads[h], :])

    def step(h, rows, chunk, diag, first, z=None):
        hs = heads[h]
        if z is None:
            z = logits(h, rows, chunk)
        sp = _softplus2(z)
        if diag:
            sp = jnp.where(before, sp, 0.0)
        cs = _dot(sp.astype(MXU_DTYPE), tri)
        own = jnp.minimum(z - cs, 0.0)
        if first:
            a = jnp.exp2(own)
            run_scr[h, rows] = cs[:, 0:1]
        else:
            run = run_scr[h, rows]
            a = jnp.exp2(own - run)
            run_scr[h, rows] = run + cs[:, 0:1]
        if diag:
            a = jnp.where(before, a, 0.0)
        pv = _dot(a.astype(MXU_DTYPE), v_ref[pl.ds(pl.multiple_of(chunk * ck, ck), ck), hs])
        if first:
            acc_scr[h, rows] = pv
        else:
            acc_scr[h, rows] += pv

    plan = []
    for g in range(nsub):
        rows = slice(g * ck, (g + 1) * ck)
        plan += [(h, rows, i * nsub + g, True, True) for h in range(SB_HEADS)]
        for back in range(g):
            plan += [(h, rows, i * nsub + g - 1 - back, False, False) for h in range(SB_HEADS)]
    zs = [logits(h, rows, chunk) for h, rows, chunk, _, _ in plan]
    for args, z in zip(plan, zs):
        step(*args, z)

    everything = slice(0, tb)

    def lookahead(chunk, z_ref):
        for h in range(SB_HEADS):
            z_ref[h] = logits(h, everything, jnp.maximum(chunk, 0))

    def exhausted():
        least = functools.reduce(jnp.minimum, [run_scr[h] for h in range(SB_HEADS)])
        return (jnp.min(least) >= SB_EXHAUSTED_LOG2).astype(jnp.int32)

    def pair(state):
        j, _ = state
        cur = i * nsub - 1 - 2 * j
        lookahead(cur - 1, zb_scr)
        for h in range(SB_HEADS):
            step(h, everything, cur, False, False, za_scr[h])
        done_scr[0] = exhausted()

        @pl.when(done_scr[0] == 0)
        def _():
            lookahead(cur - 2, za_scr)
            for h in range(SB_HEADS):
                step(h, everything, cur - 1, False, False, zb_scr[h])
            done_scr[0] = exhausted()

        return j + 1, done_scr[0]

    lookahead(i * nsub - 1, za_scr)
    lax.while_loop(lambda state: (state[0] < (i * nsub) // 2) & (state[1] == 0), pair,
                   (jnp.int32(0), jnp.int32(0)))
    for h, hs in enumerate(heads):
        out_ref[:, hs] = acc_scr[h].astype(out_ref.dtype)


def _sb(q, kt, v):
    s = q.shape[0]
    tb = min(SB_BLOCK, s)
    return pl.pallas_call(
        _sb_kernel,
        grid=(s // tb,),
        in_specs=[pl.BlockSpec((tb, SB_W), lambda i: (i, 0)),
                  _resident(kt.shape, lambda i: (0, 0, 0)),
                  _resident((s, SB_W), lambda i: (0, 0))],
        out_specs=pl.BlockSpec((tb, SB_W), lambda i: (i, 0)),
        out_shape=jax.ShapeDtypeStruct((s, SB_W), MXU_DTYPE),
        scratch_shapes=[pltpu.VMEM((SB_HEADS, tb, SB_HEAD_DIM), F32), pltpu.VMEM((SB_HEADS, tb, 1), F32),
                        pltpu.VMEM((SB_HEADS, tb, kt.shape[2]), F32), pltpu.VMEM((SB_HEADS, tb, kt.shape[2]), F32),
                        pltpu.SMEM((1,), jnp.int32)],
        compiler_params=_params(1),
        name="sb",
    )(q, kt, v)


def _mix_kernel(x_ref, nsa_ref, sb_ref, mg_ref, wn_ref, ws_ref, wo_ref, g_ref, out_ref):
    d = x_ref.shape[1]
    y_nsa = _dot(nsa_ref[...], wn_ref[...])
    y_sb = _dot(sb_ref[...], ws_ref[...])
    merged = mg_ref[:, 0:d].astype(F32) * y_nsa + mg_ref[:, d:2 * d].astype(F32) * y_sb
    mixed = _dot(merged.astype(MXU_DTYPE), wo_ref[...])
    out_ref[...] = x_ref[...] + _rms(mixed, g_ref[...])


def _mix(x, nsa_o, sb_o, mg, wn_all, ws_all, wo_all, g_all, layer):
    s, d = x.shape
    t = min(ROW_TILE, s)
    row = lambda n: pl.BlockSpec((t, n), lambda i: (i, 0))
    lay = lambda a: _resident((None,) + a.shape[1:], lambda i: (layer, 0, 0))
    return pl.pallas_call(
        _mix_kernel,
        grid=(s // t,),
        in_specs=[row(d), row(NSA_W), row(SB_W), row(2 * d), lay(wn_all), lay(ws_all), lay(wo_all), lay(g_all)],
        out_specs=row(d),
        out_shape=jax.ShapeDtypeStruct((s, d), F32),
        compiler_params=_params(1),
        name="mix",
    )(x, nsa_o, sb_o, mg, wn_all, ws_all, wo_all, g_all)


def _ffn_kernel(x_ref, gin_ref, w1_ref, w2_ref, gout_ref, out_ref):
    x = x_ref[...]
    d = x.shape[1]
    hb = _rms(x, gin_ref[...]).astype(MXU_DTYPE)
    ff = jnp.zeros_like(x)
    for c in range(w1_ref.shape[1] // d):
        up = _dot(hb, w1_ref[:, c * d:(c + 1) * d])
        ff = ff + _dot(jnp.square(jnp.maximum(up, 0.0)).astype(MXU_DTYPE), w2_ref[c * d:(c + 1) * d, :])
    out_ref[...] = x + _rms(ff, gout_ref[...])


def _ffn(x, gin_all, w1_all, w2_all, gout_all, layer):
    s, d = x.shape
    t = min(ROW_TILE, s)
    row = lambda n: pl.BlockSpec((t, n), lambda i: (i, 0))
    lay = lambda a: _resident((None,) + a.shape[1:], lambda i: (layer, 0, 0))
    return pl.pallas_call(
        _ffn_kernel,
        grid=(s // t,),
        in_specs=[row(d), lay(gin_all), lay(w1_all), lay(w2_all), lay(gout_all)],
        out_specs=row(d),
        out_shape=jax.ShapeDtypeStruct((s, d), F32),
        compiler_params=_params(1),
        name="ffn",
    )(x, gin_all, w1_all, w2_all, gout_all)


def _regroup_w_in(w_in):
    gate_lo, gate_hi = _C_GATE, _C_GATE + 3 * NSA_HEADS
    pad = jnp.zeros(w_in.shape[:2] + (LANES - 3 * NSA_HEADS,), w_in.dtype)
    return jnp.concatenate([w_in[..., :gate_lo], w_in[..., gate_lo:gate_hi], pad, w_in[..., gate_hi:]],
                           axis=-1).astype(MXU_DTYPE)


def kernel(x, positions, norm_g, w_in, cmp_pe, cmp_w1, cmp_w2, w_nsa_o, w_sb_o, w_out, w_ff1, w_ff2):
    b, s, d = x.shape
    depth = w_in.shape[0]
    ncp, ns = s // CMP_STRIDE, s // SEL_BLOCK
    nsp = -(-ns // LANES) * LANES
    half_w = CMP_STRIDE * HEAD_DIM

    w_in_r = _regroup_w_in(w_in)
    pe = cmp_pe.reshape(depth, 2, 2, half_w)
    w1 = cmp_w1.astype(MXU_DTYPE)
    w2 = cmp_w2.astype(MXU_DTYPE)
    wn, ws, wo = w_nsa_o.astype(MXU_DTYPE), w_sb_o.astype(MXU_DTYPE), w_out.astype(MXU_DTYPE)
    wf1, wf2 = w_ff1.astype(MXU_DTYPE), w_ff2.astype(MXU_DTYPE)
    g_pre, g_mix, g_ffn_in, g_ffn_out = (norm_g[:, n][:, None, :] for n in range(4))

    dim = jnp.arange(LANES) % HEAD_DIM
    half = ROT_DIM // 2
    inv_freq = jnp.power(ROPE_THETA, (dim % half).astype(F32) * (-2.0 / ROT_DIM))
    invf = jnp.where(dim < ROT_DIM, inv_freq, 0.0)[None, :].astype(F32)
    c_start = CMP_STRIDE * jnp.arange(ncp)[None, :]
    s_start = SEL_BLOCK * jnp.arange(nsp)[:, None]
    ovlt = ((c_start < s_start + SEL_BLOCK) & (c_start + CMP_BLOCK > s_start) & (s_start < s)).astype(MXU_DTYPE)

    outs = []
    for bi in range(b):
        xb = x[bi]
        pos = positions[bi][:, None]
        for layer in range(depth):
            (qpt, qrt, kc, vc, ks, vst, kw, vwt, gates_t, sbq, sbkt, sbv, mg) = _inproj(
                xb, g_pre, pos, invf, w_in_r, layer)
            kcmp, vcmpt = _compress(kc.reshape(ncp, half_w), vc.reshape(ncp, half_w), pe, w1, w2, layer)
            nsa_o = _nsa(qpt, qrt, gates_t, kcmp, vcmpt, ovlt, ks, vst, kw, vwt)
            sb_o = _sb(sbq, sbkt, sbv)
            xb = _mix(xb, nsa_o, sb_o, mg, wn, ws, wo, g_mix, layer)
            xb = _ffn(xb, g_ffn_in, wf1, wf2, g_ffn_out, layer)
        outs.append(xb)
    return jnp.stack(outs, axis=0)
```

```python
import functools

import jax
import jax.numpy as jnp
from jax import lax
from jax.experimental import pallas as pl
from jax.experimental.pallas import tpu as pltpu

F32 = jnp.float32
MXU_DTYPE = jnp.bfloat16

HEAD_DIM = 64
NSA_HEADS = 8
SB_HEADS = 4
SB_HEAD_DIM = 128
ROPE_THETA = 500000.0
ROT_DIM = HEAD_DIM // 4
CMP_BLOCK = 32
CMP_STRIDE = 16
SEL_BLOCK = 64
SEL_TOP_N = 8
WINDOW = 512
RMS_EPS = 1e-6
NSA_W = NSA_HEADS * HEAD_DIM
SB_W = SB_HEADS * SB_HEAD_DIM
LANES = 128
VT_ROWS = HEAD_DIM + 16
MASKED = -32768.0
LOG2_E = 1.4426950408889634
VMEM_LIMIT = 56 * 1024 * 1024

ROW_TILE = 1024
NSA_Q_BLOCK = 256
NSA_KEY_CHUNK = 512
SEL_CODE_BLOCKS = HEAD_DIM
CMP_VARIANTS = 4
SB_BLOCK = 512
SB_KEY_CHUNK = 256
SB_EXHAUSTED_LOG2 = 160.0


def _dot(a, b):
    return jnp.dot(a, b, preferred_element_type=F32)


def _rms(x, g):
    return x * lax.rsqrt(jnp.mean(x * x, axis=-1, keepdims=True) + RMS_EPS) * g


def _params(n_grid_dims):
    return pltpu.CompilerParams(dimension_semantics=("arbitrary",) * n_grid_dims,
                                vmem_limit_bytes=VMEM_LIMIT)


def _resident(block_shape, index_map):
    return pl.BlockSpec(block_shape, index_map, pipeline_mode=pl.Buffered(1))


_C_KV = NSA_W
_C_GATE = _C_KV + 6 * HEAD_DIM
_C_SB = _C_GATE + LANES
_C_MERGE = _C_SB + 3 * SB_W


def _inproj_kernel(x_ref, g_ref, pos_ref, invf_ref, w_ref,
                   qpt_ref, qrt_ref, kc_ref, vc_ref, ks_ref, vst_ref, kw_ref, vwt_ref,
                   gate_ref, sbq_ref, sbk_ref, sbv_ref, mg_ref):
    t, d_model = x_ref.shape
    hb = _rms(x_ref[...], g_ref[...]).astype(MXU_DTYPE)

    ang = pos_ref[...].astype(F32) * invf_ref[...]
    cos, sin = jnp.cos(ang), jnp.sin(ang)
    lane = lax.broadcasted_iota(jnp.int32, (1, LANES), 1)
    dim = lane % HEAD_DIM
    half = ROT_DIM // 2
    sin_up = jnp.where((dim >= half) & (dim < ROT_DIM), sin, 0.0)
    sin_dn = jnp.where(dim < half, -sin, 0.0)
    low = lane < HEAD_DIM

    def rope(xg):
        return xg * cos + pltpu.roll(xg, half, 1) * sin_up + pltpu.roll(xg, LANES - half, 1) * sin_dn

    pa = _dot(hb, w_ref[:, 0:_C_GATE])
    scale = LOG2_E * HEAD_DIM ** -0.5
    for j in range(NSA_W // LANES):
        qg = pa[:, j * LANES:(j + 1) * LANES]
        qpt_ref[j * LANES:(j + 1) * LANES, :] = (qg * scale).T.astype(qpt_ref.dtype)
        qrt_ref[j * LANES:(j + 1) * LANES, :] = (rope(qg) * scale).T.astype(qrt_ref.dtype)
    kc_ref[...] = pa[:, _C_KV:_C_KV + HEAD_DIM]
    vc_ref[...] = pa[:, _C_KV + HEAD_DIM:_C_KV + 2 * HEAD_DIM]

    row = pl.program_id(0) * t + lax.broadcasted_iota(jnp.int32, (t, 1), 0)
    code = jnp.where(lane - HEAD_DIM == (row // SEL_BLOCK) % SEL_CODE_BLOCKS, 1.0, 0.0)
    ksg = pa[:, _C_KV + 2 * HEAD_DIM:_C_KV + 4 * HEAD_DIM]
    ks_ref[...] = jnp.where(low, rope(ksg), code).astype(ks_ref.dtype)
    vs_t = jnp.where(low, pltpu.roll(ksg, HEAD_DIM, 1), 1.0).T[0:VT_ROWS]
    for c in range(vst_ref.shape[0]):
        vst_ref[c] = vs_t[:, c * vst_ref.shape[2]:(c + 1) * vst_ref.shape[2]].astype(vst_ref.dtype)
    kwg = pa[:, _C_KV + 4 * HEAD_DIM:_C_KV + 6 * HEAD_DIM]
    kw_ref[...] = jnp.where(low, rope(kwg), 0.0).astype(kw_ref.dtype)
    vw_t = jnp.where(low, pltpu.roll(kwg, HEAD_DIM, 1), 1.0).T[0:VT_ROWS]
    for c in range(vwt_ref.shape[0]):
        vwt_ref[c] = vw_t[:, c * vwt_ref.shape[2]:(c + 1) * vwt_ref.shape[2]].astype(vwt_ref.dtype)

    gate_ref[...] = jax.nn.sigmoid(_dot(hb, w_ref[:, _C_GATE:_C_SB])).T
    sb = _dot(hb, w_ref[:, _C_SB:_C_MERGE])
    sbq_ref[...] = (sb[:, 0:SB_W] * (LOG2_E * SB_HEAD_DIM ** -0.5)).astype(sbq_ref.dtype)
    sbk_t = sb[:, SB_W:2 * SB_W].T
    for c in range(sbk_ref.shape[0]):
        sbk_ref[c] = sbk_t[:, c * sbk_ref.shape[2]:(c + 1) * sbk_ref.shape[2]].astype(sbk_ref.dtype)
    sbv_ref[...] = sb[:, 2 * SB_W:3 * SB_W].astype(sbv_ref.dtype)
    for c in range(2):
        mg_ref[:, c * d_model:(c + 1) * d_model] = jax.nn.sigmoid(
            _dot(hb, w_ref[:, _C_MERGE + c * d_model:_C_MERGE + (c + 1) * d_model])).astype(mg_ref.dtype)


def _inproj(x, g, pos, invf, w_all, layer):
    s, d = x.shape
    t = min(ROW_TILE, s)
    wcols = w_all.shape[2]
    kck = min(NSA_KEY_CHUNK, s)
    sck = min(SB_KEY_CHUNK, s)
    wck = min(NSA_Q_BLOCK, s)
    row = lambda n: pl.BlockSpec((t, n), lambda i: (i, 0))
    col = lambda n: pl.BlockSpec((n, t), lambda i: (0, i))
    slab = lambda n, ck: pl.BlockSpec((t // ck, n, ck), lambda i: (i, 0, 0))
    sds = jax.ShapeDtypeStruct
    out_shape = [
        sds((NSA_W, s), MXU_DTYPE), sds((NSA_W, s), MXU_DTYPE),
        sds((s, HEAD_DIM), F32), sds((s, HEAD_DIM), F32),
        sds((s, LANES), MXU_DTYPE), sds((s // kck, VT_ROWS, kck), MXU_DTYPE),
        sds((s, LANES), MXU_DTYPE), sds((s // wck, VT_ROWS, wck), MXU_DTYPE),
        sds((LANES, s), F32),
        sds((s, SB_W), MXU_DTYPE), sds((s // sck, SB_W, sck), MXU_DTYPE), sds((s, SB_W), MXU_DTYPE),
        sds((s, 2 * d), MXU_DTYPE),
    ]
    out_specs = [col(NSA_W), col(NSA_W), row(HEAD_DIM), row(HEAD_DIM),
                 row(LANES), slab(VT_ROWS, kck), row(LANES), slab(VT_ROWS, wck),
                 col(LANES), row(SB_W), slab(SB_W, sck), row(SB_W), row(2 * d)]
    return pl.pallas_call(
        _inproj_kernel,
        grid=(s // t,),
        in_specs=[row(d),
                  _resident((None, 1, d), lambda i: (layer, 0, 0)),
                  row(1),
                  _resident((1, LANES), lambda i: (0, 0)),
                  _resident((None, d, wcols), lambda i: (layer, 0, 0))],
        out_specs=out_specs,
        out_shape=out_shape,
        compiler_params=_params(1),
        name="inproj",
    )(x, g, pos, invf, w_all)


def _compress_kernel(kc_ref, vc_ref, pe_ref, w1_ref, w2_ref, kcmp_ref, vcmpt_ref):
    nr, half_w = kc_ref.shape
    outs = []
    for kv, r_ref in enumerate((kc_ref, vc_ref)):
        r = r_ref[...]
        ha = _dot((r + pe_ref[kv, 0:1, :]).astype(MXU_DTYPE), w1_ref[kv, 0:half_w, :])
        hb = _dot((r + pe_ref[kv, 1:2, :]).astype(MXU_DTYPE), w1_ref[kv, half_w:2 * half_w, :])
        hid = ha + pltpu.roll(hb, nr - 1, 0)
        outs.append(_dot(jax.nn.gelu(hid).astype(MXU_DTYPE), w2_ref[kv]))
    pad = jnp.zeros_like(outs[0])
    kcmp_ref[...] = jnp.concatenate([outs[0], pad], axis=1).astype(kcmp_ref.dtype)
    vcmpt_ref[...] = jnp.concatenate([outs[1], pad], axis=1).T[0:HEAD_DIM].astype(vcmpt_ref.dtype)


def _compress(kc, vc, pe_all, w1_all, w2_all, layer):
    nr, half_w = kc.shape
    hidden = w1_all.shape[3]
    full = lambda shape: pl.BlockSpec(shape, lambda i: (0,) * len(shape))
    return pl.pallas_call(
        _compress_kernel,
        grid=(1,),
        in_specs=[full((nr, half_w)), full((nr, half_w)),
                  pl.BlockSpec((None, 2, 2, half_w), lambda i: (layer, 0, 0, 0)),
                  pl.BlockSpec((None, 2, 2 * half_w, hidden), lambda i: (layer, 0, 0, 0)),
                  pl.BlockSpec((None, 2, hidden, HEAD_DIM), lambda i: (layer, 0, 0, 0))],
        out_specs=[full((nr, LANES)), full((HEAD_DIM, nr))],
        out_shape=[jax.ShapeDtypeStruct((nr, LANES), MXU_DTYPE), jax.ShapeDtypeStruct((HEAD_DIM, nr), MXU_DTYPE)],
        compiler_params=_params(1),
        name="compress",
    )(kc, vc, pe_all, w1_all, w2_all)


def _nsa_kernel(qpt_ref, qrt_ref, gate_ref, kcmp_ref, vcmpt_ref, ovlt_ref, ks_ref, vst_ref, kw_ref, vwt_ref,
                out_ref, qaug_scr, mask_scr, m_scr, acc_scr, sa_scr, sb_scr, ta_scr, tb_scr, ocmp_scr, owin_scr, imp_scr, *, n_sel):
    tq = qpt_ref.shape[1]
    seq = ks_ref.shape[0]
    ncp = kcmp_ref.shape[0]
    nsp = ovlt_ref.shape[0]
    ck = vst_ref.shape[2]
    wck = vwt_ref.shape[2]
    nh = NSA_HEADS
    q0 = pl.program_id(0) * tq
    t = q0 + lax.broadcasted_iota(jnp.int32, (1, tq), 1)
    cols = [slice(h * tq, (h + 1) * tq) for h in range(nh)]

    def aug(qt_ref, h, tail):
        return jnp.concatenate([qt_ref[h * HEAD_DIM:(h + 1) * HEAD_DIM, :], tail], axis=0)

    zeros_tail = jnp.zeros((HEAD_DIM, tq), MXU_DTYPE)
    def cmp_branch(rows):
        qp_aug = jnp.concatenate([aug(qpt_ref, h, zeros_tail) for h in range(nh)], axis=1)
        sc_all = _dot(kcmp_ref[0:rows, :], qp_aug)
        cmp_last = CMP_STRIDE * lax.broadcasted_iota(jnp.int32, (rows, 1), 0) + (CMP_BLOCK - 1)
        vis_c = cmp_last <= t
        psum = jnp.zeros((rows, tq), F32)
        p_cmp = []
        for h in range(nh):
            sc = jnp.where(vis_c, sc_all[:, cols[h]], -1e30)
            top = jnp.max(sc, axis=0, keepdims=True)
            e = jnp.exp2(sc - jnp.where(top > -1e29, top, 0.0))
            den = jnp.sum(e, axis=0, keepdims=True)
            p = e * (1.0 / jnp.where(den > 0.0, den, 1.0))
            psum = psum + p
            p_cmp.append(p.astype(MXU_DTYPE))
        o_cmp = _dot(vcmpt_ref[:, 0:rows], jnp.concatenate(p_cmp, axis=1))
        p_hi = psum.astype(MXU_DTYPE)
        p_lo = (psum - p_hi.astype(F32)).astype(MXU_DTYPE)
        imp_scr[...] = _dot(ovlt_ref[:, 0:rows], p_hi) + _dot(ovlt_ref[:, 0:rows], p_lo)
        ocmp_scr[...] = o_cmp

    n_var = CMP_VARIANTS if ncp % (CMP_VARIANTS * LANES) == 0 else 1
    step = ncp // n_var
    n_vis = (q0 + tq - CMP_BLOCK) // CMP_STRIDE + 1
    variant = jnp.clip((n_vis + step - 1) // step - 1, 0, n_var - 1)
    for v in range(n_var):
        pl.when(variant == v)(functools.partial(cmp_branch, (v + 1) * step))
    imp = imp_scr[...]

    qr_aug = jnp.concatenate([aug(qrt_ref, h, zeros_tail) for h in range(nh)], axis=1)
    span = min(WINDOW + tq, seq)
    start = pl.multiple_of(jnp.maximum(q0 + tq - span, 0), wck)
    sw_all = _dot(kw_ref[pl.ds(start, span), :], qr_aug)

    def window_finish():
        kpos_w = start + lax.broadcasted_iota(jnp.int32, (span, 1), 0)
        vis_w = (kpos_w <= t) & (kpos_w > t - WINDOW)
        p_win = []
        for h in range(nh):
            sw = jnp.where(vis_w, sw_all[:, cols[h]], -1e30)
            p_win.append(jnp.exp2(sw - jnp.max(sw, axis=0, keepdims=True)).astype(MXU_DTYPE))
        p_win = jnp.concatenate(p_win, axis=1)
        ow = jnp.zeros((VT_ROWS, nh * tq), F32)
        for j in range(span // wck):
            ow = ow + _dot(vwt_ref[start // wck + j], p_win[j * wck:(j + 1) * wck, :])
        owin_scr[...] = ow[0:HEAD_DIM] * (1.0 / ow[HEAD_DIM:HEAD_DIM + 1])

    blk = lax.broadcasted_iota(jnp.int32, (nsp, 1), 0)
    cur = t // SEL_BLOCK
    valid = blk <= cur
    forced = (blk == 0) | (blk == cur) | (blk == cur - 1)
    n_forced = 3
    assert n_sel >= n_forced
    score = jnp.where(valid, jnp.where(forced, -jnp.inf, imp), -1.0)
    chosen = jnp.where(forced & valid, 1.0, 0.0)
    for _ in range(n_sel - n_forced):
        best = jnp.max(score, axis=0, keepdims=True)
        idx = jnp.min(jnp.where(score == best, blk, nsp), axis=0, keepdims=True)
        hit = blk == idx
        chosen = jnp.where(hit, 1.0, chosen)
        score = jnp.where(hit, -jnp.inf, score)
    mask_scr[0:nsp, :] = ((chosen - 1.0) * (-MASKED)).astype(mask_scr.dtype)
    mask_scr[nsp:nsp + SEL_CODE_BLOCKS, :] = jnp.full((SEL_CODE_BLOCKS, tq), MASKED, mask_scr.dtype)

    for h in range(nh):
        qaug_scr[0:HEAD_DIM, cols[h]] = qrt_ref[h * HEAD_DIM:(h + 1) * HEAD_DIM, :]
    m_scr[...] = jnp.full(m_scr.shape, -1e30, F32)
    acc_scr[...] = jnp.zeros(acc_scr.shape, F32)

    c_diag = q0 // ck

    def sel_scores(c, s_ref, top_ref, limit):
        k0 = pl.multiple_of(jnp.minimum(c, c_diag) * ck, ck)
        code0 = jnp.where(c < limit, (k0 // (SEL_BLOCK * SEL_CODE_BLOCKS)) * SEL_CODE_BLOCKS, nsp)
        mrows = mask_scr[pl.ds(pl.multiple_of(code0, SEL_CODE_BLOCKS), SEL_CODE_BLOCKS), :]
        for h in range(nh):
            qaug_scr[HEAD_DIM:2 * HEAD_DIM, cols[h]] = mrows
        s_all = _dot(ks_ref[pl.ds(k0, ck), :], qaug_scr[...])
        s_ref[...] = s_all
        top_ref[...] = jnp.max(s_all, axis=0, keepdims=True)

    def sel_update(c, s_ref, top_ref, causal):
        m_old = m_scr[...]
        if causal:
            kpos = c * ck + lax.broadcasted_iota(jnp.int32, (ck, 1), 0)
            bias = jnp.where(kpos <= t, 0.0, MASKED)
            tops = [jnp.max(s_ref[:, cols[h]] + bias, axis=0, keepdims=True) for h in range(nh)]
            m_new = jnp.maximum(m_old, jnp.concatenate(tops, axis=1))
        else:
            m_new = jnp.maximum(m_old, top_ref[...])
        p_all = []
        for h in range(nh):
            s = s_ref[:, cols[h]]
            if causal:
                s = s + bias
            p_all.append(jnp.exp2(s - m_new[:, cols[h]]).astype(MXU_DTYPE))
        pv = _dot(vst_ref[jnp.minimum(c, c_diag)], jnp.concatenate(p_all, axis=1))
        acc_scr[...] = jnp.exp2(m_old - m_new) * acc_scr[...] + pv
        m_scr[...] = m_new

    def sel_pair(c):
        sel_scores(c + 1, sa_scr, ta_scr, c_diag)
        sel_update(c, sb_scr, tb_scr, False)
        sel_scores(c + 2, sb_scr, tb_scr, c_diag)
        sel_update(c + 1, sa_scr, ta_scr, False)

    def sel_quad(j, carry):
        sel_pair(4 * j)
        sel_pair(4 * j + 2)
        return carry

    sel_scores(c_diag, sa_scr, ta_scr, c_diag + 1)
    sel_scores(0, sb_scr, tb_scr, c_diag)
    window_finish()
    sel_update(c_diag, sa_scr, ta_scr, True)
    lax.fori_loop(0, (c_diag + 1) // 2, lambda j, carry: (sel_pair(2 * j), carry)[1], 0)
    acc = acc_scr[...]
    o_sel = acc[0:HEAD_DIM] * (1.0 / acc[HEAD_DIM:HEAD_DIM + 1])

    o_cmp = ocmp_scr[...]
    o_win = owin_scr[...]
    gates = gate_ref[...]
    merged = []
    for h in range(nh):
        merged.append(gates[3 * h:3 * h + 1, :] * o_cmp[:, cols[h]]
                      + gates[3 * h + 1:3 * h + 2, :] * o_sel[:, cols[h]]
                      + gates[3 * h + 2:3 * h + 3, :] * o_win[:, cols[h]])
    out_ref[...] = jnp.concatenate(merged, axis=0).T.astype(out_ref.dtype)


def _nsa(qpt, qrt, gates_t, kcmp, vcmpt, ovlt, ks, vst, kw, vwt):
    s = ks.shape[0]
    tq = min(NSA_Q_BLOCK, s)
    nsp = ovlt.shape[0]
    col = lambda n: pl.BlockSpec((n, tq), lambda i: (0, i))
    res = lambda a: _resident(a.shape, lambda i: (0,) * a.ndim)
    return pl.pallas_call(
        functools.partial(_nsa_kernel, n_sel=min(SEL_TOP_N, s // SEL_BLOCK)),
        grid=(s // tq,),
        in_specs=[col(NSA_W), col(NSA_W), col(LANES), res(kcmp), res(vcmpt), res(ovlt),
                  res(ks), res(vst), res(kw), res(vwt)],
        out_specs=pl.BlockSpec((tq, NSA_W), lambda i: (i, 0)),
        out_shape=jax.ShapeDtypeStruct((s, NSA_W), MXU_DTYPE),
        scratch_shapes=[pltpu.VMEM((LANES, NSA_HEADS * tq), MXU_DTYPE),
                        pltpu.VMEM((nsp + SEL_CODE_BLOCKS, tq), MXU_DTYPE),
                        pltpu.VMEM((1, NSA_HEADS * tq), F32), pltpu.VMEM((VT_ROWS, NSA_HEADS * tq), F32),
                        pltpu.VMEM((vst.shape[2], NSA_HEADS * tq), F32),
                        pltpu.VMEM((vst.shape[2], NSA_HEADS * tq), F32),
                        pltpu.VMEM((1, NSA_HEADS * tq), F32), pltpu.VMEM((1, NSA_HEADS * tq), F32),
                        pltpu.VMEM((HEAD_DIM, NSA_HEADS * tq), F32), pltpu.VMEM((HEAD_DIM, NSA_HEADS * tq), F32),
                        pltpu.VMEM((nsp, tq), F32)],
        compiler_params=_params(1),
        name="nsa",
    )(qpt, qrt, gates_t, kcmp, vcmpt, ovlt, ks, vst, kw, vwt)


def _softplus2(z2):
    neg_abs = lax.bitcast_convert_type(lax.bitcast_convert_type(z2, jnp.uint32) | jnp.uint32(0x80000000), F32)
    return jnp.maximum(z2, 0.0) + jnp.log2(1.0 + jnp.exp2(neg_abs))


def _sb_kernel(q_ref, k_ref, v_ref, out_ref, acc_scr, run_scr, za_scr, zb_scr, done_scr):
    assert (q_ref.shape[0] // k_ref.shape[2]) % 2 == 0
    tb = q_ref.shape[0]
    ck = k_ref.shape[2]
    nsub = tb // ck
    i = pl.program_id(0)
    r = lax.broadcasted_iota(jnp.int32, (ck, ck), 0)
    c = lax.broadcasted_iota(jnp.int32, (ck, ck), 1)
    tri = jnp.where(r >= c, 1.0, 0.0).astype(MXU_DTYPE)
    before = c < r
    heads = [slice(h * SB_HEAD_DIM, (h + 1) * SB_HEAD_DIM) for h in range(SB_HEADS)]

    def logits(h, rows, chunk):
        return _dot(q_ref[rows, heads[h]], k_ref[chunk, heads[h], :])

    def step(h, rows, chunk, diag, first, z=None):
        hs = heads[h]
        if z is None:
            z = logits(h, rows, chunk)
        sp = _softplus2(z)
        if diag:
            sp = jnp.where(before, sp, 0.0)
        cs = _dot(sp.astype(MXU_DTYPE), tri)
        own = jnp.minimum(z - cs, 0.0)
        if first:
            a = jnp.exp2(own)
            run_scr[h, rows] = cs[:, 0:1]
        else:
            run = run_scr[h, rows]
            a = jnp.exp2(own - run)
            run_scr[h, rows] = run + cs[:, 0:1]
        if diag:
            a = jnp.where(before, a, 0.0)
        pv = _dot(a.astype(MXU_DTYPE), v_ref[pl.ds(pl.multiple_of(chunk * ck, ck), ck), hs])
        if first:
            acc_scr[h, rows] = pv
        else:
            acc_scr[h, rows] += pv

    plan = []
    for g in range(nsub):
        rows = slice(g * ck, (g + 1) * ck)
        plan += [(h, rows, i * nsub + g, True, True) for h in range(SB_HEADS)]
        for back in range(g):
            plan += [(h, rows, i * nsub + g - 1 - back, False, False) for h in range(SB_HEADS)]
    zs = [logits(h, rows, chunk) for h, rows, chunk, _, _ in plan]
    for args, z in zip(plan, zs):
        step(*args, z)

    everything = slice(0, tb)

    def lookahead(chunk, z_ref):
        for h in range(SB_HEADS):
            z_ref[h] = logits(h, everything, jnp.maximum(chunk, 0))

    def exhausted():
        least = functools.reduce(jnp.minimum, [run_scr[h] for h in range(SB_HEADS)])
        return (jnp.min(least) >= SB_EXHAUSTED_LOG2).astype(jnp.int32)

    def pair(state):
        j, _ = state
        cur = i * nsub - 1 - 2 * j
        lookahead(cur - 1, zb_scr)
        for h in range(SB_HEADS):
            step(h, everything, cur, False, False, za_scr[h])
        done_scr[0] = exhausted()

        @pl.when(done_scr[0] == 0)
        def _():
            lookahead(cur - 2, za_scr)
            for h in range(SB_HEADS):
                step(h, everything, cur - 1, False, False, zb_scr[h])
            done_scr[0] = exhausted()

        return j + 1, done_scr[0]

    lookahead(i * nsub - 1, za_scr)
    lax.while_loop(lambda state: (state[0] < (i * nsub) // 2) & (state[1] == 0), pair,
                   (jnp.int32(0), jnp.int32(0)))
    for h, hs in enumerate(heads):
        out_ref[:, hs] = acc_scr[h].astype(out_ref.dtype)


def _sb(q, kt, v):
    s = q.shape[0]
    tb = min(SB_BLOCK, s)
    return pl.pallas_call(
        _sb_kernel,
        grid=(s // tb,),
        in_specs=[pl.BlockSpec((tb, SB_W), lambda i: (i, 0)),
                  _resident(kt.shape, lambda i: (0, 0, 0)),
                  _resident((s, SB_W), lambda i: (0, 0))],
        out_specs=pl.BlockSpec((tb, SB_W), lambda i: (i, 0)),
        out_shape=jax.ShapeDtypeStruct((s, SB_W), MXU_DTYPE),
        scratch_shapes=[pltpu.VMEM((SB_HEADS, tb, SB_HEAD_DIM), F32), pltpu.VMEM((SB_HEADS, tb, 1), F32),
                        pltpu.VMEM((SB_HEADS, tb, kt.shape[2]), F32), pltpu.VMEM((SB_HEADS, tb, kt.shape[2]), F32),
                        pltpu.SMEM((1,), jnp.int32)],
        compiler_params=_params(1),
        name="sb",
    )(q, kt, v)


def _mix_kernel(x_ref, nsa_ref, sb_ref, mg_ref, wn_ref, ws_ref, wo_ref, g_ref, out_ref):
    d = x_ref.shape[1]
    y_nsa = _dot(nsa_ref[...], wn_ref[...])
    y_sb = _dot(sb_ref[...], ws_ref[...])
    merged = mg_ref[:, 0:d].astype(F32) * y_nsa + mg_ref[:, d:2 * d].astype(F32) * y_sb
    mixed = _dot(merged.astype(MXU_DTYPE), wo_ref[...])
    out_ref[...] = x_ref[...] + _rms(mixed, g_ref[...])


def _mix(x, nsa_o, sb_o, mg, wn_all, ws_all, wo_all, g_all, layer):
    s, d = x.shape
    t = min(ROW_TILE, s)
    row = lambda n: pl.BlockSpec((t, n), lambda i: (i, 0))
    lay = lambda a: _resident((None,) + a.shape[1:], lambda i: (layer, 0, 0))
    return pl.pallas_call(
        _mix_kernel,
        grid=(s // t,),
        in_specs=[row(d), row(NSA_W), row(SB_W), row(2 * d), lay(wn_all), lay(ws_all), lay(wo_all), lay(g_all)],
        out_specs=row(d),
        out_shape=jax.ShapeDtypeStruct((s, d), F32),
        compiler_params=_params(1),
        name="mix",
    )(x, nsa_o, sb_o, mg, wn_all, ws_all, wo_all, g_all)


def _ffn_kernel(x_ref, gin_ref, w1_ref, w2_ref, gout_ref, out_ref):
    x = x_ref[...]
    d = x.shape[1]
    hb = _rms(x, gin_ref[...]).astype(MXU_DTYPE)
    ff = jnp.zeros_like(x)
    for c in range(w1_ref.shape[1] // d):
        up = _dot(hb, w1_ref[:, c * d:(c + 1) * d])
        ff = ff + _dot(jnp.square(jnp.maximum(up, 0.0)).astype(MXU_DTYPE), w2_ref[c * d:(c + 1) * d, :])
    out_ref[...] = x + _rms(ff, gout_ref[...])


def _ffn(x, gin_all, w1_all, w2_all, gout_all, layer):
    s, d = x.shape
    t = min(ROW_TILE, s)
    row = lambda n: pl.BlockSpec((t, n), lambda i: (i, 0))
    lay = lambda a: _resident((None,) + a.shape[1:], lambda i: (layer, 0, 0))
    return pl.pallas_call(
        _ffn_kernel,
        grid=(s // t,),
        in_specs=[row(d), lay(gin_all), lay(w1_all), lay(w2_all), lay(gout_all)],
        out_specs=row(d),
        out_shape=jax.ShapeDtypeStruct((s, d), F32),
        compiler_params=_params(1),
        name="ffn",
    )(x, gin_all, w1_all, w2_all, gout_all)


def _regroup_w_in(w_in):
    gate_lo, gate_hi = _C_GATE, _C_GATE + 3 * NSA_HEADS
    pad = jnp.zeros(w_in.shape[:2] + (LANES - 3 * NSA_HEADS,), w_in.dtype)
    return jnp.concatenate([w_in[..., :gate_lo], w_in[..., gate_lo:gate_hi], pad, w_in[..., gate_hi:]],
                           axis=-1).astype(MXU_DTYPE)


def kernel(x, positions, norm_g, w_in, cmp_pe, cmp_w1, cmp_w2, w_nsa_o, w_sb_o, w_out, w_ff1, w_ff2):
    b, s, d = x.shape
    depth = w_in.shape[0]
    ncp, ns = s // CMP_STRIDE, s // SEL_BLOCK
    nsp = -(-ns // LANES) * LANES
    half_w = CMP_STRIDE * HEAD_DIM

    w_in_r = _regroup_w_in(w_in)
    pe = cmp_pe.reshape(depth, 2, 2, half_w)
    w1 = cmp_w1.astype(MXU_DTYPE)
    w2 = cmp_w2.astype(MXU_DTYPE)
    wn, ws, wo = w_nsa_o.astype(MXU_DTYPE), w_sb_o.astype(MXU_DTYPE), w_out.astype(MXU_DTYPE)
    wf1, wf2 = w_ff1.astype(MXU_DTYPE), w_ff2.astype(MXU_DTYPE)
    g_pre, g_mix, g_ffn_in, g_ffn_out = (norm_g[:, n][:, None, :] for n in range(4))

    dim = jnp.arange(LANES) % HEAD_DIM
    half = ROT_DIM // 2
    inv_freq = jnp.power(ROPE_THETA, (dim % half).astype(F32) * (-2.0 / ROT_DIM))
    invf = jnp.where(dim < ROT_DIM, inv_freq, 0.0)[None, :].astype(F32)
    c_start = CMP_STRIDE * jnp.arange(ncp)[None, :]
    s_start = SEL_BLOCK * jnp.arange(nsp)[:, None]
    ovlt = ((c_start < s_start + SEL_BLOCK) & (c_start + CMP_BLOCK > s_start) & (s_start < s)).astype(MXU_DTYPE)

    outs = []
    for bi in range(b):
        xb = x[bi]
        pos = positions[bi][:, None]
        for layer in range(depth):
            (qpt, qrt, kc, vc, ks, vst, kw, vwt, gates_t, sbq, sbkt, sbv, mg) = _inproj(
                xb, g_pre, pos, invf, w_in_r, layer)
            kcmp, vcmpt = _compress(kc.reshape(ncp, half_w), vc.reshape(ncp, half_w), pe, w1, w2, layer)
            nsa_o = _nsa(qpt, qrt, gates_t, kcmp, vcmpt, ovlt, ks, vst, kw, vwt)
            sb_o = _sb(sbq, sbkt, sbv)
            xb = _mix(xb, nsa_o, sb_o, mg, wn, ws, wo, g_mix, layer)
            xb = _ffn(xb, g_ffn_in, wf1, wf2, g_ffn_out, layer)
        outs.append(xb)
    return jnp.stack(outs, axis=0)
```

```python
import functools

import jax
import jax.numpy as jnp
from jax import lax
from jax.experimental import pallas as pl
from jax.experimental.pallas import tpu as pltpu

F32 = jnp.float32
MXU_DTYPE = jnp.bfloat16

HEAD_DIM = 64
NSA_HEADS = 8
SB_HEADS = 4
SB_HEAD_DIM = 128
ROPE_THETA = 500000.0
ROT_DIM = HEAD_DIM // 4
CMP_BLOCK = 32
CMP_STRIDE = 16
SEL_BLOCK = 64
SEL_TOP_N = 8
WINDOW = 512
RMS_EPS = 1e-6
NSA_W = NSA_HEADS * HEAD_DIM
SB_W = SB_HEADS * SB_HEAD_DIM
LANES = 128
VT_ROWS = HEAD_DIM + 16
MASKED = -32768.0
LOG2_E = 1.4426950408889634
VMEM_LIMIT = 56 * 1024 * 1024

ROW_TILE = 1024
NSA_Q_BLOCK = 256
NSA_KEY_CHUNK = 512
SEL_CODE_BLOCKS = HEAD_DIM
CMP_VARIANTS = 4
SB_BLOCK = 512
SB_KEY_CHUNK = 256
SB_EXHAUSTED_LOG2 = 160.0


def _dot(a, b):
    return jnp.dot(a, b, preferred_element_type=F32)


def _rms(x, g):
    return x * lax.rsqrt(jnp.mean(x * x, axis=-1, keepdims=True) + RMS_EPS) * g


def _params(n_grid_dims):
    return pltpu.CompilerParams(dimension_semantics=("arbitrary",) * n_grid_dims,
                                vmem_limit_bytes=VMEM_LIMIT)


def _resident(block_shape, index_map):
    return pl.BlockSpec(block_shape, index_map, pipeline_mode=pl.Buffered(1))


_C_KV = NSA_W
_C_GATE = _C_KV + 6 * HEAD_DIM
_C_SB = _C_GATE + LANES
_C_MERGE = _C_SB + 3 * SB_W


def _inproj_kernel(x_ref, g_ref, cos_ref, sin_ref, w_ref,
                   qpt_ref, qrt_ref, kc_ref, vc_ref, ks_ref, vst_ref, kw_ref, vwt_ref,
                   gate_ref, sbq_ref, sbk_ref, sbv_ref, mg_ref):
    t, d_model = x_ref.shape
    hb = _rms(x_ref[...], g_ref[...]).astype(MXU_DTYPE)

    cos, sin = cos_ref[...], sin_ref[...]
    lane = lax.broadcasted_iota(jnp.int32, (1, LANES), 1)
    dim = lane % HEAD_DIM
    half = ROT_DIM // 2
    sin_up = jnp.where((dim >= half) & (dim < ROT_DIM), sin, 0.0)
    sin_dn = jnp.where(dim < half, -sin, 0.0)
    low = lane < HEAD_DIM

    def rope(xg):
        return xg * cos + pltpu.roll(xg, half, 1) * sin_up + pltpu.roll(xg, LANES - half, 1) * sin_dn

    pa = _dot(hb, w_ref[:, 0:_C_GATE])
    scale = LOG2_E * HEAD_DIM ** -0.5
    for j in range(NSA_W // LANES):
        qg = pa[:, j * LANES:(j + 1) * LANES]
        qpt_ref[j * LANES:(j + 1) * LANES, :] = (qg * scale).T.astype(qpt_ref.dtype)
        qrt_ref[j * LANES:(j + 1) * LANES, :] = (rope(qg) * scale).T.astype(qrt_ref.dtype)
    kc_ref[...] = pa[:, _C_KV:_C_KV + HEAD_DIM]
    vc_ref[...] = pa[:, _C_KV + HEAD_DIM:_C_KV + 2 * HEAD_DIM]

    row = pl.program_id(0) * t + lax.broadcasted_iota(jnp.int32, (t, 1), 0)
    code = jnp.where(lane - HEAD_DIM == (row // SEL_BLOCK) % SEL_CODE_BLOCKS, 1.0, 0.0)
    ksg = pa[:, _C_KV + 2 * HEAD_DIM:_C_KV + 4 * HEAD_DIM]
    ks_ref[...] = jnp.where(low, rope(ksg), code).astype(ks_ref.dtype)
    vs_t = jnp.where(low, pltpu.roll(ksg, HEAD_DIM, 1), 1.0).T[0:VT_ROWS]
    for c in range(vst_ref.shape[0]):
        vst_ref[c] = vs_t[:, c * vst_ref.shape[2]:(c + 1) * vst_ref.shape[2]].astype(vst_ref.dtype)
    kwg = pa[:, _C_KV + 4 * HEAD_DIM:_C_KV + 6 * HEAD_DIM]
    kw_ref[...] = jnp.where(low, rope(kwg), 0.0).astype(kw_ref.dtype)
    vw_t = jnp.where(low, pltpu.roll(kwg, HEAD_DIM, 1), 1.0).T[0:VT_ROWS]
    for c in range(vwt_ref.shape[0]):
        vwt_ref[c] = vw_t[:, c * vwt_ref.shape[2]:(c + 1) * vwt_ref.shape[2]].astype(vwt_ref.dtype)

    gate_ref[...] = jax.nn.sigmoid(_dot(hb, w_ref[:, _C_GATE:_C_SB])).T
    sb = _dot(hb, w_ref[:, _C_SB:_C_MERGE])
    sbq_ref[...] = (sb[:, 0:SB_W] * (LOG2_E * SB_HEAD_DIM ** -0.5)).astype(sbq_ref.dtype)
    sbk_t = sb[:, SB_W:2 * SB_W].T
    for c in range(sbk_ref.shape[0]):
        sbk_ref[c] = sbk_t[:, c * sbk_ref.shape[2]:(c + 1) * sbk_ref.shape[2]].astype(sbk_ref.dtype)
    sbv_ref[...] = sb[:, 2 * SB_W:3 * SB_W].astype(sbv_ref.dtype)
    for c in range(2):
        mg_ref[:, c * d_model:(c + 1) * d_model] = jax.nn.sigmoid(
            _dot(hb, w_ref[:, _C_MERGE + c * d_model:_C_MERGE + (c + 1) * d_model])).astype(mg_ref.dtype)


def _inproj(x, g, cos, sin, w_all, layer):
    s, d = x.shape
    t = min(ROW_TILE, s)
    wcols = w_all.shape[2]
    kck = min(NSA_KEY_CHUNK, s)
    sck = min(SB_KEY_CHUNK, s)
    wck = min(NSA_Q_BLOCK, s)
    row = lambda n: pl.BlockSpec((t, n), lambda i: (i, 0))
    col = lambda n: pl.BlockSpec((n, t), lambda i: (0, i))
    slab = lambda n, ck: pl.BlockSpec((t // ck, n, ck), lambda i: (i, 0, 0))
    sds = jax.ShapeDtypeStruct
    out_shape = [
        sds((NSA_W, s), MXU_DTYPE), sds((NSA_W, s), MXU_DTYPE),
        sds((s, HEAD_DIM), F32), sds((s, HEAD_DIM), F32),
        sds((s, LANES), MXU_DTYPE), sds((s // kck, VT_ROWS, kck), MXU_DTYPE),
        sds((s, LANES), MXU_DTYPE), sds((s // wck, VT_ROWS, wck), MXU_DTYPE),
        sds((LANES, s), F32),
        sds((s, SB_W), MXU_DTYPE), sds((s // sck, SB_W, sck), MXU_DTYPE), sds((s, SB_W), MXU_DTYPE),
        sds((s, 2 * d), MXU_DTYPE),
    ]
    out_specs = [col(NSA_W), col(NSA_W), row(HEAD_DIM), row(HEAD_DIM),
                 row(LANES), slab(VT_ROWS, kck), row(LANES), slab(VT_ROWS, wck),
                 col(LANES), row(SB_W), slab(SB_W, sck), row(SB_W), row(2 * d)]
    return pl.pallas_call(
        _inproj_kernel,
        grid=(s // t,),
        in_specs=[row(d),
                  _resident((None, 1, d), lambda i: (layer, 0, 0)),
                  row(LANES),
                  row(LANES),
                  _resident((None, d, wcols), lambda i: (layer, 0, 0))],
        out_specs=out_specs,
        out_shape=out_shape,
        compiler_params=_params(1),
        name="inproj",
    )(x, g, cos, sin, w_all)


def _compress_kernel(kc_ref, vc_ref, pe_ref, w1_ref, w2_ref, kcmp_ref, vcmpt_ref):
    nr, half_w = kc_ref.shape
    outs = []
    for kv, r_ref in enumerate((kc_ref, vc_ref)):
        r = r_ref[...]
        ha = _dot((r + pe_ref[kv, 0:1, :]).astype(MXU_DTYPE), w1_ref[kv, 0:half_w, :])
        hb = _dot((r + pe_ref[kv, 1:2, :]).astype(MXU_DTYPE), w1_ref[kv, half_w:2 * half_w, :])
        hid = ha + pltpu.roll(hb, nr - 1, 0)
        outs.append(_dot(jax.nn.gelu(hid).astype(MXU_DTYPE), w2_ref[kv]))
    pad = jnp.zeros_like(outs[0])
    kcmp_ref[...] = jnp.concatenate([outs[0], pad], axis=1).astype(kcmp_ref.dtype)
    vcmpt_ref[...] = jnp.concatenate([outs[1], pad], axis=1).T[0:HEAD_DIM].astype(vcmpt_ref.dtype)


def _compress(kc, vc, pe_all, w1_all, w2_all, layer):
    nr, half_w = kc.shape
    hidden = w1_all.shape[3]
    full = lambda shape: pl.BlockSpec(shape, lambda i: (0,) * len(shape))
    return pl.pallas_call(
        _compress_kernel,
        grid=(1,),
        in_specs=[full((nr, half_w)), full((nr, half_w)),
                  pl.BlockSpec((None, 2, 2, half_w), lambda i: (layer, 0, 0, 0)),
                  pl.BlockSpec((None, 2, 2 * half_w, hidden), lambda i: (layer, 0, 0, 0)),
                  pl.BlockSpec((None, 2, hidden, HEAD_DIM), lambda i: (layer, 0, 0, 0))],
        out_specs=[full((nr, LANES)), full((HEAD_DIM, nr))],
        out_shape=[jax.ShapeDtypeStruct((nr, LANES), MXU_DTYPE), jax.ShapeDtypeStruct((HEAD_DIM, nr), MXU_DTYPE)],
        compiler_params=_params(1),
        name="compress",
    )(kc, vc, pe_all, w1_all, w2_all)


def _nsa_kernel(qpt_ref, qrt_ref, gate_ref, kcmp_ref, vcmpt_ref, ovlt_ref, ks_ref, vst_ref, kw_ref, vwt_ref,
                out_ref, qaug_scr, mask_scr, m_scr, acc_scr, sa_scr, sb_scr, ta_scr, tb_scr, ocmp_scr, owin_scr, imp_scr, *, n_sel):
    tq = qpt_ref.shape[1]
    seq = ks_ref.shape[0]
    ncp = kcmp_ref.shape[0]
    nsp = ovlt_ref.shape[0]
    ck = vst_ref.shape[2]
    wck = vwt_ref.shape[2]
    nh = NSA_HEADS
    q0 = pl.program_id(0) * tq
    t = q0 + lax.broadcasted_iota(jnp.int32, (1, tq), 1)
    cols = [slice(h * tq, (h + 1) * tq) for h in range(nh)]

    def aug(qt_ref, h, tail):
        return jnp.concatenate([qt_ref[h * HEAD_DIM:(h + 1) * HEAD_DIM, :], tail], axis=0)

    zeros_tail = jnp.zeros((HEAD_DIM, tq), MXU_DTYPE)
    def cmp_branch(rows):
        qp_aug = jnp.concatenate([aug(qpt_ref, h, zeros_tail) for h in range(nh)], axis=1)
        sc_all = _dot(kcmp_ref[0:rows, :], qp_aug)
        cmp_last = CMP_STRIDE * lax.broadcasted_iota(jnp.int32, (rows, 1), 0) + (CMP_BLOCK - 1)
        vis_c = cmp_last <= t
        psum = jnp.zeros((rows, tq), F32)
        p_cmp = []
        for h in range(nh):
            sc = jnp.where(vis_c, sc_all[:, cols[h]], -1e30)
            top = jnp.max(sc, axis=0, keepdims=True)
            e = jnp.exp2(sc - jnp.where(top > -1e29, top, 0.0))
            den = jnp.sum(e, axis=0, keepdims=True)
            p = e * (1.0 / jnp.where(den > 0.0, den, 1.0))
            psum = psum + p
            p_cmp.append(p.astype(MXU_DTYPE))
        o_cmp = _dot(vcmpt_ref[:, 0:rows], jnp.concatenate(p_cmp, axis=1))
        p_hi = psum.astype(MXU_DTYPE)
        p_lo = (psum - p_hi.astype(F32)).astype(MXU_DTYPE)
        imp_scr[...] = _dot(ovlt_ref[:, 0:rows], p_hi) + _dot(ovlt_ref[:, 0:rows], p_lo)
        ocmp_scr[...] = o_cmp

    n_var = CMP_VARIANTS if ncp % (CMP_VARIANTS * LANES) == 0 else 1
    step = ncp // n_var
    n_vis = (q0 + tq - CMP_BLOCK) // CMP_STRIDE + 1
    variant = jnp.clip((n_vis + step - 1) // step - 1, 0, n_var - 1)
    for v in range(n_var):
        pl.when(variant == v)(functools.partial(cmp_branch, (v + 1) * step))
    imp = imp_scr[...]

    qr_aug = jnp.concatenate([aug(qrt_ref, h, zeros_tail) for h in range(nh)], axis=1)
    span = min(WINDOW + tq, seq)
    start = pl.multiple_of(jnp.maximum(q0 + tq - span, 0), wck)
    sw_all = _dot(kw_ref[pl.ds(start, span), :], qr_aug)

    def window_finish():
        kpos_w = start + lax.broadcasted_iota(jnp.int32, (span, 1), 0)
        vis_w = (kpos_w <= t) & (kpos_w > t - WINDOW)
        p_win = []
        for h in range(nh):
            sw = jnp.where(vis_w, sw_all[:, cols[h]], -1e30)
            p_win.append(jnp.exp2(sw - jnp.max(sw, axis=0, keepdims=True)).astype(MXU_DTYPE))
        p_win = jnp.concatenate(p_win, axis=1)
        ow = jnp.zeros((VT_ROWS, nh * tq), F32)
        for j in range(span // wck):
            ow = ow + _dot(vwt_ref[start // wck + j], p_win[j * wck:(j + 1) * wck, :])
        owin_scr[...] = ow[0:HEAD_DIM] * (1.0 / ow[HEAD_DIM:HEAD_DIM + 1])

    blk = lax.broadcasted_iota(jnp.int32, (nsp, 1), 0)
    cur = t // SEL_BLOCK
    valid = blk <= cur
    forced = (blk == 0) | (blk == cur) | (blk == cur - 1)
    n_forced = 3
    assert n_sel >= n_forced
    score = jnp.where(valid, jnp.where(forced, -jnp.inf, imp), -1.0)
    chosen = jnp.where(forced & valid, 1.0, 0.0)
    for _ in range(n_sel - n_forced):
        best = jnp.max(score, axis=0, keepdims=True)
        idx = jnp.min(jnp.where(score == best, blk, nsp), axis=0, keepdims=True)
        hit = blk == idx
        chosen = jnp.where(hit, 1.0, chosen)
        score = jnp.where(hit, -jnp.inf, score)
    mask_scr[0:nsp, :] = ((chosen - 1.0) * (-MASKED)).astype(mask_scr.dtype)
    mask_scr[nsp:nsp + SEL_CODE_BLOCKS, :] = jnp.full((SEL_CODE_BLOCKS, tq), MASKED, mask_scr.dtype)

    for h in range(nh):
        qaug_scr[0:HEAD_DIM, cols[h]] = qrt_ref[h * HEAD_DIM:(h + 1) * HEAD_DIM, :]
    m_scr[...] = jnp.full(m_scr.shape, -1e30, F32)
    acc_scr[...] = jnp.zeros(acc_scr.shape, F32)

    c_diag = q0 // ck

    def sel_scores(c, s_ref, top_ref, limit):
        k0 = pl.multiple_of(jnp.minimum(c, c_diag) * ck, ck)
        code0 = jnp.where(c < limit, (k0 // (SEL_BLOCK * SEL_CODE_BLOCKS)) * SEL_CODE_BLOCKS, nsp)
        mrows = mask_scr[pl.ds(pl.multiple_of(code0, SEL_CODE_BLOCKS), SEL_CODE_BLOCKS), :]
        for h in range(nh):
            qaug_scr[HEAD_DIM:2 * HEAD_DIM, cols[h]] = mrows
        s_all = _dot(ks_ref[pl.ds(k0, ck), :], qaug_scr[...])
        s_ref[...] = s_all
        top_ref[...] = jnp.max(s_all, axis=0, keepdims=True)

    def sel_update(c, s_ref, top_ref, causal):
        m_old = m_scr[...]
        if causal:
            kpos = c * ck + lax.broadcasted_iota(jnp.int32, (ck, 1), 0)
            bias = jnp.where(kpos <= t, 0.0, MASKED)
            tops = [jnp.max(s_ref[:, cols[h]] + bias, axis=0, keepdims=True) for h in range(nh)]
            m_new = jnp.maximum(m_old, jnp.concatenate(tops, axis=1))
        else:
            m_new = jnp.maximum(m_old, top_ref[...])
        p_all = []
        for h in range(nh):
            s = s_ref[:, cols[h]]
            if causal:
                s = s + bias
            p_all.append(jnp.exp2(s - m_new[:, cols[h]]).astype(MXU_DTYPE))
        pv = _dot(vst_ref[jnp.minimum(c, c_diag)], jnp.concatenate(p_all, axis=1))
        acc_scr[...] = jnp.exp2(m_old - m_new) * acc_scr[...] + pv
        m_scr[...] = m_new

    def sel_pair(c):
        sel_scores(c + 1, sa_scr, ta_scr, c_diag)
        sel_update(c, sb_scr, tb_scr, False)
        sel_scores(c + 2, sb_scr, tb_scr, c_diag)
        sel_update(c + 1, sa_scr, ta_scr, False)

    def sel_quad(j, carry):
        sel_pair(4 * j)
        sel_pair(4 * j + 2)
        return carry

    sel_scores(c_diag, sa_scr, ta_scr, c_diag + 1)
    sel_scores(0, sb_scr, tb_scr, c_diag)
    window_finish()
    sel_update(c_diag, sa_scr, ta_scr, True)
    n_quads = c_diag // 4
    lax.fori_loop(0, n_quads, sel_quad, 0)
    lax.fori_loop(0, (c_diag - 4 * n_quads + 1) // 2, lambda j, carry: (sel_pair(4 * n_quads + 2 * j), carry)[1], 0)
    acc = acc_scr[...]
    o_sel = acc[0:HEAD_DIM] * (1.0 / acc[HEAD_DIM:HEAD_DIM + 1])

    o_cmp = ocmp_scr[...]
    o_win = owin_scr[...]
    gates = gate_ref[...]
    merged = []
    for h in range(nh):
        merged.append(gates[3 * h:3 * h + 1, :] * o_cmp[:, cols[h]]
                      + gates[3 * h + 1:3 * h + 2, :] * o_sel[:, cols[h]]
                      + gates[3 * h + 2:3 * h + 3, :] * o_win[:, cols[h]])
    out_ref[...] = jnp.concatenate(merged, axis=0).T.astype(out_ref.dtype)


def _nsa(qpt, qrt, gates_t, kcmp, vcmpt, ovlt, ks, vst, kw, vwt):
    s = ks.shape[0]
    tq = min(NSA_Q_BLOCK, s)
    nsp = ovlt.shape[0]
    col = lambda n: pl.BlockSpec((n, tq), lambda i: (0, i))
    res = lambda a: _resident(a.shape, lambda i: (0,) * a.ndim)
    return pl.pallas_call(
        functools.partial(_nsa_kernel, n_sel=min(SEL_TOP_N, s // SEL_BLOCK)),
        grid=(s // tq,),
        in_specs=[col(NSA_W), col(NSA_W), col(LANES), res(kcmp), res(vcmpt), res(ovlt),
                  res(ks), res(vst), res(kw), res(vwt)],
        out_specs=pl.BlockSpec((tq, NSA_W), lambda i: (i, 0)),
        out_shape=jax.ShapeDtypeStruct((s, NSA_W), MXU_DTYPE),
        scratch_shapes=[pltpu.VMEM((LANES, NSA_HEADS * tq), MXU_DTYPE),
                        pltpu.VMEM((nsp + SEL_CODE_BLOCKS, tq), MXU_DTYPE),
                        pltpu.VMEM((1, NSA_HEADS * tq), F32), pltpu.VMEM((VT_ROWS, NSA_HEADS * tq), F32),
                        pltpu.VMEM((vst.shape[2], NSA_HEADS * tq), F32),
                        pltpu.VMEM((vst.shape[2], NSA_HEADS * tq), F32),
                        pltpu.VMEM((1, NSA_HEADS * tq), F32), pltpu.VMEM((1, NSA_HEADS * tq), F32),
                        pltpu.VMEM((HEAD_DIM, NSA_HEADS * tq), F32), pltpu.VMEM((HEAD_DIM, NSA_HEADS * tq), F32),
                        pltpu.VMEM((nsp, tq), F32)],
        compiler_params=_params(1),
        name="nsa",
    )(qpt, qrt, gates_t, kcmp, vcmpt, ovlt, ks, vst, kw, vwt)


def _softplus2(z2):
    neg_abs = lax.bitcast_convert_type(lax.bitcast_convert_type(z2, jnp.uint32) | jnp.uint32(0x80000000), F32)
    return jnp.maximum(z2, 0.0) + jnp.log2(1.0 + jnp.exp2(neg_abs))


def _sb_kernel(q_ref, k_ref, v_ref, out_ref, acc_scr, run_scr, za_scr, zb_scr, done_scr):
    assert (q_ref.shape[0] // k_ref.shape[2]) % 2 == 0
    tb = q_ref.shape[0]
    ck = k_ref.shape[2]
    nsub = tb // ck
    i = pl.program_id(0)
    r = lax.broadcasted_iota(jnp.int32, (ck, ck), 0)
    c = lax.broadcasted_iota(jnp.int32, (ck, ck), 1)
    tri = jnp.where(r >= c, 1.0, 0.0).astype(MXU_DTYPE)
    before = c < r
    heads = [slice(h * SB_HEAD_DIM, (h + 1) * SB_HEAD_DIM) for h in range(SB_HEADS)]

    def logits(h, rows, chunk):
        return _dot(q_ref[rows, heads[h]], k_ref[chunk, heads[h], :])

    def step(h, rows, chunk, diag, first, z=None):
        hs = heads[h]
        if z is None:
            z = logits(h, rows, chunk)
        sp = _softplus2(z)
        if diag:
            sp = jnp.where(before, sp, 0.0)
        cs = _dot(sp.astype(MXU_DTYPE), tri)
        own = jnp.minimum(z - cs, 0.0)
        if first:
            a = jnp.exp2(own)
            run_scr[h, rows] = cs[:, 0:1]
        else:
            run = run_scr[h, rows]
            a = jnp.exp2(own - run)
            run_scr[h, rows] = run + cs[:, 0:1]
        if diag:
            a = jnp.where(before, a, 0.0)
        pv = _dot(a.astype(MXU_DTYPE), v_ref[pl.ds(pl.multiple_of(chunk * ck, ck), ck), hs])
        if first:
            acc_scr[h, rows] = pv
        else:
            acc_scr[h, rows] += pv

    plan = []
    for g in range(nsub):
        rows = slice(g * ck, (g + 1) * ck)
        plan += [(h, rows, i * nsub + g, True, True) for h in range(SB_HEADS)]
        for back in range(g):
            plan += [(h, rows, i * nsub + g - 1 - back, False, False) for h in range(SB_HEADS)]
    zs = [logits(h, rows, chunk) for h, rows, chunk, _, _ in plan]
    for args, z in zip(plan, zs):
        step(*args, z)

    everything = slice(0, tb)

    def lookahead(chunk, z_ref):
        for h in range(SB_HEADS):
            z_ref[h] = logits(h, everything, jnp.maximum(chunk, 0))

    def exhausted():
        least = functools.reduce(jnp.minimum, [run_scr[h] for h in range(SB_HEADS)])
        return (jnp.min(least) >= SB_EXHAUSTED_LOG2).astype(jnp.int32)

    def pair(state):
        j, _ = state
        cur = i * nsub - 1 - 2 * j
        lookahead(cur - 1, zb_scr)
        for h in range(SB_HEADS):
            step(h, everything, cur, False, False, za_scr[h])
        done_scr[0] = exhausted()

        @pl.when(done_scr[0] == 0)
        def _():
            lookahead(cur - 2, za_scr)
            for h in range(SB_HEADS):
                step(h, everything, cur - 1, False, False, zb_scr[h])
            done_scr[0] = exhausted()

        return j + 1, done_scr[0]

    lookahead(i * nsub - 1, za_scr)
    lax.while_loop(lambda state: (state[0] < (i * nsub) // 2) & (state[1] == 0), pair,
                   (jnp.int32(0), jnp.int32(0)))
    for h, hs in enumerate(heads):
        out_ref[:, hs] = acc_scr[h].astype(out_ref.dtype)


def _sb(q, kt, v):
    s = q.shape[0]
    tb = min(SB_BLOCK, s)
    return pl.pallas_call(
        _sb_kernel,
        grid=(s // tb,),
        in_specs=[pl.BlockSpec((tb, SB_W), lambda i: (i, 0)),
                  _resident(kt.shape, lambda i: (0, 0, 0)),
                  _resident((s, SB_W), lambda i: (0, 0))],
        out_specs=pl.BlockSpec((tb, SB_W), lambda i: (i, 0)),
        out_shape=jax.ShapeDtypeStruct((s, SB_W), MXU_DTYPE),
        scratch_shapes=[pltpu.VMEM((SB_HEADS, tb, SB_HEAD_DIM), F32), pltpu.VMEM((SB_HEADS, tb, 1), F32),
                        pltpu.VMEM((SB_HEADS, tb, kt.shape[2]), F32), pltpu.VMEM((SB_HEADS, tb, kt.shape[2]), F32),
                        pltpu.SMEM((1,), jnp.int32)],
        compiler_params=_params(1),
        name="sb",
    )(q, kt, v)


def _mix_kernel(x_ref, nsa_ref, sb_ref, mg_ref, wn_ref, ws_ref, wo_ref, g_ref, out_ref):
    d = x_ref.shape[1]
    y_nsa = _dot(nsa_ref[...], wn_ref[...])
    y_sb = _dot(sb_ref[...], ws_ref[...])
    merged = mg_ref[:, 0:d].astype(F32) * y_nsa + mg_ref[:, d:2 * d].astype(F32) * y_sb
    mixed = _dot(merged.astype(MXU_DTYPE), wo_ref[...])
    out_ref[...] = x_ref[...] + _rms(mixed, g_ref[...])


def _mix(x, nsa_o, sb_o, mg, wn_all, ws_all, wo_all, g_all, layer):
    s, d = x.shape
    t = min(ROW_TILE, s)
    row = lambda n: pl.BlockSpec((t, n), lambda i: (i, 0))
    lay = lambda a: _resident((None,) + a.shape[1:], lambda i: (layer, 0, 0))
    return pl.pallas_call(
        _mix_kernel,
        grid=(s // t,),
        in_specs=[row(d), row(NSA_W), row(SB_W), row(2 * d), lay(wn_all), lay(ws_all), lay(wo_all), lay(g_all)],
        out_specs=row(d),
        out_shape=jax.ShapeDtypeStruct((s, d), F32),
        compiler_params=_params(1),
        name="mix",
    )(x, nsa_o, sb_o, mg, wn_all, ws_all, wo_all, g_all)


def _ffn_kernel(x_ref, gin_ref, w1_ref, w2_ref, gout_ref, out_ref):
    x = x_ref[...]
    d = x.shape[1]
    hb = _rms(x, gin_ref[...]).astype(MXU_DTYPE)
    ff = jnp.zeros_like(x)
    for c in range(w1_ref.shape[1] // d):
        up = _dot(hb, w1_ref[:, c * d:(c + 1) * d])
        ff = ff + _dot(jnp.square(jnp.maximum(up, 0.0)).astype(MXU_DTYPE), w2_ref[c * d:(c + 1) * d, :])
    out_ref[...] = x + _rms(ff, gout_ref[...])


def _ffn(x, gin_all, w1_all, w2_all, gout_all, layer):
    s, d = x.shape
    t = min(ROW_TILE, s)
    row = lambda n: pl.BlockSpec((t, n), lambda i: (i, 0))
    lay = lambda a: _resident((None,) + a.shape[1:], lambda i: (layer, 0, 0))
    return pl.pallas_call(
        _ffn_kernel,
        grid=(s // t,),
        in_specs=[row(d), lay(gin_all), lay(w1_all), lay(w2_all), lay(gout_all)],
        out_specs=row(d),
        out_shape=jax.ShapeDtypeStruct((s, d), F32),
        compiler_params=_params(1),
        name="ffn",
    )(x, gin_all, w1_all, w2_all, gout_all)


def _regroup_w_in(w_in):
    gate_lo, gate_hi = _C_GATE, _C_GATE + 3 * NSA_HEADS
    pad = jnp.zeros(w_in.shape[:2] + (LANES - 3 * NSA_HEADS,), w_in.dtype)
    return jnp.concatenate([w_in[..., :gate_lo], w_in[..., gate_lo:gate_hi], pad, w_in[..., gate_hi:]],
                           axis=-1).astype(MXU_DTYPE)


def kernel(x, positions, norm_g, w_in, cmp_pe, cmp_w1, cmp_w2, w_nsa_o, w_sb_o, w_out, w_ff1, w_ff2):
    b, s, d = x.shape
    depth = w_in.shape[0]
    ncp, ns = s // CMP_STRIDE, s // SEL_BLOCK
    nsp = -(-ns // LANES) * LANES
    half_w = CMP_STRIDE * HEAD_DIM

    w_in_r = _regroup_w_in(w_in)
    pe = cmp_pe.reshape(depth, 2, 2, half_w)
    w1 = cmp_w1.astype(MXU_DTYPE)
    w2 = cmp_w2.astype(MXU_DTYPE)
    wn, ws, wo = w_nsa_o.astype(MXU_DTYPE), w_sb_o.astype(MXU_DTYPE), w_out.astype(MXU_DTYPE)
    wf1, wf2 = w_ff1.astype(MXU_DTYPE), w_ff2.astype(MXU_DTYPE)
    g_pre, g_mix, g_ffn_in, g_ffn_out = (norm_g[:, n][:, None, :] for n in range(4))

    dim = jnp.arange(LANES) % HEAD_DIM
    half = ROT_DIM // 2
    inv_freq = jnp.power(ROPE_THETA, (dim % half).astype(F32) * (-2.0 / ROT_DIM))
    invf = jnp.where(dim < ROT_DIM, inv_freq, 0.0)[None, :].astype(F32)
    c_start = CMP_STRIDE * jnp.arange(ncp)[None, :]
    s_start = SEL_BLOCK * jnp.arange(nsp)[:, None]
    ovlt = ((c_start < s_start + SEL_BLOCK) & (c_start + CMP_BLOCK > s_start) & (s_start < s)).astype(MXU_DTYPE)

    outs = []
    for bi in range(b):
        xb = x[bi]
        ang = positions[bi][:, None].astype(F32) * invf
        cos, sin = jnp.cos(ang), jnp.sin(ang)
        for layer in range(depth):
            (qpt, qrt, kc, vc, ks, vst, kw, vwt, gates_t, sbq, sbkt, sbv, mg) = _inproj(
                xb, g_pre, cos, sin, w_in_r, layer)
            kcmp, vcmpt = _compress(kc.reshape(ncp, half_w), vc.reshape(ncp, half_w), pe, w1, w2, layer)
            nsa_o = _nsa(qpt, qrt, gates_t, kcmp, vcmpt, ovlt, ks, vst, kw, vwt)
            sb_o = _sb(sbq, sbkt, sbv)
            xb = _mix(xb, nsa_o, sb_o, mg, wn, ws, wo, g_mix, layer)
            xb = _ffn(xb, g_ffn_in, wf1, wf2, g_ffn_out, layer)
        outs.append(xb)
    return jnp.stack(outs, axis=0)
```

```python
import functools

import jax
import jax.numpy as jnp
from jax import lax
from jax.experimental import pallas as pl
from jax.experimental.pallas import tpu as pltpu

F32 = jnp.float32
MXU_DTYPE = jnp.bfloat16

HEAD_DIM = 64
NSA_HEADS = 8
SB_HEADS = 4
SB_HEAD_DIM = 128
ROPE_THETA = 500000.0
ROT_DIM = HEAD_DIM // 4
CMP_BLOCK = 32
CMP_STRIDE = 16
SEL_BLOCK = 64
SEL_TOP_N = 8
WINDOW = 512
RMS_EPS = 1e-6
NSA_W = NSA_HEADS * HEAD_DIM
SB_W = SB_HEADS * SB_HEAD_DIM
LANES = 128
VT_ROWS = HEAD_DIM + 16
MASKED = -32768.0
LOG2_E = 1.4426950408889634
VMEM_LIMIT = 56 * 1024 * 1024

ROW_TILE = 1024
NSA_Q_BLOCK = 256
NSA_KEY_CHUNK = 512
SEL_CODE_BLOCKS = HEAD_DIM
CMP_VARIANTS = 4
SB_BLOCK = 512
SB_KEY_CHUNK = 256
SB_EXHAUSTED_LOG2 = 160.0


def _dot(a, b):
    return jnp.dot(a, b, preferred_element_type=F32)


def _rms(x, g):
    return x * lax.rsqrt(jnp.mean(x * x, axis=-1, keepdims=True) + RMS_EPS) * g


def _params(n_grid_dims):
    return pltpu.CompilerParams(dimension_semantics=("arbitrary",) * n_grid_dims,
                                vmem_limit_bytes=VMEM_LIMIT)


def _resident(block_shape, index_map):
    return pl.BlockSpec(block_shape, index_map, pipeline_mode=pl.Buffered(1))


_C_KV = NSA_W
_C_GATE = _C_KV + 6 * HEAD_DIM
_C_SB = _C_GATE + LANES


def _inproj_kernel(x_ref, g_ref, pos_ref, invf_ref, wa_ref, wb_ref,
                   qpt_ref, qrt_ref, kc_ref, vc_ref, ks_ref, vst_ref, kw_ref, vwt_ref,
                   gate_ref, sbq_ref, sbk_ref, sbv_ref, mg_ref):
    t, d_model = x_ref.shape
    hb = _rms(x_ref[...], g_ref[...]).astype(MXU_DTYPE)

    ang = pos_ref[...].astype(F32) * invf_ref[...]
    cos, sin = jnp.cos(ang), jnp.sin(ang)
    lane = lax.broadcasted_iota(jnp.int32, (1, LANES), 1)
    dim = lane % HEAD_DIM
    half = ROT_DIM // 2
    sin_up = jnp.where((dim >= half) & (dim < ROT_DIM), sin, 0.0)
    sin_dn = jnp.where(dim < half, -sin, 0.0)
    low = lane < HEAD_DIM

    def rope(xg):
        return xg * cos + pltpu.roll(xg, half, 1) * sin_up + pltpu.roll(xg, LANES - half, 1) * sin_dn

    pa = _dot(hb, wa_ref[:, 0:_C_GATE])
    scale = LOG2_E * HEAD_DIM ** -0.5
    for j in range(NSA_W // LANES):
        qg = pa[:, j * LANES:(j + 1) * LANES]
        qpt_ref[j * LANES:(j + 1) * LANES, :] = (qg * scale).T.astype(qpt_ref.dtype)
        qrt_ref[j * LANES:(j + 1) * LANES, :] = (rope(qg) * scale).T.astype(qrt_ref.dtype)
    kc_ref[...] = pa[:, _C_KV:_C_KV + HEAD_DIM]
    vc_ref[...] = pa[:, _C_KV + HEAD_DIM:_C_KV + 2 * HEAD_DIM]

    row = pl.program_id(0) * t + lax.broadcasted_iota(jnp.int32, (t, 1), 0)
    code = jnp.where(lane - HEAD_DIM == (row // SEL_BLOCK) % SEL_CODE_BLOCKS, 1.0, 0.0)
    ksg = pa[:, _C_KV + 2 * HEAD_DIM:_C_KV + 4 * HEAD_DIM]
    ks_ref[...] = jnp.where(low, rope(ksg), code).astype(ks_ref.dtype)
    vs_t = jnp.where(low, pltpu.roll(ksg, HEAD_DIM, 1), 1.0).T[0:VT_ROWS]
    for c in range(vst_ref.shape[0]):
        vst_ref[c] = vs_t[:, c * vst_ref.shape[2]:(c + 1) * vst_ref.shape[2]].astype(vst_ref.dtype)
    kwg = pa[:, _C_KV + 4 * HEAD_DIM:_C_KV + 6 * HEAD_DIM]
    kw_ref[...] = jnp.where(low, rope(kwg), 0.0).astype(kw_ref.dtype)
    vw_t = jnp.where(low, pltpu.roll(kwg, HEAD_DIM, 1), 1.0).T[0:VT_ROWS]
    for c in range(vwt_ref.shape[0]):
        vwt_ref[c] = vw_t[:, c * vwt_ref.shape[2]:(c + 1) * vwt_ref.shape[2]].astype(vwt_ref.dtype)

    gate_ref[...] = jax.nn.sigmoid(_dot(hb, wa_ref[:, _C_GATE:_C_SB])).T
    sb = _dot(hb, wb_ref[:, 0:3 * SB_W])
    sbq_ref[...] = (sb[:, 0:SB_W] * (LOG2_E * SB_HEAD_DIM ** -0.5)).astype(sbq_ref.dtype)
    sbk_t = sb[:, SB_W:2 * SB_W].T
    for c in range(sbk_ref.shape[0]):
        sbk_ref[c] = sbk_t[:, c * sbk_ref.shape[2]:(c + 1) * sbk_ref.shape[2]].astype(sbk_ref.dtype)
    sbv_ref[...] = sb[:, 2 * SB_W:3 * SB_W].astype(sbv_ref.dtype)
    for c in range(2):
        mg_ref[:, c * d_model:(c + 1) * d_model] = jax.nn.sigmoid(
            _dot(hb, wb_ref[:, 3 * SB_W + c * d_model:3 * SB_W + (c + 1) * d_model])).astype(mg_ref.dtype)


def _inproj(x, g, pos, invf, w_head, w_tail, layer):
    s, d = x.shape
    t = min(ROW_TILE, s)
    kck = min(NSA_KEY_CHUNK, s)
    sck = min(SB_KEY_CHUNK, s)
    wck = min(NSA_Q_BLOCK, s)
    row = lambda n: pl.BlockSpec((t, n), lambda i: (i, 0))
    col = lambda n: pl.BlockSpec((n, t), lambda i: (0, i))
    slab = lambda n, ck: pl.BlockSpec((t // ck, n, ck), lambda i: (i, 0, 0))
    sds = jax.ShapeDtypeStruct
    out_shape = [
        sds((NSA_W, s), MXU_DTYPE), sds((NSA_W, s), MXU_DTYPE),
        sds((s, HEAD_DIM), F32), sds((s, HEAD_DIM), F32),
        sds((s, LANES), MXU_DTYPE), sds((s // kck, VT_ROWS, kck), MXU_DTYPE),
        sds((s, LANES), MXU_DTYPE), sds((s // wck, VT_ROWS, wck), MXU_DTYPE),
        sds((LANES, s), F32),
        sds((s, SB_W), MXU_DTYPE), sds((s // sck, SB_W, sck), MXU_DTYPE), sds((s, SB_W), MXU_DTYPE),
        sds((s, 2 * d), MXU_DTYPE),
    ]
    out_specs = [col(NSA_W), col(NSA_W), row(HEAD_DIM), row(HEAD_DIM),
                 row(LANES), slab(VT_ROWS, kck), row(LANES), slab(VT_ROWS, wck),
                 col(LANES), row(SB_W), slab(SB_W, sck), row(SB_W), row(2 * d)]
    return pl.pallas_call(
        _inproj_kernel,
        grid=(s // t,),
        in_specs=[row(d),
                  _resident((None, 1, d), lambda i: (layer, 0, 0)),
                  row(1),
                  _resident((1, LANES), lambda i: (0, 0)),
                  _resident((None, d, w_head.shape[2]), lambda i: (layer, 0, 0)),
                  _resident((None, d, w_tail.shape[2]), lambda i: (layer, 0, 0))],
        out_specs=out_specs,
        out_shape=out_shape,
        compiler_params=_params(1),
        name="inproj",
    )(x, g, pos, invf, w_head, w_tail)


def _compress_kernel(kc_ref, vc_ref, pe_ref, w1_ref, w2_ref, kcmp_ref, vcmpt_ref):
    nr, half_w = kc_ref.shape
    outs = []
    for kv, r_ref in enumerate((kc_ref, vc_ref)):
        r = r_ref[...]
        ha = _dot((r + pe_ref[kv, 0:1, :]).astype(MXU_DTYPE), w1_ref[kv, 0:half_w, :])
        hb = _dot((r + pe_ref[kv, 1:2, :]).astype(MXU_DTYPE), w1_ref[kv, half_w:2 * half_w, :])
        hid = ha + pltpu.roll(hb, nr - 1, 0)
        outs.append(_dot(jax.nn.gelu(hid).astype(MXU_DTYPE), w2_ref[kv]))
    pad = jnp.zeros_like(outs[0])
    kcmp_ref[...] = jnp.concatenate([outs[0], pad], axis=1).astype(kcmp_ref.dtype)
    vcmpt_ref[...] = jnp.concatenate([outs[1], pad], axis=1).T[0:HEAD_DIM].astype(vcmpt_ref.dtype)


def _compress(kc, vc, pe_all, w1_all, w2_all, layer):
    nr, half_w = kc.shape
    hidden = w1_all.shape[3]
    full = lambda shape: pl.BlockSpec(shape, lambda i: (0,) * len(shape))
    return pl.pallas_call(
        _compress_kernel,
        grid=(1,),
        in_specs=[full((nr, half_w)), full((nr, half_w)),
                  pl.BlockSpec((None, 2, 2, half_w), lambda i: (layer, 0, 0, 0)),
                  pl.BlockSpec((None, 2, 2 * half_w, hidden), lambda i: (layer, 0, 0, 0)),
                  pl.BlockSpec((None, 2, hidden, HEAD_DIM), lambda i: (layer, 0, 0, 0))],
        out_specs=[full((nr, LANES)), full((HEAD_DIM, nr))],
        out_shape=[jax.ShapeDtypeStruct((nr, LANES), MXU_DTYPE), jax.ShapeDtypeStruct((HEAD_DIM, nr), MXU_DTYPE)],
        compiler_params=_params(1),
        name="compress",
    )(kc, vc, pe_all, w1_all, w2_all)


def _nsa_kernel(qpt_ref, qrt_ref, gate_ref, kcmp_ref, vcmpt_ref, ovlt_ref, ks_ref, vst_ref, kw_ref, vwt_ref,
                out_ref, qaug_scr, mask_scr, m_scr, acc_scr, sa_scr, sb_scr, ta_scr, tb_scr, ocmp_scr, owin_scr, imp_scr, *, n_sel):
    tq = qpt_ref.shape[1]
    seq = ks_ref.shape[0]
    ncp = kcmp_ref.shape[0]
    nsp = ovlt_ref.shape[0]
    ck = vst_ref.shape[2]
    wck = vwt_ref.shape[2]
    nh = NSA_HEADS
    q0 = pl.program_id(0) * tq
    t = q0 + lax.broadcasted_iota(jnp.int32, (1, tq), 1)
    cols = [slice(h * tq, (h + 1) * tq) for h in range(nh)]

    def aug(qt_ref, h, tail):
        return jnp.concatenate([qt_ref[h * HEAD_DIM:(h + 1) * HEAD_DIM, :], tail], axis=0)

    zeros_tail = jnp.zeros((HEAD_DIM, tq), MXU_DTYPE)
    def cmp_branch(rows):
        qp_aug = jnp.concatenate([aug(qpt_ref, h, zeros_tail) for h in range(nh)], axis=1)
        sc_all = _dot(kcmp_ref[0:rows, :], qp_aug)
        cmp_last = CMP_STRIDE * lax.broadcasted_iota(jnp.int32, (rows, 1), 0) + (CMP_BLOCK - 1)
        vis_c = cmp_last <= t
        psum = jnp.zeros((rows, tq), F32)
        p_cmp = []
        for h in range(nh):
            sc = jnp.where(vis_c, sc_all[:, cols[h]], -1e30)
            top = jnp.max(sc, axis=0, keepdims=True)
            e = jnp.exp2(sc - jnp.where(top > -1e29, top, 0.0))
            den = jnp.sum(e, axis=0, keepdims=True)
            p = e * (1.0 / jnp.where(den > 0.0, den, 1.0))
            psum = psum + p
            p_cmp.append(p.astype(MXU_DTYPE))
        o_cmp = _dot(vcmpt_ref[:, 0:rows], jnp.concatenate(p_cmp, axis=1))
        p_hi = psum.astype(MXU_DTYPE)
        p_lo = (psum - p_hi.astype(F32)).astype(MXU_DTYPE)
        imp_scr[...] = _dot(ovlt_ref[:, 0:rows], p_hi) + _dot(ovlt_ref[:, 0:rows], p_lo)
        ocmp_scr[...] = o_cmp

    n_var = CMP_VARIANTS if ncp % (CMP_VARIANTS * LANES) == 0 else 1
    step = ncp // n_var
    n_vis = (q0 + tq - CMP_BLOCK) // CMP_STRIDE + 1
    variant = jnp.clip((n_vis + step - 1) // step - 1, 0, n_var - 1)
    for v in range(n_var):
        pl.when(variant == v)(functools.partial(cmp_branch, (v + 1) * step))
    imp = imp_scr[...]

    qr_aug = jnp.concatenate([aug(qrt_ref, h, zeros_tail) for h in range(nh)], axis=1)
    span = min(WINDOW + tq, seq)
    start = pl.multiple_of(jnp.maximum(q0 + tq - span, 0), wck)
    sw_all = _dot(kw_ref[pl.ds(start, span), :], qr_aug)

    def window_finish():
        kpos_w = start + lax.broadcasted_iota(jnp.int32, (span, 1), 0)
        vis_w = (kpos_w <= t) & (kpos_w > t - WINDOW)
        p_win = []
        for h in range(nh):
            sw = jnp.where(vis_w, sw_all[:, cols[h]], -1e30)
            p_win.append(jnp.exp2(sw - jnp.max(sw, axis=0, keepdims=True)).astype(MXU_DTYPE))
        p_win = jnp.concatenate(p_win, axis=1)
        ow = jnp.zeros((VT_ROWS, nh * tq), F32)
        for j in range(span // wck):
            ow = ow + _dot(vwt_ref[start // wck + j], p_win[j * wck:(j + 1) * wck, :])
        owin_scr[...] = ow[0:HEAD_DIM] * (1.0 / ow[HEAD_DIM:HEAD_DIM + 1])

    blk = lax.broadcasted_iota(jnp.int32, (nsp, 1), 0)
    cur = t // SEL_BLOCK
    valid = blk <= cur
    forced = (blk == 0) | (blk == cur) | (blk == cur - 1)
    n_forced = 3
    assert n_sel >= n_forced
    score = jnp.where(valid, jnp.where(forced, -jnp.inf, imp), -1.0)
    chosen = jnp.where(forced & valid, 1.0, 0.0)
    for _ in range(n_sel - n_forced):
        best = jnp.max(score, axis=0, keepdims=True)
        idx = jnp.min(jnp.where(score == best, blk, nsp), axis=0, keepdims=True)
        hit = blk == idx
        chosen = jnp.where(hit, 1.0, chosen)
        score = jnp.where(hit, -jnp.inf, score)
    mask_scr[0:nsp, :] = ((chosen - 1.0) * (-MASKED)).astype(mask_scr.dtype)
    mask_scr[nsp:nsp + SEL_CODE_BLOCKS, :] = jnp.full((SEL_CODE_BLOCKS, tq), MASKED, mask_scr.dtype)

    for h in range(nh):
        qaug_scr[0:HEAD_DIM, cols[h]] = qrt_ref[h * HEAD_DIM:(h + 1) * HEAD_DIM, :]
    m_scr[...] = jnp.full(m_scr.shape, -1e30, F32)
    acc_scr[...] = jnp.zeros(acc_scr.shape, F32)

    c_diag = q0 // ck

    def sel_scores(c, s_ref, top_ref, limit):
        k0 = pl.multiple_of(jnp.minimum(c, c_diag) * ck, ck)
        code0 = jnp.where(c < limit, (k0 // (SEL_BLOCK * SEL_CODE_BLOCKS)) * SEL_CODE_BLOCKS, nsp)
        mrows = mask_scr[pl.ds(pl.multiple_of(code0, SEL_CODE_BLOCKS), SEL_CODE_BLOCKS), :]
        for h in range(nh):
            qaug_scr[HEAD_DIM:2 * HEAD_DIM, cols[h]] = mrows
        s_all = _dot(ks_ref[pl.ds(k0, ck), :], qaug_scr[...])
        s_ref[...] = s_all
        top_ref[...] = jnp.max(s_all, axis=0, keepdims=True)

    def sel_update(c, s_ref, top_ref, causal):
        m_old = m_scr[...]
        if causal:
            kpos = c * ck + lax.broadcasted_iota(jnp.int32, (ck, 1), 0)
            bias = jnp.where(kpos <= t, 0.0, MASKED)
            tops = [jnp.max(s_ref[:, cols[h]] + bias, axis=0, keepdims=True) for h in range(nh)]
            m_new = jnp.maximum(m_old, jnp.concatenate(tops, axis=1))
        else:
            m_new = jnp.maximum(m_old, top_ref[...])
        p_all = []
        for h in range(nh):
            s = s_ref[:, cols[h]]
            if causal:
                s = s + bias
            p_all.append(jnp.exp2(s - m_new[:, cols[h]]).astype(MXU_DTYPE))
        pv = _dot(vst_ref[jnp.minimum(c, c_diag)], jnp.concatenate(p_all, axis=1))
        acc_scr[...] = jnp.exp2(m_old - m_new) * acc_scr[...] + pv
        m_scr[...] = m_new

    def sel_pair(c):
        sel_scores(c + 1, sa_scr, ta_scr, c_diag)
        sel_update(c, sb_scr, tb_scr, False)
        sel_scores(c + 2, sb_scr, tb_scr, c_diag)
        sel_update(c + 1, sa_scr, ta_scr, False)

    def sel_quad(j, carry):
        sel_pair(4 * j)
        sel_pair(4 * j + 2)
        return carry

    sel_scores(c_diag, sa_scr, ta_scr, c_diag + 1)
    sel_scores(0, sb_scr, tb_scr, c_diag)
    window_finish()
    sel_update(c_diag, sa_scr, ta_scr, True)
    n_quads = c_diag // 4
    lax.fori_loop(0, n_quads, sel_quad, 0)
    lax.fori_loop(0, (c_diag - 4 * n_quads + 1) // 2, lambda j, carry: (sel_pair(4 * n_quads + 2 * j), carry)[1], 0)
    acc = acc_scr[...]
    o_sel = acc[0:HEAD_DIM] * (1.0 / acc[HEAD_DIM:HEAD_DIM + 1])

    o_cmp = ocmp_scr[...]
    o_win = owin_scr[...]
    gates = gate_ref[...]
    merged = []
    for h in range(nh):
        merged.append(gates[3 * h:3 * h + 1, :] * o_cmp[:, cols[h]]
                      + gates[3 * h + 1:3 * h + 2, :] * o_sel[:, cols[h]]
                      + gates[3 * h + 2:3 * h + 3, :] * o_win[:, cols[h]])
    out_ref[...] = jnp.concatenate(merged, axis=0).T.astype(out_ref.dtype)


def _nsa(qpt, qrt, gates_t, kcmp, vcmpt, ovlt, ks, vst, kw, vwt):
    s = ks.shape[0]
    tq = min(NSA_Q_BLOCK, s)
    nsp = ovlt.shape[0]
    col = lambda n: pl.BlockSpec((n, tq), lambda i: (0, i))
    res = lambda a: _resident(a.shape, lambda i: (0,) * a.ndim)
    return pl.pallas_call(
        functools.partial(_nsa_kernel, n_sel=min(SEL_TOP_N, s // SEL_BLOCK)),
        grid=(s // tq,),
        in_specs=[col(NSA_W), col(NSA_W), col(LANES), res(kcmp), res(vcmpt), res(ovlt),
                  res(ks), res(vst), res(kw), res(vwt)],
        out_specs=pl.BlockSpec((tq, NSA_W), lambda i: (i, 0)),
        out_shape=jax.ShapeDtypeStruct((s, NSA_W), MXU_DTYPE),
        scratch_shapes=[pltpu.VMEM((LANES, NSA_HEADS * tq), MXU_DTYPE),
                        pltpu.VMEM((nsp + SEL_CODE_BLOCKS, tq), MXU_DTYPE),
                        pltpu.VMEM((1, NSA_HEADS * tq), F32), pltpu.VMEM((VT_ROWS, NSA_HEADS * tq), F32),
                        pltpu.VMEM((vst.shape[2], NSA_HEADS * tq), F32),
                        pltpu.VMEM((vst.shape[2], NSA_HEADS * tq), F32),
                        pltpu.VMEM((1, NSA_HEADS * tq), F32), pltpu.VMEM((1, NSA_HEADS * tq), F32),
                        pltpu.VMEM((HEAD_DIM, NSA_HEADS * tq), F32), pltpu.VMEM((HEAD_DIM, NSA_HEADS * tq), F32),
                        pltpu.VMEM((nsp, tq), F32)],
        compiler_params=_params(1),
        name="nsa",
    )(qpt, qrt, gates_t, kcmp, vcmpt, ovlt, ks, vst, kw, vwt)


def _softplus2(z2):
    neg_abs = lax.bitcast_convert_type(lax.bitcast_convert_type(z2, jnp.uint32) | jnp.uint32(0x80000000), F32)
    return jnp.maximum(z2, 0.0) + jnp.log2(1.0 + jnp.exp2(neg_abs))


def _sb_kernel(q_ref, k_ref, v_ref, out_ref, acc_scr, run_scr, za_scr, zb_scr, done_scr):
    assert (q_ref.shape[0] // k_ref.shape[2]) % 2 == 0
    tb = q_ref.shape[0]
    ck = k_ref.shape[2]
    nsub = tb // ck
    i = pl.program_id(0)
    r = lax.broadcasted_iota(jnp.int32, (ck, ck), 0)
    c = lax.broadcasted_iota(jnp.int32, (ck, ck), 1)
    tri = jnp.where(r >= c, 1.0, 0.0).astype(MXU_DTYPE)
    before = c < r
    heads = [slice(h * SB_HEAD_DIM, (h + 1) * SB_HEAD_DIM) for h in range(SB_HEADS)]

    def logits(h, rows, chunk):
        return _dot(q_ref[rows, heads[h]], k_ref[chunk, heads[h], :])

    def step(h, rows, chunk, diag, first, z=None):
        hs = heads[h]
        if z is None:
            z = logits(h, rows, chunk)
        sp = _softplus2(z)
        if diag:
            sp = jnp.where(before, sp, 0.0)
        cs = _dot(sp.astype(MXU_DTYPE), tri)
        own = jnp.minimum(z - cs, 0.0)
        if first:
            a = jnp.exp2(own)
            run_scr[h, rows] = cs[:, 0:1]
        else:
            run = run_scr[h, rows]
            a = jnp.exp2(own - run)
            run_scr[h, rows] = run + cs[:, 0:1]
        if diag:
            a = jnp.where(before, a, 0.0)
        pv = _dot(a.astype(MXU_DTYPE), v_ref[pl.ds(pl.multiple_of(chunk * ck, ck), ck), hs])
        if first:
            acc_scr[h, rows] = pv
        else:
            acc_scr[h, rows] += pv

    plan = []
    for g in range(nsub):
        rows = slice(g * ck, (g + 1) * ck)
        plan += [(h, rows, i * nsub + g, True, True) for h in range(SB_HEADS)]
        for back in range(g):
            plan += [(h, rows, i * nsub + g - 1 - back, False, False) for h in range(SB_HEADS)]
    zs = [logits(h, rows, chunk) for h, rows, chunk, _, _ in plan]
    for args, z in zip(plan, zs):
        step(*args, z)

    everything = slice(0, tb)

    def lookahead(chunk, z_ref):
        for h in range(SB_HEADS):
            z_ref[h] = logits(h, everything, jnp.maximum(chunk, 0))

    def exhausted():
        least = functools.reduce(jnp.minimum, [run_scr[h] for h in range(SB_HEADS)])
        return (jnp.min(least) >= SB_EXHAUSTED_LOG2).astype(jnp.int32)

    def pair(state):
        j, _ = state
        cur = i * nsub - 1 - 2 * j
        lookahead(cur - 1, zb_scr)
        for h in range(SB_HEADS):
            step(h, everything, cur, False, False, za_scr[h])
        done_scr[0] = exhausted()

        @pl.when(done_scr[0] == 0)
        def _():
            lookahead(cur - 2, za_scr)
            for h in range(SB_HEADS):
                step(h, everything, cur - 1, False, False, zb_scr[h])
            done_scr[0] = exhausted()

        return j + 1, done_scr[0]

    lookahead(i * nsub - 1, za_scr)
    lax.while_loop(lambda state: (state[0] < (i * nsub) // 2) & (state[1] == 0), pair,
                   (jnp.int32(0), jnp.int32(0)))
    for h, hs in enumerate(heads):
        out_ref[:, hs] = acc_scr[h].astype(out_ref.dtype)


def _sb(q, kt, v):
    s = q.shape[0]
    tb = min(SB_BLOCK, s)
    return pl.pallas_call(
        _sb_kernel,
        grid=(s // tb,),
        in_specs=[pl.BlockSpec((tb, SB_W), lambda i: (i, 0)),
                  _resident(kt.shape, lambda i: (0, 0, 0)),
                  _resident((s, SB_W), lambda i: (0, 0))],
        out_specs=pl.BlockSpec((tb, SB_W), lambda i: (i, 0)),
        out_shape=jax.ShapeDtypeStruct((s, SB_W), MXU_DTYPE),
        scratch_shapes=[pltpu.VMEM((SB_HEADS, tb, SB_HEAD_DIM), F32), pltpu.VMEM((SB_HEADS, tb, 1), F32),
                        pltpu.VMEM((SB_HEADS, tb, kt.shape[2]), F32), pltpu.VMEM((SB_HEADS, tb, kt.shape[2]), F32),
                        pltpu.SMEM((1,), jnp.int32)],
        compiler_params=_params(1),
        name="sb",
    )(q, kt, v)


def _mix_kernel(x_ref, nsa_ref, sb_ref, mg_ref, wn_ref, ws_ref, wo_ref, g_ref, out_ref):
    d = x_ref.shape[1]
    y_nsa = _dot(nsa_ref[...], wn_ref[...])
    y_sb = _dot(sb_ref[...], ws_ref[...])
    merged = mg_ref[:, 0:d].astype(F32) * y_nsa + mg_ref[:, d:2 * d].astype(F32) * y_sb
    mixed = _dot(merged.astype(MXU_DTYPE), wo_ref[...])
    out_ref[...] = x_ref[...] + _rms(mixed, g_ref[...])


def _mix(x, nsa_o, sb_o, mg, wn_all, ws_all, wo_all, g_all, layer):
    s, d = x.shape
    t = min(ROW_TILE, s)
    row = lambda n: pl.BlockSpec((t, n), lambda i: (i, 0))
    lay = lambda a: _resident((None,) + a.shape[1:], lambda i: (layer, 0, 0))
    return pl.pallas_call(
        _mix_kernel,
        grid=(s // t,),
        in_specs=[row(d), row(NSA_W), row(SB_W), row(2 * d), lay(wn_all), lay(ws_all), lay(wo_all), lay(g_all)],
        out_specs=row(d),
        out_shape=jax.ShapeDtypeStruct((s, d), F32),
        compiler_params=_params(1),
        name="mix",
    )(x, nsa_o, sb_o, mg, wn_all, ws_all, wo_all, g_all)


def _ffn_kernel(x_ref, gin_ref, w1_ref, w2_ref, gout_ref, out_ref):
    x = x_ref[...]
    d = x.shape[1]
    hb = _rms(x, gin_ref[...]).astype(MXU_DTYPE)
    ff = jnp.zeros_like(x)
    for c in range(w1_ref.shape[1] // d):
        up = _dot(hb, w1_ref[:, c * d:(c + 1) * d])
        ff = ff + _dot(jnp.square(jnp.maximum(up, 0.0)).astype(MXU_DTYPE), w2_ref[c * d:(c + 1) * d, :])
    out_ref[...] = x + _rms(ff, gout_ref[...])


def _ffn(x, gin_all, w1_all, w2_all, gout_all, layer):
    s, d = x.shape
    t = min(ROW_TILE, s)
    row = lambda n: pl.BlockSpec((t, n), lambda i: (i, 0))
    lay = lambda a: _resident((None,) + a.shape[1:], lambda i: (layer, 0, 0))
    return pl.pallas_call(
        _ffn_kernel,
        grid=(s // t,),
        in_specs=[row(d), lay(gin_all), lay(w1_all), lay(w2_all), lay(gout_all)],
        out_specs=row(d),
        out_shape=jax.ShapeDtypeStruct((s, d), F32),
        compiler_params=_params(1),
        name="ffn",
    )(x, gin_all, w1_all, w2_all, gout_all)


def _split_w_in(w_in):
    return (w_in[..., :_C_SB].astype(MXU_DTYPE), w_in[..., _C_GATE + 3 * NSA_HEADS:].astype(MXU_DTYPE))


def kernel(x, positions, norm_g, w_in, cmp_pe, cmp_w1, cmp_w2, w_nsa_o, w_sb_o, w_out, w_ff1, w_ff2):
    b, s, d = x.shape
    depth = w_in.shape[0]
    ncp, ns = s // CMP_STRIDE, s // SEL_BLOCK
    nsp = -(-ns // LANES) * LANES
    half_w = CMP_STRIDE * HEAD_DIM

    w_head, w_tail = _split_w_in(w_in)
    pe = cmp_pe.reshape(depth, 2, 2, half_w)
    w1 = cmp_w1.astype(MXU_DTYPE)
    w2 = cmp_w2.astype(MXU_DTYPE)
    wn, ws, wo = w_nsa_o.astype(MXU_DTYPE), w_sb_o.astype(MXU_DTYPE), w_out.astype(MXU_DTYPE)
    wf1, wf2 = w_ff1.astype(MXU_DTYPE), w_ff2.astype(MXU_DTYPE)
    g_pre, g_mix, g_ffn_in, g_ffn_out = (norm_g[:, n][:, None, :] for n in range(4))

    dim = jnp.arange(LANES) % HEAD_DIM
    half = ROT_DIM // 2
    inv_freq = jnp.power(ROPE_THETA, (dim % half).astype(F32) * (-2.0 / ROT_DIM))
    invf = jnp.where(dim < ROT_DIM, inv_freq, 0.0)[None, :].astype(F32)
    c_start = CMP_STRIDE * jnp.arange(ncp)[None, :]
    s_start = SEL_BLOCK * jnp.arange(nsp)[:, None]
    ovlt = ((c_start < s_start + SEL_BLOCK) & (c_start + CMP_BLOCK > s_start) & (s_start < s)).astype(MXU_DTYPE)

    outs = []
    for bi in range(b):
        xb = x[bi]
        pos = positions[bi][:, None]
        for layer in range(depth):
            (qpt, qrt, kc, vc, ks, vst, kw, vwt, gates_t, sbq, sbkt, sbv, mg) = _inproj(
                xb, g_pre, pos, invf, w_head, w_tail, layer)
            kcmp, vcmpt = _compress(kc.reshape(ncp, half_w), vc.reshape(ncp, half_w), pe, w1, w2, layer)
            nsa_o = _nsa(qpt, qrt, gates_t, kcmp, vcmpt, ovlt, ks, vst, kw, vwt)
            sb_o = _sb(sbq, sbkt, sbv)
            xb = _mix(xb, nsa_o, sb_o, mg, wn, ws, wo, g_mix, layer)
            xb = _ffn(xb, g_ffn_in, wf1, wf2, g_ffn_out, layer)
        outs.append(xb)
    return jnp.stack(outs, axis=0)
```

```python
import functools

import jax
import jax.numpy as jnp
from jax import lax
from jax.experimental import pallas as pl
from jax.experimental.pallas import tpu as pltpu

F32 = jnp.float32
MXU_DTYPE = jnp.bfloat16

HEAD_DIM = 64
NSA_HEADS = 8
SB_HEADS = 4
SB_HEAD_DIM = 128
ROPE_THETA = 500000.0
ROT_DIM = HEAD_DIM // 4
CMP_BLOCK = 32
CMP_STRIDE = 16
SEL_BLOCK = 64
SEL_TOP_N = 8
WINDOW = 512
RMS_EPS = 1e-6
NSA_W = NSA_HEADS * HEAD_DIM
SB_W = SB_HEADS * SB_HEAD_DIM
LANES = 128
VT_ROWS = HEAD_DIM + 16
MASKED = -32768.0
LOG2_E = 1.4426950408889634
VMEM_LIMIT = 56 * 1024 * 1024

ROW_TILE = 1024
NSA_Q_BLOCK = 256
NSA_KEY_CHUNK = 512
SEL_CODE_BLOCKS = HEAD_DIM
CMP_VARIANTS = 4
SB_BLOCK = 512
SB_KEY_CHUNK = 256
SB_EXHAUSTED_LOG2 = 1100.0


def _dot(a, b):
    return jnp.dot(a, b, preferred_element_type=F32)


def _rms(x, g):
    return x * lax.rsqrt(jnp.mean(x * x, axis=-1, keepdims=True) + RMS_EPS) * g


def _params(n_grid_dims):
    return pltpu.CompilerParams(dimension_semantics=("arbitrary",) * n_grid_dims,
                                vmem_limit_bytes=VMEM_LIMIT)


def _resident(block_shape, index_map):
    return pl.BlockSpec(block_shape, index_map, pipeline_mode=pl.Buffered(1))


_C_KV = NSA_W
_C_GATE = _C_KV + 6 * HEAD_DIM
_C_SB = _C_GATE + LANES
_C_MERGE = _C_SB + 3 * SB_W


def _inproj_kernel(x_ref, g_ref, pos_ref, invf_ref, w_ref,
                   qpt_ref, qrt_ref, kc_ref, vc_ref, ks_ref, vst_ref, kw_ref, vwt_ref,
                   gate_ref, sbq_ref, sbk_ref, sbv_ref, mg_ref):
    t, d_model = x_ref.shape
    hb = _rms(x_ref[...], g_ref[...]).astype(MXU_DTYPE)

    ang = pos_ref[...].astype(F32) * invf_ref[...]
    cos, sin = jnp.cos(ang), jnp.sin(ang)
    lane = lax.broadcasted_iota(jnp.int32, (1, LANES), 1)
    dim = lane % HEAD_DIM
    half = ROT_DIM // 2
    sin_up = jnp.where((dim >= half) & (dim < ROT_DIM), sin, 0.0)
    sin_dn = jnp.where(dim < half, -sin, 0.0)
    low = lane < HEAD_DIM

    def rope(xg):
        return xg * cos + pltpu.roll(xg, half, 1) * sin_up + pltpu.roll(xg, LANES - half, 1) * sin_dn

    pa = _dot(hb, w_ref[:, 0:_C_GATE])
    scale = LOG2_E * HEAD_DIM ** -0.5
    for j in range(NSA_W // LANES):
        qg = pa[:, j * LANES:(j + 1) * LANES]
        qpt_ref[j * LANES:(j + 1) * LANES, :] = (qg * scale).T.astype(qpt_ref.dtype)
        qrt_ref[j * LANES:(j + 1) * LANES, :] = (rope(qg) * scale).T.astype(qrt_ref.dtype)
    kc_ref[...] = pa[:, _C_KV:_C_KV + HEAD_DIM]
    vc_ref[...] = pa[:, _C_KV + HEAD_DIM:_C_KV + 2 * HEAD_DIM]

    row = pl.program_id(0) * t + lax.broadcasted_iota(jnp.int32, (t, 1), 0)
    code = jnp.where(lane - HEAD_DIM == (row // SEL_BLOCK) % SEL_CODE_BLOCKS, 1.0, 0.0)
    ksg = pa[:, _C_KV + 2 * HEAD_DIM:_C_KV + 4 * HEAD_DIM]
    ks_ref[...] = jnp.where(low, rope(ksg), code).astype(ks_ref.dtype)
    vs_t = jnp.where(low, pltpu.roll(ksg, HEAD_DIM, 1), 1.0).T[0:VT_ROWS]
    for c in range(vst_ref.shape[0]):
        vst_ref[c] = vs_t[:, c * vst_ref.shape[2]:(c + 1) * vst_ref.shape[2]].astype(vst_ref.dtype)
    kwg = pa[:, _C_KV + 4 * HEAD_DIM:_C_KV + 6 * HEAD_DIM]
    kw_ref[...] = jnp.where(low, rope(kwg), 0.0).astype(kw_ref.dtype)
    vw_t = jnp.where(low, pltpu.roll(kwg, HEAD_DIM, 1), 1.0).T[0:VT_ROWS]
    for c in range(vwt_ref.shape[0]):
        vwt_ref[c] = vw_t[:, c * vwt_ref.shape[2]:(c + 1) * vwt_ref.shape[2]].astype(vwt_ref.dtype)

    gate_ref[...] = jax.nn.sigmoid(_dot(hb, w_ref[:, _C_GATE:_C_SB])).T
    sb = _dot(hb, w_ref[:, _C_SB:_C_MERGE])
    sbq_ref[...] = (sb[:, 0:SB_W] * (LOG2_E * SB_HEAD_DIM ** -0.5)).astype(sbq_ref.dtype)
    sbk_t = sb[:, SB_W:2 * SB_W].T
    for c in range(sbk_ref.shape[0]):
        sbk_ref[c] = sbk_t[:, c * sbk_ref.shape[2]:(c + 1) * sbk_ref.shape[2]].astype(sbk_ref.dtype)
    sbv_ref[...] = sb[:, 2 * SB_W:3 * SB_W].astype(sbv_ref.dtype)
    for c in range(2):
        mg_ref[:, c * d_model:(c + 1) * d_model] = jax.nn.sigmoid(
            _dot(hb, w_ref[:, _C_MERGE + c * d_model:_C_MERGE + (c + 1) * d_model])).astype(mg_ref.dtype)


def _inproj(x, g, pos, invf, w_all, layer):
    s, d = x.shape
    t = min(ROW_TILE, s)
    wcols = w_all.shape[2]
    kck = min(NSA_KEY_CHUNK, s)
    sck = min(SB_KEY_CHUNK, s)
    wck = min(NSA_Q_BLOCK, s)
    row = lambda n: pl.BlockSpec((t, n), lambda i: (i, 0))
    col = lambda n: pl.BlockSpec((n, t), lambda i: (0, i))
    slab = lambda n, ck: pl.BlockSpec((t // ck, n, ck), lambda i: (i, 0, 0))
    sds = jax.ShapeDtypeStruct
    out_shape = [
        sds((NSA_W, s), MXU_DTYPE), sds((NSA_W, s), MXU_DTYPE),
        sds((s, HEAD_DIM), F32), sds((s, HEAD_DIM), F32),
        sds((s, LANES), MXU_DTYPE), sds((s // kck, VT_ROWS, kck), MXU_DTYPE),
        sds((s, LANES), MXU_DTYPE), sds((s // wck, VT_ROWS, wck), MXU_DTYPE),
        sds((LANES, s), F32),
        sds((s, SB_W), MXU_DTYPE), sds((s // sck, SB_W, sck), MXU_DTYPE), sds((s, SB_W), MXU_DTYPE),
        sds((s, 2 * d), MXU_DTYPE),
    ]
    out_specs = [col(NSA_W), col(NSA_W), row(HEAD_DIM), row(HEAD_DIM),
                 row(LANES), slab(VT_ROWS, kck), row(LANES), slab(VT_ROWS, wck),
                 col(LANES), row(SB_W), slab(SB_W, sck), row(SB_W), row(2 * d)]
    return pl.pallas_call(
        _inproj_kernel,
        grid=(s // t,),
        in_specs=[row(d),
                  _resident((None, 1, d), lambda i: (layer, 0, 0)),
                  row(1),
                  _resident((1, LANES), lambda i: (0, 0)),
                  _resident((None, d, wcols), lambda i: (layer, 0, 0))],
        out_specs=out_specs,
        out_shape=out_shape,
        compiler_params=_params(1),
        name="inproj",
    )(x, g, pos, invf, w_all)


def _compress_kernel(kc_ref, vc_ref, pe_ref, w1_ref, w2_ref, kcmp_ref, vcmpt_ref):
    nr, half_w = kc_ref.shape
    outs = []
    for kv, r_ref in enumerate((kc_ref, vc_ref)):
        r = r_ref[...]
        ha = _dot((r + pe_ref[kv, 0:1, :]).astype(MXU_DTYPE), w1_ref[kv, 0:half_w, :])
        hb = _dot((r + pe_ref[kv, 1:2, :]).astype(MXU_DTYPE), w1_ref[kv, half_w:2 * half_w, :])
        hid = ha + pltpu.roll(hb, nr - 1, 0)
        outs.append(_dot(jax.nn.gelu(hid).astype(MXU_DTYPE), w2_ref[kv]))
    pad = jnp.zeros_like(outs[0])
    kcmp_ref[...] = jnp.concatenate([outs[0], pad], axis=1).astype(kcmp_ref.dtype)
    vcmpt_ref[...] = jnp.concatenate([outs[1], pad], axis=1).T[0:HEAD_DIM].astype(vcmpt_ref.dtype)


def _compress(kc, vc, pe_all, w1_all, w2_all, layer):
    nr, half_w = kc.shape
    hidden = w1_all.shape[3]
    full = lambda shape: pl.BlockSpec(shape, lambda i: (0,) * len(shape))
    return pl.pallas_call(
        _compress_kernel,
        grid=(1,),
        in_specs=[full((nr, half_w)), full((nr, half_w)),
                  pl.BlockSpec((None, 2, 2, half_w), lambda i: (layer, 0, 0, 0)),
                  pl.BlockSpec((None, 2, 2 * half_w, hidden), lambda i: (layer, 0, 0, 0)),
                  pl.BlockSpec((None, 2, hidden, HEAD_DIM), lambda i: (layer, 0, 0, 0))],
        out_specs=[full((nr, LANES)), full((HEAD_DIM, nr))],
        out_shape=[jax.ShapeDtypeStruct((nr, LANES), MXU_DTYPE), jax.ShapeDtypeStruct((HEAD_DIM, nr), MXU_DTYPE)],
        compiler_params=_params(1),
        name="compress",
    )(kc, vc, pe_all, w1_all, w2_all)


def _nsa_kernel(qpt_ref, qrt_ref, gate_ref, kcmp_ref, vcmpt_ref, ovlt_ref, ks_ref, vst_ref, kw_ref, vwt_ref,
                out_ref, qaug_scr, mask_scr, m_scr, acc_scr, sa_scr, sb_scr, ta_scr, tb_scr, ocmp_scr, owin_scr, imp_scr, *, n_sel):
    tq = qpt_ref.shape[1]
    seq = ks_ref.shape[0]
    ncp = kcmp_ref.shape[0]
    nsp = ovlt_ref.shape[0]
    ck = vst_ref.shape[2]
    wck = vwt_ref.shape[2]
    nh = NSA_HEADS
    q0 = pl.program_id(0) * tq
    t = q0 + lax.broadcasted_iota(jnp.int32, (1, tq), 1)
    cols = [slice(h * tq, (h + 1) * tq) for h in range(nh)]

    def aug(qt_ref, h, tail):
        return jnp.concatenate([qt_ref[h * HEAD_DIM:(h + 1) * HEAD_DIM, :], tail], axis=0)

    zeros_tail = jnp.zeros((HEAD_DIM, tq), MXU_DTYPE)
    def cmp_branch(rows):
        qp_aug = jnp.concatenate([aug(qpt_ref, h, zeros_tail) for h in range(nh)], axis=1)
        sc_all = _dot(kcmp_ref[0:rows, :], qp_aug)
        cmp_last = CMP_STRIDE * lax.broadcasted_iota(jnp.int32, (rows, 1), 0) + (CMP_BLOCK - 1)
        vis_c = cmp_last <= t
        psum = jnp.zeros((rows, tq), F32)
        p_cmp = []
        for h in range(nh):
            sc = jnp.where(vis_c, sc_all[:, cols[h]], -1e30)
            top = jnp.max(sc, axis=0, keepdims=True)
            e = jnp.exp2(sc - jnp.where(top > -1e29, top, 0.0))
            den = jnp.sum(e, axis=0, keepdims=True)
            p = e * (1.0 / jnp.where(den > 0.0, den, 1.0))
            psum = psum + p
            p_cmp.append(p.astype(MXU_DTYPE))
        o_cmp = _dot(vcmpt_ref[:, 0:rows], jnp.concatenate(p_cmp, axis=1))
        p_hi = psum.astype(MXU_DTYPE)
        p_lo = (psum - p_hi.astype(F32)).astype(MXU_DTYPE)
        imp_scr[...] = _dot(ovlt_ref[:, 0:rows], p_hi) + _dot(ovlt_ref[:, 0:rows], p_lo)
        ocmp_scr[...] = o_cmp

    n_var = CMP_VARIANTS if ncp % (CMP_VARIANTS * LANES) == 0 else 1
    step = ncp // n_var
    n_vis = (q0 + tq - CMP_BLOCK) // CMP_STRIDE + 1
    variant = jnp.clip((n_vis + step - 1) // step - 1, 0, n_var - 1)
    for v in range(n_var):
        pl.when(variant == v)(functools.partial(cmp_branch, (v + 1) * step))
    imp = imp_scr[...]

    qr_aug = jnp.concatenate([aug(qrt_ref, h, zeros_tail) for h in range(nh)], axis=1)
    span = min(WINDOW + tq, seq)
    start = pl.multiple_of(jnp.maximum(q0 + tq - span, 0), wck)
    sw_all = _dot(kw_ref[pl.ds(start, span), :], qr_aug)

    def window_finish():
        kpos_w = start + lax.broadcasted_iota(jnp.int32, (span, 1), 0)
        vis_w = (kpos_w <= t) & (kpos_w > t - WINDOW)
        p_win = []
        for h in range(nh):
            sw = jnp.where(vis_w, sw_all[:, cols[h]], -1e30)
            p_win.append(jnp.exp2(sw - jnp.max(sw, axis=0, keepdims=True)).astype(MXU_DTYPE))
        p_win = jnp.concatenate(p_win, axis=1)
        ow = jnp.zeros((VT_ROWS, nh * tq), F32)
        for j in range(span // wck):
            ow = ow + _dot(vwt_ref[start // wck + j], p_win[j * wck:(j + 1) * wck, :])
        owin_scr[...] = ow[0:HEAD_DIM] * (1.0 / ow[HEAD_DIM:HEAD_DIM + 1])

    blk = lax.broadcasted_iota(jnp.int32, (nsp, 1), 0)
    cur = t // SEL_BLOCK
    valid = blk <= cur
    forced = (blk == 0) | (blk == cur) | (blk == cur - 1)
    n_forced = 3
    assert n_sel >= n_forced
    score = jnp.where(valid, jnp.where(forced, -jnp.inf, imp), -1.0)
    chosen = jnp.where(forced & valid, 1.0, 0.0)
    for _ in range(n_sel - n_forced):
        best = jnp.max(score, axis=0, keepdims=True)
        idx = jnp.min(jnp.where(score == best, blk, nsp), axis=0, keepdims=True)
        hit = blk == idx
        chosen = jnp.where(hit, 1.0, chosen)
        score = jnp.where(hit, -jnp.inf, score)
    mask_scr[0:nsp, :] = ((chosen - 1.0) * (-MASKED)).astype(mask_scr.dtype)
    mask_scr[nsp:nsp + SEL_CODE_BLOCKS, :] = jnp.full((SEL_CODE_BLOCKS, tq), MASKED, mask_scr.dtype)

    for h in range(nh):
        qaug_scr[0:HEAD_DIM, cols[h]] = qrt_ref[h * HEAD_DIM:(h + 1) * HEAD_DIM, :]
    m_scr[...] = jnp.full(m_scr.shape, -1e30, F32)
    acc_scr[...] = jnp.zeros(acc_scr.shape, F32)

    c_diag = q0 // ck

    def sel_scores(c, s_ref, top_ref, limit):
        k0 = pl.multiple_of(jnp.minimum(c, c_diag) * ck, ck)
        code0 = jnp.where(c < limit, (k0 // (SEL_BLOCK * SEL_CODE_BLOCKS)) * SEL_CODE_BLOCKS, nsp)
        mrows = mask_scr[pl.ds(pl.multiple_of(code0, SEL_CODE_BLOCKS), SEL_CODE_BLOCKS), :]
        for h in range(nh):
            qaug_scr[HEAD_DIM:2 * HEAD_DIM, cols[h]] = mrows
        s_all = _dot(ks_ref[pl.ds(k0, ck), :], qaug_scr[...])
        s_ref[...] = s_all
        top_ref[...] = jnp.max(s_all, axis=0, keepdims=True)

    def sel_update(c, s_ref, top_ref, causal):
        m_old = m_scr[...]
        if causal:
            kpos = c * ck + lax.broadcasted_iota(jnp.int32, (ck, 1), 0)
            bias = jnp.where(kpos <= t, 0.0, MASKED)
            tops = [jnp.max(s_ref[:, cols[h]] + bias, axis=0, keepdims=True) for h in range(nh)]
            m_new = jnp.maximum(m_old, jnp.concatenate(tops, axis=1))
        else:
            m_new = jnp.maximum(m_old, top_ref[...])
        p_all = []
        for h in range(nh):
            s = s_ref[:, cols[h]]
            if causal:
                s = s + bias
            p_all.append(jnp.exp2(s - m_new[:, cols[h]]).astype(MXU_DTYPE))
        pv = _dot(vst_ref[jnp.minimum(c, c_diag)], jnp.concatenate(p_all, axis=1))
        acc_scr[...] = jnp.exp2(m_old - m_new) * acc_scr[...] + pv
        m_scr[...] = m_new

    def sel_pair(c):
        sel_scores(c + 1, sa_scr, ta_scr, c_diag)
        sel_update(c, sb_scr, tb_scr, False)
        sel_scores(c + 2, sb_scr, tb_scr, c_diag)
        sel_update(c + 1, sa_scr, ta_scr, False)

    def sel_quad(j, carry):
        sel_pair(4 * j)
        sel_pair(4 * j + 2)
        return carry

    sel_scores(c_diag, sa_scr, ta_scr, c_diag + 1)
    sel_scores(0, sb_scr, tb_scr, c_diag)
    window_finish()
    sel_update(c_diag, sa_scr, ta_scr, True)
    n_quads = c_diag // 4
    lax.fori_loop(0, n_quads, sel_quad, 0)
    lax.fori_loop(0, (c_diag - 4 * n_quads + 1) // 2, lambda j, carry: (sel_pair(4 * n_quads + 2 * j), carry)[1], 0)
    acc = acc_scr[...]
    o_sel = acc[0:HEAD_DIM] * (1.0 / acc[HEAD_DIM:HEAD_DIM + 1])

    o_cmp = ocmp_scr[...]
    o_win = owin_scr[...]
    gates = gate_ref[...]
    merged = []
    for h in range(nh):
        merged.append(gates[3 * h:3 * h + 1, :] * o_cmp[:, cols[h]]
                      + gates[3 * h + 1:3 * h + 2, :] * o_sel[:, cols[h]]
                      + gates[3 * h + 2:3 * h + 3, :] * o_win[:, cols[h]])
    out_ref[...] = jnp.concatenate(merged, axis=0).T.astype(out_ref.dtype)


def _nsa(qpt, qrt, gates_t, kcmp, vcmpt, ovlt, ks, vst, kw, vwt):
    s = ks.shape[0]
    tq = min(NSA_Q_BLOCK, s)
    nsp = ovlt.shape[0]
    col = lambda n: pl.BlockSpec((n, tq), lambda i: (0, i))
    res = lambda a: _resident(a.shape, lambda i: (0,) * a.ndim)
    return pl.pallas_call(
        functools.partial(_nsa_kernel, n_sel=min(SEL_TOP_N, s // SEL_BLOCK)),
        grid=(s // tq,),
        in_specs=[col(NSA_W), col(NSA_W), col(LANES), res(kcmp), res(vcmpt), res(ovlt),
                  res(ks), res(vst), res(kw), res(vwt)],
        out_specs=pl.BlockSpec((tq, NSA_W), lambda i: (i, 0)),
        out_shape=jax.ShapeDtypeStruct((s, NSA_W), MXU_DTYPE),
        scratch_shapes=[pltpu.VMEM((LANES, NSA_HEADS * tq), MXU_DTYPE),
                        pltpu.VMEM((nsp + SEL_CODE_BLOCKS, tq), MXU_DTYPE),
                        pltpu.VMEM((1, NSA_HEADS * tq), F32), pltpu.VMEM((VT_ROWS, NSA_HEADS * tq), F32),
                        pltpu.VMEM((vst.shape[2], NSA_HEADS * tq), F32),
                        pltpu.VMEM((vst.shape[2], NSA_HEADS * tq), F32),
                        pltpu.VMEM((1, NSA_HEADS * tq), F32), pltpu.VMEM((1, NSA_HEADS * tq), F32),
                        pltpu.VMEM((HEAD_DIM, NSA_HEADS * tq), F32), pltpu.VMEM((HEAD_DIM, NSA_HEADS * tq), F32),
                        pltpu.VMEM((nsp, tq), F32)],
        compiler_params=_params(1),
        name="nsa",
    )(qpt, qrt, gates_t, kcmp, vcmpt, ovlt, ks, vst, kw, vwt)


def _softplus2(z2):
    neg_abs = lax.bitcast_convert_type(lax.bitcast_convert_type(z2, jnp.uint32) | jnp.uint32(0x80000000), F32)
    return jnp.maximum(z2, 0.0) + jnp.log2(1.0 + jnp.exp2(neg_abs))


def _sb_kernel(q_ref, k_ref, v_ref, out_ref, acc_scr, run_scr, za_scr, zb_scr, done_scr):
    assert (q_ref.shape[0] // k_ref.shape[2]) % 2 == 0
    tb = q_ref.shape[0]
    ck = k_ref.shape[2]
    nsub = tb // ck
    i = pl.program_id(0)
    r = lax.broadcasted_iota(jnp.int32, (ck, ck), 0)
    c = lax.broadcasted_iota(jnp.int32, (ck, ck), 1)
    tri = jnp.where(r >= c, 1.0, 0.0).astype(MXU_DTYPE)
    before = c < r
    heads = [slice(h * SB_HEAD_DIM, (h + 1) * SB_HEAD_DIM) for h in range(SB_HEADS)]

    def logits(h, rows, chunk):
        return _dot(q_ref[rows, heads[h]], k_ref[chunk, heads[h], :])

    def step(h, rows, chunk, diag, first, z=None):
        hs = heads[h]
        if z is None:
            z = logits(h, rows, chunk)
        sp = _softplus2(z)
        if diag:
            sp = jnp.where(before, sp, 0.0)
        cs = _dot(sp.astype(MXU_DTYPE), tri)
        own = jnp.minimum(z - cs, 0.0)
        if first:
            a = jnp.exp2(own)
            run_scr[h, rows] = cs[:, 0:1]
        else:
            run = run_scr[h, rows]
            a = jnp.exp2(own - run)
            run_scr[h, rows] = run + cs[:, 0:1]
        if diag:
            a = jnp.where(before, a, 0.0)
        pv = _dot(a.astype(MXU_DTYPE), v_ref[pl.ds(pl.multiple_of(chunk * ck, ck), ck), hs])
        if first:
            acc_scr[h, rows] = pv
        else:
            acc_scr[h, rows] += pv

    plan = []
    for g in range(nsub):
        rows = slice(g * ck, (g + 1) * ck)
        plan += [(h, rows, i * nsub + g, True, True) for h in range(SB_HEADS)]
        for back in range(g):
            plan += [(h, rows, i * nsub + g - 1 - back, False, False) for h in range(SB_HEADS)]
    zs = [logits(h, rows, chunk) for h, rows, chunk, _, _ in plan]
    for args, z in zip(plan, zs):
        step(*args, z)

    everything = slice(0, tb)

    def lookahead(chunk, z_ref):
        for h in range(SB_HEADS):
            z_ref[h] = logits(h, everything, jnp.maximum(chunk, 0))

    def exhausted():
        least = functools.reduce(jnp.minimum, [run_scr[h] for h in range(SB_HEADS)])
        return (jnp.min(least) >= SB_EXHAUSTED_LOG2).astype(jnp.int32)

    def pair(state):
        j, _ = state
        cur = i * nsub - 1 - 2 * j
        lookahead(cur - 1, zb_scr)
        for h in range(SB_HEADS):
            step(h, everything, cur, False, False, za_scr[h])
        done_scr[0] = exhausted()

        @pl.when(done_scr[0] == 0)
        def _():
            lookahead(cur - 2, za_scr)
            for h in range(SB_HEADS):
                step(h, everything, cur - 1, False, False, zb_scr[h])
            done_scr[0] = exhausted()

        return j + 1, done_scr[0]

    lookahead(i * nsub - 1, za_scr)
    lax.while_loop(lambda state: (state[0] < (i * nsub) // 2) & (state[1] == 0), pair,
                   (jnp.int32(0), jnp.int32(0)))
    for h, hs in enumerate(heads):
        out_ref[:, hs] = acc_scr[h].astype(out_ref.dtype)


def _sb(q, kt, v):
    s = q.shape[0]
    tb = min(SB_BLOCK, s)
    return pl.pallas_call(
        _sb_kernel,
        grid=(s // tb,),
        in_specs=[pl.BlockSpec((tb, SB_W), lambda i: (i, 0)),
                  _resident(kt.shape, lambda i: (0, 0, 0)),
                  _resident((s, SB_W), lambda i: (0, 0))],
        out_specs=pl.BlockSpec((tb, SB_W), lambda i: (i, 0)),
        out_shape=jax.ShapeDtypeStruct((s, SB_W), MXU_DTYPE),
        scratch_shapes=[pltpu.VMEM((SB_HEADS, tb, SB_HEAD_DIM), F32), pltpu.VMEM((SB_HEADS, tb, 1), F32),
                        pltpu.VMEM((SB_HEADS, tb, kt.shape[2]), F32), pltpu.VMEM((SB_HEADS, tb, kt.shape[2]), F32),
                        pltpu.SMEM((1,), jnp.int32)],
        compiler_params=_params(1),
        name="sb",
    )(q, kt, v)


def _mix_kernel(x_ref, nsa_ref, sb_ref, mg_ref, wn_ref, ws_ref, wo_ref, g_ref, out_ref):
    d = x_ref.shape[1]
    y_nsa = _dot(nsa_ref[...], wn_ref[...])
    y_sb = _dot(sb_ref[...], ws_ref[...])
    merged = mg_ref[:, 0:d].astype(F32) * y_nsa + mg_ref[:, d:2 * d].astype(F32) * y_sb
    mixed = _dot(merged.astype(MXU_DTYPE), wo_ref[...])
    out_ref[...] = x_ref[...] + _rms(mixed, g_ref[...])


def _mix(x, nsa_o, sb_o, mg, wn_all, ws_all, wo_all, g_all, layer):
    s, d = x.shape
    t = min(ROW_TILE, s)
    row = lambda n: pl.BlockSpec((t, n), lambda i: (i, 0))
    lay = lambda a: _resident((None,) + a.shape[1:], lambda i: (layer, 0, 0))
    return pl.pallas_call(
        _mix_kernel,
        grid=(s // t,),
        in_specs=[row(d), row(NSA_W), row(SB_W), row(2 * d), lay(wn_all), lay(ws_all), lay(wo_all), lay(g_all)],
        out_specs=row(d),
        out_shape=jax.ShapeDtypeStruct((s, d), F32),
        compiler_params=_params(1),
        name="mix",
    )(x, nsa_o, sb_o, mg, wn_all, ws_all, wo_all, g_all)


def _ffn_kernel(x_ref, gin_ref, w1_ref, w2_ref, gout_ref, out_ref):
    x = x_ref[...]
    d = x.shape[1]
    hb = _rms(x, gin_ref[...]).astype(MXU_DTYPE)
    ff = jnp.zeros_like(x)
    for c in range(w1_ref.shape[1] // d):
        up = _dot(hb, w1_ref[:, c * d:(c + 1) * d])
        ff = ff + _dot(jnp.square(jnp.maximum(up, 0.0)).astype(MXU_DTYPE), w2_ref[c * d:(c + 1) * d, :])
    out_ref[...] = x + _rms(ff, gout_ref[...])


def _ffn(x, gin_all, w1_all, w2_all, gout_all, layer):
    s, d = x.shape
    t = min(ROW_TILE, s)
    row = lambda n: pl.BlockSpec((t, n), lambda i: (i, 0))
    lay = lambda a: _resident((None,) + a.shape[1:], lambda i: (layer, 0, 0))
    return pl.pallas_call(
        _ffn_kernel,
        grid=(s // t,),
        in_specs=[row(d), lay(gin_all), lay(w1_all), lay(w2_all), lay(gout_all)],
        out_specs=row(d),
        out_shape=jax.ShapeDtypeStruct((s, d), F32),
        compiler_params=_params(1),
        name="ffn",
    )(x, gin_all, w1_all, w2_all, gout_all)


def _regroup_w_in(w_in):
    gate_lo, gate_hi = _C_GATE, _C_GATE + 3 * NSA_HEADS
    pad = jnp.zeros(w_in.shape[:2] + (LANES - 3 * NSA_HEADS,), w_in.dtype)
    return jnp.concatenate([w_in[..., :gate_lo], w_in[..., gate_lo:gate_hi], pad, w_in[..., gate_hi:]],
                           axis=-1).astype(MXU_DTYPE)


def kernel(x, positions, norm_g, w_in, cmp_pe, cmp_w1, cmp_w2, w_nsa_o, w_sb_o, w_out, w_ff1, w_ff2):
    b, s, d = x.shape
    depth = w_in.shape[0]
    ncp, ns = s // CMP_STRIDE, s // SEL_BLOCK
    nsp = -(-ns // LANES) * LANES
    half_w = CMP_STRIDE * HEAD_DIM

    w_in_r = _regroup_w_in(w_in)
    pe = cmp_pe.reshape(depth, 2, 2, half_w)
    w1 = cmp_w1.astype(MXU_DTYPE)
    w2 = cmp_w2.astype(MXU_DTYPE)
    wn, ws, wo = w_nsa_o.astype(MXU_DTYPE), w_sb_o.astype(MXU_DTYPE), w_out.astype(MXU_DTYPE)
    wf1, wf2 = w_ff1.astype(MXU_DTYPE), w_ff2.astype(MXU_DTYPE)
    g_pre, g_mix, g_ffn_in, g_ffn_out = (norm_g[:, n][:, None, :] for n in range(4))

    dim = jnp.arange(LANES) % HEAD_DIM
    half = ROT_DIM // 2
    inv_freq = jnp.power(ROPE_THETA, (dim % half).astype(F32) * (-2.0 / ROT_DIM))
    invf = jnp.where(dim < ROT_DIM, inv_freq, 0.0)[None, :].astype(F32)
    c_start = CMP_STRIDE * jnp.arange(ncp)[None, :]
    s_start = SEL_BLOCK * jnp.arange(nsp)[:, None]
    ovlt = ((c_start < s_start + SEL_BLOCK) & (c_start + CMP_BLOCK > s_start) & (s_start < s)).astype(MXU_DTYPE)

    outs = []
    for bi in range(b):
        xb = x[bi]
        pos = positions[bi][:, None]
        for layer in range(depth):
            (qpt, qrt, kc, vc, ks, vst, kw, vwt, gates_t, sbq, sbkt, sbv, mg) = _inproj(
                xb, g_pre, pos, invf, w_in_r, layer)
            kcmp, vcmpt = _compress(kc.reshape(ncp, half_w), vc.reshape(ncp, half_w), pe, w1, w2, layer)
            nsa_o = _nsa(qpt, qrt, gates_t, kcmp, vcmpt, ovlt, ks, vst, kw, vwt)
            sb_o = _sb(sbq, sbkt, sbv)
            xb = _mix(xb, nsa_o, sb_o, mg, wn, ws, wo, g_mix, layer)
            xb = _ffn(xb, g_ffn_in, wf1, wf2, g_ffn_out, layer)
        outs.append(xb)
    return jnp.stack(outs, axis=0)
```

```python
import functools

import jax
import jax.numpy as jnp
from jax import lax
from jax.experimental import pallas as pl
from jax.experimental.pallas import tpu as pltpu

F32 = jnp.float32
MXU_DTYPE = jnp.bfloat16

HEAD_DIM = 64
NSA_HEADS = 8
SB_HEADS = 4
SB_HEAD_DIM = 128
ROPE_THETA = 500000.0
ROT_DIM = HEAD_DIM // 4
CMP_BLOCK = 32
CMP_STRIDE = 16
SEL_BLOCK = 64
SEL_TOP_N = 8
WINDOW = 512
RMS_EPS = 1e-6
NSA_W = NSA_HEADS * HEAD_DIM
SB_W = SB_HEADS * SB_HEAD_DIM
LANES = 128
VT_ROWS = HEAD_DIM + 16
MASKED = -32768.0
LOG2_E = 1.4426950408889634
VMEM_LIMIT = 56 * 1024 * 1024

ROW_TILE = 1024
FUSED_ROW_TILE = 512
NSA_Q_BLOCK = 256
NSA_KEY_CHUNK = 512
SEL_CODE_BLOCKS = HEAD_DIM
CMP_VARIANTS = 4
SB_BLOCK = 512
SB_KEY_CHUNK = 256
SB_EXHAUSTED_LOG2 = 160.0


def _dot(a, b):
    return jnp.dot(a, b, preferred_element_type=F32)


def _rms(x, g):
    return x * lax.rsqrt(jnp.mean(x * x, axis=-1, keepdims=True) + RMS_EPS) * g


def _params(n_grid_dims):
    return pltpu.CompilerParams(dimension_semantics=("arbitrary",) * n_grid_dims,
                                vmem_limit_bytes=VMEM_LIMIT)


def _resident(block_shape, index_map):
    return pl.BlockSpec(block_shape, index_map, pipeline_mode=pl.Buffered(1))


_C_KV = NSA_W
_C_GATE = _C_KV + 6 * HEAD_DIM
_C_SB = _C_GATE + LANES
_C_MERGE = _C_SB + 3 * SB_W


def _inproj_kernel(x_ref, g_ref, pos_ref, invf_ref, w_ref,
                   qpt_ref, qrt_ref, kc_ref, vc_ref, ks_ref, vst_ref, kw_ref, vwt_ref,
                   gate_ref, sbq_ref, sbk_ref, sbv_ref, mg_ref):
    t, d_model = x_ref.shape
    hb = _rms(x_ref[...], g_ref[...]).astype(MXU_DTYPE)

    ang = pos_ref[...].astype(F32) * invf_ref[...]
    cos, sin = jnp.cos(ang), jnp.sin(ang)
    lane = lax.broadcasted_iota(jnp.int32, (1, LANES), 1)
    dim = lane % HEAD_DIM
    half = ROT_DIM // 2
    sin_up = jnp.where((dim >= half) & (dim < ROT_DIM), sin, 0.0)
    sin_dn = jnp.where(dim < half, -sin, 0.0)
    low = lane < HEAD_DIM

    def rope(xg):
        return xg * cos + pltpu.roll(xg, half, 1) * sin_up + pltpu.roll(xg, LANES - half, 1) * sin_dn

    pa = _dot(hb, w_ref[:, 0:_C_GATE])
    scale = LOG2_E * HEAD_DIM ** -0.5
    for j in range(NSA_W // LANES):
        qg = pa[:, j * LANES:(j + 1) * LANES]
        qpt_ref[j * LANES:(j + 1) * LANES, :] = (qg * scale).T.astype(qpt_ref.dtype)
        qrt_ref[j * LANES:(j + 1) * LANES, :] = (rope(qg) * scale).T.astype(qrt_ref.dtype)
    kc_ref[...] = pa[:, _C_KV:_C_KV + HEAD_DIM]
    vc_ref[...] = pa[:, _C_KV + HEAD_DIM:_C_KV + 2 * HEAD_DIM]

    row = pl.program_id(0) * t + lax.broadcasted_iota(jnp.int32, (t, 1), 0)
    code = jnp.where(lane - HEAD_DIM == (row // SEL_BLOCK) % SEL_CODE_BLOCKS, 1.0, 0.0)
    ksg = pa[:, _C_KV + 2 * HEAD_DIM:_C_KV + 4 * HEAD_DIM]
    ks_ref[...] = jnp.where(low, rope(ksg), code).astype(ks_ref.dtype)
    vs_t = jnp.where(low, pltpu.roll(ksg, HEAD_DIM, 1), 1.0).T[0:VT_ROWS]
    for c in range(vst_ref.shape[0]):
        vst_ref[c] = vs_t[:, c * vst_ref.shape[2]:(c + 1) * vst_ref.shape[2]].astype(vst_ref.dtype)
    kwg = pa[:, _C_KV + 4 * HEAD_DIM:_C_KV + 6 * HEAD_DIM]
    kw_ref[...] = jnp.where(low, rope(kwg), 0.0).astype(kw_ref.dtype)
    vw_t = jnp.where(low, pltpu.roll(kwg, HEAD_DIM, 1), 1.0).T[0:VT_ROWS]
    for c in range(vwt_ref.shape[0]):
        vwt_ref[c] = vw_t[:, c * vwt_ref.shape[2]:(c + 1) * vwt_ref.shape[2]].astype(vwt_ref.dtype)

    gate_ref[...] = jax.nn.sigmoid(_dot(hb, w_ref[:, _C_GATE:_C_SB])).T
    sb = _dot(hb, w_ref[:, _C_SB:_C_MERGE])
    sbq_ref[...] = (sb[:, 0:SB_W] * (LOG2_E * SB_HEAD_DIM ** -0.5)).astype(sbq_ref.dtype)
    sbk_t = sb[:, SB_W:2 * SB_W].T
    for c in range(sbk_ref.shape[0]):
        sbk_ref[c] = sbk_t[:, c * sbk_ref.shape[2]:(c + 1) * sbk_ref.shape[2]].astype(sbk_ref.dtype)
    sbv_ref[...] = sb[:, 2 * SB_W:3 * SB_W].astype(sbv_ref.dtype)
    for c in range(2):
        mg_ref[:, c * d_model:(c + 1) * d_model] = jax.nn.sigmoid(
            _dot(hb, w_ref[:, _C_MERGE + c * d_model:_C_MERGE + (c + 1) * d_model])).astype(mg_ref.dtype)


def _inproj(x, g, pos, invf, w_all, layer):
    s, d = x.shape
    t = min(ROW_TILE, s)
    wcols = w_all.shape[2]
    kck = min(NSA_KEY_CHUNK, s)
    sck = min(SB_KEY_CHUNK, s)
    wck = min(NSA_Q_BLOCK, s)
    row = lambda n: pl.BlockSpec((t, n), lambda i: (i, 0))
    col = lambda n: pl.BlockSpec((n, t), lambda i: (0, i))
    slab = lambda n, ck: pl.BlockSpec((t // ck, n, ck), lambda i: (i, 0, 0))
    sds = jax.ShapeDtypeStruct
    out_shape = [
        sds((NSA_W, s), MXU_DTYPE), sds((NSA_W, s), MXU_DTYPE),
        sds((s, HEAD_DIM), F32), sds((s, HEAD_DIM), F32),
        sds((s, LANES), MXU_DTYPE), sds((s // kck, VT_ROWS, kck), MXU_DTYPE),
        sds((s, LANES), MXU_DTYPE), sds((s // wck, VT_ROWS, wck), MXU_DTYPE),
        sds((LANES, s), F32),
        sds((s, SB_W), MXU_DTYPE), sds((s // sck, SB_W, sck), MXU_DTYPE), sds((s, SB_W), MXU_DTYPE),
        sds((s, 2 * d), MXU_DTYPE),
    ]
    out_specs = [col(NSA_W), col(NSA_W), row(HEAD_DIM), row(HEAD_DIM),
                 row(LANES), slab(VT_ROWS, kck), row(LANES), slab(VT_ROWS, wck),
                 col(LANES), row(SB_W), slab(SB_W, sck), row(SB_W), row(2 * d)]
    return pl.pallas_call(
        _inproj_kernel,
        grid=(s // t,),
        in_specs=[row(d),
                  _resident((None, 1, d), lambda i: (layer, 0, 0)),
                  row(1),
                  _resident((1, LANES), lambda i: (0, 0)),
                  _resident((None, d, wcols), lambda i: (layer, 0, 0))],
        out_specs=out_specs,
        out_shape=out_shape,
        compiler_params=_params(1),
        name="inproj",
    )(x, g, pos, invf, w_all)


def _compress_kernel(kc_ref, vc_ref, pe_ref, w1_ref, w2_ref, kcmp_ref, vcmpt_ref):
    nr, half_w = kc_ref.shape
    outs = []
    for kv, r_ref in enumerate((kc_ref, vc_ref)):
        r = r_ref[...]
        ha = _dot((r + pe_ref[kv, 0:1, :]).astype(MXU_DTYPE), w1_ref[kv, 0:half_w, :])
        hb = _dot((r + pe_ref[kv, 1:2, :]).astype(MXU_DTYPE), w1_ref[kv, half_w:2 * half_w, :])
        hid = ha + pltpu.roll(hb, nr - 1, 0)
        outs.append(_dot(jax.nn.gelu(hid).astype(MXU_DTYPE), w2_ref[kv]))
    pad = jnp.zeros_like(outs[0])
    kcmp_ref[...] = jnp.concatenate([outs[0], pad], axis=1).astype(kcmp_ref.dtype)
    vcmpt_ref[...] = jnp.concatenate([outs[1], pad], axis=1).T[0:HEAD_DIM].astype(vcmpt_ref.dtype)


def _compress(kc, vc, pe_all, w1_all, w2_all, layer):
    nr, half_w = kc.shape
    hidden = w1_all.shape[3]
    full = lambda shape: pl.BlockSpec(shape, lambda i: (0,) * len(shape))
    return pl.pallas_call(
        _compress_kernel,
        grid=(1,),
        in_specs=[full((nr, half_w)), full((nr, half_w)),
                  pl.BlockSpec((None, 2, 2, half_w), lambda i: (layer, 0, 0, 0)),
                  pl.BlockSpec((None, 2, 2 * half_w, hidden), lambda i: (layer, 0, 0, 0)),
                  pl.BlockSpec((None, 2, hidden, HEAD_DIM), lambda i: (layer, 0, 0, 0))],
        out_specs=[full((nr, LANES)), full((HEAD_DIM, nr))],
        out_shape=[jax.ShapeDtypeStruct((nr, LANES), MXU_DTYPE), jax.ShapeDtypeStruct((HEAD_DIM, nr), MXU_DTYPE)],
        compiler_params=_params(1),
        name="compress",
    )(kc, vc, pe_all, w1_all, w2_all)


def _nsa_kernel(qpt_ref, qrt_ref, gate_ref, kcmp_ref, vcmpt_ref, ovlt_ref, ks_ref, vst_ref, kw_ref, vwt_ref,
                out_ref, qaug_scr, mask_scr, m_scr, acc_scr, sa_scr, sb_scr, ta_scr, tb_scr, ocmp_scr, owin_scr, imp_scr, *, n_sel):
    tq = qpt_ref.shape[1]
    seq = ks_ref.shape[0]
    ncp = kcmp_ref.shape[0]
    nsp = ovlt_ref.shape[0]
    ck = vst_ref.shape[2]
    wck = vwt_ref.shape[2]
    nh = NSA_HEADS
    q0 = pl.program_id(0) * tq
    t = q0 + lax.broadcasted_iota(jnp.int32, (1, tq), 1)
    cols = [slice(h * tq, (h + 1) * tq) for h in range(nh)]

    def aug(qt_ref, h, tail):
        return jnp.concatenate([qt_ref[h * HEAD_DIM:(h + 1) * HEAD_DIM, :], tail], axis=0)

    zeros_tail = jnp.zeros((HEAD_DIM, tq), MXU_DTYPE)
    def cmp_branch(rows):
        qp_aug = jnp.concatenate([aug(qpt_ref, h, zeros_tail) for h in range(nh)], axis=1)
        sc_all = _dot(kcmp_ref[0:rows, :], qp_aug)
        cmp_last = CMP_STRIDE * lax.broadcasted_iota(jnp.int32, (rows, 1), 0) + (CMP_BLOCK - 1)
        vis_c = cmp_last <= t
        psum = jnp.zeros((rows, tq), F32)
        p_cmp = []
        for h in range(nh):
            sc = jnp.where(vis_c, sc_all[:, cols[h]], -1e30)
            top = jnp.max(sc, axis=0, keepdims=True)
            e = jnp.exp2(sc - jnp.where(top > -1e29, top, 0.0))
            den = jnp.sum(e, axis=0, keepdims=True)
            p = e * (1.0 / jnp.where(den > 0.0, den, 1.0))
            psum = psum + p
            p_cmp.append(p.astype(MXU_DTYPE))
        o_cmp = _dot(vcmpt_ref[:, 0:rows], jnp.concatenate(p_cmp, axis=1))
        p_hi = psum.astype(MXU_DTYPE)
        p_lo = (psum - p_hi.astype(F32)).astype(MXU_DTYPE)
        imp_scr[...] = _dot(ovlt_ref[:, 0:rows], p_hi) + _dot(ovlt_ref[:, 0:rows], p_lo)
        ocmp_scr[...] = o_cmp

    n_var = CMP_VARIANTS if ncp % (CMP_VARIANTS * LANES) == 0 else 1
    step = ncp // n_var
    n_vis = (q0 + tq - CMP_BLOCK) // CMP_STRIDE + 1
    variant = jnp.clip((n_vis + step - 1) // step - 1, 0, n_var - 1)
    for v in range(n_var):
        pl.when(variant == v)(functools.partial(cmp_branch, (v + 1) * step))
    imp = imp_scr[...]

    qr_aug = jnp.concatenate([aug(qrt_ref, h, zeros_tail) for h in range(nh)], axis=1)
    span = min(WINDOW + tq, seq)
    start = pl.multiple_of(jnp.maximum(q0 + tq - span, 0), wck)
    sw_all = _dot(kw_ref[pl.ds(start, span), :], qr_aug)

    def window_finish():
        kpos_w = start + lax.broadcasted_iota(jnp.int32, (span, 1), 0)
        vis_w = (kpos_w <= t) & (kpos_w > t - WINDOW)
        p_win = []
        for h in range(nh):
            sw = jnp.where(vis_w, sw_all[:, cols[h]], -1e30)
            p_win.append(jnp.exp2(sw - jnp.max(sw, axis=0, keepdims=True)).astype(MXU_DTYPE))
        p_win = jnp.concatenate(p_win, axis=1)
        ow = jnp.zeros((VT_ROWS, nh * tq), F32)
        for j in range(span // wck):
            ow = ow + _dot(vwt_ref[start // wck + j], p_win[j * wck:(j + 1) * wck, :])
        owin_scr[...] = ow[0:HEAD_DIM] * (1.0 / ow[HEAD_DIM:HEAD_DIM + 1])

    blk = lax.broadcasted_iota(jnp.int32, (nsp, 1), 0)
    cur = t // SEL_BLOCK
    valid = blk <= cur
    forced = (blk == 0) | (blk == cur) | (blk == cur - 1)
    n_forced = 3
    assert n_sel >= n_forced
    score = jnp.where(valid, jnp.where(forced, -jnp.inf, imp), -1.0)
    chosen = jnp.where(forced & valid, 1.0, 0.0)
    for _ in range(n_sel - n_forced):
        best = jnp.max(score, axis=0, keepdims=True)
        idx = jnp.min(jnp.where(score == best, blk, nsp), axis=0, keepdims=True)
        hit = blk == idx
        chosen = jnp.where(hit, 1.0, chosen)
        score = jnp.where(hit, -jnp.inf, score)
    mask_scr[0:nsp, :] = ((chosen - 1.0) * (-MASKED)).astype(mask_scr.dtype)
    mask_scr[nsp:nsp + SEL_CODE_BLOCKS, :] = jnp.full((SEL_CODE_BLOCKS, tq), MASKED, mask_scr.dtype)

    for h in range(nh):
        qaug_scr[0:HEAD_DIM, cols[h]] = qrt_ref[h * HEAD_DIM:(h + 1) * HEAD_DIM, :]
    m_scr[...] = jnp.full(m_scr.shape, -1e30, F32)
    acc_scr[...] = jnp.zeros(acc_scr.shape, F32)

    c_diag = q0 // ck

    def sel_scores(c, s_ref, top_ref, limit):
        k0 = pl.multiple_of(jnp.minimum(c, c_diag) * ck, ck)
        code0 = jnp.where(c < limit, (k0 // (SEL_BLOCK * SEL_CODE_BLOCKS)) * SEL_CODE_BLOCKS, nsp)
        mrows = mask_scr[pl.ds(pl.multiple_of(code0, SEL_CODE_BLOCKS), SEL_CODE_BLOCKS), :]
        for h in range(nh):
            qaug_scr[HEAD_DIM:2 * HEAD_DIM, cols[h]] = mrows
        s_all = _dot(ks_ref[pl.ds(k0, ck), :], qaug_scr[...])
        s_ref[...] = s_all
        top_ref[...] = jnp.max(s_all, axis=0, keepdims=True)

    def sel_update(c, s_ref, top_ref, causal):
        m_old = m_scr[...]
        if causal:
            kpos = c * ck + lax.broadcasted_iota(jnp.int32, (ck, 1), 0)
            bias = jnp.where(kpos <= t, 0.0, MASKED)
            tops = [jnp.max(s_ref[:, cols[h]] + bias, axis=0, keepdims=True) for h in range(nh)]
            m_new = jnp.maximum(m_old, jnp.concatenate(tops, axis=1))
        else:
            m_new = jnp.maximum(m_old, top_ref[...])
        p_all = []
        for h in range(nh):
            s = s_ref[:, cols[h]]
            if causal:
                s = s + bias
            p_all.append(jnp.exp2(s - m_new[:, cols[h]]).astype(MXU_DTYPE))
        pv = _dot(vst_ref[jnp.minimum(c, c_diag)], jnp.concatenate(p_all, axis=1))
        acc_scr[...] = jnp.exp2(m_old - m_new) * acc_scr[...] + pv
        m_scr[...] = m_new

    def sel_pair(c):
        sel_scores(c + 1, sa_scr, ta_scr, c_diag)
        sel_update(c, sb_scr, tb_scr, False)
        sel_scores(c + 2, sb_scr, tb_scr, c_diag)
        sel_update(c + 1, sa_scr, ta_scr, False)

    def sel_quad(j, carry):
        sel_pair(4 * j)
        sel_pair(4 * j + 2)
        return carry

    sel_scores(c_diag, sa_scr, ta_scr, c_diag + 1)
    sel_scores(0, sb_scr, tb_scr, c_diag)
    window_finish()
    sel_update(c_diag, sa_scr, ta_scr, True)
    n_quads = c_diag // 4
    lax.fori_loop(0, n_quads, sel_quad, 0)
    lax.fori_loop(0, (c_diag - 4 * n_quads + 1) // 2, lambda j, carry: (sel_pair(4 * n_quads + 2 * j), carry)[1], 0)
    acc = acc_scr[...]
    o_sel = acc[0:HEAD_DIM] * (1.0 / acc[HEAD_DIM:HEAD_DIM + 1])

    o_cmp = ocmp_scr[...]
    o_win = owin_scr[...]
    gates = gate_ref[...]
    merged = []
    for h in range(nh):
        merged.append(gates[3 * h:3 * h + 1, :] * o_cmp[:, cols[h]]
                      + gates[3 * h + 1:3 * h + 2, :] * o_sel[:, cols[h]]
                      + gates[3 * h + 2:3 * h + 3, :] * o_win[:, cols[h]])
    out_ref[...] = jnp.concatenate(merged, axis=0).T.astype(out_ref.dtype)


def _nsa(qpt, qrt, gates_t, kcmp, vcmpt, ovlt, ks, vst, kw, vwt):
    s = ks.shape[0]
    tq = min(NSA_Q_BLOCK, s)
    nsp = ovlt.shape[0]
    col = lambda n: pl.BlockSpec((n, tq), lambda i: (0, i))
    res = lambda a: _resident(a.shape, lambda i: (0,) * a.ndim)
    return pl.pallas_call(
        functools.partial(_nsa_kernel, n_sel=min(SEL_TOP_N, s // SEL_BLOCK)),
        grid=(s // tq,),
        in_specs=[col(NSA_W), col(NSA_W), col(LANES), res(kcmp), res(vcmpt), res(ovlt),
                  res(ks), res(vst), res(kw), res(vwt)],
        out_specs=pl.BlockSpec((tq, NSA_W), lambda i: (i, 0)),
        out_shape=jax.ShapeDtypeStruct((s, NSA_W), MXU_DTYPE),
        scratch_shapes=[pltpu.VMEM((LANES, NSA_HEADS * tq), MXU_DTYPE),
                        pltpu.VMEM((nsp + SEL_CODE_BLOCKS, tq), MXU_DTYPE),
                        pltpu.VMEM((1, NSA_HEADS * tq), F32), pltpu.VMEM((VT_ROWS, NSA_HEADS * tq), F32),
                        pltpu.VMEM((vst.shape[2], NSA_HEADS * tq), F32),
                        pltpu.VMEM((vst.shape[2], NSA_HEADS * tq), F32),
                        pltpu.VMEM((1, NSA_HEADS * tq), F32), pltpu.VMEM((1, NSA_HEADS * tq), F32),
                        pltpu.VMEM((HEAD_DIM, NSA_HEADS * tq), F32), pltpu.VMEM((HEAD_DIM, NSA_HEADS * tq), F32),
                        pltpu.VMEM((nsp, tq), F32)],
        compiler_params=_params(1),
        name="nsa",
    )(qpt, qrt, gates_t, kcmp, vcmpt, ovlt, ks, vst, kw, vwt)


def _softplus2(z2):
    neg_abs = lax.bitcast_convert_type(lax.bitcast_convert_type(z2, jnp.uint32) | jnp.uint32(0x80000000), F32)
    return jnp.maximum(z2, 0.0) + jnp.log2(1.0 + jnp.exp2(neg_abs))


def _sb_kernel(q_ref, k_ref, v_ref, out_ref, acc_scr, run_scr, za_scr, zb_scr, done_scr):
    assert (q_ref.shape[0] // k_ref.shape[2]) % 2 == 0
    tb = q_ref.shape[0]
    ck = k_ref.shape[2]
    nsub = tb // ck
    i = pl.program_id(0)
    r = lax.broadcasted_iota(jnp.int32, (ck, ck), 0)
    c = lax.broadcasted_iota(jnp.int32, (ck, ck), 1)
    tri = jnp.where(r >= c, 1.0, 0.0).astype(MXU_DTYPE)
    before = c < r
    heads = [slice(h * SB_HEAD_DIM, (h + 1) * SB_HEAD_DIM) for h in range(SB_HEADS)]

    def logits(h, rows, chunk):
        return _dot(q_ref[rows, heads[h]], k_ref[chunk, heads[h], :])

    def step(h, rows, chunk, diag, first, z=None):
        hs = heads[h]
        if z is None:
            z = logits(h, rows, chunk)
        sp = _softplus2(z)
        if diag:
            sp = jnp.where(before, sp, 0.0)
        cs = _dot(sp.astype(MXU_DTYPE), tri)
        own = jnp.minimum(z - cs, 0.0)
        if first:
            a = jnp.exp2(own)
            run_scr[h, rows] = cs[:, 0:1]
        else:
            run = run_scr[h, rows]
            a = jnp.exp2(own - run)
            run_scr[h, rows] = run + cs[:, 0:1]
        if diag:
            a = jnp.where(before, a, 0.0)
        pv = _dot(a.astype(MXU_DTYPE), v_ref[pl.ds(pl.multiple_of(chunk * ck, ck), ck), hs])
        if first:
            acc_scr[h, rows] = pv
        else:
            acc_scr[h, rows] += pv

    plan = []
    for g in range(nsub):
        rows = slice(g * ck, (g + 1) * ck)
        plan += [(h, rows, i * nsub + g, True, True) for h in range(SB_HEADS)]
        for back in range(g):
            plan += [(h, rows, i * nsub + g - 1 - back, False, False) for h in range(SB_HEADS)]
    zs = [logits(h, rows, chunk) for h, rows, chunk, _, _ in plan]
    for args, z in zip(plan, zs):
        step(*args, z)

    everything = slice(0, tb)

    def lookahead(chunk, z_ref):
        for h in range(SB_HEADS):
            z_ref[h] = logits(h, everything, jnp.maximum(chunk, 0))

    def exhausted():
        least = functools.reduce(jnp.minimum, [run_scr[h] for h in range(SB_HEADS)])
        return (jnp.min(least) >= SB_EXHAUSTED_LOG2).astype(jnp.int32)

    def pair(state):
        j, _ = state
        cur = i * nsub - 1 - 2 * j
        lookahead(cur - 1, zb_scr)
        for h in range(SB_HEADS):
            step(h, everything, cur, False, False, za_scr[h])
        done_scr[0] = exhausted()

        @pl.when(done_scr[0] == 0)
        def _():
            lookahead(cur - 2, za_scr)
            for h in range(SB_HEADS):
                step(h, everything, cur - 1, False, False, zb_scr[h])
            done_scr[0] = exhausted()

        return j + 1, done_scr[0]

    lookahead(i * nsub - 1, za_scr)
    lax.while_loop(lambda state: (state[0] < (i * nsub) // 2) & (state[1] == 0), pair,
                   (jnp.int32(0), jnp.int32(0)))
    for h, hs in enumerate(heads):
        out_ref[:, hs] = acc_scr[h].astype(out_ref.dtype)


def _sb(q, kt, v):
    s = q.shape[0]
    tb = min(SB_BLOCK, s)
    return pl.pallas_call(
        _sb_kernel,
        grid=(s // tb,),
        in_specs=[pl.BlockSpec((tb, SB_W), lambda i: (i, 0)),
                  _resident(kt.shape, lambda i: (0, 0, 0)),
                  _resident((s, SB_W), lambda i: (0, 0))],
        out_specs=pl.BlockSpec((tb, SB_W), lambda i: (i, 0)),
        out_shape=jax.ShapeDtypeStruct((s, SB_W), MXU_DTYPE),
        scratch_shapes=[pltpu.VMEM((SB_HEADS, tb, SB_HEAD_DIM), F32), pltpu.VMEM((SB_HEADS, tb, 1), F32),
                        pltpu.VMEM((SB_HEADS, tb, kt.shape[2]), F32), pltpu.VMEM((SB_HEADS, tb, kt.shape[2]), F32),
                        pltpu.SMEM((1,), jnp.int32)],
        compiler_params=_params(1),
        name="sb",
    )(q, kt, v)


def _mix_kernel(x_ref, nsa_ref, sb_ref, mg_ref, wn_ref, ws_ref, wo_ref, g_ref, out_ref):
    d = x_ref.shape[1]
    y_nsa = _dot(nsa_ref[...], wn_ref[...])
    y_sb = _dot(sb_ref[...], ws_ref[...])
    merged = mg_ref[:, 0:d].astype(F32) * y_nsa + mg_ref[:, d:2 * d].astype(F32) * y_sb
    mixed = _dot(merged.astype(MXU_DTYPE), wo_ref[...])
    out_ref[...] = x_ref[...] + _rms(mixed, g_ref[...])


def _mix(x, nsa_o, sb_o, mg, wn_all, ws_all, wo_all, g_all, layer):
    s, d = x.shape
    t = min(ROW_TILE, s)
    row = lambda n: pl.BlockSpec((t, n), lambda i: (i, 0))
    lay = lambda a: _resident((None,) + a.shape[1:], lambda i: (layer, 0, 0))
    return pl.pallas_call(
        _mix_kernel,
        grid=(s // t,),
        in_specs=[row(d), row(NSA_W), row(SB_W), row(2 * d), lay(wn_all), lay(ws_all), lay(wo_all), lay(g_all)],
        out_specs=row(d),
        out_shape=jax.ShapeDtypeStruct((s, d), F32),
        compiler_params=_params(1),
        name="mix",
    )(x, nsa_o, sb_o, mg, wn_all, ws_all, wo_all, g_all)


def _ffn_kernel(x_ref, gin_ref, w1_ref, w2_ref, gout_ref, out_ref):
    x = x_ref[...]
    d = x.shape[1]
    hb = _rms(x, gin_ref[...]).astype(MXU_DTYPE)
    ff = jnp.zeros_like(x)
    for c in range(w1_ref.shape[1] // d):
        up = _dot(hb, w1_ref[:, c * d:(c + 1) * d])
        ff = ff + _dot(jnp.square(jnp.maximum(up, 0.0)).astype(MXU_DTYPE), w2_ref[c * d:(c + 1) * d, :])
    out_ref[...] = x + _rms(ff, gout_ref[...])


def _ffn(x, gin_all, w1_all, w2_all, gout_all, layer):
    s, d = x.shape
    t = min(ROW_TILE, s)
    row = lambda n: pl.BlockSpec((t, n), lambda i: (i, 0))
    lay = lambda a: _resident((None,) + a.shape[1:], lambda i: (layer, 0, 0))
    return pl.pallas_call(
        _ffn_kernel,
        grid=(s // t,),
        in_specs=[row(d), lay(gin_all), lay(w1_all), lay(w2_all), lay(gout_all)],
        out_specs=row(d),
        out_shape=jax.ShapeDtypeStruct((s, d), F32),
        compiler_params=_params(1),
        name="ffn",
    )(x, gin_all, w1_all, w2_all, gout_all)


def _mixffn_kernel(x_ref, nsa_ref, sb_ref, mg_ref, wn_ref, ws_ref, wo_ref, gmix_ref,
                   gin_ref, w1_ref, w2_ref, gout_ref, out_ref):
    d = x_ref.shape[1]
    y_nsa = _dot(nsa_ref[...], wn_ref[...])
    y_sb = _dot(sb_ref[...], ws_ref[...])
    merged = mg_ref[:, 0:d].astype(F32) * y_nsa + mg_ref[:, d:2 * d].astype(F32) * y_sb
    x = x_ref[...] + _rms(_dot(merged.astype(MXU_DTYPE), wo_ref[...]), gmix_ref[...])
    hb = _rms(x, gin_ref[...]).astype(MXU_DTYPE)
    ff = jnp.zeros_like(x)
    for c in range(w1_ref.shape[1] // d):
        up = _dot(hb, w1_ref[:, c * d:(c + 1) * d])
        ff = ff + _dot(jnp.square(jnp.maximum(up, 0.0)).astype(MXU_DTYPE), w2_ref[c * d:(c + 1) * d, :])
    out_ref[...] = x + _rms(ff, gout_ref[...])


def _mixffn(x, nsa_o, sb_o, mg, wn_all, ws_all, wo_all, gmix_all, gin_all, w1_all, w2_all, gout_all, layer):
    s, d = x.shape
    t = min(FUSED_ROW_TILE, s)
    row = lambda n: pl.BlockSpec((t, n), lambda i: (i, 0))
    lay = lambda a: _resident((None,) + a.shape[1:], lambda i: (layer, 0, 0))
    return pl.pallas_call(
        _mixffn_kernel,
        grid=(s // t,),
        in_specs=[row(d), row(NSA_W), row(SB_W), row(2 * d), lay(wn_all), lay(ws_all), lay(wo_all), lay(gmix_all),
                  lay(gin_all), lay(w1_all), lay(w2_all), lay(gout_all)],
        out_specs=row(d),
        out_shape=jax.ShapeDtypeStruct((s, d), F32),
        compiler_params=_params(1),
        name="mixffn",
    )(x, nsa_o, sb_o, mg, wn_all, ws_all, wo_all, gmix_all, gin_all, w1_all, w2_all, gout_all)


def _regroup_w_in(w_in):
    gate_lo, gate_hi = _C_GATE, _C_GATE + 3 * NSA_HEADS
    pad = jnp.zeros(w_in.shape[:2] + (LANES - 3 * NSA_HEADS,), w_in.dtype)
    return jnp.concatenate([w_in[..., :gate_lo], w_in[..., gate_lo:gate_hi], pad, w_in[..., gate_hi:]],
                           axis=-1).astype(MXU_DTYPE)


def kernel(x, positions, norm_g, w_in, cmp_pe, cmp_w1, cmp_w2, w_nsa_o, w_sb_o, w_out, w_ff1, w_ff2):
    b, s, d = x.shape
    depth = w_in.shape[0]
    ncp, ns = s // CMP_STRIDE, s // SEL_BLOCK
    nsp = -(-ns // LANES) * LANES
    half_w = CMP_STRIDE * HEAD_DIM

    w_in_r = _regroup_w_in(w_in)
    pe = cmp_pe.reshape(depth, 2, 2, half_w)
    w1 = cmp_w1.astype(MXU_DTYPE)
    w2 = cmp_w2.astype(MXU_DTYPE)
    wn, ws, wo = w_nsa_o.astype(MXU_DTYPE), w_sb_o.astype(MXU_DTYPE), w_out.astype(MXU_DTYPE)
    wf1, wf2 = w_ff1.astype(MXU_DTYPE), w_ff2.astype(MXU_DTYPE)
    g_pre, g_mix, g_ffn_in, g_ffn_out = (norm_g[:, n][:, None, :] for n in range(4))

    dim = jnp.arange(LANES) % HEAD_DIM
    half = ROT_DIM // 2
    inv_freq = jnp.power(ROPE_THETA, (dim % half).astype(F32) * (-2.0 / ROT_DIM))
    invf = jnp.where(dim < ROT_DIM, inv_freq, 0.0)[None, :].astype(F32)
    c_start = CMP_STRIDE * jnp.arange(ncp)[None, :]
    s_start = SEL_BLOCK * jnp.arange(nsp)[:, None]
    ovlt = ((c_start < s_start + SEL_BLOCK) & (c_start + CMP_BLOCK > s_start) & (s_start < s)).astype(MXU_DTYPE)

    outs = []
    for bi in range(b):
        xb = x[bi]
        pos = positions[bi][:, None]
        for layer in range(depth):
            (qpt, qrt, kc, vc, ks, vst, kw, vwt, gates_t, sbq, sbkt, sbv, mg) = _inproj(
                xb, g_pre, pos, invf, w_in_r, layer)
            kcmp, vcmpt = _compress(kc.reshape(ncp, half_w), vc.reshape(ncp, half_w), pe, w1, w2, layer)
            nsa_o = _nsa(qpt, qrt, gates_t, kcmp, vcmpt, ovlt, ks, vst, kw, vwt)
            sb_o = _sb(sbq, sbkt, sbv)
            xb = _mixffn(xb, nsa_o, sb_o, mg, wn, ws, wo, g_mix, g_ffn_in, wf1, wf2, g_ffn_out, layer)
        outs.append(xb)
    return jnp.stack(outs, axis=0)
```

```python
import functools

import jax
import jax.numpy as jnp
from jax import lax
from jax.experimental import pallas as pl
from jax.experimental.pallas import tpu as pltpu

F32 = jnp.float32
MXU_DTYPE = jnp.bfloat16

HEAD_DIM = 64
NSA_HEADS = 8
SB_HEADS = 4
SB_HEAD_DIM = 128
ROPE_THETA = 500000.0
ROT_DIM = HEAD_DIM // 4
CMP_BLOCK = 32
CMP_STRIDE = 16
SEL_BLOCK = 64
SEL_TOP_N = 8
WINDOW = 512
RMS_EPS = 1e-6
NSA_W = NSA_HEADS * HEAD_DIM
SB_W = SB_HEADS * SB_HEAD_DIM
LANES = 128
VT_ROWS = HEAD_DIM + 16
MASKED = -32768.0
LOG2_E = 1.4426950408889634
VMEM_LIMIT = 56 * 1024 * 1024

ROW_TILE = 1024
NSA_Q_BLOCK = 256
NSA_KEY_CHUNK = 512
SEL_CODE_BLOCKS = HEAD_DIM
CMP_VARIANTS = 4
SB_BLOCK = 512
SB_KEY_CHUNK = 256
SB_EXHAUSTED_LOG2 = 160.0


def _dot(a, b):
    return jnp.dot(a, b, preferred_element_type=F32)


def _rms(x, g):
    return x * lax.rsqrt(jnp.mean(x * x, axis=-1, keepdims=True) + RMS_EPS) * g


def _params(n_grid_dims):
    return pltpu.CompilerParams(dimension_semantics=("parallel",) * n_grid_dims,
                                vmem_limit_bytes=VMEM_LIMIT)


def _resident(block_shape, index_map):
    return pl.BlockSpec(block_shape, index_map, pipeline_mode=pl.Buffered(1))


_C_KV = NSA_W
_C_GATE = _C_KV + 6 * HEAD_DIM
_C_SB = _C_GATE + LANES
_C_MERGE = _C_SB + 3 * SB_W


def _inproj_kernel(x_ref, g_ref, pos_ref, invf_ref, w_ref,
                   qpt_ref, qrt_ref, kc_ref, vc_ref, ks_ref, vst_ref, kw_ref, vwt_ref,
                   gate_ref, sbq_ref, sbk_ref, sbv_ref, mg_ref):
    t, d_model = x_ref.shape
    hb = _rms(x_ref[...], g_ref[...]).astype(MXU_DTYPE)

    ang = pos_ref[...].astype(F32) * invf_ref[...]
    cos, sin = jnp.cos(ang), jnp.sin(ang)
    lane = lax.broadcasted_iota(jnp.int32, (1, LANES), 1)
    dim = lane % HEAD_DIM
    half = ROT_DIM // 2
    sin_up = jnp.where((dim >= half) & (dim < ROT_DIM), sin, 0.0)
    sin_dn = jnp.where(dim < half, -sin, 0.0)
    low = lane < HEAD_DIM

    def rope(xg):
        return xg * cos + pltpu.roll(xg, half, 1) * sin_up + pltpu.roll(xg, LANES - half, 1) * sin_dn

    pa = _dot(hb, w_ref[:, 0:_C_GATE])
    scale = LOG2_E * HEAD_DIM ** -0.5
    for j in range(NSA_W // LANES):
        qg = pa[:, j * LANES:(j + 1) * LANES]
        qpt_ref[j * LANES:(j + 1) * LANES, :] = (qg * scale).T.astype(qpt_ref.dtype)
        qrt_ref[j * LANES:(j + 1) * LANES, :] = (rope(qg) * scale).T.astype(qrt_ref.dtype)
    kc_ref[...] = pa[:, _C_KV:_C_KV + HEAD_DIM]
    vc_ref[...] = pa[:, _C_KV + HEAD_DIM:_C_KV + 2 * HEAD_DIM]

    row = pl.program_id(0) * t + lax.broadcasted_iota(jnp.int32, (t, 1), 0)
    code = jnp.where(lane - HEAD_DIM == (row // SEL_BLOCK) % SEL_CODE_BLOCKS, 1.0, 0.0)
    ksg = pa[:, _C_KV + 2 * HEAD_DIM:_C_KV + 4 * HEAD_DIM]
    ks_ref[...] = jnp.where(low, rope(ksg), code).astype(ks_ref.dtype)
    vs_t = jnp.where(low, pltpu.roll(ksg, HEAD_DIM, 1), 1.0).T[0:VT_ROWS]
    for c in range(vst_ref.shape[0]):
        vst_ref[c] = vs_t[:, c * vst_ref.shape[2]:(c + 1) * vst_ref.shape[2]].astype(vst_ref.dtype)
    kwg = pa[:, _C_KV + 4 * HEAD_DIM:_C_KV + 6 * HEAD_DIM]
    kw_ref[...] = jnp.where(low, rope(kwg), 0.0).astype(kw_ref.dtype)
    vw_t = jnp.where(low, pltpu.roll(kwg, HEAD_DIM, 1), 1.0).T[0:VT_ROWS]
    for c in range(vwt_ref.shape[0]):
        vwt_ref[c] = vw_t[:, c * vwt_ref.shape[2]:(c + 1) * vwt_ref.shape[2]].astype(vwt_ref.dtype)

    gate_ref[...] = jax.nn.sigmoid(_dot(hb, w_ref[:, _C_GATE:_C_SB])).T
    sb = _dot(hb, w_ref[:, _C_SB:_C_MERGE])
    sbq_ref[...] = (sb[:, 0:SB_W] * (LOG2_E * SB_HEAD_DIM ** -0.5)).astype(sbq_ref.dtype)
    sbk_t = sb[:, SB_W:2 * SB_W].T
    for c in range(sbk_ref.shape[0]):
        sbk_ref[c] = sbk_t[:, c * sbk_ref.shape[2]:(c + 1) * sbk_ref.shape[2]].astype(sbk_ref.dtype)
    sbv_ref[...] = sb[:, 2 * SB_W:3 * SB_W].astype(sbv_ref.dtype)
    for c in range(2):
        mg_ref[:, c * d_model:(c + 1) * d_model] = jax.nn.sigmoid(
            _dot(hb, w_ref[:, _C_MERGE + c * d_model:_C_MERGE + (c + 1) * d_model])).astype(mg_ref.dtype)


def _inproj(x, g, pos, invf, w_all, layer):
    s, d = x.shape
    t = min(ROW_TILE, s)
    wcols = w_all.shape[2]
    kck = min(NSA_KEY_CHUNK, s)
    sck = min(SB_KEY_CHUNK, s)
    wck = min(NSA_Q_BLOCK, s)
    row = lambda n: pl.BlockSpec((t, n), lambda i: (i, 0))
    col = lambda n: pl.BlockSpec((n, t), lambda i: (0, i))
    slab = lambda n, ck: pl.BlockSpec((t // ck, n, ck), lambda i: (i, 0, 0))
    sds = jax.ShapeDtypeStruct
    out_shape = [
        sds((NSA_W, s), MXU_DTYPE), sds((NSA_W, s), MXU_DTYPE),
        sds((s, HEAD_DIM), F32), sds((s, HEAD_DIM), F32),
        sds((s, LANES), MXU_DTYPE), sds((s // kck, VT_ROWS, kck), MXU_DTYPE),
        sds((s, LANES), MXU_DTYPE), sds((s // wck, VT_ROWS, wck), MXU_DTYPE),
        sds((LANES, s), F32),
        sds((s, SB_W), MXU_DTYPE), sds((s // sck, SB_W, sck), MXU_DTYPE), sds((s, SB_W), MXU_DTYPE),
        sds((s, 2 * d), MXU_DTYPE),
    ]
    out_specs = [col(NSA_W), col(NSA_W), row(HEAD_DIM), row(HEAD_DIM),
                 row(LANES), slab(VT_ROWS, kck), row(LANES), slab(VT_ROWS, wck),
                 col(LANES), row(SB_W), slab(SB_W, sck), row(SB_W), row(2 * d)]
    return pl.pallas_call(
        _inproj_kernel,
        grid=(s // t,),
        in_specs=[row(d),
                  _resident((None, 1, d), lambda i: (layer, 0, 0)),
                  row(1),
                  _resident((1, LANES), lambda i: (0, 0)),
                  _resident((None, d, wcols), lambda i: (layer, 0, 0))],
        out_specs=out_specs,
        out_shape=out_shape,
        compiler_params=_params(1),
        name="inproj",
    )(x, g, pos, invf, w_all)


def _compress_kernel(kc_ref, vc_ref, pe_ref, w1_ref, w2_ref, kcmp_ref, vcmpt_ref):
    nr, half_w = kc_ref.shape
    outs = []
    for kv, r_ref in enumerate((kc_ref, vc_ref)):
        r = r_ref[...]
        ha = _dot((r + pe_ref[kv, 0:1, :]).astype(MXU_DTYPE), w1_ref[kv, 0:half_w, :])
        hb = _dot((r + pe_ref[kv, 1:2, :]).astype(MXU_DTYPE), w1_ref[kv, half_w:2 * half_w, :])
        hid = ha + pltpu.roll(hb, nr - 1, 0)
        outs.append(_dot(jax.nn.gelu(hid).astype(MXU_DTYPE), w2_ref[kv]))
    pad = jnp.zeros_like(outs[0])
    kcmp_ref[...] = jnp.concatenate([outs[0], pad], axis=1).astype(kcmp_ref.dtype)
    vcmpt_ref[...] = jnp.concatenate([outs[1], pad], axis=1).T[0:HEAD_DIM].astype(vcmpt_ref.dtype)


def _compress(kc, vc, pe_all, w1_all, w2_all, layer):
    nr, half_w = kc.shape
    hidden = w1_all.shape[3]
    full = lambda shape: pl.BlockSpec(shape, lambda i: (0,) * len(shape))
    return pl.pallas_call(
        _compress_kernel,
        grid=(1,),
        in_specs=[full((nr, half_w)), full((nr, half_w)),
                  pl.BlockSpec((None, 2, 2, half_w), lambda i: (layer, 0, 0, 0)),
                  pl.BlockSpec((None, 2, 2 * half_w, hidden), lambda i: (layer, 0, 0, 0)),
                  pl.BlockSpec((None, 2, hidden, HEAD_DIM), lambda i: (layer, 0, 0, 0))],
        out_specs=[full((nr, LANES)), full((HEAD_DIM, nr))],
        out_shape=[jax.ShapeDtypeStruct((nr, LANES), MXU_DTYPE), jax.ShapeDtypeStruct((HEAD_DIM, nr), MXU_DTYPE)],
        compiler_params=_params(1),
        name="compress",
    )(kc, vc, pe_all, w1_all, w2_all)


def _nsa_kernel(qpt_ref, qrt_ref, gate_ref, kcmp_ref, vcmpt_ref, ovlt_ref, ks_ref, vst_ref, kw_ref, vwt_ref,
                out_ref, qaug_scr, mask_scr, m_scr, acc_scr, sa_scr, sb_scr, ta_scr, tb_scr, ocmp_scr, owin_scr, imp_scr, *, n_sel):
    tq = qpt_ref.shape[1]
    seq = ks_ref.shape[0]
    ncp = kcmp_ref.shape[0]
    nsp = ovlt_ref.shape[0]
    ck = vst_ref.shape[2]
    wck = vwt_ref.shape[2]
    nh = NSA_HEADS
    q0 = pl.program_id(0) * tq
    t = q0 + lax.broadcasted_iota(jnp.int32, (1, tq), 1)
    cols = [slice(h * tq, (h + 1) * tq) for h in range(nh)]

    def aug(qt_ref, h, tail):
        return jnp.concatenate([qt_ref[h * HEAD_DIM:(h + 1) * HEAD_DIM, :], tail], axis=0)

    zeros_tail = jnp.zeros((HEAD_DIM, tq), MXU_DTYPE)
    def cmp_branch(rows):
        qp_aug = jnp.concatenate([aug(qpt_ref, h, zeros_tail) for h in range(nh)], axis=1)
        sc_all = _dot(kcmp_ref[0:rows, :], qp_aug)
        cmp_last = CMP_STRIDE * lax.broadcasted_iota(jnp.int32, (rows, 1), 0) + (CMP_BLOCK - 1)
        vis_c = cmp_last <= t
        psum = jnp.zeros((rows, tq), F32)
        p_cmp = []
        for h in range(nh):
            sc = jnp.where(vis_c, sc_all[:, cols[h]], -1e30)
            top = jnp.max(sc, axis=0, keepdims=True)
            e = jnp.exp2(sc - jnp.where(top > -1e29, top, 0.0))
            den = jnp.sum(e, axis=0, keepdims=True)
            p = e * (1.0 / jnp.where(den > 0.0, den, 1.0))
            psum = psum + p
            p_cmp.append(p.astype(MXU_DTYPE))
        o_cmp = _dot(vcmpt_ref[:, 0:rows], jnp.concatenate(p_cmp, axis=1))
        p_hi = psum.astype(MXU_DTYPE)
        p_lo = (psum - p_hi.astype(F32)).astype(MXU_DTYPE)
        imp_scr[...] = _dot(ovlt_ref[:, 0:rows], p_hi) + _dot(ovlt_ref[:, 0:rows], p_lo)
        ocmp_scr[...] = o_cmp

    n_var = CMP_VARIANTS if ncp % (CMP_VARIANTS * LANES) == 0 else 1
    step = ncp // n_var
    n_vis = (q0 + tq - CMP_BLOCK) // CMP_STRIDE + 1
    variant = jnp.clip((n_vis + step - 1) // step - 1, 0, n_var - 1)
    for v in range(n_var):
        pl.when(variant == v)(functools.partial(cmp_branch, (v + 1) * step))
    imp = imp_scr[...]

    qr_aug = jnp.concatenate([aug(qrt_ref, h, zeros_tail) for h in range(nh)], axis=1)
    span = min(WINDOW + tq, seq)
    start = pl.multiple_of(jnp.maximum(q0 + tq - span, 0), wck)
    sw_all = _dot(kw_ref[pl.ds(start, span), :], qr_aug)

    def window_finish():
        kpos_w = start + lax.broadcasted_iota(jnp.int32, (span, 1), 0)
        vis_w = (kpos_w <= t) & (kpos_w > t - WINDOW)
        p_win = []
        for h in range(nh):
            sw = jnp.where(vis_w, sw_all[:, cols[h]], -1e30)
            p_win.append(jnp.exp2(sw - jnp.max(sw, axis=0, keepdims=True)).astype(MXU_DTYPE))
        p_win = jnp.concatenate(p_win, axis=1)
        ow = jnp.zeros((VT_ROWS, nh * tq), F32)
        for j in range(span // wck):
            ow = ow + _dot(vwt_ref[start // wck + j], p_win[j * wck:(j + 1) * wck, :])
        owin_scr[...] = ow[0:HEAD_DIM] * (1.0 / ow[HEAD_DIM:HEAD_DIM + 1])

    blk = lax.broadcasted_iota(jnp.int32, (nsp, 1), 0)
    cur = t // SEL_BLOCK
    valid = blk <= cur
    forced = (blk == 0) | (blk == cur) | (blk == cur - 1)
    n_forced = 3
    assert n_sel >= n_forced
    score = jnp.where(valid, jnp.where(forced, -jnp.inf, imp), -1.0)
    chosen = jnp.where(forced & valid, 1.0, 0.0)
    for _ in range(n_sel - n_forced):
        best = jnp.max(score, axis=0, keepdims=True)
        idx = jnp.min(jnp.where(score == best, blk, nsp), axis=0, keepdims=True)
        hit = blk == idx
        chosen = jnp.where(hit, 1.0, chosen)
        score = jnp.where(hit, -jnp.inf, score)
    mask_scr[0:nsp, :] = ((chosen - 1.0) * (-MASKED)).astype(mask_scr.dtype)
    mask_scr[nsp:nsp + SEL_CODE_BLOCKS, :] = jnp.full((SEL_CODE_BLOCKS, tq), MASKED, mask_scr.dtype)

    for h in range(nh):
        qaug_scr[0:HEAD_DIM, cols[h]] = qrt_ref[h * HEAD_DIM:(h + 1) * HEAD_DIM, :]
    m_scr[...] = jnp.full(m_scr.shape, -1e30, F32)
    acc_scr[...] = jnp.zeros(acc_scr.shape, F32)

    c_diag = q0 // ck

    def sel_scores(c, s_ref, top_ref, limit):
        k0 = pl.multiple_of(jnp.minimum(c, c_diag) * ck, ck)
        code0 = jnp.where(c < limit, (k0 // (SEL_BLOCK * SEL_CODE_BLOCKS)) * SEL_CODE_BLOCKS, nsp)
        mrows = mask_scr[pl.ds(pl.multiple_of(code0, SEL_CODE_BLOCKS), SEL_CODE_BLOCKS), :]
        for h in range(nh):
            qaug_scr[HEAD_DIM:2 * HEAD_DIM, cols[h]] = mrows
        s_all = _dot(ks_ref[pl.ds(k0, ck), :], qaug_scr[...])
        s_ref[...] = s_all
        top_ref[...] = jnp.max(s_all, axis=0, keepdims=True)

    def sel_update(c, s_ref, top_ref, causal):
        m_old = m_scr[...]
        if causal:
            kpos = c * ck + lax.broadcasted_iota(jnp.int32, (ck, 1), 0)
            bias = jnp.where(kpos <= t, 0.0, MASKED)
            tops = [jnp.max(s_ref[:, cols[h]] + bias, axis=0, keepdims=True) for h in range(nh)]
            m_new = jnp.maximum(m_old, jnp.concatenate(tops, axis=1))
        else:
            m_new = jnp.maximum(m_old, top_ref[...])
        p_all = []
        for h in range(nh):
            s = s_ref[:, cols[h]]
            if causal:
                s = s + bias
            p_all.append(jnp.exp2(s - m_new[:, cols[h]]).astype(MXU_DTYPE))
        pv = _dot(vst_ref[jnp.minimum(c, c_diag)], jnp.concatenate(p_all, axis=1))
        acc_scr[...] = jnp.exp2(m_old - m_new) * acc_scr[...] + pv
        m_scr[...] = m_new

    def sel_pair(c):
        sel_scores(c + 1, sa_scr, ta_scr, c_diag)
        sel_update(c, sb_scr, tb_scr, False)
        sel_scores(c + 2, sb_scr, tb_scr, c_diag)
        sel_update(c + 1, sa_scr, ta_scr, False)

    def sel_quad(j, carry):
        sel_pair(4 * j)
        sel_pair(4 * j + 2)
        return carry

    sel_scores(c_diag, sa_scr, ta_scr, c_diag + 1)
    sel_scores(0, sb_scr, tb_scr, c_diag)
    window_finish()
    sel_update(c_diag, sa_scr, ta_scr, True)
    n_quads = c_diag // 4
    lax.fori_loop(0, n_quads, sel_quad, 0)
    lax.fori_loop(0, (c_diag - 4 * n_quads + 1) // 2, lambda j, carry: (sel_pair(4 * n_quads + 2 * j), carry)[1], 0)
    acc = acc_scr[...]
    o_sel = acc[0:HEAD_DIM] * (1.0 / acc[HEAD_DIM:HEAD_DIM + 1])

    o_cmp = ocmp_scr[...]
    o_win = owin_scr[...]
    gates = gate_ref[...]
    merged = []
    for h in range(nh):
        merged.append(gates[3 * h:3 * h + 1, :] * o_cmp[:, cols[h]]
                      + gates[3 * h + 1:3 * h + 2, :] * o_sel[:, cols[h]]
                      + gates[3 * h + 2:3 * h + 3, :] * o_win[:, cols[h]])
    out_ref[...] = jnp.concatenate(merged, axis=0).T.astype(out_ref.dtype)


def _nsa(qpt, qrt, gates_t, kcmp, vcmpt, ovlt, ks, vst, kw, vwt):
    s = ks.shape[0]
    tq = min(NSA_Q_BLOCK, s)
    nsp = ovlt.shape[0]
    col = lambda n: pl.BlockSpec((n, tq), lambda i: (0, i))
    res = lambda a: _resident(a.shape, lambda i: (0,) * a.ndim)
    return pl.pallas_call(
        functools.partial(_nsa_kernel, n_sel=min(SEL_TOP_N, s // SEL_BLOCK)),
        grid=(s // tq,),
        in_specs=[col(NSA_W), col(NSA_W), col(LANES), res(kcmp), res(vcmpt), res(ovlt),
                  res(ks), res(vst), res(kw), res(vwt)],
        out_specs=pl.BlockSpec((tq, NSA_W), lambda i: (i, 0)),
        out_shape=jax.ShapeDtypeStruct((s, NSA_W), MXU_DTYPE),
        scratch_shapes=[pltpu.VMEM((LANES, NSA_HEADS * tq), MXU_DTYPE),
                        pltpu.VMEM((nsp + SEL_CODE_BLOCKS, tq), MXU_DTYPE),
                        pltpu.VMEM((1, NSA_HEADS * tq), F32), pltpu.VMEM((VT_ROWS, NSA_HEADS * tq), F32),
                        pltpu.VMEM((vst.shape[2], NSA_HEADS * tq), F32),
                        pltpu.VMEM((vst.shape[2], NSA_HEADS * tq), F32),
                        pltpu.VMEM((1, NSA_HEADS * tq), F32), pltpu.VMEM((1, NSA_HEADS * tq), F32),
                        pltpu.VMEM((HEAD_DIM, NSA_HEADS * tq), F32), pltpu.VMEM((HEAD_DIM, NSA_HEADS * tq), F32),
                        pltpu.VMEM((nsp, tq), F32)],
        compiler_params=_params(1),
        name="nsa",
    )(qpt, qrt, gates_t, kcmp, vcmpt, ovlt, ks, vst, kw, vwt)


def _softplus2(z2):
    neg_abs = lax.bitcast_convert_type(lax.bitcast_convert_type(z2, jnp.uint32) | jnp.uint32(0x80000000), F32)
    return jnp.maximum(z2, 0.0) + jnp.log2(1.0 + jnp.exp2(neg_abs))


def _sb_kernel(q_ref, k_ref, v_ref, out_ref, acc_scr, run_scr, za_scr, zb_scr, done_scr):
    assert (q_ref.shape[0] // k_ref.shape[2]) % 2 == 0
    tb = q_ref.shape[0]
    ck = k_ref.shape[2]
    nsub = tb // ck
    i = pl.program_id(0)
    r = lax.broadcasted_iota(jnp.int32, (ck, ck), 0)
    c = lax.broadcasted_iota(jnp.int32, (ck, ck), 1)
    tri = jnp.where(r >= c, 1.0, 0.0).astype(MXU_DTYPE)
    before = c < r
    heads = [slice(h * SB_HEAD_DIM, (h + 1) * SB_HEAD_DIM) for h in range(SB_HEADS)]

    def logits(h, rows, chunk):
        return _dot(q_ref[rows, heads[h]], k_ref[chunk, heads[h], :])

    def step(h, rows, chunk, diag, first, z=None):
        hs = heads[h]
        if z is None:
            z = logits(h, rows, chunk)
        sp = _softplus2(z)
        if diag:
            sp = jnp.where(before, sp, 0.0)
        cs = _dot(sp.astype(MXU_DTYPE), tri)
        own = jnp.minimum(z - cs, 0.0)
        if first:
            a = jnp.exp2(own)
            run_scr[h, rows] = cs[:, 0:1]
        else:
            run = run_scr[h, rows]
            a = jnp.exp2(own - run)
            run_scr[h, rows] = run + cs[:, 0:1]
        if diag:
            a = jnp.where(before, a, 0.0)
        pv = _dot(a.astype(MXU_DTYPE), v_ref[pl.ds(pl.multiple_of(chunk * ck, ck), ck), hs])
        if first:
            acc_scr[h, rows] = pv
        else:
            acc_scr[h, rows] += pv

    plan = []
    for g in range(nsub):
        rows = slice(g * ck, (g + 1) * ck)
        plan += [(h, rows, i * nsub + g, True, True) for h in range(SB_HEADS)]
        for back in range(g):
            plan += [(h, rows, i * nsub + g - 1 - back, False, False) for h in range(SB_HEADS)]
    zs = [logits(h, rows, chunk) for h, rows, chunk, _, _ in plan]
    for args, z in zip(plan, zs):
        step(*args, z)

    everything = slice(0, tb)

    def lookahead(chunk, z_ref):
        for h in range(SB_HEADS):
            z_ref[h] = logits(h, everything, jnp.maximum(chunk, 0))

    def exhausted():
        least = functools.reduce(jnp.minimum, [run_scr[h] for h in range(SB_HEADS)])
        return (jnp.min(least) >= SB_EXHAUSTED_LOG2).astype(jnp.int32)

    def pair(state):
        j, _ = state
        cur = i * nsub - 1 - 2 * j
        lookahead(cur - 1, zb_scr)
        for h in range(SB_HEADS):
            step(h, everything, cur, False, False, za_scr[h])
        done_scr[0] = exhausted()

        @pl.when(done_scr[0] == 0)
        def _():
            lookahead(cur - 2, za_scr)
            for h in range(SB_HEADS):
                step(h, everything, cur - 1, False, False, zb_scr[h])
            done_scr[0] = exhausted()

        return j + 1, done_scr[0]

    lookahead(i * nsub - 1, za_scr)
    lax.while_loop(lambda state: (state[0] < (i * nsub) // 2) & (state[1] == 0), pair,
                   (jnp.int32(0), jnp.int32(0)))
    for h, hs in enumerate(heads):
        out_ref[:, hs] = acc_scr[h].astype(out_ref.dtype)


def _sb(q, kt, v):
    s = q.shape[0]
    tb = min(SB_BLOCK, s)
    return pl.pallas_call(
        _sb_kernel,
        grid=(s // tb,),
        in_specs=[pl.BlockSpec((tb, SB_W), lambda i: (i, 0)),
                  _resident(kt.shape, lambda i: (0, 0, 0)),
                  _resident((s, SB_W), lambda i: (0, 0))],
        out_specs=pl.BlockSpec((tb, SB_W), lambda i: (i, 0)),
        out_shape=jax.ShapeDtypeStruct((s, SB_W), MXU_DTYPE),
        scratch_shapes=[pltpu.VMEM((SB_HEADS, tb, SB_HEAD_DIM), F32), pltpu.VMEM((SB_HEADS, tb, 1), F32),
                        pltpu.VMEM((SB_HEADS, tb, kt.shape[2]), F32), pltpu.VMEM((SB_HEADS, tb, kt.shape[2]), F32),
                        pltpu.SMEM((1,), jnp.int32)],
        compiler_params=_params(1),
        name="sb",
    )(q, kt, v)


def _mix_kernel(x_ref, nsa_ref, sb_ref, mg_ref, wn_ref, ws_ref, wo_ref, g_ref, out_ref):
    d = x_ref.shape[1]
    y_nsa = _dot(nsa_ref[...], wn_ref[...])
    y_sb = _dot(sb_ref[...], ws_ref[...])
    merged = mg_ref[:, 0:d].astype(F32) * y_nsa + mg_ref[:, d:2 * d].astype(F32) * y_sb
    mixed = _dot(merged.astype(MXU_DTYPE), wo_ref[...])
    out_ref[...] = x_ref[...] + _rms(mixed, g_ref[...])


def _mix(x, nsa_o, sb_o, mg, wn_all, ws_all, wo_all, g_all, layer):
    s, d = x.shape
    t = min(ROW_TILE, s)
    row = lambda n: pl.BlockSpec((t, n), lambda i: (i, 0))
    lay = lambda a: _resident((None,) + a.shape[1:], lambda i: (layer, 0, 0))
    return pl.pallas_call(
        _mix_kernel,
        grid=(s // t,),
        in_specs=[row(d), row(NSA_W), row(SB_W), row(2 * d), lay(wn_all), lay(ws_all), lay(wo_all), lay(g_all)],
        out_specs=row(d),
        out_shape=jax.ShapeDtypeStruct((s, d), F32),
        compiler_params=_params(1),
        name="mix",
    )(x, nsa_o, sb_o, mg, wn_all, ws_all, wo_all, g_all)


def _ffn_kernel(x_ref, gin_ref, w1_ref, w2_ref, gout_ref, out_ref):
    x = x_ref[...]
    d = x.shape[1]
    hb = _rms(x, gin_ref[...]).astype(MXU_DTYPE)
    ff = jnp.zeros_like(x)
    for c in range(w1_ref.shape[1] // d):
        up = _dot(hb, w1_ref[:, c * d:(c + 1) * d])
        ff = ff + _dot(jnp.square(jnp.maximum(up, 0.0)).astype(MXU_DTYPE), w2_ref[c * d:(c + 1) * d, :])
    out_ref[...] = x + _rms(ff, gout_ref[...])


def _ffn(x, gin_all, w1_all, w2_all, gout_all, layer):
    s, d = x.shape
    t = min(ROW_TILE, s)
    row = lambda n: pl.BlockSpec((t, n), lambda i: (i, 0))
    lay = lambda a: _resident((None,) + a.shape[1:], lambda i: (layer, 0, 0))
    return pl.pallas_call(
        _ffn_kernel,
        grid=(s // t,),
        in_specs=[row(d), lay(gin_all), lay(w1_all), lay(w2_all), lay(gout_all)],
        out_specs=row(d),
        out_shape=jax.ShapeDtypeStruct((s, d), F32),
        compiler_params=_params(1),
        name="ffn",
    )(x, gin_all, w1_all, w2_all, gout_all)


def _regroup_w_in(w_in):
    gate_lo, gate_hi = _C_GATE, _C_GATE + 3 * NSA_HEADS
    pad = jnp.zeros(w_in.shape[:2] + (LANES - 3 * NSA_HEADS,), w_in.dtype)
    return jnp.concatenate([w_in[..., :gate_lo], w_in[..., gate_lo:gate_hi], pad, w_in[..., gate_hi:]],
                           axis=-1).astype(MXU_DTYPE)


def kernel(x, positions, norm_g, w_in, cmp_pe, cmp_w1, cmp_w2, w_nsa_o, w_sb_o, w_out, w_ff1, w_ff2):
    b, s, d = x.shape
    depth = w_in.shape[0]
    ncp, ns = s // CMP_STRIDE, s // SEL_BLOCK
    nsp = -(-ns // LANES) * LANES
    half_w = CMP_STRIDE * HEAD_DIM

    w_in_r = _regroup_w_in(w_in)
    pe = cmp_pe.reshape(depth, 2, 2, half_w)
    w1 = cmp_w1.astype(MXU_DTYPE)
    w2 = cmp_w2.astype(MXU_DTYPE)
    wn, ws, wo = w_nsa_o.astype(MXU_DTYPE), w_sb_o.astype(MXU_DTYPE), w_out.astype(MXU_DTYPE)
    wf1, wf2 = w_ff1.astype(MXU_DTYPE), w_ff2.astype(MXU_DTYPE)
    g_pre, g_mix, g_ffn_in, g_ffn_out = (norm_g[:, n][:, None, :] for n in range(4))

    dim = jnp.arange(LANES) % HEAD_DIM
    half = ROT_DIM // 2
    inv_freq = jnp.power(ROPE_THETA, (dim % half).astype(F32) * (-2.0 / ROT_DIM))
    invf = jnp.where(dim < ROT_DIM, inv_freq, 0.0)[None, :].astype(F32)
    c_start = CMP_STRIDE * jnp.arange(ncp)[None, :]
    s_start = SEL_BLOCK * jnp.arange(nsp)[:, None]
    ovlt = ((c_start < s_start + SEL_BLOCK) & (c_start + CMP_BLOCK > s_start) & (s_start < s)).astype(MXU_DTYPE)

    outs = []
    for bi in range(b):
        xb = x[bi]
        pos = positions[bi][:, None]
        for layer in range(depth):
            (qpt, qrt, kc, vc, ks, vst, kw, vwt, gates_t, sbq, sbkt, sbv, mg) = _inproj(
                xb, g_pre, pos, invf, w_in_r, layer)
            kcmp, vcmpt = _compress(kc.reshape(ncp, half_w), vc.reshape(ncp, half_w), pe, w1, w2, layer)
            nsa_o = _nsa(qpt, qrt, gates_t, kcmp, vcmpt, ovlt, ks, vst, kw, vwt)
            sb_o = _sb(sbq, sbkt, sbv)
            xb = _mix(xb, nsa_o, sb_o, mg, wn, ws, wo, g_mix, layer)
            xb = _ffn(xb, g_ffn_in, wf1, wf2, g_ffn_out, layer)
        outs.append(xb)
    return jnp.stack(outs, axis=0)
```
